```python
import jax
import jax.numpy as jnp
from jax import lax
import numpy as np

D_MODEL = 1024
BATCH = 8
SEQ = 8192
DEPTH = 2

N_META = 16
NORM_EPS = 1e-6
SSM_D_INNER = 2 * D_MODEL
SSM_HEAD_DIM = 64
SSM_HEADS = SSM_D_INNER // SSM_HEAD_DIM
SSM_GROUPS = 4
SSM_HEADS_PER_GROUP = SSM_HEADS // SSM_GROUPS
SSM_STATE = 128
SSM_CONV = 4
SSM_CHUNK = 256
SSM_CONV_DIM = SSM_D_INNER + 2 * SSM_GROUPS * SSM_STATE
SSM_IN_DIM = SSM_D_INNER + SSM_CONV_DIM + SSM_HEADS
SB_HEAD_DIM = 64
SB_HEADS = D_MODEL // SB_HEAD_DIM
SB_WIDTH = SB_HEADS * SB_HEAD_DIM
SB_Q_BLOCK = 128
D_FF = 256 * ((8 * D_MODEL // 3 + 255) // 256)
FFN_CONV = 3

kernel_name = 'hybrid_ssd_stickbreaking_yoco'


def _rmsnorm(x, g):
    x32 = x.astype(jnp.float32)
    y = x32 * lax.rsqrt(jnp.mean(x32 * x32, axis=-1, keepdims=True) + NORM_EPS)
    return (y * g.astype(jnp.float32)).astype(x.dtype)


def _causal_dwconv(x, w, bias):
    width = w.shape[0]
    L = x.shape[1]
    xp = jnp.pad(x, ((0, 0), (width - 1, 0), (0, 0)))
    y = xp[:, 0:L] * w[0] + bias
    for k in range(1, width):
        y = y + xp[:, k:k + L] * w[k]
    return y


def _ssd_mixer(u, w_in, conv_w, conv_b, dt_bias, a_log, d_skip, gate_g, w_out):
    b, L, _ = u.shape
    G, E, P, N, Q = SSM_GROUPS, SSM_HEADS_PER_GROUP, SSM_HEAD_DIM, SSM_STATE, SSM_CHUNK
    f32 = jnp.float32
    z, xbc, dt_raw = jnp.split(u @ w_in, [SSM_D_INNER, SSM_D_INNER + SSM_CONV_DIM], axis=-1)
    xbc = jax.nn.silu(_causal_dwconv(xbc, conv_w, conv_b))
    xs, b_in, c_in = jnp.split(xbc, [SSM_D_INNER, SSM_D_INNER + G * N], axis=-1)
    dt = jax.nn.softplus(dt_raw.astype(f32) + dt_bias.astype(f32))
    a = -jnp.exp(a_log.astype(f32))
    pf = (-N_META) % Q
    pe = (-(pf + L)) % Q
    nc = (pf + L + pe) // Q

    def to_chunks(t, tail):
        return jnp.pad(t, ((0, 0), (pf, pe), (0, 0))).reshape((b, nc, Q) + tail)

    x_c = to_chunks(xs, (G, E, P)).astype(f32)
    b_c = to_chunks(b_in, (G, N)).astype(f32)
    c_c = to_chunks(c_in, (G, N)).astype(f32)
    dt_c = to_chunks(dt, (G, E))
    xdt = x_c * dt_c[..., None]
    a_cs = jnp.cumsum(jnp.transpose(dt_c * a.reshape(G, E), (0, 3, 4, 1, 2)), axis=-1)
    causal = jnp.tril(jnp.ones((Q, Q), dtype=bool))
    decay_in = jnp.exp(jnp.where(causal, a_cs[..., :, None] - a_cs[..., None, :], -jnp.inf))
    cb = jnp.einsum('bclgn,bcsgn->bgcls', c_c, b_c)
    y_diag = jnp.einsum('bgcls,bgecls,bcsgep->bclgep', cb, decay_in, xdt)
    decay_to_end = jnp.exp(a_cs[..., -1:] - a_cs)
    chunk_states = jnp.einsum('bclgn,bgecl,bclgep->cbgepn', b_c, decay_to_end, xdt)
    chunk_decay = jnp.moveaxis(jnp.exp(a_cs[..., -1]), -1, 0)

    def step(state, inp):
        s_new, d = inp
        return state * d[..., None, None] + s_new, state

    _, prev_states = lax.scan(step, jnp.zeros((b, G, E, P, N), f32), (chunk_states, chunk_decay))
    y_off = jnp.einsum('bclgn,cbgepn,bgecl->bclgep', c_c, prev_states, jnp.exp(a_cs))
    y = (y_diag + y_off).reshape(b, nc * Q, SSM_D_INNER)[:, pf:pf + L]
    y = y + (xs.reshape(b, L, SSM_HEADS, P).astype(f32) * d_skip.astype(f32)[:, None]).reshape(b, L, SSM_D_INNER)
    hg = (y * jax.nn.silu(z.astype(f32))).reshape(b, L, G, SSM_D_INNER // G)
    hg = hg * lax.rsqrt(jnp.mean(hg * hg, axis=-1, keepdims=True) + NORM_EPS)
    hg = hg.reshape(b, L, SSM_D_INNER) * gate_g.astype(f32)
    return hg.astype(u.dtype) @ w_out


def _stick_breaking_attention(q, k, v):
    b, L, H, Dh = q.shape
    lp = -(-L // SB_Q_BLOCK) * SB_Q_BLOCK
    pad = ((0, 0), (0, lp - L), (0, 0), (0, 0))
    q, k, v = jnp.pad(q, pad), jnp.pad(k, pad), jnp.pad(v, pad)
    scale = Dh ** -0.5
    outs = []
    for i in range(lp // SB_Q_BLOCK):
        t0, t1 = i * SB_Q_BLOCK, (i + 1) * SB_Q_BLOCK
        logits = jnp.einsum('bthd,bshd->bhts', q[:, t0:t1], k[:, :t1]).astype(jnp.float32) * scale
        t_idx = t0 + jnp.arange(SB_Q_BLOCK)[:, None]
        s_idx = jnp.arange(t1)[None, :]
        visible = s_idx < t_idx
        log_keep = jnp.where(visible, jax.nn.log_sigmoid(-logits), 0.0)
        later = lax.cumsum(log_keep, axis=3, reverse=True) - log_keep
        log_w = jnp.where(visible, jax.nn.log_sigmoid(logits) + later, -jnp.inf)
        w = jnp.exp(log_w).astype(v.dtype)
        outs.append(jnp.einsum('bhts,bshd->bthd', w, v[:, :t1]))
    return jnp.concatenate(outs, axis=1)[:, :L]


def _conv_ffn(u, w_up, conv_w, conv_b, w_down):
    h = _causal_dwconv(u @ w_up, conv_w, conv_b)
    g, val = jnp.split(h, 2, axis=-1)
    return (jax.nn.silu(g) * val) @ w_down


def _fwd_setup_inputs(seed: int = 0) -> dict:
    key = jax.random.key(seed)
    ks = jax.random.split(key, 32)
    n_a = DEPTH // 2
    n_b = DEPTH - n_a
    f32 = jnp.float32

    def nrm(k, shape, scale):
        return jax.random.normal(k, shape, f32) * scale

    def gain(k, shape):
        return 1.0 + 0.02 * jax.random.normal(k, shape, f32)

    dt0 = jnp.exp(jax.random.uniform(ks[6], (n_a, SSM_HEADS), f32, np.log(1e-3), np.log(1e-1)))
    dt_bias = dt0 + jnp.log(-jnp.expm1(-dt0))
    return {
        'x': jax.random.normal(ks[0], (BATCH, SEQ, D_MODEL), f32),
        'meta_tokens': nrm(ks[1], (N_META, D_MODEL), 1.0),
        'ssd_norm': gain(ks[2], (n_a, D_MODEL)),
        'ssd_w_in': nrm(ks[3], (n_a, D_MODEL, SSM_IN_DIM), D_MODEL ** -0.5),
        'ssd_conv_w': nrm(ks[4], (n_a, SSM_CONV, SSM_CONV_DIM), SSM_CONV ** -0.5),
        'ssd_conv_b': nrm(ks[5], (n_a, SSM_CONV_DIM), 0.02),
        'ssd_dt_bias': dt_bias,
        'ssd_a_log': jnp.log(jax.random.uniform(ks[7], (n_a, SSM_HEADS), f32, 1.0, 16.0)),
        'ssd_d_skip': jax.random.uniform(ks[8], (n_a, SSM_HEADS), f32, 0.5, 1.5),
        'ssd_gate_norm': gain(ks[9], (n_a, SSM_D_INNER)),
        'ssd_w_out': nrm(ks[10], (n_a, SSM_D_INNER, D_MODEL), SSM_D_INNER ** -0.5),
        'kv_norm': gain(ks[11], (D_MODEL,)),
        'w_kv': nrm(ks[12], (D_MODEL, 2 * SB_WIDTH), D_MODEL ** -0.5),
        'sb_norm': gain(ks[13], (n_b, D_MODEL)),
        'sb_w_q': nrm(ks[14], (n_b, D_MODEL, SB_WIDTH), D_MODEL ** -0.5),
        'sb_w_o': nrm(ks[15], (n_b, SB_WIDTH, D_MODEL), SB_WIDTH ** -0.5),
        'ffn_norm': gain(ks[16], (DEPTH, D_MODEL)),
        'ffn_w_up': nrm(ks[17], (DEPTH, D_MODEL, 2 * D_FF), D_MODEL ** -0.5),
        'ffn_conv_w': nrm(ks[18], (DEPTH, FFN_CONV, 2 * D_FF), FFN_CONV ** -0.5),
        'ffn_conv_b': nrm(ks[19], (DEPTH, 2 * D_FF), 0.02),
        'ffn_w_down': nrm(ks[20], (DEPTH, D_FF, D_MODEL), D_FF ** -0.5),
        'final_norm': gain(ks[21], (D_MODEL,)),
    }


def _fwd_reference(x, meta_tokens, ssd_norm, ssd_w_in, ssd_conv_w, ssd_conv_b, ssd_dt_bias, ssd_a_log,
              ssd_d_skip, ssd_gate_norm, ssd_w_out, kv_norm, w_kv, sb_norm, sb_w_q, sb_w_o,
              ffn_norm, ffn_w_up, ffn_conv_w, ffn_conv_b, ffn_w_down, final_norm):
    b = x.shape[0]
    n_a = DEPTH // 2
    h = jnp.concatenate([jnp.broadcast_to(meta_tokens[None], (b, N_META, D_MODEL)).astype(x.dtype), x], axis=1)
    L = h.shape[1]
    k_shared = None
    v_shared = None
    for layer in range(DEPTH):
        if layer < n_a:
            h = h + _ssd_mixer(_rmsnorm(h, ssd_norm[layer]), ssd_w_in[layer], ssd_conv_w[layer],
                               ssd_conv_b[layer], ssd_dt_bias[layer], ssd_a_log[layer],
                               ssd_d_skip[layer], ssd_gate_norm[layer], ssd_w_out[layer])
        else:
            if layer == n_a:
                kv = _rmsnorm(h, kv_norm) @ w_kv
                k_shared, v_shared = jnp.split(kv.reshape(b, L, 2, SB_HEADS, SB_HEAD_DIM), 2, axis=2)
                k_shared, v_shared = k_shared[:, :, 0], v_shared[:, :, 0]
            j = layer - n_a
            q = (_rmsnorm(h, sb_norm[j]) @ sb_w_q[j]).reshape(b, L, SB_HEADS, SB_HEAD_DIM)
            o = _stick_breaking_attention(q, k_shared, v_shared).reshape(b, L, SB_WIDTH)
            h = h + o @ sb_w_o[j]
        h = h + _conv_ffn(_rmsnorm(h, ffn_norm[layer]), ffn_w_up[layer], ffn_conv_w[layer],
                          ffn_conv_b[layer], ffn_w_down[layer])
    return _rmsnorm(h, final_norm)[:, N_META:]


import jax as _jax
import jax.numpy as _jnp

TWIN_FORMAT = 'train_step'
FWD_PARAMS = ['x', 'meta_tokens', 'ssd_norm', 'ssd_w_in', 'ssd_conv_w', 'ssd_conv_b', 'ssd_dt_bias', 'ssd_a_log', 'ssd_d_skip', 'ssd_gate_norm', 'ssd_w_out', 'kv_norm', 'w_kv', 'sb_norm', 'sb_w_q', 'sb_w_o', 'ffn_norm', 'ffn_w_up', 'ffn_conv_w', 'ffn_conv_b', 'ffn_w_down', 'final_norm']
TWIN_WEIGHTS = ['meta_tokens', 'ssd_norm', 'ssd_w_in', 'ssd_conv_w', 'ssd_conv_b', 'ssd_dt_bias', 'ssd_a_log', 'ssd_d_skip', 'ssd_gate_norm', 'ssd_w_out', 'kv_norm', 'w_kv', 'sb_norm', 'sb_w_q', 'sb_w_o', 'ffn_norm', 'ffn_w_up', 'ffn_conv_w', 'ffn_conv_b', 'ffn_w_down', 'final_norm']
TWIN_DIFF_INPUT = 'x'
TWIN_INPUTS = ['x', 'meta_tokens', 'ssd_norm', 'ssd_w_in', 'ssd_conv_w', 'ssd_conv_b', 'ssd_dt_bias', 'ssd_a_log', 'ssd_d_skip', 'ssd_gate_norm', 'ssd_w_out', 'kv_norm', 'w_kv', 'sb_norm', 'sb_w_q', 'sb_w_o', 'ffn_norm', 'ffn_w_up', 'ffn_conv_w', 'ffn_conv_b', 'ffn_w_down', 'final_norm', 'loss_target', 'm_meta_tokens', 'm_ssd_norm', 'm_ssd_w_in', 'm_ssd_conv_w', 'm_ssd_conv_b', 'm_ssd_dt_bias', 'm_ssd_a_log', 'm_ssd_d_skip', 'm_ssd_gate_norm', 'm_ssd_w_out', 'm_kv_norm', 'm_w_kv', 'm_sb_norm', 'm_sb_w_q', 'm_sb_w_o', 'm_ffn_norm', 'm_ffn_w_up', 'm_ffn_conv_w', 'm_ffn_conv_b', 'm_ffn_w_down', 'm_final_norm', 'v_meta_tokens', 'v_ssd_norm', 'v_ssd_w_in', 'v_ssd_conv_w', 'v_ssd_conv_b', 'v_ssd_dt_bias', 'v_ssd_a_log', 'v_ssd_d_skip', 'v_ssd_gate_norm', 'v_ssd_w_out', 'v_kv_norm', 'v_w_kv', 'v_sb_norm', 'v_sb_w_q', 'v_sb_w_o', 'v_ffn_norm', 'v_ffn_w_up', 'v_ffn_conv_w', 'v_ffn_conv_b', 'v_ffn_w_down', 'v_final_norm']
TWIN_OUTPUTS = ['loss', 'grad_x', 'grad_meta_tokens', 'grad_ssd_norm', 'grad_ssd_w_in', 'grad_ssd_conv_w', 'grad_ssd_conv_b', 'grad_ssd_dt_bias', 'grad_ssd_a_log', 'grad_ssd_d_skip', 'grad_ssd_gate_norm', 'grad_ssd_w_out', 'grad_kv_norm', 'grad_w_kv', 'grad_sb_norm', 'grad_sb_w_q', 'grad_sb_w_o', 'grad_ffn_norm', 'grad_ffn_w_up', 'grad_ffn_conv_w', 'grad_ffn_conv_b', 'grad_ffn_w_down', 'grad_final_norm', 'delta_meta_tokens', 'delta_ssd_norm', 'delta_ssd_w_in', 'delta_ssd_conv_w', 'delta_ssd_conv_b', 'delta_ssd_dt_bias', 'delta_ssd_a_log', 'delta_ssd_d_skip', 'delta_ssd_gate_norm', 'delta_ssd_w_out', 'delta_kv_norm', 'delta_w_kv', 'delta_sb_norm', 'delta_sb_w_q', 'delta_sb_w_o', 'delta_ffn_norm', 'delta_ffn_w_up', 'delta_ffn_conv_w', 'delta_ffn_conv_b', 'delta_ffn_w_down', 'delta_final_norm', 'new_m_meta_tokens', 'new_m_ssd_norm', 'new_m_ssd_w_in', 'new_m_ssd_conv_w', 'new_m_ssd_conv_b', 'new_m_ssd_dt_bias', 'new_m_ssd_a_log', 'new_m_ssd_d_skip', 'new_m_ssd_gate_norm', 'new_m_ssd_w_out', 'new_m_kv_norm', 'new_m_w_kv', 'new_m_sb_norm', 'new_m_sb_w_q', 'new_m_sb_w_o', 'new_m_ffn_norm', 'new_m_ffn_w_up', 'new_m_ffn_conv_w', 'new_m_ffn_conv_b', 'new_m_ffn_w_down', 'new_m_final_norm', 'new_v_meta_tokens', 'new_v_ssd_norm', 'new_v_ssd_w_in', 'new_v_ssd_conv_w', 'new_v_ssd_conv_b', 'new_v_ssd_dt_bias', 'new_v_ssd_a_log', 'new_v_ssd_d_skip', 'new_v_ssd_gate_norm', 'new_v_ssd_w_out', 'new_v_kv_norm', 'new_v_w_kv', 'new_v_sb_norm', 'new_v_sb_w_q', 'new_v_sb_w_o', 'new_v_ffn_norm', 'new_v_ffn_w_up', 'new_v_ffn_conv_w', 'new_v_ffn_conv_b', 'new_v_ffn_w_down', 'new_v_final_norm']
TWIN_LEAF_KINDS = {'loss': 'loss', 'grad_x': 'grad_x', 'grad_meta_tokens': 'grad_w', 'grad_ssd_norm': 'grad_w', 'grad_ssd_w_in': 'grad_w', 'grad_ssd_conv_w': 'grad_w', 'grad_ssd_conv_b': 'grad_w', 'grad_ssd_dt_bias': 'grad_w', 'grad_ssd_a_log': 'grad_w', 'grad_ssd_d_skip': 'grad_w', 'grad_ssd_gate_norm': 'grad_w', 'grad_ssd_w_out': 'grad_w', 'grad_kv_norm': 'grad_w', 'grad_w_kv': 'grad_w', 'grad_sb_norm': 'grad_w', 'grad_sb_w_q': 'grad_w', 'grad_sb_w_o': 'grad_w', 'grad_ffn_norm': 'grad_w', 'grad_ffn_w_up': 'grad_w', 'grad_ffn_conv_w': 'grad_w', 'grad_ffn_conv_b': 'grad_w', 'grad_ffn_w_down': 'grad_w', 'grad_final_norm': 'grad_w', 'delta_meta_tokens': 'delta_w', 'delta_ssd_norm': 'delta_w', 'delta_ssd_w_in': 'delta_w', 'delta_ssd_conv_w': 'delta_w', 'delta_ssd_conv_b': 'delta_w', 'delta_ssd_dt_bias': 'delta_w', 'delta_ssd_a_log': 'delta_w', 'delta_ssd_d_skip': 'delta_w', 'delta_ssd_gate_norm': 'delta_w', 'delta_ssd_w_out': 'delta_w', 'delta_kv_norm': 'delta_w', 'delta_w_kv': 'delta_w', 'delta_sb_norm': 'delta_w', 'delta_sb_w_q': 'delta_w', 'delta_sb_w_o': 'delta_w', 'delta_ffn_norm': 'delta_w', 'delta_ffn_w_up': 'delta_w', 'delta_ffn_conv_w': 'delta_w', 'delta_ffn_conv_b': 'delta_w', 'delta_ffn_w_down': 'delta_w', 'delta_final_norm': 'delta_w', 'new_m_meta_tokens': 'new_m', 'new_m_ssd_norm': 'new_m', 'new_m_ssd_w_in': 'new_m', 'new_m_ssd_conv_w': 'new_m', 'new_m_ssd_conv_b': 'new_m', 'new_m_ssd_dt_bias': 'new_m', 'new_m_ssd_a_log': 'new_m', 'new_m_ssd_d_skip': 'new_m', 'new_m_ssd_gate_norm': 'new_m', 'new_m_ssd_w_out': 'new_m', 'new_m_kv_norm': 'new_m', 'new_m_w_kv': 'new_m', 'new_m_sb_norm': 'new_m', 'new_m_sb_w_q': 'new_m', 'new_m_sb_w_o': 'new_m', 'new_m_ffn_norm': 'new_m', 'new_m_ffn_w_up': 'new_m', 'new_m_ffn_conv_w': 'new_m', 'new_m_ffn_conv_b': 'new_m', 'new_m_ffn_w_down': 'new_m', 'new_m_final_norm': 'new_m', 'new_v_meta_tokens': 'new_v', 'new_v_ssd_norm': 'new_v', 'new_v_ssd_w_in': 'new_v', 'new_v_ssd_conv_w': 'new_v', 'new_v_ssd_conv_b': 'new_v', 'new_v_ssd_dt_bias': 'new_v', 'new_v_ssd_a_log': 'new_v', 'new_v_ssd_d_skip': 'new_v', 'new_v_ssd_gate_norm': 'new_v', 'new_v_ssd_w_out': 'new_v', 'new_v_kv_norm': 'new_v', 'new_v_w_kv': 'new_v', 'new_v_sb_norm': 'new_v', 'new_v_sb_w_q': 'new_v', 'new_v_sb_w_o': 'new_v', 'new_v_ffn_norm': 'new_v', 'new_v_ffn_w_up': 'new_v', 'new_v_ffn_conv_w': 'new_v', 'new_v_ffn_conv_b': 'new_v', 'new_v_ffn_w_down': 'new_v', 'new_v_final_norm': 'new_v'}


def _forward(args):
    return _fwd_reference(*[args[k] for k in FWD_PARAMS])


def _output_shape():
    def fwd():
        inp = _fwd_setup_inputs(0)
        return _fwd_reference(*[inp[k] for k in FWD_PARAMS])
    out = _jax.eval_shape(fwd)
    return out.shape, out.dtype

N_MICROBATCH = 1
ADAM_LR = 0.001
ADAM_B1 = 0.9
ADAM_B2 = 0.999
ADAM_EPS = 1e-08
ADAM_WD = 0.01
ADAM_STEP = 10
PER_EXAMPLE_BATCH_AXIS = {'x': 0, 'loss_target': 0}
SHARED_INPUTS = []
_WEIGHT_DTYPES = {'meta_tokens': _jnp.float32, 'ssd_norm': _jnp.float32, 'ssd_w_in': _jnp.float32, 'ssd_conv_w': _jnp.float32, 'ssd_conv_b': _jnp.float32, 'ssd_dt_bias': _jnp.float32, 'ssd_a_log': _jnp.float32, 'ssd_d_skip': _jnp.float32, 'ssd_gate_norm': _jnp.float32, 'ssd_w_out': _jnp.float32, 'kv_norm': _jnp.float32, 'w_kv': _jnp.float32, 'sb_norm': _jnp.float32, 'sb_w_q': _jnp.float32, 'sb_w_o': _jnp.float32, 'ffn_norm': _jnp.float32, 'ffn_w_up': _jnp.float32, 'ffn_conv_w': _jnp.float32, 'ffn_conv_b': _jnp.float32, 'ffn_w_down': _jnp.float32, 'final_norm': _jnp.float32}
MOMENT_SCALE = {'meta_tokens': 8.060319e-03, 'ssd_norm': 3.143737e-01, 'ssd_w_in': 1.399858e-01, 'ssd_conv_w': 1.273595e-01, 'ssd_conv_b': 1.769779e-01, 'ssd_dt_bias': 2.416068e-01, 'ssd_a_log': 4.560561e-01, 'ssd_d_skip': 7.194318e-01, 'ssd_gate_norm': 1.504880e-01, 'ssd_w_out': 2.128212e-01, 'kv_norm': 1.221081e-01, 'w_kv': 8.127627e-02, 'sb_norm': 5.247746e-02, 'sb_w_q': 4.729015e-02, 'sb_w_o': 1.052215e-01, 'ffn_norm': 1.433646e-01, 'ffn_w_up': 5.928158e-02, 'ffn_conv_w': 6.150111e-02, 'ffn_conv_b': 5.985390e-02, 'ffn_w_down': 9.678046e-02, 'final_norm': 6.410568e+01}


def _to_microbatches(a, axis):
    t = _jnp.moveaxis(a, axis, 0)
    t = t.reshape((N_MICROBATCH, t.shape[0] // N_MICROBATCH) + t.shape[1:])
    return _jnp.moveaxis(t, 1, axis + 1)


def setup_inputs(seed: int = 0) -> dict:
    inp = _fwd_setup_inputs(seed)
    key = _jax.random.fold_in(_jax.random.key(seed), 7919)
    shape, _ = _output_shape()
    out = dict(inp)
    out["loss_target"] = _jax.random.normal(_jax.random.fold_in(key, 0), shape, _jnp.float32)
    for i, name in enumerate(TWIN_WEIGHTS):
        w = inp[name].astype(_jnp.float32)
        if MOMENT_SCALE is None:
            s = _jnp.sqrt(_jnp.mean(_jnp.square(w)) + 1e-30)
        else:
            s = MOMENT_SCALE[name]
        km, kv = _jax.random.split(_jax.random.fold_in(key, i + 1))
        out[name] = w
        out["m_" + name] = s * _jax.random.normal(km, w.shape, _jnp.float32)
        out["v_" + name] = (s * s) * _jax.random.uniform(kv, w.shape, _jnp.float32, 0.5, 1.5)
    if N_MICROBATCH > 1:
        for name, axis in PER_EXAMPLE_BATCH_AXIS.items():
            out[name] = _to_microbatches(out[name], axis)
    return {'x': out['x'], 'meta_tokens': out['meta_tokens'], 'ssd_norm': out['ssd_norm'], 'ssd_w_in': out['ssd_w_in'], 'ssd_conv_w': out['ssd_conv_w'], 'ssd_conv_b': out['ssd_conv_b'], 'ssd_dt_bias': out['ssd_dt_bias'], 'ssd_a_log': out['ssd_a_log'], 'ssd_d_skip': out['ssd_d_skip'], 'ssd_gate_norm': out['ssd_gate_norm'], 'ssd_w_out': out['ssd_w_out'], 'kv_norm': out['kv_norm'], 'w_kv': out['w_kv'], 'sb_norm': out['sb_norm'], 'sb_w_q': out['sb_w_q'], 'sb_w_o': out['sb_w_o'], 'ffn_norm': out['ffn_norm'], 'ffn_w_up': out['ffn_w_up'], 'ffn_conv_w': out['ffn_conv_w'], 'ffn_conv_b': out['ffn_conv_b'], 'ffn_w_down': out['ffn_w_down'], 'final_norm': out['final_norm'], 'loss_target': out['loss_target'], 'm_meta_tokens': out['m_meta_tokens'], 'm_ssd_norm': out['m_ssd_norm'], 'm_ssd_w_in': out['m_ssd_w_in'], 'm_ssd_conv_w': out['m_ssd_conv_w'], 'm_ssd_conv_b': out['m_ssd_conv_b'], 'm_ssd_dt_bias': out['m_ssd_dt_bias'], 'm_ssd_a_log': out['m_ssd_a_log'], 'm_ssd_d_skip': out['m_ssd_d_skip'], 'm_ssd_gate_norm': out['m_ssd_gate_norm'], 'm_ssd_w_out': out['m_ssd_w_out'], 'm_kv_norm': out['m_kv_norm'], 'm_w_kv': out['m_w_kv'], 'm_sb_norm': out['m_sb_norm'], 'm_sb_w_q': out['m_sb_w_q'], 'm_sb_w_o': out['m_sb_w_o'], 'm_ffn_norm': out['m_ffn_norm'], 'm_ffn_w_up': out['m_ffn_w_up'], 'm_ffn_conv_w': out['m_ffn_conv_w'], 'm_ffn_conv_b': out['m_ffn_conv_b'], 'm_ffn_w_down': out['m_ffn_w_down'], 'm_final_norm': out['m_final_norm'], 'v_meta_tokens': out['v_meta_tokens'], 'v_ssd_norm': out['v_ssd_norm'], 'v_ssd_w_in': out['v_ssd_w_in'], 'v_ssd_conv_w': out['v_ssd_conv_w'], 'v_ssd_conv_b': out['v_ssd_conv_b'], 'v_ssd_dt_bias': out['v_ssd_dt_bias'], 'v_ssd_a_log': out['v_ssd_a_log'], 'v_ssd_d_skip': out['v_ssd_d_skip'], 'v_ssd_gate_norm': out['v_ssd_gate_norm'], 'v_ssd_w_out': out['v_ssd_w_out'], 'v_kv_norm': out['v_kv_norm'], 'v_w_kv': out['v_w_kv'], 'v_sb_norm': out['v_sb_norm'], 'v_sb_w_q': out['v_sb_w_q'], 'v_sb_w_o': out['v_sb_w_o'], 'v_ffn_norm': out['v_ffn_norm'], 'v_ffn_w_up': out['v_ffn_w_up'], 'v_ffn_conv_w': out['v_ffn_conv_w'], 'v_ffn_conv_b': out['v_ffn_conv_b'], 'v_ffn_w_down': out['v_ffn_w_down'], 'v_final_norm': out['v_final_norm']}


def _loss(weights, diff, rest, loss_target):
    with _jax.named_scope("forward"):
        args = {**rest, TWIN_DIFF_INPUT: diff, **{k: w.astype(_WEIGHT_DTYPES[k]) for k, w in weights.items()}}
        y = _forward(args)
    with _jax.named_scope("loss_head"):
        err = _jnp.square(y.astype(_jnp.float32) - loss_target)
        return 0.5 * _jnp.sum(_jnp.mean(err, axis=-1)) if err.ndim else 0.5 * err


def _adamw(w, g, m, v):
    m = ADAM_B1 * m + (1.0 - ADAM_B1) * g
    v = ADAM_B2 * v + (1.0 - ADAM_B2) * _jnp.square(g)
    m_hat = m / (1.0 - ADAM_B1 ** ADAM_STEP)
    v_hat = v / (1.0 - ADAM_B2 ** ADAM_STEP)
    delta = -ADAM_LR * (m_hat / (_jnp.sqrt(v_hat) + ADAM_EPS) + ADAM_WD * w)
    return delta, m, v


def reference(x, meta_tokens, ssd_norm, ssd_w_in, ssd_conv_w, ssd_conv_b, ssd_dt_bias, ssd_a_log, ssd_d_skip, ssd_gate_norm, ssd_w_out, kv_norm, w_kv, sb_norm, sb_w_q, sb_w_o, ffn_norm, ffn_w_up, ffn_conv_w, ffn_conv_b, ffn_w_down, final_norm, loss_target, m_meta_tokens, m_ssd_norm, m_ssd_w_in, m_ssd_conv_w, m_ssd_conv_b, m_ssd_dt_bias, m_ssd_a_log, m_ssd_d_skip, m_ssd_gate_norm, m_ssd_w_out, m_kv_norm, m_w_kv, m_sb_norm, m_sb_w_q, m_sb_w_o, m_ffn_norm, m_ffn_w_up, m_ffn_conv_w, m_ffn_conv_b, m_ffn_w_down, m_final_norm, v_meta_tokens, v_ssd_norm, v_ssd_w_in, v_ssd_conv_w, v_ssd_conv_b, v_ssd_dt_bias, v_ssd_a_log, v_ssd_d_skip, v_ssd_gate_norm, v_ssd_w_out, v_kv_norm, v_w_kv, v_sb_norm, v_sb_w_q, v_sb_w_o, v_ffn_norm, v_ffn_w_up, v_ffn_conv_w, v_ffn_conv_b, v_ffn_w_down, v_final_norm):
    given = dict(x=x, meta_tokens=meta_tokens, ssd_norm=ssd_norm, ssd_w_in=ssd_w_in, ssd_conv_w=ssd_conv_w, ssd_conv_b=ssd_conv_b, ssd_dt_bias=ssd_dt_bias, ssd_a_log=ssd_a_log, ssd_d_skip=ssd_d_skip, ssd_gate_norm=ssd_gate_norm, ssd_w_out=ssd_w_out, kv_norm=kv_norm, w_kv=w_kv, sb_norm=sb_norm, sb_w_q=sb_w_q, sb_w_o=sb_w_o, ffn_norm=ffn_norm, ffn_w_up=ffn_w_up, ffn_conv_w=ffn_conv_w, ffn_conv_b=ffn_conv_b, ffn_w_down=ffn_w_down, final_norm=final_norm, loss_target=loss_target, m_meta_tokens=m_meta_tokens, m_ssd_norm=m_ssd_norm, m_ssd_w_in=m_ssd_w_in, m_ssd_conv_w=m_ssd_conv_w, m_ssd_conv_b=m_ssd_conv_b, m_ssd_dt_bias=m_ssd_dt_bias, m_ssd_a_log=m_ssd_a_log, m_ssd_d_skip=m_ssd_d_skip, m_ssd_gate_norm=m_ssd_gate_norm, m_ssd_w_out=m_ssd_w_out, m_kv_norm=m_kv_norm, m_w_kv=m_w_kv, m_sb_norm=m_sb_norm, m_sb_w_q=m_sb_w_q, m_sb_w_o=m_sb_w_o, m_ffn_norm=m_ffn_norm, m_ffn_w_up=m_ffn_w_up, m_ffn_conv_w=m_ffn_conv_w, m_ffn_conv_b=m_ffn_conv_b, m_ffn_w_down=m_ffn_w_down, m_final_norm=m_final_norm, v_meta_tokens=v_meta_tokens, v_ssd_norm=v_ssd_norm, v_ssd_w_in=v_ssd_w_in, v_ssd_conv_w=v_ssd_conv_w, v_ssd_conv_b=v_ssd_conv_b, v_ssd_dt_bias=v_ssd_dt_bias, v_ssd_a_log=v_ssd_a_log, v_ssd_d_skip=v_ssd_d_skip, v_ssd_gate_norm=v_ssd_gate_norm, v_ssd_w_out=v_ssd_w_out, v_kv_norm=v_kv_norm, v_w_kv=v_w_kv, v_sb_norm=v_sb_norm, v_sb_w_q=v_sb_w_q, v_sb_w_o=v_sb_w_o, v_ffn_norm=v_ffn_norm, v_ffn_w_up=v_ffn_w_up, v_ffn_conv_w=v_ffn_conv_w, v_ffn_conv_b=v_ffn_conv_b, v_ffn_w_down=v_ffn_w_down, v_final_norm=v_final_norm)
    weights = {n: given[n] for n in TWIN_WEIGHTS}
    shared = {n: given[n] for n in SHARED_INPUTS}
    per_example = {n: given[n] for n in ['x']}
    grad_fn = _jax.value_and_grad(_loss, argnums=(0, 1))

    def one_microbatch(ex, loss_target):
        ex = dict(ex)
        diff = ex.pop(TWIN_DIFF_INPUT)
        return grad_fn(weights, diff, {**shared, **ex}, loss_target)

    if N_MICROBATCH == 1:
        loss, (grad_w, grad_x) = one_microbatch(per_example, given["loss_target"])
    else:
        def body(carry, xs):
            loss_sum, grad_sum = carry
            l_k, (gw_k, gx_k) = one_microbatch(xs[0], xs[1])
            with _jax.named_scope("update"):
                return (loss_sum + l_k, _jax.tree.map(_jnp.add, grad_sum, gw_k)), gx_k

        init = (_jnp.zeros((), _jnp.float32), _jax.tree.map(_jnp.zeros_like, weights))
        (loss, grad_w), grad_x = _jax.lax.scan(body, init, (per_example, given["loss_target"]))
    with _jax.named_scope("update"):
        delta_w, new_m, new_v = {}, {}, {}
        for n in TWIN_WEIGHTS:
            delta_w[n], new_m[n], new_v[n] = _adamw(weights[n], grad_w[n], given["m_" + n], given["v_" + n])
    return (loss, grad_x, *[grad_w[n] for n in TWIN_WEIGHTS], *[delta_w[n] for n in TWIN_WEIGHTS],
            *[new_m[n] for n in TWIN_WEIGHTS], *[new_v[n] for n in TWIN_WEIGHTS])
```

```python
import collections
import functools
import math

import jax
import jax.numpy as jnp
from jax import lax
from jax.experimental import pallas as pl
from jax.experimental.pallas import tpu as pltpu

F32 = jnp.float32
BF16 = jnp.bfloat16
NORM_EPS = 1e-6
N_META = 16
SSM_HEAD_DIM = 64
SSM_GROUPS = 4
SSM_STATE = 128
SSM_CONV = 4
SSM_CHUNK = 256
SB_HEAD_DIM = 64
FFN_CONV = 3
FFN_IL = 256
ATT_T = 256
ADAM_LR, ADAM_B1, ADAM_B2, ADAM_EPS, ADAM_WD, ADAM_STEP = 0.001, 0.9, 0.999, 1e-08, 0.01, 10
N_DEV = 8
VMEM_LIMIT = 56 * 1024 * 1024

Cfg = collections.namedtuple("Cfg", "D SEQ LP PF DI H HG GW CONV_DIM DFF HS")


def _make_cfg(d_model, seq, d_ff):
    pf = (-N_META) % SSM_CHUNK
    lp = pf + N_META + seq
    assert lp % SSM_CHUNK == 0 and (pf + N_META) == SSM_CHUNK
    di = 2 * d_model
    h = di // SSM_HEAD_DIM
    return Cfg(D=d_model, SEQ=seq, LP=lp, PF=pf, DI=di, H=h, HG=h // SSM_GROUPS, GW=di // SSM_GROUPS,
               CONV_DIM=di + 2 * SSM_GROUPS * SSM_STATE, DFF=d_ff, HS=d_model // SB_HEAD_DIM)


def _tile(n, pref, mult):
    t = (min(pref, n) // mult) * mult
    while t > mult and n % t:
        t -= mult
    assert t >= mult and n % t == 0, (n, pref, mult)
    return t


def _cparams(sem):
    return pltpu.CompilerParams(dimension_semantics=sem, vmem_limit_bytes=VMEM_LIMIT)


def _dot(a, b):
    return jnp.dot(a, b, preferred_element_type=F32)


def _dot_nt(a, b):
    return lax.dot_general(a, b, (((1,), (1,)), ((), ())), preferred_element_type=F32)


def _dot_tn(a, b):
    return lax.dot_general(a, b, (((0,), (0,)), ((), ())), preferred_element_type=F32)


def _split3(v):
    hi = v.astype(BF16)
    r1 = v - hi.astype(F32)
    mid = r1.astype(BF16)
    lo = (r1 - mid.astype(F32)).astype(BF16)
    return hi, mid, lo


def _split2(v):
    hi = v.astype(BF16)
    lo = (v - hi.astype(F32)).astype(BF16)
    return hi, lo


def _dot3(a_f32, b_bf16):
    hi, mid, lo = _split3(a_f32)
    return _dot(hi, b_bf16) + _dot(mid, b_bf16) + _dot(lo, b_bf16)


def _dot3_left(a_bf16, b_f32):
    hi, mid, lo = _split3(b_f32)
    return _dot(a_bf16, hi) + _dot(a_bf16, mid) + _dot(a_bf16, lo)


def _sigmoid(x):
    return 1.0 / (1.0 + jnp.exp(-x))


def _softplus(x):
    return jnp.maximum(x, 0.0) + jnp.log(1.0 + jnp.exp(-jnp.abs(x)))


def _row_ids(i, tm, shape_cols=1):
    return i * tm + lax.broadcasted_iota(jnp.int32, (tm, shape_cols), 0)


def _norm_mm(h, g, w, *, out_dtype, name, scale=None):
    lp, d = h.shape
    n = w.shape[1]
    tm = _tile(lp, 768, 8)
    tn = _tile(n, 512, 128)

    def body(h_ref, g_ref, w_ref, y_ref, u_ref):
        @pl.when(pl.program_id(1) == 0)
        def _():
            x = h_ref[...]
            r = lax.rsqrt(jnp.mean(x * x, axis=-1, keepdims=True) + NORM_EPS)
            u_ref[...] = (x * r * g_ref[...]).astype(BF16)
        y = _dot(u_ref[...], w_ref[...])
        if scale is not None:
            y = y * scale
        y_ref[...] = y.astype(out_dtype)

    return pl.pallas_call(
        body, name=name, grid=(lp // tm, n // tn),
        in_specs=[pl.BlockSpec((tm, d), lambda i, j: (i, 0)), pl.BlockSpec((1, d), lambda i, j: (0, 0)),
                  pl.BlockSpec((d, tn), lambda i, j: (0, j))],
        out_specs=[pl.BlockSpec((tm, tn), lambda i, j: (i, j)), pl.BlockSpec((tm, d), lambda i, j: (i, 0))],
        out_shape=[jax.ShapeDtypeStruct((lp, n), out_dtype), jax.ShapeDtypeStruct((lp, d), BF16)],
        compiler_params=_cparams(("arbitrary", "arbitrary")),
    )(h, g, w)


def _mm_res(a, w, h, *, pf, name):
    lp, k = a.shape
    d = w.shape[1]
    tm = _tile(lp, 256, 8)

    def body(a_ref, w_ref, h_ref, o_ref):
        y = _dot(a_ref[...].astype(BF16), w_ref[...])
        rows = _row_ids(pl.program_id(0), tm)
        o_ref[...] = h_ref[...] + jnp.where(rows >= pf, y, 0.0)

    return pl.pallas_call(
        body, name=name, grid=(lp // tm,),
        in_specs=[pl.BlockSpec((tm, k), lambda i: (i, 0)), pl.BlockSpec((k, d), lambda i: (0, 0)),
                  pl.BlockSpec((tm, d), lambda i: (i, 0))],
        out_specs=pl.BlockSpec((tm, d), lambda i: (i, 0)),
        out_shape=jax.ShapeDtypeStruct((lp, d), F32),
        compiler_params=_cparams(("arbitrary",)),
    )(a, w, h)


def _mm_nt(dy, w, *, out_dtype, pf, name):
    lp, n = dy.shape
    k = w.shape[0]
    tm = _tile(lp, 768, 8)
    tk = _tile(k, 512, 128)

    def body(dy_ref, w_ref, o_ref):
        y = _dot_nt(dy_ref[...].astype(BF16), w_ref[...])
        rows = _row_ids(pl.program_id(0), tm)
        o_ref[...] = jnp.where(rows >= pf, y, 0.0).astype(out_dtype)

    return pl.pallas_call(
        body, name=name, grid=(lp // tm, k // tk),
        in_specs=[pl.BlockSpec((tm, n), lambda i, j: (i, 0)), pl.BlockSpec((tk, n), lambda i, j: (j, 0))],
        out_specs=pl.BlockSpec((tm, tk), lambda i, j: (i, j)),
        out_shape=jax.ShapeDtypeStruct((lp, k), out_dtype),
        compiler_params=_cparams(("arbitrary", "arbitrary")),
    )(dy, w)


def _mm_nt_normbwd(dy, w, h, g, dh_in, *, pf, name, scale=None):
    lp, n = dy.shape
    d = w.shape[0]
    tm = _tile(lp, 256, 8)
    tn = _tile(n, 1024, 128)
    nj = n // tn

    def body(dy_ref, w_ref, h_ref, g_ref, dhin_ref, dh_ref, dg_ref, acc_ref):
        i, j = pl.program_id(0), pl.program_id(1)

        @pl.when(j == 0)
        def _():
            acc_ref[...] = jnp.zeros_like(acc_ref)

        @pl.when((i == 0) & (j == 0))
        def _():
            dg_ref[...] = jnp.zeros_like(dg_ref)

        acc_ref[...] += _dot_nt(dy_ref[...].astype(BF16), w_ref[...])

        @pl.when(j == nj - 1)
        def _():
            du = acc_ref[...]
            if scale is not None:
                du = du * scale
            x = h_ref[...]
            r = lax.rsqrt(jnp.mean(x * x, axis=-1, keepdims=True) + NORM_EPS)
            xhat = x * r
            dg_ref[...] += jnp.sum(du * xhat, axis=0, keepdims=True)
            dxh = du * g_ref[...]
            dx = r * (dxh - xhat * jnp.mean(dxh * xhat, axis=-1, keepdims=True))
            rows = _row_ids(i, tm)
            dh_ref[...] = jnp.where(rows >= pf, dhin_ref[...] + dx, 0.0)

    return pl.pallas_call(
        body, name=name, grid=(lp // tm, nj),
        in_specs=[pl.BlockSpec((tm, tn), lambda i, j: (i, j)), pl.BlockSpec((d, tn), lambda i, j: (0, j)),
                  pl.BlockSpec((tm, d), lambda i, j: (i, 0)), pl.BlockSpec((1, d), lambda i, j: (0, 0)),
                  pl.BlockSpec((tm, d), lambda i, j: (i, 0))],
        out_specs=[pl.BlockSpec((tm, d), lambda i, j: (i, 0)), pl.BlockSpec((1, d), lambda i, j: (0, 0))],
        out_shape=[jax.ShapeDtypeStruct((lp, d), F32), jax.ShapeDtypeStruct((1, d), F32)],
        scratch_shapes=[pltpu.VMEM((tm, d), F32)],
        compiler_params=_cparams(("arbitrary", "arbitrary")),
    )(dy, w, h, g, dh_in)


def _mm_tn(a, b, *, name, scale=None):
    lp, k = a.shape
    n = b.shape[1]
    tm = _tile(lp, 768, 8)
    tk = _tile(k, 512, 128)
    tn = _tile(n, 1024, 128)
    nm = lp // tm

    def body(a_ref, b_ref, o_ref, acc_ref):
        m = pl.program_id(2)

        @pl.when(m == 0)
        def _():
            acc_ref[...] = jnp.zeros_like(acc_ref)

        acc_ref[...] += _dot_tn(a_ref[...].astype(BF16), b_ref[...].astype(BF16))

        @pl.when(m == nm - 1)
        def _():
            o_ref[...] = acc_ref[...] if scale is None else acc_ref[...] * scale

    return pl.pallas_call(
        body, name=name, grid=(k // tk, n // tn, nm),
        in_specs=[pl.BlockSpec((tm, tk), lambda i, j, m: (m, i)), pl.BlockSpec((tm, tn), lambda i, j, m: (m, j))],
        out_specs=pl.BlockSpec((tk, tn), lambda i, j, m: (i, j)),
        out_shape=jax.ShapeDtypeStruct((k, n), F32),
        scratch_shapes=[pltpu.VMEM((tk, tn), F32)],
        compiler_params=_cparams(("arbitrary", "arbitrary", "arbitrary")),
    )(a, b)


HALO = 8


def _conv_taps(ext_ref, w_ref, tm, width, cols=slice(None)):
    return [ext_ref[pl.ds(HALO - (width - 1 - k), tm), cols] for k in range(width)]


def _causal_conv(ext_ref, x_ref, w_ref, b_ref, tm, width):
    @pl.when(pl.program_id(1) == 0)
    def _():
        ext_ref[0:HALO, :] = jnp.zeros((HALO, ext_ref.shape[1]), F32)

    ext_ref[HALO:, :] = x_ref[...]
    taps = _conv_taps(ext_ref, w_ref, tm, width)
    acc = b_ref[...] + taps[0] * w_ref[0:1, :]
    for k in range(1, width):
        acc = acc + taps[k] * w_ref[k:k + 1, :]
    return acc, taps


def _ssd_conv_fwd(zx, cw, cb, *, cfg, name):
    lp, cd, di = cfg.LP, cfg.CONV_DIM, cfg.DI
    tc = _tile(math.gcd(di, cd), 512, 128)
    tm = _tile(lp, 768, 8)
    off = di // tc

    def body(x_ref, w_ref, b_ref, o_ref, ext_ref):
        acc, _ = _causal_conv(ext_ref, x_ref, w_ref, b_ref, tm, SSM_CONV)
        rows = _row_ids(pl.program_id(1), tm)
        o_ref[...] = jnp.where(rows >= cfg.PF, acc * _sigmoid(acc), 0.0)
        ext_ref[0:HALO, :] = x_ref[tm - HALO:tm, :]

    return pl.pallas_call(
        body, name=name, grid=(cd // tc, lp // tm),
        in_specs=[pl.BlockSpec((tm, tc), lambda j, i: (i, j + off)), pl.BlockSpec((SSM_CONV, tc), lambda j, i: (0, j)),
                  pl.BlockSpec((1, tc), lambda j, i: (0, j))],
        out_specs=pl.BlockSpec((tm, tc), lambda j, i: (i, j)),
        out_shape=jax.ShapeDtypeStruct((lp, cd), F32),
        scratch_shapes=[pltpu.VMEM((tm + HALO, tc), F32)],
        compiler_params=_cparams(("arbitrary", "arbitrary")),
    )(zx, cw, cb)


def _ssd_conv_bwd_pre(zx, dxbc, cw, cb, *, cfg, name):
    lp, cd, di = cfg.LP, cfg.CONV_DIM, cfg.DI
    tc = _tile(math.gcd(di, cd), 512, 128)
    tm = _tile(lp, 768, 8)
    off = di // tc

    def body(x_ref, d_ref, w_ref, b_ref, dc_ref, dw_ref, db_ref, ext_ref):
        i = pl.program_id(1)
        c, taps = _causal_conv(ext_ref, x_ref, w_ref, b_ref, tm, SSM_CONV)
        sg = _sigmoid(c)
        rows = _row_ids(i, tm)
        dc = jnp.where(rows >= cfg.PF, d_ref[...] * sg * (1.0 + c * (1.0 - sg)), 0.0)
        dc_ref[...] = dc

        @pl.when(i == 0)
        def _():
            dw_ref[...] = jnp.zeros_like(dw_ref)
            db_ref[...] = jnp.zeros_like(db_ref)

        db_ref[...] += jnp.sum(dc, axis=0, keepdims=True)
        for k in range(SSM_CONV):
            dw_ref[k:k + 1, :] += jnp.sum(dc * taps[k], axis=0, keepdims=True)
        ext_ref[0:HALO, :] = x_ref[tm - HALO:tm, :]

    return pl.pallas_call(
        body, name=name, grid=(cd // tc, lp // tm),
        in_specs=[pl.BlockSpec((tm, tc), lambda j, i: (i, j + off)), pl.BlockSpec((tm, tc), lambda j, i: (i, j)),
                  pl.BlockSpec((SSM_CONV, tc), lambda j, i: (0, j)), pl.BlockSpec((1, tc), lambda j, i: (0, j))],
        out_specs=[pl.BlockSpec((tm, tc), lambda j, i: (i, j)), pl.BlockSpec((SSM_CONV, tc), lambda j, i: (0, j)),
                   pl.BlockSpec((1, tc), lambda j, i: (0, j))],
        out_shape=[jax.ShapeDtypeStruct((lp, cd), F32), jax.ShapeDtypeStruct((SSM_CONV, cd), F32),
                   jax.ShapeDtypeStruct((1, cd), F32)],
        scratch_shapes=[pltpu.VMEM((tm + HALO, tc), F32)],
        compiler_params=_cparams(("arbitrary", "arbitrary")),
    )(zx, dxbc, cw, cb)


def _ffn_conv_fwd(up, cw, cb, *, cfg, name):
    lp, dff = cfg.LP, cfg.DFF
    tc = 2 * FFN_IL
    tm = _tile(lp, 768, 8)

    def body(x_ref, w_ref, b_ref, o_ref, ext_ref):
        hc, _ = _causal_conv(ext_ref, x_ref, w_ref, b_ref, tm, FFN_CONV)
        gc, vc = hc[:, :FFN_IL], hc[:, FFN_IL:]
        rows = _row_ids(pl.program_id(1), tm)
        o_ref[...] = jnp.where(rows >= cfg.PF, gc * _sigmoid(gc) * vc, 0.0).astype(BF16)
        ext_ref[0:HALO, :] = x_ref[tm - HALO:tm, :]

    return pl.pallas_call(
        body, name=name, grid=(dff // FFN_IL, lp // tm),
        in_specs=[pl.BlockSpec((tm, tc), lambda j, i: (i, j)), pl.BlockSpec((FFN_CONV, tc), lambda j, i: (0, j)),
                  pl.BlockSpec((1, tc), lambda j, i: (0, j))],
        out_specs=pl.BlockSpec((tm, FFN_IL), lambda j, i: (i, j)),
        out_shape=jax.ShapeDtypeStruct((lp, dff), BF16),
        scratch_shapes=[pltpu.VMEM((tm + HALO, tc), F32)],
        compiler_params=_cparams(("arbitrary", "arbitrary")),
    )(up, cw, cb)


def _ffn_conv_bwd_pre(up, dact, cw, cb, *, cfg, name):
    lp, dff = cfg.LP, cfg.DFF
    tc = 2 * FFN_IL
    tm = _tile(lp, 768, 8)

    def body(x_ref, d_ref, w_ref, b_ref, dc_ref, dw_ref, db_ref, ext_ref):
        i = pl.program_id(1)
        hc, taps = _causal_conv(ext_ref, x_ref, w_ref, b_ref, tm, FFN_CONV)
        gc, vc = hc[:, :FFN_IL], hc[:, FFN_IL:]
        sg = _sigmoid(gc)
        rows = _row_ids(i, tm)
        da = jnp.where(rows >= cfg.PF, d_ref[...].astype(F32), 0.0)
        dg = da * vc * sg * (1.0 + gc * (1.0 - sg))
        dv = da * gc * sg
        dc_ref[:, :FFN_IL] = dg
        dc_ref[:, FFN_IL:] = dv

        @pl.when(i == 0)
        def _():
            dw_ref[...] = jnp.zeros_like(dw_ref)
            db_ref[...] = jnp.zeros_like(db_ref)

        db_ref[:, :FFN_IL] += jnp.sum(dg, axis=0, keepdims=True)
        db_ref[:, FFN_IL:] += jnp.sum(dv, axis=0, keepdims=True)
        for k in range(FFN_CONV):
            dw_ref[k:k + 1, :FFN_IL] += jnp.sum(dg * taps[k][:, :FFN_IL], axis=0, keepdims=True)
            dw_ref[k:k + 1, FFN_IL:] += jnp.sum(dv * taps[k][:, FFN_IL:], axis=0, keepdims=True)
        ext_ref[0:HALO, :] = x_ref[tm - HALO:tm, :]

    return pl.pallas_call(
        body, name=name, grid=(dff // FFN_IL, lp // tm),
        in_specs=[pl.BlockSpec((tm, tc), lambda j, i: (i, j)), pl.BlockSpec((tm, FFN_IL), lambda j, i: (i, j)),
                  pl.BlockSpec((FFN_CONV, tc), lambda j, i: (0, j)), pl.BlockSpec((1, tc), lambda j, i: (0, j))],
        out_specs=[pl.BlockSpec((tm, tc), lambda j, i: (i, j)), pl.BlockSpec((FFN_CONV, tc), lambda j, i: (0, j)),
                   pl.BlockSpec((1, tc), lambda j, i: (0, j))],
        out_shape=[jax.ShapeDtypeStruct((lp, 2 * dff), F32), jax.ShapeDtypeStruct((FFN_CONV, 2 * dff), F32),
                   jax.ShapeDtypeStruct((1, 2 * dff), F32)],
        scratch_shapes=[pltpu.VMEM((tm + HALO, tc), F32)],
        compiler_params=_cparams(("arbitrary", "arbitrary")),
    )(up, dact, cw, cb)


def _head_select(lane, head):
    return ((lane >= head * SSM_HEAD_DIM) & (lane < (head + 1) * SSM_HEAD_DIM)).astype(BF16)


def _ssd_terms(x, bm, cm, dtr, bias, alog, dsk, valid, cfg):
    q, hg, gw, p = SSM_CHUNK, cfg.HG, cfg.GW, SSM_HEAD_DIM
    t = {}
    e_mat = _head_select(lax.broadcasted_iota(jnp.int32, (hg, gw), 1), lax.broadcasted_iota(jnp.int32, (hg, gw), 0))
    li = lax.broadcasted_iota(jnp.int32, (q, q), 0)
    si = lax.broadcasted_iota(jnp.int32, (q, q), 1)
    tri = li >= si
    tril = tri.astype(BF16)
    triu = (li <= si).astype(BF16)
    pre = dtr + bias
    dtv = jnp.where(valid, _softplus(pre), 0.0)
    a_head = -jnp.exp(alog)
    a = dtv * a_head
    cs = _dot3_left(tril, a)
    hi, mid, lo = _split3(a)
    cst = _dot_tn(hi, triu) + _dot_tn(mid, triu) + _dot_tn(lo, triu)
    cs_last = jnp.sum(a, axis=0, keepdims=True)
    dte = jnp.exp(jnp.minimum(cs_last - cs, 0.0))
    ecs = jnp.exp(cs)
    t.update(e_mat=e_mat, tri=tri, tril=tril, triu=triu, pre=pre, dtv=dtv, a_head=a_head, cs=cs, cst=cst,
             dec=jnp.exp(cs_last), dte=dte, ecs=ecs)
    t["dtv_x"] = _dot3(dtv, e_mat)
    t["ecs_x"] = _dot3(ecs, e_mat)
    t["dte_x"] = _dot3(dte, e_mat)
    t["dec_x"] = _dot3(t["dec"], e_mat)
    t["d_x"] = _dot3(dsk, e_mat)
    t["xdt"] = x * t["dtv_x"]
    t["gm"] = _dot_nt(cm.astype(BF16), bm.astype(BF16))
    return t


def _ssd_decay(t, e):
    diff = t["cs"][:, e:e + 1] - t["cst"][e:e + 1, :]
    return jnp.where(t["tri"], jnp.exp(jnp.minimum(diff, 0.0)), 0.0)


def _head_mask(b):
    lane = lax.broadcasted_iota(jnp.int32, (1, 2 * SSM_HEAD_DIM), 1)
    return (lane >= SSM_HEAD_DIM * b) & (lane < SSM_HEAD_DIM * (b + 1))


def _ssd_specs(cfg):
    q, g, gw, ns, hg, di = SSM_CHUNK, SSM_GROUPS, cfg.GW, SSM_STATE, cfg.HG, cfg.DI
    return dict(
        x=pl.BlockSpec((q, gw), lambda j, c: (c, j)),
        b=pl.BlockSpec((q, ns), lambda j, c: (c, di // ns + j)),
        c=pl.BlockSpec((q, ns), lambda j, c: (c, di // ns + g + j)),
        z=pl.BlockSpec((q, gw), lambda j, c: (c, j)),
        dtr=pl.BlockSpec((None, q, hg), lambda j, c: (j, c, 0)),
        prm=pl.BlockSpec((None, 8, hg), lambda j, c: (j, 0, 0)),
        gate=pl.BlockSpec((1, gw), lambda j, c: (0, j)),
        st=pl.BlockSpec((None, None, ns, gw), lambda j, c: (c, j, 0, 0)),
    )


def _ssd_fwd(xbc, zx, dtr, prm, gate_g, *, cfg, name):
    q, g, gw, ns, hg, p = SSM_CHUNK, SSM_GROUPS, cfg.GW, SSM_STATE, cfg.HG, SSM_HEAD_DIM
    nc = cfg.LP // q
    sp = _ssd_specs(cfg)

    def body(x_ref, b_ref, c_ref, z_ref, dtr_ref, prm_ref, gg_ref, y_ref, hn_ref, st_ref, s_ref):
        c = pl.program_id(1)

        @pl.when(c == 0)
        def _():
            s_ref[...] = jnp.zeros((ns, gw), F32)

        s_prev = s_ref[...]
        st_ref[...] = s_prev
        x, bm, cm = x_ref[...], b_ref[...], c_ref[...]
        valid = _row_ids(c, q) >= cfg.PF
        t = _ssd_terms(x, bm, cm, dtr_ref[...], prm_ref[0:1, :], prm_ref[1:2, :], prm_ref[2:3, :], valid, cfg)
        parts = []
        for pr in range(hg // 2):
            xp = t["xdt"][:, 2 * p * pr:2 * p * (pr + 1)]
            acc = None
            for b in range(2):
                m = (t["gm"] * _ssd_decay(t, 2 * pr + b)).astype(BF16)
                d = _dot(m, jnp.where(_head_mask(b), xp, 0.0).astype(BF16))
                acc = d if acc is None else acc + d
            parts.append(acc)
        y_diag = jnp.concatenate(parts, axis=1)
        y_off = _dot(cm.astype(BF16), s_prev.astype(BF16)) * t["ecs_x"]
        y = y_diag + y_off + x * t["d_x"]
        y_ref[...] = y
        s_ref[...] = s_prev * t["dec_x"] + _dot_tn(bm.astype(BF16), (t["xdt"] * t["dte_x"]).astype(BF16))
        z = z_ref[...]
        hgate = y * z * _sigmoid(z)
        r = lax.rsqrt(jnp.mean(hgate * hgate, axis=-1, keepdims=True) + NORM_EPS)
        hn_ref[...] = (hgate * r * gg_ref[...]).astype(BF16)

    return pl.pallas_call(
        body, name=name, grid=(g, nc),
        in_specs=[sp["x"], sp["b"], sp["c"], sp["z"], sp["dtr"], sp["prm"], sp["gate"]],
        out_specs=[sp["x"], sp["x"], sp["st"]],
        out_shape=[jax.ShapeDtypeStruct((cfg.LP, cfg.DI), F32), jax.ShapeDtypeStruct((cfg.LP, cfg.DI), BF16),
                   jax.ShapeDtypeStruct((nc, g, ns, gw), F32)],
        scratch_shapes=[pltpu.VMEM((ns, gw), F32)],
        compiler_params=_cparams(("arbitrary", "arbitrary")),
    )(xbc, xbc, xbc, zx, dtr, prm, gate_g)


def _ssd_bwd(xbc, zx, dtr, prm, gate_g, y, states, dhn, *, cfg, name):
    q, g, gw, ns, hg, p = SSM_CHUNK, SSM_GROUPS, cfg.GW, SSM_STATE, cfg.HG, SSM_HEAD_DIM
    nc = cfg.LP // q
    sp = _ssd_specs(cfg)
    rev = lambda spec: pl.BlockSpec(spec.block_shape, (lambda f: (lambda j, c: f(j, nc - 1 - c)))(spec.index_map))
    bc_spec = pl.BlockSpec((q, ns), lambda j, c: (nc - 1 - c, j))

    def body(x_ref, b_ref, c_ref, z_ref, dtr_ref, prm_ref, gg_ref, y_ref, st_ref, dhn_ref,
             dx_ref, db_ref, dc_ref, dz_ref, ddtr_ref, dprm_ref, dgg_ref, ds_ref):
        ci = pl.program_id(1)
        c = nc - 1 - ci

        @pl.when(ci == 0)
        def _():
            ds_ref[...] = jnp.zeros((ns, gw), F32)
            dprm_ref[...] = jnp.zeros((8, hg), F32)
            dgg_ref[...] = jnp.zeros((8, gw), F32)

        ds_next = ds_ref[...]
        s_prev = st_ref[...]
        x, bm, cm = x_ref[...], b_ref[...], c_ref[...]
        valid = _row_ids(c, q) >= cfg.PF
        t = _ssd_terms(x, bm, cm, dtr_ref[...], prm_ref[0:1, :], prm_ref[1:2, :], prm_ref[2:3, :], valid, cfg)
        et_mat = _head_select(lax.broadcasted_iota(jnp.int32, (gw, hg), 0), lax.broadcasted_iota(jnp.int32, (gw, hg), 1))
        heads = lambda v: _dot3(v, et_mat)
        yv, z = y_ref[...], z_ref[...]
        sz = _sigmoid(z)
        silu = z * sz
        hgate = yv * silu
        r = lax.rsqrt(jnp.mean(hgate * hgate, axis=-1, keepdims=True) + NORM_EPS)
        hhat = hgate * r
        dhn = dhn_ref[...]
        dgg = jnp.sum(dhn * hhat, axis=0, keepdims=True)
        dhh = dhn * gg_ref[...]
        dhgate = r * (dhh - hhat * jnp.mean(dhh * hhat, axis=-1, keepdims=True))
        dy = dhgate * silu
        dz_ref[...] = dhgate * yv * sz * (1.0 + z * (1.0 - sz))
        bmb, cmb = bm.astype(BF16), cm.astype(BF16)
        xdt = t["xdt"]
        dye = dy * t["ecs_x"]
        dxdt_state = _dot(bmb, ds_next.astype(BF16)) * t["dte_x"]
        dg_acc = None
        dparts = []
        li = lax.broadcasted_iota(jnp.int32, (q, q), 0)
        si = lax.broadcasted_iota(jnp.int32, (q, q), 1)
        head_id = lax.broadcasted_iota(jnp.int32, (1, hg), 1)
        da = None
        for pr in range(hg // 2):
            sl = slice(2 * p * pr, 2 * p * (pr + 1))
            xp, dyp = xdt[:, sl], dy[:, sl]
            acc = None
            for b in range(2):
                lm = _ssd_decay(t, 2 * pr + b)
                m = (t["gm"] * lm).astype(BF16)
                dym = jnp.where(_head_mask(b), dyp, 0.0).astype(BF16)
                d = _dot_tn(m, dym)
                acc = d if acc is None else acc + d
                dm = _dot_nt(dym, xp.astype(BF16)) * lm
                dg_acc = dm if dg_acc is None else dg_acc + dm
                corner = _dot(t["triu"], (dm * t["gm"]).astype(BF16))
                da_e = jnp.sum(jnp.where(si < li, corner, 0.0), axis=1, keepdims=True)
                da_e = da_e * (head_id == 2 * pr + b).astype(F32)
                da = da_e if da is None else da + da_e
            dparts.append(acc)
        dxdt = jnp.concatenate(dparts, axis=1) + dxdt_state
        dgb = dg_acc.astype(BF16)
        dc_ref[...] = _dot(dgb, bmb) + _dot_nt(dye.astype(BF16), s_prev.astype(BF16))
        xde = (xdt * t["dte_x"]).astype(BF16)
        db_ref[...] = _dot_tn(dgb, cmb) + _dot_nt(xde, ds_next.astype(BF16))
        ds_ref[...] = ds_next * t["dec_x"] + _dot_tn(cmb, dye.astype(BF16))
        y_off = _dot(cmb, s_prev.astype(BF16)) * t["ecs_x"]
        tril_strict = (li > si).astype(BF16)
        da = da + _dot3_left(t["triu"], heads(dy * y_off)) + _dot3_left(tril_strict, heads(xdt * dxdt_state)) \
            + t["dec"] * heads(jnp.sum(ds_next * s_prev, axis=0, keepdims=True))
        ddtv = da * t["a_head"] + heads(dxdt * x)
        ddtr = jnp.where(valid, ddtv * _sigmoid(t["pre"]), 0.0)
        ddtr_ref[...] = ddtr
        dx_ref[...] = dxdt * t["dtv_x"] + dy * t["d_x"]
        dalog = jnp.sum(da * t["dtv"], axis=0, keepdims=True) * t["a_head"]
        dprm_ref[0:1, :] += jnp.sum(ddtr, axis=0, keepdims=True)
        dprm_ref[1:2, :] += dalog
        dprm_ref[2:3, :] += heads(jnp.sum(dy * x, axis=0, keepdims=True))
        dgg_ref[0:1, :] += dgg

    return pl.pallas_call(
        body, name=name, grid=(g, nc),
        in_specs=[rev(sp["x"]), rev(sp["b"]), rev(sp["c"]), rev(sp["z"]), rev(sp["dtr"]), sp["prm"], sp["gate"],
                  rev(sp["x"]), rev(sp["st"]), rev(sp["x"])],
        out_specs=[rev(sp["x"]), bc_spec, bc_spec, rev(sp["x"]), rev(sp["dtr"]),
                   pl.BlockSpec((None, 8, hg), lambda j, c: (j, 0, 0)), pl.BlockSpec((None, 8, gw), lambda j, c: (j, 0, 0))],
        out_shape=[jax.ShapeDtypeStruct((cfg.LP, cfg.DI), F32), jax.ShapeDtypeStruct((cfg.LP, g * ns), F32),
                   jax.ShapeDtypeStruct((cfg.LP, g * ns), F32), jax.ShapeDtypeStruct((cfg.LP, cfg.DI), F32),
                   jax.ShapeDtypeStruct((g, cfg.LP, hg), F32), jax.ShapeDtypeStruct((g, 8, hg), F32),
                   jax.ShapeDtypeStruct((g, 8, gw), F32)],
        scratch_shapes=[pltpu.VMEM((ns, gw), F32)],
        compiler_params=_cparams(("arbitrary", "arbitrary")),
    )(xbc, xbc, xbc, zx, dtr, prm, gate_g, y, states, dhn)


def _att_consts():
    t = ATT_T
    rows = lax.broadcasted_iota(jnp.int32, (t, t), 0)
    cols = lax.broadcasted_iota(jnp.int32, (t, t), 1)
    lane = lax.broadcasted_iota(jnp.int32, (1, 2 * SB_HEAD_DIM), 1)
    return cols - rows, cols, (rows > cols).astype(BF16), lane < SB_HEAD_DIM


def _suffix_sum(v, ustrict):
    hi, lo = _split2(v)
    return _dot(hi, ustrict) + _dot(lo, ustrict)


def _att_probs(qm, k, vis, ustrict, carry):
    s = _dot_nt(qm, k)
    l1p = jnp.log(1.0 + jnp.exp(-jnp.abs(s)))
    ls = jnp.minimum(s, 0.0) - l1p
    lk = jnp.where(vis, ls - s, 0.0)
    w = jnp.where(vis, jnp.exp(ls + _suffix_sum(lk, ustrict) + carry), 0.0)
    return ls, lk, w


def _attn_fwd(q, kv, *, cfg, name):
    t, lp, d = ATT_T, cfg.LP, cfg.D
    hp, nq = d // (2 * SB_HEAD_DIM), lp // ATT_T
    scale = SB_HEAD_DIM ** -0.5

    def body(q_ref, kv_ref, o_ref):
        qb = pl.program_id(1)
        dmat, cols, ustrict, lo_half = _att_consts()
        q2 = q_ref[...] * scale
        qms = [jnp.where(lo_half, q2, 0).astype(BF16), jnp.where(lo_half, 0, q2).astype(BF16)]

        def step(j, st):
            kb = qb - j
            off = pl.multiple_of(kb * t, t)
            k = kv_ref[pl.ds(off, t), 0:2 * SB_HEAD_DIM]
            v = kv_ref[pl.ds(off, t), 2 * SB_HEAD_DIM:]
            vis = (dmat < j * t) & (cols >= cfg.PF - kb * t)
            new = []
            for b in range(2):
                acc, carry = st[2 * b], st[2 * b + 1]
                _, lk, w = _att_probs(qms[b], k, vis, ustrict, carry)
                w_hi, w_lo = _split2(w)
                new += [acc + _dot(w_hi, v) + _dot(w_lo, v), carry + jnp.sum(lk, axis=1, keepdims=True)]
            return tuple(new)

        z_acc, z_c = jnp.zeros((t, 2 * SB_HEAD_DIM), F32), jnp.zeros((t, 1), F32)
        st = lax.fori_loop(0, qb + 1, step, (z_acc, z_c, z_acc, z_c))
        o_ref[...] = jnp.where(lo_half, st[0], st[2])

    return pl.pallas_call(
        body, name=name, grid=(hp, nq),
        in_specs=[pl.BlockSpec((t, 128), lambda h, i: (i, h)), pl.BlockSpec((lp, 256), lambda h, i: (0, h))],
        out_specs=pl.BlockSpec((t, 128), lambda h, i: (i, h)),
        out_shape=jax.ShapeDtypeStruct((lp, d), F32),
        compiler_params=_cparams(("arbitrary", "arbitrary")),
    )(q, kv)


def _attn_bwd(q, kv, o, do, *, cfg, name):
    t, lp, d = ATT_T, cfg.LP, cfg.D
    hp, nq = d // (2 * SB_HEAD_DIM), lp // ATT_T
    scale = SB_HEAD_DIM ** -0.5

    def body(q_ref, kv_ref, o_ref, do_ref, dq_ref, dkv_ref):
        qb = pl.program_id(1)

        @pl.when(qb == 0)
        def _():
            dkv_ref[...] = jnp.zeros_like(dkv_ref)

        dmat, cols, ustrict, lo_half = _att_consts()
        q2 = q_ref[...] * scale
        do2 = do_ref[...]
        qms = [jnp.where(lo_half, q2, 0).astype(BF16), jnp.where(lo_half, 0, q2).astype(BF16)]
        doms = [jnp.where(lo_half, do2, 0).astype(BF16), jnp.where(lo_half, 0, do2).astype(BF16)]
        prod = do2.astype(F32) * o_ref[...]
        etots = [jnp.sum(jnp.where(lo_half, prod, 0.0), axis=1, keepdims=True),
                 jnp.sum(jnp.where(lo_half, 0.0, prod), axis=1, keepdims=True)]

        def step(j, st):
            kb = qb - j
            off = pl.multiple_of(kb * t, t)
            k = kv_ref[pl.ds(off, t), 0:2 * SB_HEAD_DIM]
            v = kv_ref[pl.ds(off, t), 2 * SB_HEAD_DIM:]
            vis = (dmat < j * t) & (cols >= cfg.PF - kb * t)
            new = []
            dk = dv = None
            for b in range(2):
                dq, carry, ecarry = st[3 * b], st[3 * b + 1], st[3 * b + 2]
                ls, lk, w = _att_probs(qms[b], k, vis, ustrict, carry)
                e = w * _dot_nt(doms[b], v)
                e_before = etots[b] - (e + _suffix_sum(e, ustrict) + ecarry)
                sig = jnp.exp(ls)
                dz = jnp.where(vis, e * (1.0 - sig) - sig * e_before, 0.0).astype(BF16)
                dkb = _dot_tn(dz, qms[b])
                dvb = _dot_tn(w.astype(BF16), doms[b])
                dk = dkb if dk is None else dk + dkb
                dv = dvb if dv is None else dv + dvb
                new += [dq + _dot(dz, k), carry + jnp.sum(lk, axis=1, keepdims=True),
                        ecarry + jnp.sum(e, axis=1, keepdims=True)]
            dkv_ref[pl.ds(off, t), 0:2 * SB_HEAD_DIM] += dk
            dkv_ref[pl.ds(off, t), 2 * SB_HEAD_DIM:] += dv
            return tuple(new)

        z_acc, z_c = jnp.zeros((t, 2 * SB_HEAD_DIM), F32), jnp.zeros((t, 1), F32)
        st = lax.fori_loop(0, qb + 1, step, (z_acc, z_c, z_c, z_acc, z_c, z_c))
        dq_ref[...] = jnp.where(lo_half, st[0], st[3]) * scale

    return pl.pallas_call(
        body, name=name, grid=(hp, nq),
        in_specs=[pl.BlockSpec((t, 128), lambda h, i: (i, h)), pl.BlockSpec((lp, 256), lambda h, i: (0, h)),
                  pl.BlockSpec((t, 128), lambda h, i: (i, h)), pl.BlockSpec((t, 128), lambda h, i: (i, h))],
        out_specs=[pl.BlockSpec((t, 128), lambda h, i: (i, h)), pl.BlockSpec((lp, 256), lambda h, i: (0, h))],
        out_shape=[jax.ShapeDtypeStruct((lp, d), F32), jax.ShapeDtypeStruct((lp, 2 * d), F32)],
        compiler_params=_cparams(("arbitrary", "arbitrary")),
    )(q, kv, o, do)


def _loss_head(h, g, target, *, cfg, name):
    lp, d = h.shape
    tm = SSM_CHUNK
    first = (cfg.PF + N_META) // tm
    assert (cfg.PF + N_META) % tm == 0

    def body(h_ref, g_ref, t_ref, loss_ref, dh_ref, dg_ref):
        i = pl.program_id(0)

        @pl.when(i == 0)
        def _():
            loss_ref[...] = jnp.zeros_like(loss_ref)
            dg_ref[...] = jnp.zeros_like(dg_ref)

        x = h_ref[...]
        r = lax.rsqrt(jnp.mean(x * x, axis=-1, keepdims=True) + NORM_EPS)
        xhat = x * r
        live = i >= first
        diff = jnp.where(live, xhat * g_ref[...] - t_ref[...], 0.0)
        loss_ref[...] += 0.5 * jnp.sum(jnp.mean(diff * diff, axis=-1, keepdims=True))
        dy = diff * (1.0 / d)
        dg_ref[...] += jnp.sum(dy * xhat, axis=0, keepdims=True)
        dxh = dy * g_ref[...]
        dh_ref[...] = r * (dxh - xhat * jnp.mean(dxh * xhat, axis=-1, keepdims=True))

    return pl.pallas_call(
        body, name=name, grid=(lp // tm,),
        in_specs=[pl.BlockSpec((tm, d), lambda i: (i, 0)), pl.BlockSpec((1, d), lambda i: (0, 0)),
                  pl.BlockSpec((tm, d), lambda i: (jnp.maximum(i - first, 0), 0))],
        out_specs=[pl.BlockSpec((8, 128), lambda i: (0, 0)), pl.BlockSpec((tm, d), lambda i: (i, 0)),
                   pl.BlockSpec((1, d), lambda i: (0, 0))],
        out_shape=[jax.ShapeDtypeStruct((8, 128), F32), jax.ShapeDtypeStruct((lp, d), F32),
                   jax.ShapeDtypeStruct((1, d), F32)],
        compiler_params=_cparams(("arbitrary",)),
    )(h, g, target)


def _conv_bwd_input(dc, cw, *, width, name):
    lp, cd = dc.shape
    tc = _tile(cd, 512, 128)
    tm = _tile(lp, 768, 8)
    ni = lp // tm

    def body(d_ref, w_ref, o_ref, ext_ref):
        @pl.when(pl.program_id(1) == 0)
        def _():
            ext_ref[tm:, :] = jnp.zeros((HALO, tc), F32)

        ext_ref[0:tm, :] = d_ref[...]
        acc = ext_ref[pl.ds(width - 1, tm), :] * w_ref[0:1, :]
        for k in range(1, width):
            acc = acc + ext_ref[pl.ds(width - 1 - k, tm), :] * w_ref[k:k + 1, :]
        o_ref[...] = acc.astype(BF16)
        ext_ref[tm:, :] = d_ref[0:HALO, :]

    return pl.pallas_call(
        body, name=name, grid=(cd // tc, ni),
        in_specs=[pl.BlockSpec((tm, tc), lambda j, i: (ni - 1 - i, j)), pl.BlockSpec((width, tc), lambda j, i: (0, j))],
        out_specs=pl.BlockSpec((tm, tc), lambda j, i: (ni - 1 - i, j)),
        out_shape=jax.ShapeDtypeStruct((lp, cd), BF16),
        scratch_shapes=[pltpu.VMEM((tm + HALO, tc), F32)],
        compiler_params=_cparams(("arbitrary", "arbitrary")),
    )(dc, cw)


MATMUL_WEIGHTS = ("ssd_w_in", "ssd_w_out", "w_kv", "sb_w_q", "sb_w_o", "ffn_w_up", "ffn_w_down")


def _il(a, cfg):
    nb = cfg.DFF // FFN_IL
    lead = a.shape[:-1]
    a = a.reshape(lead + (2, nb, FFN_IL))
    return jnp.swapaxes(a, -3, -2).reshape(lead + (2 * cfg.DFF,))


def _unil(a, cfg):
    nb = cfg.DFF // FFN_IL
    lead = a.shape[:-1]
    a = a.reshape(lead + (nb, 2, FFN_IL))
    return jnp.swapaxes(a, -3, -2).reshape(lead + (2 * cfg.DFF,))


def _pair(a, cfg):
    hp = cfg.D // 128
    lead = a.shape[:-1]
    return jnp.swapaxes(a.reshape(lead + (2, hp, 128)), -3, -2).reshape(lead + (2 * cfg.D,))


def _unpair(a, cfg):
    hp = cfg.D // 128
    lead = a.shape[:-1]
    return jnp.swapaxes(a.reshape(lead + (hp, 2, 128)), -3, -2).reshape(lead + (2 * cfg.D,))


def _prepare(full, cfg):
    d, di, cd, h = cfg.D, cfg.DI, cfg.CONV_DIM, cfg.H
    w_in = full["ssd_w_in"][0].astype(BF16)
    prm = jnp.stack([full["ssd_dt_bias"][0], full["ssd_a_log"][0], full["ssd_d_skip"][0]]).astype(F32)
    prm = jnp.pad(prm.reshape(3, SSM_GROUPS, cfg.HG).transpose(1, 0, 2), ((0, 0), (0, 5), (0, 0)))
    p = dict(
        meta=full["meta_tokens"].astype(F32),
        ssd_norm=full["ssd_norm"].astype(F32).reshape(1, d),
        w_in_main=w_in[:, :di + cd],
        w_in_dt=jnp.pad(w_in[:, di + cd:], ((0, 0), (0, 128 - h))),
        conv_w=full["ssd_conv_w"][0].astype(F32), conv_b=full["ssd_conv_b"].astype(F32).reshape(1, cd),
        prm=prm, gate=full["ssd_gate_norm"].astype(F32).reshape(1, di),
        w_out=full["ssd_w_out"][0].astype(BF16),
        kv_norm=full["kv_norm"].astype(F32).reshape(1, d), w_kv=_pair(full["w_kv"].astype(BF16), cfg),
        sb_norm=full["sb_norm"].astype(F32).reshape(1, d),
        w_q=full["sb_w_q"][0].astype(BF16), w_o=full["sb_w_o"][0].astype(BF16),
        final_norm=full["final_norm"].astype(F32).reshape(1, d),
    )
    for i in range(2):
        p[f"ffn_norm{i}"] = full["ffn_norm"][i].astype(F32).reshape(1, d)
        p[f"w_up{i}"] = _il(full["ffn_w_up"][i].astype(BF16), cfg)
        p[f"fcw{i}"] = _il(full["ffn_conv_w"][i].astype(F32), cfg)
        p[f"fcb{i}"] = _il(full["ffn_conv_b"][i].astype(F32), cfg).reshape(1, 2 * cfg.DFF)
        p[f"w_down{i}"] = full["ffn_w_down"][i].astype(BF16)
    return p


def _ffn_fwd(h, p, i, cfg):
    up, u = _norm_mm(h, p[f"ffn_norm{i}"], p[f"w_up{i}"], out_dtype=F32, name=f"ffn{i}_up")
    act = _ffn_conv_fwd(up, p[f"fcw{i}"], p[f"fcb{i}"], cfg=cfg, name=f"ffn{i}_conv")
    return _mm_res(act, p[f"w_down{i}"], h, pf=cfg.PF, name=f"ffn{i}_down"), (h, u, up, act)


def _ffn_bwd(dh, saved, p, i, cfg, grads):
    h, u, up, act = saved
    dact = _mm_nt(dh, p[f"w_down{i}"], out_dtype=F32, pf=cfg.PF, name=f"ffn{i}_dact")
    grads[f"w_down{i}"] = _mm_tn(act, dh, name=f"ffn{i}_dwdown")
    dhc, grads[f"fcw{i}"], grads[f"fcb{i}"] = _ffn_conv_bwd_pre(up, dact, p[f"fcw{i}"], p[f"fcb{i}"], cfg=cfg,
                                                                 name=f"ffn{i}_dconv")
    dup = _conv_bwd_input(dhc, p[f"fcw{i}"], width=FFN_CONV, name=f"ffn{i}_dup")
    grads[f"w_up{i}"] = _mm_tn(u, dup, name=f"ffn{i}_dwup")
    dh, grads[f"ffn_norm{i}"] = _mm_nt_normbwd(dup, p[f"w_up{i}"], h, p[f"ffn_norm{i}"], dh, pf=cfg.PF,
                                               name=f"ffn{i}_dh")
    return dh


def _local_step(x, target, p, cfg):
    d, di, cd, h_, lp, pf = cfg.D, cfg.DI, cfg.CONV_DIM, cfg.H, cfg.LP, cfg.PF
    g = {}
    h0 = jnp.concatenate([jnp.zeros((pf, d), F32), p["meta"], x], axis=0)
    zx, u1 = _norm_mm(h0, p["ssd_norm"], p["w_in_main"], out_dtype=F32, name="ssd_in")
    dt_raw, _ = _norm_mm(h0, p["ssd_norm"], p["w_in_dt"], out_dtype=F32, name="ssd_in_dt")
    dtr = dt_raw[:, :h_].reshape(lp, SSM_GROUPS, cfg.HG).transpose(1, 0, 2)
    xbc = _ssd_conv_fwd(zx, p["conv_w"], p["conv_b"], cfg=cfg, name="ssd_conv")
    y, hn, states = _ssd_fwd(xbc, zx, dtr, p["prm"], p["gate"], cfg=cfg, name="ssd_scan")
    h1 = _mm_res(hn, p["w_out"], h0, pf=pf, name="ssd_out")
    h2, ffn0 = _ffn_fwd(h1, p, 0, cfg)
    kv, ukv = _norm_mm(h2, p["kv_norm"], p["w_kv"], out_dtype=BF16, name="kv_proj")
    q, uq = _norm_mm(h2, p["sb_norm"], p["w_q"], out_dtype=BF16, name="q_proj")
    o = _attn_fwd(q, kv, cfg=cfg, name="attn_fwd")
    h3 = _mm_res(o, p["w_o"], h2, pf=pf, name="attn_out")
    h4, ffn1 = _ffn_fwd(h3, p, 1, cfg)
    loss8, dh, g["final_norm"] = _loss_head(h4, p["final_norm"], target, cfg=cfg, name="loss_head")
    dh = _ffn_bwd(dh, ffn1, p, 1, cfg, g)
    do = _mm_nt(dh, p["w_o"], out_dtype=BF16, pf=pf, name="attn_do")
    g["w_o"] = _mm_tn(o, dh, name="attn_dwo")
    dq, dkv = _attn_bwd(q, kv, o, do, cfg=cfg, name="attn_bwd")
    g["w_q"] = _mm_tn(uq, dq, name="attn_dwq")
    g["w_kv"] = _mm_tn(ukv, dkv, name="attn_dwkv")
    dh, g["sb_norm"] = _mm_nt_normbwd(dq, p["w_q"], h2, p["sb_norm"], dh, pf=pf, name="attn_dhq")
    dh, g["kv_norm"] = _mm_nt_normbwd(dkv, p["w_kv"], h2, p["kv_norm"], dh, pf=pf, name="attn_dhkv")
    dh = _ffn_bwd(dh, ffn0, p, 0, cfg, g)
    dhn = _mm_nt(dh, p["w_out"], out_dtype=F32, pf=pf, name="ssd_dhn")
    g["w_out"] = _mm_tn(hn, dh, name="ssd_dwout")
    dx, db, dc, dz, ddtr, dprm, dgate = _ssd_bwd(xbc, zx, dtr, p["prm"], p["gate"], y, states, dhn, cfg=cfg,
                                                 name="ssd_scan_bwd")
    dconv, g["conv_w"], g["conv_b"] = _ssd_conv_bwd_pre(zx, jnp.concatenate([dx, db, dc], axis=1), p["conv_w"],
                                                        p["conv_b"], cfg=cfg, name="ssd_dconv")
    dxpre = _conv_bwd_input(dconv, p["conv_w"], width=SSM_CONV, name="ssd_dxpre")
    dzx = jnp.concatenate([dz.astype(BF16), dxpre], axis=1)
    ddt = jnp.pad(ddtr.transpose(1, 0, 2).reshape(lp, h_), ((0, 0), (0, 128 - h_)))
    dw_main = _mm_tn(u1, dzx, name="ssd_dwin")
    dw_dt = _mm_tn(u1, ddt, name="ssd_dwin_dt")
    dh, gn1 = _mm_nt_normbwd(dzx, p["w_in_main"], h0, p["ssd_norm"], dh, pf=pf, name="ssd_dh")
    dh, gn2 = _mm_nt_normbwd(ddt, p["w_in_dt"], h0, p["ssd_norm"], dh, pf=pf, name="ssd_dh_dt")
    heads = lambda r: dprm[:, r, :].reshape(1, h_)
    out = dict(
        meta_tokens=dh[pf:pf + N_META], ssd_norm=gn1 + gn2,
        ssd_w_in=jnp.concatenate([dw_main, dw_dt[:, :h_]], axis=1)[None],
        ssd_conv_w=g["conv_w"][None], ssd_conv_b=g["conv_b"],
        ssd_dt_bias=heads(0), ssd_a_log=heads(1), ssd_d_skip=heads(2),
        ssd_gate_norm=dgate[:, 0, :].reshape(1, di), ssd_w_out=g["w_out"][None],
        kv_norm=g["kv_norm"].reshape(d), w_kv=_unpair(g["w_kv"], cfg),
        sb_norm=g["sb_norm"], sb_w_q=g["w_q"][None], sb_w_o=g["w_o"][None],
        ffn_norm=jnp.concatenate([g["ffn_norm0"], g["ffn_norm1"]], axis=0),
        ffn_w_up=jnp.stack([_unil(g["w_up0"], cfg), _unil(g["w_up1"], cfg)]),
        ffn_conv_w=jnp.stack([_unil(g["fcw0"], cfg), _unil(g["fcw1"], cfg)]),
        ffn_conv_b=jnp.concatenate([_unil(g["fcb0"], cfg), _unil(g["fcb1"], cfg)], axis=0),
        ffn_w_down=jnp.stack([g["w_down0"], g["w_down1"]]),
        final_norm=g["final_norm"].reshape(d),
    )
    return loss8[0, 0], dh[pf + N_META:], out


MESH_AXES = ("x", "y", "c")
HBM_SPEC = pl.BlockSpec(memory_space=pltpu.HBM)


def _all_gather(blk, *, name):
    r, w = blk.shape

    def body(x_ref, out_ref, send_sems, recv_sems, local_sem):
        x, y, c = lax.axis_index("x"), lax.axis_index("y"), lax.axis_index("c")
        me, sibling = (x, y, c), (x, y, 1 - c)
        chips = [(1 - x, y), (x, 1 - y), (1 - x, 1 - y)]

        def slot(px, py, pc):
            return out_ref.at[4 * px + 2 * py + pc]

        def copy(k, block, to, src=None):
            return pltpu.make_async_remote_copy(
                src_ref=slot(*block) if src is None else src, dst_ref=slot(*block),
                send_sem=send_sems.at[k], recv_sem=recv_sems.at[k], device_id=to, device_id_type=pl.DeviceIdType.MESH)

        mine = pltpu.make_async_copy(x_ref, slot(*me), local_sem)
        mine.start()
        first = [copy(0, me, sibling, src=x_ref)]
        first += [copy(1 + j, me, (*chip, c), src=x_ref) for j, chip in enumerate(chips)]
        for cp in first:
            cp.start()
        passed = [copy(4 + j, (*chip, c), sibling) for j, chip in enumerate(chips)]
        for j, chip in enumerate(chips):
            copy(1 + j, (*chip, c), me).wait_recv()
            passed[j].start()
        copy(0, sibling, me).wait_recv()
        for j, chip in enumerate(chips):
            copy(4 + j, (*chip, 1 - c), me).wait_recv()
        for cp in first + passed:
            cp.wait_send()
        mine.wait()

    return pl.pallas_call(
        body, name=name, out_shape=jax.ShapeDtypeStruct((N_DEV, r, w), blk.dtype),
        in_specs=[HBM_SPEC], out_specs=HBM_SPEC,
        scratch_shapes=[pltpu.SemaphoreType.DMA((7,)), pltpu.SemaphoreType.DMA((7,)), pltpu.SemaphoreType.DMA],
    )(blk)


def _all_to_all(buf, *, name):
    n, r, w = buf.shape

    def body(x_ref, out_ref, send_sems, recv_sems, local_sem):
        x, y, c = lax.axis_index("x"), lax.axis_index("y"), lax.axis_index("c")
        me_id = 4 * x + 2 * y + c
        mine = pltpu.make_async_copy(x_ref.at[me_id], out_ref.at[me_id], local_sem)
        mine.start()
        copies = []
        for k in range(1, N_DEV):
            fx, fy, fc = (k >> 2) & 1, (k >> 1) & 1, k & 1
            px = 1 - x if fx else x
            py = 1 - y if fy else y
            pc = 1 - c if fc else c
            copies.append(pltpu.make_async_remote_copy(
                src_ref=x_ref.at[4 * px + 2 * py + pc], dst_ref=out_ref.at[me_id],
                send_sem=send_sems.at[k - 1], recv_sem=recv_sems.at[k - 1],
                device_id=(px, py, pc), device_id_type=pl.DeviceIdType.MESH))
        for cp in copies:
            cp.start()
        for cp in copies:
            cp.wait()
        mine.wait()

    return pl.pallas_call(
        body, name=name, out_shape=jax.ShapeDtypeStruct((n, r, w), buf.dtype),
        in_specs=[HBM_SPEC], out_specs=HBM_SPEC,
        scratch_shapes=[pltpu.SemaphoreType.DMA((7,)), pltpu.SemaphoreType.DMA((7,)), pltpu.SemaphoreType.DMA],
    )(buf)


def _sum_slots(buf, *, name):
    n, r, w = buf.shape
    tr = _tile(r, 1024, 8)

    def body(x_ref, o_ref):
        acc = x_ref[0]
        for s in range(1, n):
            acc = acc + x_ref[s]
        o_ref[...] = acc

    return pl.pallas_call(
        body, name=name, grid=(r // tr,),
        in_specs=[pl.BlockSpec((n, tr, w), lambda i: (0, i, 0))], out_specs=pl.BlockSpec((tr, w), lambda i: (i, 0)),
        out_shape=jax.ShapeDtypeStruct((r, w), F32), compiler_params=_cparams(("arbitrary",)),
    )(buf)


def _adamw(w, g, m, v, *, name):
    r, lanes = w.shape
    tr = _tile(r, 1024, 8)

    def body(w_ref, g_ref, m_ref, v_ref, d_ref, nm_ref, nv_ref):
        gg = g_ref[...]
        nm = ADAM_B1 * m_ref[...] + (1.0 - ADAM_B1) * gg
        nv = ADAM_B2 * v_ref[...] + (1.0 - ADAM_B2) * (gg * gg)
        m_hat = nm / (1.0 - ADAM_B1 ** ADAM_STEP)
        v_hat = nv / (1.0 - ADAM_B2 ** ADAM_STEP)
        d_ref[...] = -ADAM_LR * (m_hat / (jnp.sqrt(v_hat) + ADAM_EPS) + ADAM_WD * w_ref[...])
        nm_ref[...] = nm
        nv_ref[...] = nv

    spec = pl.BlockSpec((tr, lanes), lambda i: (i, 0))
    shp = jax.ShapeDtypeStruct((r, lanes), F32)
    return pl.pallas_call(body, name=name, grid=(r // tr,), in_specs=[spec] * 4, out_specs=[spec] * 3,
                          out_shape=[shp] * 3, compiler_params=_cparams(("arbitrary",)))(w, g, m, v)


PACK_QUANTUM = 16 * 128
INPUT_NAMES = ("x", "meta_tokens", "ssd_norm", "ssd_w_in", "ssd_conv_w", "ssd_conv_b", "ssd_dt_bias", "ssd_a_log",
               "ssd_d_skip", "ssd_gate_norm", "ssd_w_out", "kv_norm", "w_kv", "sb_norm", "sb_w_q", "sb_w_o", "ffn_norm",
               "ffn_w_up", "ffn_conv_w", "ffn_conv_b", "ffn_w_down", "final_norm")
WEIGHT_NAMES = INPUT_NAMES[1:]
SHARD_AXIS = dict(meta_tokens=1, ssd_norm=1, ssd_w_in=2, ssd_conv_w=2, ssd_conv_b=1, ssd_gate_norm=1, ssd_w_out=1, w_kv=1,
                  sb_w_q=1, sb_w_o=1, ffn_w_up=2, ffn_conv_w=2, ffn_w_down=1)
SMALL_SHARDED = ("meta_tokens", "ssd_norm", "ssd_conv_w", "ssd_conv_b", "ssd_gate_norm", "ffn_conv_w")
SHARDED = MATMUL_WEIGHTS + SMALL_SHARDED
REPLICATED = tuple(n for n in WEIGHT_NAMES if n not in SHARD_AXIS)


def _rows(shape):
    n = math.prod(shape)
    return -(-n // PACK_QUANTUM) * (PACK_QUANTUM // 128)


def _pack(arrs, dtype, lead=0):
    parts = []
    for a in arrs:
        ld = a.shape[:lead]
        n = math.prod(a.shape[lead:])
        f = a.reshape(ld + (n,)).astype(dtype)
        pad = _rows(a.shape[lead:]) * 128 - n
        if pad:
            f = jnp.pad(f, [(0, 0)] * lead + [(0, pad)])
        parts.append(f.reshape(ld + (-1, 128)))
    return jnp.concatenate(parts, axis=lead)


def _unpack(flat, shapes, lead=0):
    out, r0 = [], 0
    ld = flat.shape[:lead]
    for shp in shapes:
        rows, n = _rows(shp), math.prod(shp)
        piece = lax.slice_in_dim(flat, r0, r0 + rows, axis=lead).reshape(ld + (rows * 128,))
        out.append(lax.slice_in_dim(piece, 0, n, axis=lead).reshape(ld + tuple(shp)))
        r0 += rows
    return out


def _unshard(stacked, axis):
    a = jnp.moveaxis(stacked, 0, axis)
    shp = a.shape
    return a.reshape(shp[:axis] + (shp[axis] * shp[axis + 1],) + shp[axis + 2:])


def _to_shards(full, axis):
    shp = full.shape
    a = full.reshape(shp[:axis] + (N_DEV, shp[axis] // N_DEV) + shp[axis + 1:])
    return jnp.moveaxis(a, axis, 0)


def kernel(x, meta_tokens, ssd_norm, ssd_w_in, ssd_conv_w, ssd_conv_b, ssd_dt_bias, ssd_a_log, ssd_d_skip, ssd_gate_norm, ssd_w_out, kv_norm, w_kv, sb_norm, sb_w_q, sb_w_o, ffn_norm, ffn_w_up, ffn_conv_w, ffn_conv_b, ffn_w_down, final_norm, loss_target, m_meta_tokens, m_ssd_norm, m_ssd_w_in, m_ssd_conv_w, m_ssd_conv_b, m_ssd_dt_bias, m_ssd_a_log, m_ssd_d_skip, m_ssd_gate_norm, m_ssd_w_out, m_kv_norm, m_w_kv, m_sb_norm, m_sb_w_q, m_sb_w_o, m_ffn_norm, m_ffn_w_up, m_ffn_conv_w, m_ffn_conv_b, m_ffn_w_down, m_final_norm, v_meta_tokens, v_ssd_norm, v_ssd_w_in, v_ssd_conv_w, v_ssd_conv_b, v_ssd_dt_bias, v_ssd_a_log, v_ssd_d_skip, v_ssd_gate_norm, v_ssd_w_out, v_kv_norm, v_w_kv, v_sb_norm, v_sb_w_q, v_sb_w_o, v_ffn_norm, v_ffn_w_up, v_ffn_conv_w, v_ffn_conv_b, v_ffn_w_down, v_final_norm):
    local = dict(zip(WEIGHT_NAMES, (meta_tokens, ssd_norm, ssd_w_in, ssd_conv_w, ssd_conv_b, ssd_dt_bias, ssd_a_log, ssd_d_skip, ssd_gate_norm, ssd_w_out, kv_norm, w_kv, sb_norm, sb_w_q, sb_w_o, ffn_norm, ffn_w_up, ffn_conv_w, ffn_conv_b, ffn_w_down, final_norm)))
    mom = dict(zip(WEIGHT_NAMES, (m_meta_tokens, m_ssd_norm, m_ssd_w_in, m_ssd_conv_w, m_ssd_conv_b, m_ssd_dt_bias, m_ssd_a_log, m_ssd_d_skip, m_ssd_gate_norm, m_ssd_w_out, m_kv_norm, m_w_kv, m_sb_norm, m_sb_w_q, m_sb_w_o, m_ffn_norm, m_ffn_w_up, m_ffn_conv_w, m_ffn_conv_b, m_ffn_w_down, m_final_norm)))
    var = dict(zip(WEIGHT_NAMES, (v_meta_tokens, v_ssd_norm, v_ssd_w_in, v_ssd_conv_w, v_ssd_conv_b, v_ssd_dt_bias, v_ssd_a_log, v_ssd_d_skip, v_ssd_gate_norm, v_ssd_w_out, v_kv_norm, v_w_kv, v_sb_norm, v_sb_w_q, v_sb_w_o, v_ffn_norm, v_ffn_w_up, v_ffn_conv_w, v_ffn_conv_b, v_ffn_w_down, v_final_norm)))
    seq, d = x.shape[1], x.shape[2]
    cfg = _make_cfg(d, seq, ffn_w_down.shape[1] * N_DEV)

    big = _all_gather(_pack([local[n] for n in MATMUL_WEIGHTS], BF16), name="gather_weights")
    small = _all_gather(_pack([local[n] for n in SMALL_SHARDED], F32), name="gather_small")
    full = {n: local[n] for n in REPLICATED}
    for names, buf in ((MATMUL_WEIGHTS, big), (SMALL_SHARDED, small)):
        for n, stacked in zip(names, _unpack(buf, [local[n].shape for n in names], lead=1)):
            full[n] = _unshard(stacked, SHARD_AXIS[n])

    loss, grad_x, grads = _local_step(x[0], loss_target[0], _prepare(full, cfg), cfg)

    to_peers = _pack([_to_shards(grads[n].reshape(full[n].shape), SHARD_AXIS[n]) for n in SHARDED], F32, lead=1)
    g_sharded = _sum_slots(_all_to_all(to_peers, name="scatter_grads"), name="sum_grads")
    rep = _all_gather(_pack([grads[n].reshape(local[n].shape) for n in REPLICATED], F32), name="gather_rep_grads")
    g_flat = jnp.concatenate([g_sharded, _sum_slots(rep, name="sum_rep_grads")], axis=0)

    order = SHARDED + REPLICATED
    flat = lambda src: _pack([src[n] for n in order], F32)
    delta, new_m, new_v = _adamw(flat(local), g_flat, flat(mom), flat(var), name="adamw")
    shapes = [local[n].shape for n in order]
    pick = lambda buf: dict(zip(order, _unpack(buf, shapes)))
    g_out, d_out, m_out, v_out = pick(g_flat), pick(delta), pick(new_m), pick(new_v)
    loss = lax.psum(loss, MESH_AXES)
    return (loss, grad_x[None], *[g_out[n] for n in WEIGHT_NAMES], *[d_out[n] for n in WEIGHT_NAMES],
            *[m_out[n] for n in WEIGHT_NAMES], *[v_out[n] for n in WEIGHT_NAMES])
```

```python
import collections
import functools
import math

import jax
import jax.numpy as jnp
from jax import lax
from jax.experimental import pallas as pl
from jax.experimental.pallas import tpu as pltpu

F32 = jnp.float32
BF16 = jnp.bfloat16
NORM_EPS = 1e-6
N_META = 16
SSM_HEAD_DIM = 64
SSM_GROUPS = 4
SSM_STATE = 128
SSM_CONV = 4
SSM_CHUNK = 256
SB_HEAD_DIM = 64
FFN_CONV = 3
FFN_IL = 256
ATT_T = 256
ADAM_LR, ADAM_B1, ADAM_B2, ADAM_EPS, ADAM_WD, ADAM_STEP = 0.001, 0.9, 0.999, 1e-08, 0.01, 10
N_DEV = 8
VMEM_LIMIT = 56 * 1024 * 1024
MM_TILE = 1536

Cfg = collections.namedtuple("Cfg", "D SEQ LP PF DI H HG GW CONV_DIM DFF HS")


def _make_cfg(d_model, seq, d_ff):
    pf = (-N_META) % SSM_CHUNK
    lp = pf + N_META + seq
    assert lp % SSM_CHUNK == 0 and (pf + N_META) == SSM_CHUNK
    di = 2 * d_model
    h = di // SSM_HEAD_DIM
    return Cfg(D=d_model, SEQ=seq, LP=lp, PF=pf, DI=di, H=h, HG=h // SSM_GROUPS, GW=di // SSM_GROUPS,
               CONV_DIM=di + 2 * SSM_GROUPS * SSM_STATE, DFF=d_ff, HS=d_model // SB_HEAD_DIM)


def _tile(n, pref, mult):
    t = (min(pref, n) // mult) * mult
    while t > mult and n % t:
        t -= mult
    assert t >= mult and n % t == 0, (n, pref, mult)
    return t


def _cparams(sem):
    return pltpu.CompilerParams(dimension_semantics=sem, vmem_limit_bytes=VMEM_LIMIT)


def _dot(a, b):
    return jnp.dot(a, b, preferred_element_type=F32)


def _dot_nt(a, b):
    return lax.dot_general(a, b, (((1,), (1,)), ((), ())), preferred_element_type=F32)


def _dot_tn(a, b):
    return lax.dot_general(a, b, (((0,), (0,)), ((), ())), preferred_element_type=F32)


def _split3(v):
    hi = v.astype(BF16)
    r1 = v - hi.astype(F32)
    mid = r1.astype(BF16)
    lo = (r1 - mid.astype(F32)).astype(BF16)
    return hi, mid, lo


def _split2(v):
    hi = v.astype(BF16)
    lo = (v - hi.astype(F32)).astype(BF16)
    return hi, lo


def _dot3(a_f32, b_bf16):
    hi, mid, lo = _split3(a_f32)
    return _dot(hi, b_bf16) + _dot(mid, b_bf16) + _dot(lo, b_bf16)


def _dot3_left(a_bf16, b_f32):
    hi, mid, lo = _split3(b_f32)
    return _dot(a_bf16, hi) + _dot(a_bf16, mid) + _dot(a_bf16, lo)


def _sigmoid(x):
    return 0.5 * jnp.tanh(0.5 * x) + 0.5


def _softplus(x):
    return jnp.maximum(x, 0.0) + jnp.log(1.0 + jnp.exp(-jnp.abs(x)))


def _row_ids(i, tm, shape_cols=1):
    return i * tm + lax.broadcasted_iota(jnp.int32, (tm, shape_cols), 0)


def _norm_mm(h, g, w, *, out_dtype, name, scale=None):
    lp, d = h.shape
    n = w.shape[1]
    tm = _tile(lp, 768, 8)
    tn = _tile(n, MM_TILE, 128)

    def body(h_ref, g_ref, w_ref, y_ref, u_ref):
        @pl.when(pl.program_id(1) == 0)
        def _():
            x = h_ref[...]
            r = lax.rsqrt(jnp.mean(x * x, axis=-1, keepdims=True) + NORM_EPS)
            u_ref[...] = (x * r * g_ref[...]).astype(BF16)
        y = _dot(u_ref[...], w_ref[...])
        if scale is not None:
            y = y * scale
        y_ref[...] = y.astype(out_dtype)

    return pl.pallas_call(
        body, name=name, grid=(lp // tm, n // tn),
        in_specs=[pl.BlockSpec((tm, d), lambda i, j: (i, 0)), pl.BlockSpec((1, d), lambda i, j: (0, 0)),
                  pl.BlockSpec((d, tn), lambda i, j: (0, j))],
        out_specs=[pl.BlockSpec((tm, tn), lambda i, j: (i, j)), pl.BlockSpec((tm, d), lambda i, j: (i, 0))],
        out_shape=[jax.ShapeDtypeStruct((lp, n), out_dtype), jax.ShapeDtypeStruct((lp, d), BF16)],
        compiler_params=_cparams(("arbitrary", "arbitrary")),
    )(h, g, w)


def _mm_res(a, w, h, *, pf, name):
    lp, k = a.shape
    d = w.shape[1]
    tm = _tile(lp, 768, 8)

    def body(a_ref, w_ref, h_ref, o_ref):
        y = _dot(a_ref[...].astype(BF16), w_ref[...])
        rows = _row_ids(pl.program_id(0), tm)
        o_ref[...] = h_ref[...] + jnp.where(rows >= pf, y, 0.0)

    return pl.pallas_call(
        body, name=name, grid=(lp // tm,),
        in_specs=[pl.BlockSpec((tm, k), lambda i: (i, 0)), pl.BlockSpec((k, d), lambda i: (0, 0)),
                  pl.BlockSpec((tm, d), lambda i: (i, 0))],
        out_specs=pl.BlockSpec((tm, d), lambda i: (i, 0)),
        out_shape=jax.ShapeDtypeStruct((lp, d), F32),
        compiler_params=_cparams(("arbitrary",)),
    )(a, w, h)


def _mm_nt(dy, w, *, out_dtype, pf, name):
    lp, n = dy.shape
    k = w.shape[0]
    tm = _tile(lp, 768, 8)
    tk = _tile(k, MM_TILE, 128)

    def body(dy_ref, w_ref, o_ref):
        y = _dot_nt(dy_ref[...].astype(BF16), w_ref[...])
        rows = _row_ids(pl.program_id(0), tm)
        o_ref[...] = jnp.where(rows >= pf, y, 0.0).astype(out_dtype)

    return pl.pallas_call(
        body, name=name, grid=(lp // tm, k // tk),
        in_specs=[pl.BlockSpec((tm, n), lambda i, j: (i, 0)), pl.BlockSpec((tk, n), lambda i, j: (j, 0))],
        out_specs=pl.BlockSpec((tm, tk), lambda i, j: (i, j)),
        out_shape=jax.ShapeDtypeStruct((lp, k), out_dtype),
        compiler_params=_cparams(("arbitrary", "arbitrary")),
    )(dy, w)


def _mm_nt_normbwd(dy, w, h, g, dh_in, *, pf, name, scale=None):
    lp, n = dy.shape
    d = w.shape[0]
    tm = _tile(lp, 768, 8)
    tn = _tile(n, MM_TILE, 128)
    nj = n // tn

    def body(dy_ref, w_ref, h_ref, g_ref, dhin_ref, dh_ref, dg_ref, acc_ref):
        i, j = pl.program_id(0), pl.program_id(1)

        @pl.when(j == 0)
        def _():
            acc_ref[...] = jnp.zeros_like(acc_ref)

        @pl.when((i == 0) & (j == 0))
        def _():
            dg_ref[...] = jnp.zeros_like(dg_ref)

        acc_ref[...] += _dot_nt(dy_ref[...].astype(BF16), w_ref[...])

        @pl.when(j == nj - 1)
        def _():
            du = acc_ref[...]
            if scale is not None:
                du = du * scale
            x = h_ref[...]
            r = lax.rsqrt(jnp.mean(x * x, axis=-1, keepdims=True) + NORM_EPS)
            xhat = x * r
            dg_ref[...] += jnp.sum(du * xhat, axis=0, keepdims=True)
            dxh = du * g_ref[...]
            dx = r * (dxh - xhat * jnp.mean(dxh * xhat, axis=-1, keepdims=True))
            rows = _row_ids(i, tm)
            dh_ref[...] = jnp.where(rows >= pf, dhin_ref[...] + dx, 0.0)

    return pl.pallas_call(
        body, name=name, grid=(lp // tm, nj),
        in_specs=[pl.BlockSpec((tm, tn), lambda i, j: (i, j)), pl.BlockSpec((d, tn), lambda i, j: (0, j)),
                  pl.BlockSpec((tm, d), lambda i, j: (i, 0)), pl.BlockSpec((1, d), lambda i, j: (0, 0)),
                  pl.BlockSpec((tm, d), lambda i, j: (i, 0))],
        out_specs=[pl.BlockSpec((tm, d), lambda i, j: (i, 0)), pl.BlockSpec((1, d), lambda i, j: (0, 0))],
        out_shape=[jax.ShapeDtypeStruct((lp, d), F32), jax.ShapeDtypeStruct((1, d), F32)],
        scratch_shapes=[pltpu.VMEM((tm, d), F32)],
        compiler_params=_cparams(("arbitrary", "arbitrary")),
    )(dy, w, h, g, dh_in)


def _mm_tn(a, b, *, name, scale=None):
    lp, k = a.shape
    n = b.shape[1]
    tm = _tile(lp, 768, 8)
    tk = _tile(k, MM_TILE, 128)
    tn = _tile(n, MM_TILE, 128)
    nm = lp // tm

    def body(a_ref, b_ref, o_ref, acc_ref):
        m = pl.program_id(2)

        @pl.when(m == 0)
        def _():
            acc_ref[...] = jnp.zeros_like(acc_ref)

        acc_ref[...] += _dot_tn(a_ref[...].astype(BF16), b_ref[...].astype(BF16))

        @pl.when(m == nm - 1)
        def _():
            o_ref[...] = acc_ref[...] if scale is None else acc_ref[...] * scale

    return pl.pallas_call(
        body, name=name, grid=(k // tk, n // tn, nm),
        in_specs=[pl.BlockSpec((tm, tk), lambda i, j, m: (m, i)), pl.BlockSpec((tm, tn), lambda i, j, m: (m, j))],
        out_specs=pl.BlockSpec((tk, tn), lambda i, j, m: (i, j)),
        out_shape=jax.ShapeDtypeStruct((k, n), F32),
        scratch_shapes=[pltpu.VMEM((tk, tn), F32)],
        compiler_params=_cparams(("arbitrary", "arbitrary", "arbitrary")),
    )(a, b)


HALO = 8


def _conv_taps(ext_ref, w_ref, tm, width, cols=slice(None)):
    return [ext_ref[pl.ds(HALO - (width - 1 - k), tm), cols] for k in range(width)]


def _causal_conv(ext_ref, x_ref, w_ref, b_ref, tm, width):
    @pl.when(pl.program_id(1) == 0)
    def _():
        ext_ref[0:HALO, :] = jnp.zeros((HALO, ext_ref.shape[1]), F32)

    ext_ref[HALO:, :] = x_ref[...]
    taps = _conv_taps(ext_ref, w_ref, tm, width)
    acc = b_ref[...] + taps[0] * w_ref[0:1, :]
    for k in range(1, width):
        acc = acc + taps[k] * w_ref[k:k + 1, :]
    return acc, taps


def _ssd_conv_fwd(zx, cw, cb, *, cfg, name):
    lp, cd, di = cfg.LP, cfg.CONV_DIM, cfg.DI
    tc = _tile(math.gcd(di, cd), 512, 128)
    tm = _tile(lp, 768, 8)
    off = di // tc

    def body(x_ref, w_ref, b_ref, o_ref, ext_ref):
        acc, _ = _causal_conv(ext_ref, x_ref, w_ref, b_ref, tm, SSM_CONV)
        rows = _row_ids(pl.program_id(1), tm)
        o_ref[...] = jnp.where(rows >= cfg.PF, acc * _sigmoid(acc), 0.0)
        ext_ref[0:HALO, :] = x_ref[tm - HALO:tm, :]

    return pl.pallas_call(
        body, name=name, grid=(cd // tc, lp // tm),
        in_specs=[pl.BlockSpec((tm, tc), lambda j, i: (i, j + off)), pl.BlockSpec((SSM_CONV, tc), lambda j, i: (0, j)),
                  pl.BlockSpec((1, tc), lambda j, i: (0, j))],
        out_specs=pl.BlockSpec((tm, tc), lambda j, i: (i, j)),
        out_shape=jax.ShapeDtypeStruct((lp, cd), F32),
        scratch_shapes=[pltpu.VMEM((tm + HALO, tc), F32)],
        compiler_params=_cparams(("arbitrary", "arbitrary")),
    )(zx, cw, cb)


def _ssd_conv_bwd_pre(zx, dxbc, cw, cb, *, cfg, name):
    lp, cd, di = cfg.LP, cfg.CONV_DIM, cfg.DI
    tc = _tile(math.gcd(di, cd), 512, 128)
    tm = _tile(lp, 768, 8)
    off = di // tc

    def body(x_ref, d_ref, w_ref, b_ref, dc_ref, dw_ref, db_ref, ext_ref):
        i = pl.program_id(1)
        c, taps = _causal_conv(ext_ref, x_ref, w_ref, b_ref, tm, SSM_CONV)
        sg = _sigmoid(c)
        rows = _row_ids(i, tm)
        dc = jnp.where(rows >= cfg.PF, d_ref[...] * sg * (1.0 + c * (1.0 - sg)), 0.0)
        dc_ref[...] = dc

        @pl.when(i == 0)
        def _():
            dw_ref[...] = jnp.zeros_like(dw_ref)
            db_ref[...] = jnp.zeros_like(db_ref)

        db_ref[...] += jnp.sum(dc, axis=0, keepdims=True)
        for k in range(SSM_CONV):
            dw_ref[k:k + 1, :] += jnp.sum(dc * taps[k], axis=0, keepdims=True)
        ext_ref[0:HALO, :] = x_ref[tm - HALO:tm, :]

    return pl.pallas_call(
        body, name=name, grid=(cd // tc, lp // tm),
        in_specs=[pl.BlockSpec((tm, tc), lambda j, i: (i, j + off)), pl.BlockSpec((tm, tc), lambda j, i: (i, j)),
                  pl.BlockSpec((SSM_CONV, tc), lambda j, i: (0, j)), pl.BlockSpec((1, tc), lambda j, i: (0, j))],
        out_specs=[pl.BlockSpec((tm, tc), lambda j, i: (i, j)), pl.BlockSpec((SSM_CONV, tc), lambda j, i: (0, j)),
                   pl.BlockSpec((1, tc), lambda j, i: (0, j))],
        out_shape=[jax.ShapeDtypeStruct((lp, cd), F32), jax.ShapeDtypeStruct((SSM_CONV, cd), F32),
                   jax.ShapeDtypeStruct((1, cd), F32)],
        scratch_shapes=[pltpu.VMEM((tm + HALO, tc), F32)],
        compiler_params=_cparams(("arbitrary", "arbitrary")),
    )(zx, dxbc, cw, cb)


def _ffn_conv_fwd(up, cw, cb, *, cfg, name):
    lp, dff = cfg.LP, cfg.DFF
    tc = 2 * FFN_IL
    tm = _tile(lp, 768, 8)

    def body(x_ref, w_ref, b_ref, o_ref, ext_ref):
        hc, _ = _causal_conv(ext_ref, x_ref, w_ref, b_ref, tm, FFN_CONV)
        gc, vc = hc[:, :FFN_IL], hc[:, FFN_IL:]
        rows = _row_ids(pl.program_id(1), tm)
        o_ref[...] = jnp.where(rows >= cfg.PF, gc * _sigmoid(gc) * vc, 0.0).astype(BF16)
        ext_ref[0:HALO, :] = x_ref[tm - HALO:tm, :]

    return pl.pallas_call(
        body, name=name, grid=(dff // FFN_IL, lp // tm),
        in_specs=[pl.BlockSpec((tm, tc), lambda j, i: (i, j)), pl.BlockSpec((FFN_CONV, tc), lambda j, i: (0, j)),
                  pl.BlockSpec((1, tc), lambda j, i: (0, j))],
        out_specs=pl.BlockSpec((tm, FFN_IL), lambda j, i: (i, j)),
        out_shape=jax.ShapeDtypeStruct((lp, dff), BF16),
        scratch_shapes=[pltpu.VMEM((tm + HALO, tc), F32)],
        compiler_params=_cparams(("arbitrary", "arbitrary")),
    )(up, cw, cb)


def _ffn_conv_bwd_pre(up, dact, cw, cb, *, cfg, name):
    lp, dff = cfg.LP, cfg.DFF
    tc = 2 * FFN_IL
    tm = _tile(lp, 768, 8)

    def body(x_ref, d_ref, w_ref, b_ref, dc_ref, dw_ref, db_ref, ext_ref):
        i = pl.program_id(1)
        hc, taps = _causal_conv(ext_ref, x_ref, w_ref, b_ref, tm, FFN_CONV)
        gc, vc = hc[:, :FFN_IL], hc[:, FFN_IL:]
        sg = _sigmoid(gc)
        rows = _row_ids(i, tm)
        da = jnp.where(rows >= cfg.PF, d_ref[...].astype(F32), 0.0)
        dg = da * vc * sg * (1.0 + gc * (1.0 - sg))
        dv = da * gc * sg
        dc_ref[:, :FFN_IL] = dg
        dc_ref[:, FFN_IL:] = dv

        @pl.when(i == 0)
        def _():
            dw_ref[...] = jnp.zeros_like(dw_ref)
            db_ref[...] = jnp.zeros_like(db_ref)

        db_ref[:, :FFN_IL] += jnp.sum(dg, axis=0, keepdims=True)
        db_ref[:, FFN_IL:] += jnp.sum(dv, axis=0, keepdims=True)
        for k in range(FFN_CONV):
            dw_ref[k:k + 1, :FFN_IL] += jnp.sum(dg * taps[k][:, :FFN_IL], axis=0, keepdims=True)
            dw_ref[k:k + 1, FFN_IL:] += jnp.sum(dv * taps[k][:, FFN_IL:], axis=0, keepdims=True)
        ext_ref[0:HALO, :] = x_ref[tm - HALO:tm, :]

    return pl.pallas_call(
        body, name=name, grid=(dff // FFN_IL, lp // tm),
        in_specs=[pl.BlockSpec((tm, tc), lambda j, i: (i, j)), pl.BlockSpec((tm, FFN_IL), lambda j, i: (i, j)),
                  pl.BlockSpec((FFN_CONV, tc), lambda j, i: (0, j)), pl.BlockSpec((1, tc), lambda j, i: (0, j))],
        out_specs=[pl.BlockSpec((tm, tc), lambda j, i: (i, j)), pl.BlockSpec((FFN_CONV, tc), lambda j, i: (0, j)),
                   pl.BlockSpec((1, tc), lambda j, i: (0, j))],
        out_shape=[jax.ShapeDtypeStruct((lp, 2 * dff), F32), jax.ShapeDtypeStruct((FFN_CONV, 2 * dff), F32),
                   jax.ShapeDtypeStruct((1, 2 * dff), F32)],
        scratch_shapes=[pltpu.VMEM((tm + HALO, tc), F32)],
        compiler_params=_cparams(("arbitrary", "arbitrary")),
    )(up, dact, cw, cb)


def _head_select(lane, head):
    return ((lane >= head * SSM_HEAD_DIM) & (lane < (head + 1) * SSM_HEAD_DIM)).astype(BF16)


def _ssd_terms(x, bm, cm, dtr, bias, alog, dsk, valid, cfg):
    q, hg, gw, p = SSM_CHUNK, cfg.HG, cfg.GW, SSM_HEAD_DIM
    t = {}
    e_mat = _head_select(lax.broadcasted_iota(jnp.int32, (hg, gw), 1), lax.broadcasted_iota(jnp.int32, (hg, gw), 0))
    li = lax.broadcasted_iota(jnp.int32, (q, q), 0)
    si = lax.broadcasted_iota(jnp.int32, (q, q), 1)
    tri = li >= si
    tril = tri.astype(BF16)
    triu = (li <= si).astype(BF16)
    pre = dtr + bias
    dtv = jnp.where(valid, _softplus(pre), 0.0)
    a_head = -jnp.exp(alog)
    a = dtv * a_head
    cs = _dot3_left(tril, a)
    hi, mid, lo = _split3(a)
    cst = _dot_tn(hi, triu) + _dot_tn(mid, triu) + _dot_tn(lo, triu)
    cs_last = jnp.sum(a, axis=0, keepdims=True)
    dte = jnp.exp(jnp.minimum(cs_last - cs, 0.0))
    ecs = jnp.exp(cs)
    t.update(e_mat=e_mat, tri=tri, tril=tril, triu=triu, pre=pre, dtv=dtv, a_head=a_head, cs=cs, cst=cst,
             dec=jnp.exp(cs_last), dte=dte, ecs=ecs)
    t["dtv_x"] = _dot3(dtv, e_mat)
    t["ecs_x"] = _dot3(ecs, e_mat)
    t["dte_x"] = _dot3(dte, e_mat)
    t["dec_x"] = _dot3(t["dec"], e_mat)
    t["d_x"] = _dot3(dsk, e_mat)
    t["xdt"] = x * t["dtv_x"]
    t["gm"] = _dot_nt(cm.astype(BF16), bm.astype(BF16))
    return t


def _ssd_decay(t, e):
    diff = t["cs"][:, e:e + 1] - t["cst"][e:e + 1, :]
    return jnp.where(t["tri"], jnp.exp(jnp.minimum(diff, 0.0)), 0.0)


def _head_mask(b):
    lane = lax.broadcasted_iota(jnp.int32, (1, 2 * SSM_HEAD_DIM), 1)
    return (lane >= SSM_HEAD_DIM * b) & (lane < SSM_HEAD_DIM * (b + 1))


def _ssd_specs(cfg):
    q, g, gw, ns, hg, di = SSM_CHUNK, SSM_GROUPS, cfg.GW, SSM_STATE, cfg.HG, cfg.DI
    return dict(
        x=pl.BlockSpec((q, gw), lambda j, c: (c, j)),
        b=pl.BlockSpec((q, ns), lambda j, c: (c, di // ns + j)),
        c=pl.BlockSpec((q, ns), lambda j, c: (c, di // ns + g + j)),
        z=pl.BlockSpec((q, gw), lambda j, c: (c, j)),
        dtr=pl.BlockSpec((None, q, hg), lambda j, c: (j, c, 0)),
        prm=pl.BlockSpec((None, 8, hg), lambda j, c: (j, 0, 0)),
        gate=pl.BlockSpec((1, gw), lambda j, c: (0, j)),
        st=pl.BlockSpec((None, None, ns, gw), lambda j, c: (c, j, 0, 0)),
    )


def _ssd_fwd(xbc, zx, dtr, prm, gate_g, *, cfg, name):
    q, g, gw, ns, hg, p = SSM_CHUNK, SSM_GROUPS, cfg.GW, SSM_STATE, cfg.HG, SSM_HEAD_DIM
    nc = cfg.LP // q
    sp = _ssd_specs(cfg)

    def body(x_ref, b_ref, c_ref, z_ref, dtr_ref, prm_ref, gg_ref, y_ref, hn_ref, st_ref, s_ref):
        c = pl.program_id(1)

        @pl.when(c == 0)
        def _():
            s_ref[...] = jnp.zeros((ns, gw), F32)

        s_prev = s_ref[...]
        st_ref[...] = s_prev
        x, bm, cm = x_ref[...], b_ref[...], c_ref[...]
        valid = _row_ids(c, q) >= cfg.PF
        t = _ssd_terms(x, bm, cm, dtr_ref[...], prm_ref[0:1, :], prm_ref[1:2, :], prm_ref[2:3, :], valid, cfg)
        parts = []
        for pr in range(hg // 2):
            xp = t["xdt"][:, 2 * p * pr:2 * p * (pr + 1)]
            acc = None
            for b in range(2):
                m = (t["gm"] * _ssd_decay(t, 2 * pr + b)).astype(BF16)
                d = _dot(m, jnp.where(_head_mask(b), xp, 0.0).astype(BF16))
                acc = d if acc is None else acc + d
            parts.append(acc)
        y_diag = jnp.concatenate(parts, axis=1)
        y_off = _dot(cm.astype(BF16), s_prev.astype(BF16)) * t["ecs_x"]
        y = y_diag + y_off + x * t["d_x"]
        y_ref[...] = y
        s_ref[...] = s_prev * t["dec_x"] + _dot_tn(bm.astype(BF16), (t["xdt"] * t["dte_x"]).astype(BF16))
        z = z_ref[...]
        hgate = y * z * _sigmoid(z)
        r = lax.rsqrt(jnp.mean(hgate * hgate, axis=-1, keepdims=True) + NORM_EPS)
        hn_ref[...] = (hgate * r * gg_ref[...]).astype(BF16)

    return pl.pallas_call(
        body, name=name, grid=(g, nc),
        in_specs=[sp["x"], sp["b"], sp["c"], sp["z"], sp["dtr"], sp["prm"], sp["gate"]],
        out_specs=[sp["x"], sp["x"], sp["st"]],
        out_shape=[jax.ShapeDtypeStruct((cfg.LP, cfg.DI), F32), jax.ShapeDtypeStruct((cfg.LP, cfg.DI), BF16),
                   jax.ShapeDtypeStruct((nc, g, ns, gw), F32)],
        scratch_shapes=[pltpu.VMEM((ns, gw), F32)],
        compiler_params=_cparams(("arbitrary", "arbitrary")),
    )(xbc, xbc, xbc, zx, dtr, prm, gate_g)


def _ssd_bwd(xbc, zx, dtr, prm, gate_g, y, states, dhn, *, cfg, name):
    q, g, gw, ns, hg, p = SSM_CHUNK, SSM_GROUPS, cfg.GW, SSM_STATE, cfg.HG, SSM_HEAD_DIM
    nc = cfg.LP // q
    sp = _ssd_specs(cfg)
    rev = lambda spec: pl.BlockSpec(spec.block_shape, (lambda f: (lambda j, c: f(j, nc - 1 - c)))(spec.index_map))
    bc_spec = pl.BlockSpec((q, ns), lambda j, c: (nc - 1 - c, j))

    def body(x_ref, b_ref, c_ref, z_ref, dtr_ref, prm_ref, gg_ref, y_ref, st_ref, dhn_ref,
             dx_ref, db_ref, dc_ref, dz_ref, ddtr_ref, dprm_ref, dgg_ref, ds_ref):
        ci = pl.program_id(1)
        c = nc - 1 - ci

        @pl.when(ci == 0)
        def _():
            ds_ref[...] = jnp.zeros((ns, gw), F32)
            dprm_ref[...] = jnp.zeros((8, hg), F32)
            dgg_ref[...] = jnp.zeros((8, gw), F32)

        ds_next = ds_ref[...]
        s_prev = st_ref[...]
        x, bm, cm = x_ref[...], b_ref[...], c_ref[...]
        valid = _row_ids(c, q) >= cfg.PF
        t = _ssd_terms(x, bm, cm, dtr_ref[...], prm_ref[0:1, :], prm_ref[1:2, :], prm_ref[2:3, :], valid, cfg)
        et_mat = _head_select(lax.broadcasted_iota(jnp.int32, (gw, hg), 0), lax.broadcasted_iota(jnp.int32, (gw, hg), 1))
        heads = lambda v: _dot3(v, et_mat)
        yv, z = y_ref[...], z_ref[...]
        sz = _sigmoid(z)
        silu = z * sz
        hgate = yv * silu
        r = lax.rsqrt(jnp.mean(hgate * hgate, axis=-1, keepdims=True) + NORM_EPS)
        hhat = hgate * r
        dhn = dhn_ref[...]
        dgg = jnp.sum(dhn * hhat, axis=0, keepdims=True)
        dhh = dhn * gg_ref[...]
        dhgate = r * (dhh - hhat * jnp.mean(dhh * hhat, axis=-1, keepdims=True))
        dy = dhgate * silu
        dz_ref[...] = dhgate * yv * sz * (1.0 + z * (1.0 - sz))
        bmb, cmb = bm.astype(BF16), cm.astype(BF16)
        xdt = t["xdt"]
        dye = dy * t["ecs_x"]
        dxdt_state = _dot(bmb, ds_next.astype(BF16)) * t["dte_x"]
        dg_acc = None
        dparts = []
        li = lax.broadcasted_iota(jnp.int32, (q, q), 0)
        si = lax.broadcasted_iota(jnp.int32, (q, q), 1)
        head_id = lax.broadcasted_iota(jnp.int32, (1, hg), 1)
        da = None
        for pr in range(hg // 2):
            sl = slice(2 * p * pr, 2 * p * (pr + 1))
            xp, dyp = xdt[:, sl], dy[:, sl]
            acc = None
            for b in range(2):
                lm = _ssd_decay(t, 2 * pr + b)
                m = (t["gm"] * lm).astype(BF16)
                dym = jnp.where(_head_mask(b), dyp, 0.0).astype(BF16)
                d = _dot_tn(m, dym)
                acc = d if acc is None else acc + d
                dm = _dot_nt(dym, xp.astype(BF16)) * lm
                dg_acc = dm if dg_acc is None else dg_acc + dm
                corner = _dot(t["triu"], (dm * t["gm"]).astype(BF16))
                da_e = jnp.sum(jnp.where(si < li, corner, 0.0), axis=1, keepdims=True)
                da_e = da_e * (head_id == 2 * pr + b).astype(F32)
                da = da_e if da is None else da + da_e
            dparts.append(acc)
        dxdt = jnp.concatenate(dparts, axis=1) + dxdt_state
        dgb = dg_acc.astype(BF16)
        dc_ref[...] = _dot(dgb, bmb) + _dot_nt(dye.astype(BF16), s_prev.astype(BF16))
        xde = (xdt * t["dte_x"]).astype(BF16)
        db_ref[...] = _dot_tn(dgb, cmb) + _dot_nt(xde, ds_next.astype(BF16))
        ds_ref[...] = ds_next * t["dec_x"] + _dot_tn(cmb, dye.astype(BF16))
        y_off = _dot(cmb, s_prev.astype(BF16)) * t["ecs_x"]
        tril_strict = (li > si).astype(BF16)
        da = da + _dot3_left(t["triu"], heads(dy * y_off)) + _dot3_left(tril_strict, heads(xdt * dxdt_state)) \
            + t["dec"] * heads(jnp.sum(ds_next * s_prev, axis=0, keepdims=True))
        ddtv = da * t["a_head"] + heads(dxdt * x)
        ddtr = jnp.where(valid, ddtv * _sigmoid(t["pre"]), 0.0)
        ddtr_ref[...] = ddtr
        dx_ref[...] = dxdt * t["dtv_x"] + dy * t["d_x"]
        dalog = jnp.sum(da * t["dtv"], axis=0, keepdims=True) * t["a_head"]
        dprm_ref[0:1, :] += jnp.sum(ddtr, axis=0, keepdims=True)
        dprm_ref[1:2, :] += dalog
        dprm_ref[2:3, :] += heads(jnp.sum(dy * x, axis=0, keepdims=True))
        dgg_ref[0:1, :] += dgg

    return pl.pallas_call(
        body, name=name, grid=(g, nc),
        in_specs=[rev(sp["x"]), rev(sp["b"]), rev(sp["c"]), rev(sp["z"]), rev(sp["dtr"]), sp["prm"], sp["gate"],
                  rev(sp["x"]), rev(sp["st"]), rev(sp["x"])],
        out_specs=[rev(sp["x"]), bc_spec, bc_spec, rev(sp["x"]), rev(sp["dtr"]),
                   pl.BlockSpec((None, 8, hg), lambda j, c: (j, 0, 0)), pl.BlockSpec((None, 8, gw), lambda j, c: (j, 0, 0))],
        out_shape=[jax.ShapeDtypeStruct((cfg.LP, cfg.DI), F32), jax.ShapeDtypeStruct((cfg.LP, g * ns), F32),
                   jax.ShapeDtypeStruct((cfg.LP, g * ns), F32), jax.ShapeDtypeStruct((cfg.LP, cfg.DI), F32),
                   jax.ShapeDtypeStruct((g, cfg.LP, hg), F32), jax.ShapeDtypeStruct((g, 8, hg), F32),
                   jax.ShapeDtypeStruct((g, 8, gw), F32)],
        scratch_shapes=[pltpu.VMEM((ns, gw), F32)],
        compiler_params=_cparams(("arbitrary", "arbitrary")),
    )(xbc, xbc, xbc, zx, dtr, prm, gate_g, y, states, dhn)


ATT_STRIP = 32
ATT_NT = 2
ATT_NEG = -1e9
LOG2E = 1.4426950408889634
LN2 = 0.6931471805599453


def _att_init(u_ref, bias_ref, qb, cfg):
    t = ATT_T
    rows = lax.broadcasted_iota(jnp.int32, (t, t), 0)
    cols = lax.broadcasted_iota(jnp.int32, (t, t), 1)
    u_ref[...] = (rows > cols).astype(BF16)
    pad = cols < cfg.PF
    bias_ref[0] = jnp.zeros((t, t), F32)
    bias_ref[1] = jnp.where((cols >= rows) | (pad & (qb == 0)), ATT_NEG, 0.0)
    bias_ref[2] = jnp.where(pad, ATT_NEG, 0.0)
    bias_ref[3] = jnp.full((t, t), ATT_NEG, F32)


def _att_tile(kv_ref, kb, qb):
    t = ATT_T
    off = pl.multiple_of(jnp.maximum(kb, 0) * t, t)
    k = kv_ref[pl.ds(off, t), 0:2 * SB_HEAD_DIM]
    v = kv_ref[pl.ds(off, t), 2 * SB_HEAD_DIM:]
    kind = jnp.where(kb < 0, 3, jnp.where(kb == qb, 1, jnp.where(kb == 0, 2, 0)))
    return k, v, kind


def _att_trips(qb):
    n = qb + ATT_NT
    return lax.shift_right_logical(n, 1) if ATT_NT == 2 else lax.div(n, ATT_NT)


def _att_carry(carry_ref, rsum_ref, u, b):
    carry = carry_ref[b]
    for up in range(u):
        carry = carry + rsum_ref[2 * up + b]
    return carry


def _att_scores_strip(s_ref, bias_ref, kind, b, r0):
    sl = pl.ds(r0, ATT_STRIP)
    s = s_ref[b, sl, :] + bias_ref[kind, sl, :]
    l1p = jnp.log2(1.0 + jnp.exp2(jnp.abs(s) * (-LOG2E))) * LN2
    ls = jnp.minimum(s, 0.0) - l1p
    return ls, ls - s


def _attn_fwd(q, kv, *, cfg, name):
    t, lp, d, rs, nch = ATT_T, cfg.LP, cfg.D, ATT_STRIP, 2 * ATT_NT
    hp, nq = d // (2 * SB_HEAD_DIM), lp // ATT_T
    scale = SB_HEAD_DIM ** -0.5

    def body(q_ref, kv_ref, o_ref, u_ref, bias_ref, s_ref, hi_ref, lo_ref, cum_ref, w_ref, cb_ref, acc_ref, carry_ref,
             rsum_ref):
        qb = pl.program_id(1)
        _att_init(u_ref, bias_ref, qb, cfg)
        lo_half = lax.broadcasted_iota(jnp.int32, (1, 2 * SB_HEAD_DIM), 1) < SB_HEAD_DIM
        q2 = q_ref[...] * scale
        qms = [jnp.where(lo_half, q2, 0).astype(BF16), jnp.where(lo_half, 0, q2).astype(BF16)]
        acc_ref[...] = jnp.zeros_like(acc_ref)
        carry_ref[...] = jnp.zeros_like(carry_ref)

        def trip(j, _):
            tiles = [_att_tile(kv_ref, qb - ATT_NT * j - u, qb) for u in range(ATT_NT)]
            chains = [(u, b) for u in range(ATT_NT) for b in range(2)]
            for c, (u, b) in enumerate(chains):
                s_ref[c] = _dot_nt(qms[b], tiles[u][0])
            for c, (u, b) in enumerate(chains):
                for r0 in range(0, t, rs):
                    sl = pl.ds(r0, rs)
                    ls, lk = _att_scores_strip(s_ref, bias_ref, tiles[u][2], c, r0)
                    s_ref[c, sl, :] = ls
                    hi = lk.astype(BF16)
                    hi_ref[c, sl, :] = hi
                    lo_ref[c, sl, :] = (lk - hi.astype(F32)).astype(BF16)
                    rsum_ref[c, sl, :] = jnp.sum(lk, axis=1, keepdims=True)
            for c in range(len(chains)):
                cum_ref[c] = _dot(hi_ref[c], u_ref[...]) + _dot(lo_ref[c], u_ref[...])
            for c, (u, b) in enumerate(chains):
                cb_ref[c] = jnp.broadcast_to(_att_carry(carry_ref, rsum_ref, u, b), (t, t))
            for c, (u, b) in enumerate(chains):
                for r0 in range(0, t, rs):
                    sl = pl.ds(r0, rs)
                    x = s_ref[c, sl, :] + cum_ref[c, sl, :] + cb_ref[c, sl, :]
                    w_ref[c, sl, :] = jnp.exp2(x * LOG2E).astype(BF16)
            for c, (u, b) in enumerate(chains):
                acc_ref[b] += _dot(w_ref[c], tiles[u][1])
                carry_ref[b] += rsum_ref[c]
            return 0

        lax.fori_loop(0, _att_trips(qb), trip, 0)
        o_ref[...] = jnp.where(lo_half, acc_ref[0], acc_ref[1])

    return pl.pallas_call(
        body, name=name, grid=(hp, nq),
        in_specs=[pl.BlockSpec((t, 128), lambda h, i: (i, h)), pl.BlockSpec((lp, 256), lambda h, i: (0, h))],
        out_specs=pl.BlockSpec((t, 128), lambda h, i: (i, h)),
        out_shape=jax.ShapeDtypeStruct((lp, d), F32),
        scratch_shapes=[pltpu.VMEM((t, t), BF16), pltpu.VMEM((4, t, t), F32), pltpu.VMEM((nch, t, t), F32),
                        pltpu.VMEM((nch, t, t), BF16), pltpu.VMEM((nch, t, t), BF16), pltpu.VMEM((nch, t, t), F32),
                        pltpu.VMEM((nch, t, t), BF16), pltpu.VMEM((nch, t, t), F32),
                        pltpu.VMEM((2, t, 2 * SB_HEAD_DIM), F32), pltpu.VMEM((2, t, 1), F32), pltpu.VMEM((nch, t, 1), F32)],
        compiler_params=_cparams(("arbitrary", "arbitrary")),
    )(q, kv)


def _attn_bwd(q, kv, o, do, *, cfg, name):
    t, lp, d, rs, nch = ATT_T, cfg.LP, cfg.D, ATT_STRIP, 2 * ATT_NT
    hp, nq = d // (2 * SB_HEAD_DIM), lp // ATT_T
    scale = SB_HEAD_DIM ** -0.5

    def body(q_ref, kv_ref, o_ref, do_ref, dq_ref, dkv_ref, u_ref, bias_ref, s_ref, dw_ref, hi_ref, lo_ref, cum_ref,
             w_ref, cb_ref, acc_ref, carry_ref, ecarry_ref, etot_ref, rsum_ref, ersum_ref):
        qb = pl.program_id(1)

        @pl.when(qb == 0)
        def _():
            dkv_ref[...] = jnp.zeros_like(dkv_ref)

        _att_init(u_ref, bias_ref, qb, cfg)
        lo_half = lax.broadcasted_iota(jnp.int32, (1, 2 * SB_HEAD_DIM), 1) < SB_HEAD_DIM
        q2 = q_ref[...] * scale
        do2 = do_ref[...]
        qms = [jnp.where(lo_half, q2, 0).astype(BF16), jnp.where(lo_half, 0, q2).astype(BF16)]
        doms = [jnp.where(lo_half, do2, 0).astype(BF16), jnp.where(lo_half, 0, do2).astype(BF16)]
        prod = do2.astype(F32) * o_ref[...]
        etot_ref[0] = jnp.sum(jnp.where(lo_half, prod, 0.0), axis=1, keepdims=True)
        etot_ref[1] = jnp.sum(jnp.where(lo_half, 0.0, prod), axis=1, keepdims=True)
        acc_ref[...] = jnp.zeros_like(acc_ref)
        carry_ref[...] = jnp.zeros_like(carry_ref)
        ecarry_ref[...] = jnp.zeros_like(ecarry_ref)

        def trip(j, _):
            kbs = [qb - ATT_NT * j - u for u in range(ATT_NT)]
            tiles = [_att_tile(kv_ref, kb, qb) for kb in kbs]
            chains = [(u, b) for u in range(ATT_NT) for b in range(2)]
            for c, (u, b) in enumerate(chains):
                s_ref[c] = _dot_nt(qms[b], tiles[u][0])
                dw_ref[c] = _dot_nt(doms[b], tiles[u][1])
            for c, (u, b) in enumerate(chains):
                for r0 in range(0, t, rs):
                    sl = pl.ds(r0, rs)
                    ls, lk = _att_scores_strip(s_ref, bias_ref, tiles[u][2], c, r0)
                    s_ref[c, sl, :] = ls
                    hi = lk.astype(BF16)
                    hi_ref[c, sl, :] = hi
                    lo_ref[c, sl, :] = (lk - hi.astype(F32)).astype(BF16)
                    rsum_ref[c, sl, :] = jnp.sum(lk, axis=1, keepdims=True)
            for c in range(nch):
                cum_ref[c] = _dot(hi_ref[c], u_ref[...]) + _dot(lo_ref[c], u_ref[...])
            for c, (u, b) in enumerate(chains):
                cb_ref[c] = jnp.broadcast_to(_att_carry(carry_ref, rsum_ref, u, b), (t, t))
            for c, (u, b) in enumerate(chains):
                for r0 in range(0, t, rs):
                    sl = pl.ds(r0, rs)
                    wb = jnp.exp2((s_ref[c, sl, :] + cum_ref[c, sl, :] + cb_ref[c, sl, :]) * LOG2E).astype(BF16)
                    w_ref[c, sl, :] = wb
                    e = wb.astype(F32) * dw_ref[c, sl, :]
                    dw_ref[c, sl, :] = e
                    hi = e.astype(BF16)
                    hi_ref[c, sl, :] = hi
                    lo_ref[c, sl, :] = (e - hi.astype(F32)).astype(BF16)
                    ersum_ref[c, sl, :] = jnp.sum(e, axis=1, keepdims=True)
            for c in range(nch):
                cum_ref[c] = _dot(hi_ref[c], u_ref[...]) + _dot(lo_ref[c], u_ref[...])
            for c, (u, b) in enumerate(chains):
                cb_ref[c] = jnp.broadcast_to(etot_ref[b] - _att_carry(ecarry_ref, ersum_ref, u, b), (t, t))
            for c, (u, b) in enumerate(chains):
                for r0 in range(0, t, rs):
                    sl = pl.ds(r0, rs)
                    e = dw_ref[c, sl, :]
                    e_before = cb_ref[c, sl, :] - (e + cum_ref[c, sl, :])
                    sig = jnp.exp2(s_ref[c, sl, :] * LOG2E)
                    hi_ref[c, sl, :] = (e * (1.0 - sig) - sig * e_before).astype(BF16)
            for u in range(ATT_NT):
                off = pl.multiple_of(jnp.maximum(kbs[u], 0) * t, t)
                c0, c1 = 2 * u, 2 * u + 1
                acc_ref[0] += _dot(hi_ref[c0], tiles[u][0])
                acc_ref[1] += _dot(hi_ref[c1], tiles[u][0])
                dkv_ref[pl.ds(off, t), 0:2 * SB_HEAD_DIM] += _dot_tn(hi_ref[c0], qms[0]) + _dot_tn(hi_ref[c1], qms[1])
                dkv_ref[pl.ds(off, t), 2 * SB_HEAD_DIM:] += _dot_tn(w_ref[c0], doms[0]) + _dot_tn(w_ref[c1], doms[1])
                for b in range(2):
                    carry_ref[b] += rsum_ref[2 * u + b]
                    ecarry_ref[b] += ersum_ref[2 * u + b]
            return 0

        lax.fori_loop(0, _att_trips(qb), trip, 0)
        dq_ref[...] = jnp.where(lo_half, acc_ref[0], acc_ref[1]) * scale

    big = lambda dt: pltpu.VMEM((nch, t, t), dt)
    col = lambda n: pltpu.VMEM((n, t, 1), F32)
    return pl.pallas_call(
        body, name=name, grid=(hp, nq),
        in_specs=[pl.BlockSpec((t, 128), lambda h, i: (i, h)), pl.BlockSpec((lp, 256), lambda h, i: (0, h)),
                  pl.BlockSpec((t, 128), lambda h, i: (i, h)), pl.BlockSpec((t, 128), lambda h, i: (i, h))],
        out_specs=[pl.BlockSpec((t, 128), lambda h, i: (i, h)), pl.BlockSpec((lp, 256), lambda h, i: (0, h))],
        out_shape=[jax.ShapeDtypeStruct((lp, d), F32), jax.ShapeDtypeStruct((lp, 2 * d), F32)],
        scratch_shapes=[pltpu.VMEM((t, t), BF16), pltpu.VMEM((4, t, t), F32), big(F32), big(F32), big(BF16), big(BF16),
                        big(F32), big(BF16), big(F32), pltpu.VMEM((2, t, 2 * SB_HEAD_DIM), F32), col(2), col(2), col(2),
                        col(nch), col(nch)],
        compiler_params=_cparams(("arbitrary", "arbitrary")),
    )(q, kv, o, do)


def _loss_head(h, g, target, *, cfg, name):
    lp, d = h.shape
    tm = SSM_CHUNK
    first = (cfg.PF + N_META) // tm
    assert (cfg.PF + N_META) % tm == 0

    def body(h_ref, g_ref, t_ref, loss_ref, dh_ref, dg_ref):
        i = pl.program_id(0)

        @pl.when(i == 0)
        def _():
            loss_ref[...] = jnp.zeros_like(loss_ref)
            dg_ref[...] = jnp.zeros_like(dg_ref)

        x = h_ref[...]
        r = lax.rsqrt(jnp.mean(x * x, axis=-1, keepdims=True) + NORM_EPS)
        xhat = x * r
        live = i >= first
        diff = jnp.where(live, xhat * g_ref[...] - t_ref[...], 0.0)
        loss_ref[...] += 0.5 * jnp.sum(jnp.mean(diff * diff, axis=-1, keepdims=True))
        dy = diff * (1.0 / d)
        dg_ref[...] += jnp.sum(dy * xhat, axis=0, keepdims=True)
        dxh = dy * g_ref[...]
        dh_ref[...] = r * (dxh - xhat * jnp.mean(dxh * xhat, axis=-1, keepdims=True))

    return pl.pallas_call(
        body, name=name, grid=(lp // tm,),
        in_specs=[pl.BlockSpec((tm, d), lambda i: (i, 0)), pl.BlockSpec((1, d), lambda i: (0, 0)),
                  pl.BlockSpec((tm, d), lambda i: (jnp.maximum(i - first, 0), 0))],
        out_specs=[pl.BlockSpec((8, 128), lambda i: (0, 0)), pl.BlockSpec((tm, d), lambda i: (i, 0)),
                   pl.BlockSpec((1, d), lambda i: (0, 0))],
        out_shape=[jax.ShapeDtypeStruct((8, 128), F32), jax.ShapeDtypeStruct((lp, d), F32),
                   jax.ShapeDtypeStruct((1, d), F32)],
        compiler_params=_cparams(("arbitrary",)),
    )(h, g, target)


def _conv_bwd_input(dc, cw, *, width, name):
    lp, cd = dc.shape
    tc = _tile(cd, 512, 128)
    tm = _tile(lp, 768, 8)
    ni = lp // tm

    def body(d_ref, w_ref, o_ref, ext_ref):
        @pl.when(pl.program_id(1) == 0)
        def _():
            ext_ref[tm:, :] = jnp.zeros((HALO, tc), F32)

        ext_ref[0:tm, :] = d_ref[...]
        acc = ext_ref[pl.ds(width - 1, tm), :] * w_ref[0:1, :]
        for k in range(1, width):
            acc = acc + ext_ref[pl.ds(width - 1 - k, tm), :] * w_ref[k:k + 1, :]
        o_ref[...] = acc.astype(BF16)
        ext_ref[tm:, :] = d_ref[0:HALO, :]

    return pl.pallas_call(
        body, name=name, grid=(cd // tc, ni),
        in_specs=[pl.BlockSpec((tm, tc), lambda j, i: (ni - 1 - i, j)), pl.BlockSpec((width, tc), lambda j, i: (0, j))],
        out_specs=pl.BlockSpec((tm, tc), lambda j, i: (ni - 1 - i, j)),
        out_shape=jax.ShapeDtypeStruct((lp, cd), BF16),
        scratch_shapes=[pltpu.VMEM((tm + HALO, tc), F32)],
        compiler_params=_cparams(("arbitrary", "arbitrary")),
    )(dc, cw)


MATMUL_WEIGHTS = ("ssd_w_in", "ssd_w_out", "w_kv", "sb_w_q", "sb_w_o", "ffn_w_up", "ffn_w_down")


def _il(a, cfg):
    nb = cfg.DFF // FFN_IL
    lead = a.shape[:-1]
    a = a.reshape(lead + (2, nb, FFN_IL))
    return jnp.swapaxes(a, -3, -2).reshape(lead + (2 * cfg.DFF,))


def _unil(a, cfg):
    nb = cfg.DFF // FFN_IL
    lead = a.shape[:-1]
    a = a.reshape(lead + (nb, 2, FFN_IL))
    return jnp.swapaxes(a, -3, -2).reshape(lead + (2 * cfg.DFF,))


def _pair(a, cfg):
    hp = cfg.D // 128
    lead = a.shape[:-1]
    return jnp.swapaxes(a.reshape(lead + (2, hp, 128)), -3, -2).reshape(lead + (2 * cfg.D,))


def _unpair(a, cfg):
    hp = cfg.D // 128
    lead = a.shape[:-1]
    return jnp.swapaxes(a.reshape(lead + (hp, 2, 128)), -3, -2).reshape(lead + (2 * cfg.D,))


def _prepare(full, cfg):
    d, di, cd, h = cfg.D, cfg.DI, cfg.CONV_DIM, cfg.H
    w_in = full["ssd_w_in"][0].astype(BF16)
    prm = jnp.stack([full["ssd_dt_bias"][0], full["ssd_a_log"][0], full["ssd_d_skip"][0]]).astype(F32)
    prm = jnp.pad(prm.reshape(3, SSM_GROUPS, cfg.HG).transpose(1, 0, 2), ((0, 0), (0, 5), (0, 0)))
    p = dict(
        meta=full["meta_tokens"].astype(F32),
        ssd_norm=full["ssd_norm"].astype(F32).reshape(1, d),
        w_in_main=w_in[:, :di + cd],
        w_in_dt=jnp.pad(w_in[:, di + cd:], ((0, 0), (0, 128 - h))),
        conv_w=full["ssd_conv_w"][0].astype(F32), conv_b=full["ssd_conv_b"].astype(F32).reshape(1, cd),
        prm=prm, gate=full["ssd_gate_norm"].astype(F32).reshape(1, di),
        w_out=full["ssd_w_out"][0].astype(BF16),
        kv_norm=full["kv_norm"].astype(F32).reshape(1, d), w_kv=_pair(full["w_kv"].astype(BF16), cfg),
        sb_norm=full["sb_norm"].astype(F32).reshape(1, d),
        w_q=full["sb_w_q"][0].astype(BF16), w_o=full["sb_w_o"][0].astype(BF16),
        final_norm=full["final_norm"].astype(F32).reshape(1, d),
    )
    for i in range(2):
        p[f"ffn_norm{i}"] = full["ffn_norm"][i].astype(F32).reshape(1, d)
        p[f"w_up{i}"] = _il(full["ffn_w_up"][i].astype(BF16), cfg)
        p[f"fcw{i}"] = _il(full["ffn_conv_w"][i].astype(F32), cfg)
        p[f"fcb{i}"] = _il(full["ffn_conv_b"][i].astype(F32), cfg).reshape(1, 2 * cfg.DFF)
        p[f"w_down{i}"] = full["ffn_w_down"][i].astype(BF16)
    return p


def _ffn_fwd(h, p, i, cfg):
    up, u = _norm_mm(h, p[f"ffn_norm{i}"], p[f"w_up{i}"], out_dtype=F32, name=f"ffn{i}_up")
    act = _ffn_conv_fwd(up, p[f"fcw{i}"], p[f"fcb{i}"], cfg=cfg, name=f"ffn{i}_conv")
    return _mm_res(act, p[f"w_down{i}"], h, pf=cfg.PF, name=f"ffn{i}_down"), (h, u, up, act)


def _ffn_bwd(dh, saved, p, i, cfg, grads):
    h, u, up, act = saved
    dact = _mm_nt(dh, p[f"w_down{i}"], out_dtype=F32, pf=cfg.PF, name=f"ffn{i}_dact")
    grads[f"w_down{i}"] = _mm_tn(act, dh, name=f"ffn{i}_dwdown")
    dhc, grads[f"fcw{i}"], grads[f"fcb{i}"] = _ffn_conv_bwd_pre(up, dact, p[f"fcw{i}"], p[f"fcb{i}"], cfg=cfg,
                                                                 name=f"ffn{i}_dconv")
    dup = _conv_bwd_input(dhc, p[f"fcw{i}"], width=FFN_CONV, name=f"ffn{i}_dup")
    grads[f"w_up{i}"] = _mm_tn(u, dup, name=f"ffn{i}_dwup")
    dh, grads[f"ffn_norm{i}"] = _mm_nt_normbwd(dup, p[f"w_up{i}"], h, p[f"ffn_norm{i}"], dh, pf=cfg.PF,
                                               name=f"ffn{i}_dh")
    return dh


def _local_step(x, target, p, cfg):
    d, di, cd, h_, lp, pf = cfg.D, cfg.DI, cfg.CONV_DIM, cfg.H, cfg.LP, cfg.PF
    g = {}
    h0 = jnp.concatenate([jnp.zeros((pf, d), F32), p["meta"], x], axis=0)
    zx, u1 = _norm_mm(h0, p["ssd_norm"], p["w_in_main"], out_dtype=F32, name="ssd_in")
    dt_raw, _ = _norm_mm(h0, p["ssd_norm"], p["w_in_dt"], out_dtype=F32, name="ssd_in_dt")
    dtr = dt_raw[:, :h_].reshape(lp, SSM_GROUPS, cfg.HG).transpose(1, 0, 2)
    xbc = _ssd_conv_fwd(zx, p["conv_w"], p["conv_b"], cfg=cfg, name="ssd_conv")
    y, hn, states = _ssd_fwd(xbc, zx, dtr, p["prm"], p["gate"], cfg=cfg, name="ssd_scan")
    h1 = _mm_res(hn, p["w_out"], h0, pf=pf, name="ssd_out")
    h2, ffn0 = _ffn_fwd(h1, p, 0, cfg)
    kv, ukv = _norm_mm(h2, p["kv_norm"], p["w_kv"], out_dtype=BF16, name="kv_proj")
    q, uq = _norm_mm(h2, p["sb_norm"], p["w_q"], out_dtype=BF16, name="q_proj")
    o = _attn_fwd(q, kv, cfg=cfg, name="attn_fwd")
    h3 = _mm_res(o, p["w_o"], h2, pf=pf, name="attn_out")
    h4, ffn1 = _ffn_fwd(h3, p, 1, cfg)
    loss8, dh, g["final_norm"] = _loss_head(h4, p["final_norm"], target, cfg=cfg, name="loss_head")
    dh = _ffn_bwd(dh, ffn1, p, 1, cfg, g)
    do = _mm_nt(dh, p["w_o"], out_dtype=BF16, pf=pf, name="attn_do")
    g["w_o"] = _mm_tn(o, dh, name="attn_dwo")
    dq, dkv = _attn_bwd(q, kv, o, do, cfg=cfg, name="attn_bwd")
    g["w_q"] = _mm_tn(uq, dq, name="attn_dwq")
    g["w_kv"] = _mm_tn(ukv, dkv, name="attn_dwkv")
    dh, g["sb_norm"] = _mm_nt_normbwd(dq, p["w_q"], h2, p["sb_norm"], dh, pf=pf, name="attn_dhq")
    dh, g["kv_norm"] = _mm_nt_normbwd(dkv, p["w_kv"], h2, p["kv_norm"], dh, pf=pf, name="attn_dhkv")
    dh = _ffn_bwd(dh, ffn0, p, 0, cfg, g)
    dhn = _mm_nt(dh, p["w_out"], out_dtype=F32, pf=pf, name="ssd_dhn")
    g["w_out"] = _mm_tn(hn, dh, name="ssd_dwout")
    dx, db, dc, dz, ddtr, dprm, dgate = _ssd_bwd(xbc, zx, dtr, p["prm"], p["gate"], y, states, dhn, cfg=cfg,
                                                 name="ssd_scan_bwd")
    dconv, g["conv_w"], g["conv_b"] = _ssd_conv_bwd_pre(zx, jnp.concatenate([dx, db, dc], axis=1), p["conv_w"],
                                                        p["conv_b"], cfg=cfg, name="ssd_dconv")
    dxpre = _conv_bwd_input(dconv, p["conv_w"], width=SSM_CONV, name="ssd_dxpre")
    dzx = jnp.concatenate([dz.astype(BF16), dxpre], axis=1)
    ddt = jnp.pad(ddtr.transpose(1, 0, 2).reshape(lp, h_), ((0, 0), (0, 128 - h_)))
    dw_main = _mm_tn(u1, dzx, name="ssd_dwin")
    dw_dt = _mm_tn(u1, ddt, name="ssd_dwin_dt")
    dh, gn1 = _mm_nt_normbwd(dzx, p["w_in_main"], h0, p["ssd_norm"], dh, pf=pf, name="ssd_dh")
    dh, gn2 = _mm_nt_normbwd(ddt, p["w_in_dt"], h0, p["ssd_norm"], dh, pf=pf, name="ssd_dh_dt")
    heads = lambda r: dprm[:, r, :].reshape(1, h_)
    out = dict(
        meta_tokens=dh[pf:pf + N_META], ssd_norm=gn1 + gn2,
        ssd_w_in=jnp.concatenate([dw_main, dw_dt[:, :h_]], axis=1)[None],
        ssd_conv_w=g["conv_w"][None], ssd_conv_b=g["conv_b"],
        ssd_dt_bias=heads(0), ssd_a_log=heads(1), ssd_d_skip=heads(2),
        ssd_gate_norm=dgate[:, 0, :].reshape(1, di), ssd_w_out=g["w_out"][None],
        kv_norm=g["kv_norm"].reshape(d), w_kv=_unpair(g["w_kv"], cfg),
        sb_norm=g["sb_norm"], sb_w_q=g["w_q"][None], sb_w_o=g["w_o"][None],
        ffn_norm=jnp.concatenate([g["ffn_norm0"], g["ffn_norm1"]], axis=0),
        ffn_w_up=jnp.stack([_unil(g["w_up0"], cfg), _unil(g["w_up1"], cfg)]),
        ffn_conv_w=jnp.stack([_unil(g["fcw0"], cfg), _unil(g["fcw1"], cfg)]),
        ffn_conv_b=jnp.concatenate([_unil(g["fcb0"], cfg), _unil(g["fcb1"], cfg)], axis=0),
        ffn_w_down=jnp.stack([g["w_down0"], g["w_down1"]]),
        final_norm=g["final_norm"].reshape(d),
    )
    return loss8[0, 0], dh[pf + N_META:], out


MESH_AXES = ("x", "y", "c")
HBM_SPEC = pl.BlockSpec(memory_space=pltpu.HBM)


def _all_gather(blk, *, name):
    r, w = blk.shape

    def body(x_ref, out_ref, send_sems, recv_sems, local_sem):
        x, y, c = lax.axis_index("x"), lax.axis_index("y"), lax.axis_index("c")
        me, sibling = (x, y, c), (x, y, 1 - c)
        chips = [(1 - x, y), (x, 1 - y), (1 - x, 1 - y)]

        def slot(px, py, pc):
            return out_ref.at[4 * px + 2 * py + pc]

        def copy(k, block, to, src=None):
            return pltpu.make_async_remote_copy(
                src_ref=slot(*block) if src is None else src, dst_ref=slot(*block),
                send_sem=send_sems.at[k], recv_sem=recv_sems.at[k], device_id=to, device_id_type=pl.DeviceIdType.MESH)

        mine = pltpu.make_async_copy(x_ref, slot(*me), local_sem)
        mine.start()
        first = [copy(0, me, sibling, src=x_ref)]
        first += [copy(1 + j, me, (*chip, c), src=x_ref) for j, chip in enumerate(chips)]
        for cp in first:
            cp.start()
        passed = [copy(4 + j, (*chip, c), sibling) for j, chip in enumerate(chips)]
        for j, chip in enumerate(chips):
            copy(1 + j, (*chip, c), me).wait_recv()
            passed[j].start()
        copy(0, sibling, me).wait_recv()
        for j, chip in enumerate(chips):
            copy(4 + j, (*chip, 1 - c), me).wait_recv()
        for cp in first + passed:
            cp.wait_send()
        mine.wait()

    return pl.pallas_call(
        body, name=name, out_shape=jax.ShapeDtypeStruct((N_DEV, r, w), blk.dtype),
        in_specs=[HBM_SPEC], out_specs=HBM_SPEC,
        scratch_shapes=[pltpu.SemaphoreType.DMA((7,)), pltpu.SemaphoreType.DMA((7,)), pltpu.SemaphoreType.DMA],
    )(blk)


def _all_to_all(buf, *, name):
    n, r, w = buf.shape

    def body(x_ref, out_ref, send_sems, recv_sems, local_sem):
        x, y, c = lax.axis_index("x"), lax.axis_index("y"), lax.axis_index("c")
        me_id = 4 * x + 2 * y + c
        mine = pltpu.make_async_copy(x_ref.at[me_id], out_ref.at[me_id], local_sem)
        mine.start()
        copies = []
        for k in range(1, N_DEV):
            fx, fy, fc = (k >> 2) & 1, (k >> 1) & 1, k & 1
            px = 1 - x if fx else x
            py = 1 - y if fy else y
            pc = 1 - c if fc else c
            copies.append(pltpu.make_async_remote_copy(
                src_ref=x_ref.at[4 * px + 2 * py + pc], dst_ref=out_ref.at[me_id],
                send_sem=send_sems.at[k - 1], recv_sem=recv_sems.at[k - 1],
                device_id=(px, py, pc), device_id_type=pl.DeviceIdType.MESH))
        for cp in copies:
            cp.start()
        for cp in copies:
            cp.wait()
        mine.wait()

    return pl.pallas_call(
        body, name=name, out_shape=jax.ShapeDtypeStruct((n, r, w), buf.dtype),
        in_specs=[HBM_SPEC], out_specs=HBM_SPEC,
        scratch_shapes=[pltpu.SemaphoreType.DMA((7,)), pltpu.SemaphoreType.DMA((7,)), pltpu.SemaphoreType.DMA],
    )(buf)


def _sum_slots(buf, *, name):
    n, r, w = buf.shape
    tr = _tile(r, 1024, 16)

    def body(x_ref, o_ref):
        acc = x_ref[0].astype(F32)
        for s in range(1, n):
            acc = acc + x_ref[s].astype(F32)
        o_ref[...] = acc

    return pl.pallas_call(
        body, name=name, grid=(r // tr,),
        in_specs=[pl.BlockSpec((n, tr, w), lambda i: (0, i, 0))], out_specs=pl.BlockSpec((tr, w), lambda i: (i, 0)),
        out_shape=jax.ShapeDtypeStruct((r, w), F32), compiler_params=_cparams(("arbitrary",)),
    )(buf)


def _adamw(w, g, m, v, *, name):
    r, lanes = w.shape
    tr = _tile(r, ADAM_ROWS, 8)

    def body(w_ref, g_ref, m_ref, v_ref, d_ref, nm_ref, nv_ref):
        gg = g_ref[...]
        nm = ADAM_B1 * m_ref[...] + (1.0 - ADAM_B1) * gg
        nv = ADAM_B2 * v_ref[...] + (1.0 - ADAM_B2) * (gg * gg)
        m_hat = nm / (1.0 - ADAM_B1 ** ADAM_STEP)
        v_hat = nv / (1.0 - ADAM_B2 ** ADAM_STEP)
        d_ref[...] = -ADAM_LR * (m_hat / (jnp.sqrt(v_hat) + ADAM_EPS) + ADAM_WD * w_ref[...])
        nm_ref[...] = nm
        nv_ref[...] = nv

    spec = pl.BlockSpec((tr, lanes), lambda i: (i, 0))
    shp = jax.ShapeDtypeStruct((r, lanes), F32)
    return pl.pallas_call(body, name=name, grid=(r // tr,), in_specs=[spec] * 4, out_specs=[spec] * 3,
                          out_shape=[shp] * 3, compiler_params=_cparams(("arbitrary",)))(w, g, m, v)


PACK_QUANTUM = 16 * 128
ADAM_ROWS = 2048
INPUT_NAMES = ("x", "meta_tokens", "ssd_norm", "ssd_w_in", "ssd_conv_w", "ssd_conv_b", "ssd_dt_bias", "ssd_a_log",
               "ssd_d_skip", "ssd_gate_norm", "ssd_w_out", "kv_norm", "w_kv", "sb_norm", "sb_w_q", "sb_w_o", "ffn_norm",
               "ffn_w_up", "ffn_conv_w", "ffn_conv_b", "ffn_w_down", "final_norm")
WEIGHT_NAMES = INPUT_NAMES[1:]
SHARD_AXIS = dict(meta_tokens=1, ssd_norm=1, ssd_w_in=2, ssd_conv_w=2, ssd_conv_b=1, ssd_gate_norm=1, ssd_w_out=1, w_kv=1,
                  sb_w_q=1, sb_w_o=1, ffn_w_up=2, ffn_conv_w=2, ffn_w_down=1)
SMALL_SHARDED = ("meta_tokens", "ssd_norm", "ssd_conv_w", "ssd_conv_b", "ssd_gate_norm", "ffn_conv_w")
SHARDED = MATMUL_WEIGHTS + SMALL_SHARDED
REPLICATED = tuple(n for n in WEIGHT_NAMES if n not in SHARD_AXIS)


def _rows(shape):
    n = math.prod(shape)
    return -(-n // PACK_QUANTUM) * (PACK_QUANTUM // 128)


def _pack(arrs, dtype, lead=0):
    parts = []
    for a in arrs:
        ld = a.shape[:lead]
        n = math.prod(a.shape[lead:])
        f = a.reshape(ld + (n,)).astype(dtype)
        pad = _rows(a.shape[lead:]) * 128 - n
        if pad:
            f = jnp.pad(f, [(0, 0)] * lead + [(0, pad)])
        parts.append(f.reshape(ld + (-1, 128)))
    return jnp.concatenate(parts, axis=lead)


def _unpack(flat, shapes, lead=0):
    out, r0 = [], 0
    ld = flat.shape[:lead]
    for shp in shapes:
        rows, n = _rows(shp), math.prod(shp)
        piece = lax.slice_in_dim(flat, r0, r0 + rows, axis=lead).reshape(ld + (rows * 128,))
        out.append(lax.slice_in_dim(piece, 0, n, axis=lead).reshape(ld + tuple(shp)))
        r0 += rows
    return out


def _unshard(stacked, axis):
    a = jnp.moveaxis(stacked, 0, axis)
    shp = a.shape
    return a.reshape(shp[:axis] + (shp[axis] * shp[axis + 1],) + shp[axis + 2:])


def _to_shards(full, axis):
    shp = full.shape
    a = full.reshape(shp[:axis] + (N_DEV, shp[axis] // N_DEV) + shp[axis + 1:])
    return jnp.moveaxis(a, axis, 0)


def kernel(x, meta_tokens, ssd_norm, ssd_w_in, ssd_conv_w, ssd_conv_b, ssd_dt_bias, ssd_a_log, ssd_d_skip, ssd_gate_norm, ssd_w_out, kv_norm, w_kv, sb_norm, sb_w_q, sb_w_o, ffn_norm, ffn_w_up, ffn_conv_w, ffn_conv_b, ffn_w_down, final_norm, loss_target, m_meta_tokens, m_ssd_norm, m_ssd_w_in, m_ssd_conv_w, m_ssd_conv_b, m_ssd_dt_bias, m_ssd_a_log, m_ssd_d_skip, m_ssd_gate_norm, m_ssd_w_out, m_kv_norm, m_w_kv, m_sb_norm, m_sb_w_q, m_sb_w_o, m_ffn_norm, m_ffn_w_up, m_ffn_conv_w, m_ffn_conv_b, m_ffn_w_down, m_final_norm, v_meta_tokens, v_ssd_norm, v_ssd_w_in, v_ssd_conv_w, v_ssd_conv_b, v_ssd_dt_bias, v_ssd_a_log, v_ssd_d_skip, v_ssd_gate_norm, v_ssd_w_out, v_kv_norm, v_w_kv, v_sb_norm, v_sb_w_q, v_sb_w_o, v_ffn_norm, v_ffn_w_up, v_ffn_conv_w, v_ffn_conv_b, v_ffn_w_down, v_final_norm):
    local = dict(zip(WEIGHT_NAMES, (meta_tokens, ssd_norm, ssd_w_in, ssd_conv_w, ssd_conv_b, ssd_dt_bias, ssd_a_log, ssd_d_skip, ssd_gate_norm, ssd_w_out, kv_norm, w_kv, sb_norm, sb_w_q, sb_w_o, ffn_norm, ffn_w_up, ffn_conv_w, ffn_conv_b, ffn_w_down, final_norm)))
    mom = dict(zip(WEIGHT_NAMES, (m_meta_tokens, m_ssd_norm, m_ssd_w_in, m_ssd_conv_w, m_ssd_conv_b, m_ssd_dt_bias, m_ssd_a_log, m_ssd_d_skip, m_ssd_gate_norm, m_ssd_w_out, m_kv_norm, m_w_kv, m_sb_norm, m_sb_w_q, m_sb_w_o, m_ffn_norm, m_ffn_w_up, m_ffn_conv_w, m_ffn_conv_b, m_ffn_w_down, m_final_norm)))
    var = dict(zip(WEIGHT_NAMES, (v_meta_tokens, v_ssd_norm, v_ssd_w_in, v_ssd_conv_w, v_ssd_conv_b, v_ssd_dt_bias, v_ssd_a_log, v_ssd_d_skip, v_ssd_gate_norm, v_ssd_w_out, v_kv_norm, v_w_kv, v_sb_norm, v_sb_w_q, v_sb_w_o, v_ffn_norm, v_ffn_w_up, v_ffn_conv_w, v_ffn_conv_b, v_ffn_w_down, v_final_norm)))
    seq, d = x.shape[1], x.shape[2]
    cfg = _make_cfg(d, seq, ffn_w_down.shape[1] * N_DEV)

    big = _all_gather(_pack([local[n] for n in MATMUL_WEIGHTS], BF16), name="gather_weights")
    small = _all_gather(_pack([local[n] for n in SMALL_SHARDED], F32), name="gather_small")
    full = {n: local[n] for n in REPLICATED}
    for names, buf in ((MATMUL_WEIGHTS, big), (SMALL_SHARDED, small)):
        for n, stacked in zip(names, _unpack(buf, [local[n].shape for n in names], lead=1)):
            full[n] = _unshard(stacked, SHARD_AXIS[n])

    loss, grad_x, grads = _local_step(x[0], loss_target[0], _prepare(full, cfg), cfg)

    to_peers = _pack([_to_shards(grads[n].reshape(full[n].shape), SHARD_AXIS[n]) for n in SHARDED], BF16, lead=1)
    g_sharded = _sum_slots(_all_to_all(to_peers, name="scatter_grads"), name="sum_grads")
    rep = _all_gather(_pack([grads[n].reshape(local[n].shape) for n in REPLICATED], F32), name="gather_rep_grads")
    g_rep = _sum_slots(rep, name="sum_rep_grads")
    fill = jnp.zeros(((-(g_sharded.shape[0] + g_rep.shape[0])) % ADAM_ROWS, 128), F32)
    g_flat = jnp.concatenate([g_sharded, g_rep, fill], axis=0)

    order = SHARDED + REPLICATED
    flat = lambda src: jnp.concatenate([_pack([src[n] for n in order], F32), fill], axis=0)
    delta, new_m, new_v = _adamw(flat(local), g_flat, flat(mom), flat(var), name="adamw")
    shapes = [local[n].shape for n in order]
    pick = lambda buf: dict(zip(order, _unpack(buf, shapes)))
    g_out, d_out, m_out, v_out = pick(g_flat), pick(delta), pick(new_m), pick(new_v)
    loss = lax.psum(loss, MESH_AXES)
    return (loss, grad_x[None], *[g_out[n] for n in WEIGHT_NAMES], *[d_out[n] for n in WEIGHT_NAMES],
            *[m_out[n] for n in WEIGHT_NAMES], *[v_out[n] for n in WEIGHT_NAMES])
```

```python
import collections
import functools
import math

import jax
import jax.numpy as jnp
from jax import lax
from jax.experimental import pallas as pl
from jax.experimental.pallas import tpu as pltpu

F32 = jnp.float32
BF16 = jnp.bfloat16
NORM_EPS = 1e-6
N_META = 16
SSM_HEAD_DIM = 64
SSM_GROUPS = 4
SSM_STATE = 128
SSM_CONV = 4
SSM_CHUNK = 256
SB_HEAD_DIM = 64
FFN_CONV = 3
FFN_IL = 256
ATT_T = 256
ADAM_LR, ADAM_B1, ADAM_B2, ADAM_EPS, ADAM_WD, ADAM_STEP = 0.001, 0.9, 0.999, 1e-08, 0.01, 10
N_DEV = 8
VMEM_LIMIT = 56 * 1024 * 1024
MM_TILE = 1536

Cfg = collections.namedtuple("Cfg", "D SEQ LP PF DI H HG GW CONV_DIM DFF HS")


def _make_cfg(d_model, seq, d_ff):
    pf = (-N_META) % SSM_CHUNK
    lp = pf + N_META + seq
    assert lp % SSM_CHUNK == 0 and (pf + N_META) == SSM_CHUNK
    di = 2 * d_model
    h = di // SSM_HEAD_DIM
    return Cfg(D=d_model, SEQ=seq, LP=lp, PF=pf, DI=di, H=h, HG=h // SSM_GROUPS, GW=di // SSM_GROUPS,
               CONV_DIM=di + 2 * SSM_GROUPS * SSM_STATE, DFF=d_ff, HS=d_model // SB_HEAD_DIM)


def _tile(n, pref, mult):
    t = (min(pref, n) // mult) * mult
    while t > mult and n % t:
        t -= mult
    assert t >= mult and n % t == 0, (n, pref, mult)
    return t


def _cparams(sem):
    return pltpu.CompilerParams(dimension_semantics=sem, vmem_limit_bytes=VMEM_LIMIT)


def _dot(a, b):
    return jnp.dot(a, b, preferred_element_type=F32)


def _dot_nt(a, b):
    return lax.dot_general(a, b, (((1,), (1,)), ((), ())), preferred_element_type=F32)


def _dot_tn(a, b):
    return lax.dot_general(a, b, (((0,), (0,)), ((), ())), preferred_element_type=F32)


def _split3(v):
    hi = v.astype(BF16)
    r1 = v - hi.astype(F32)
    mid = r1.astype(BF16)
    lo = (r1 - mid.astype(F32)).astype(BF16)
    return hi, mid, lo


def _split2(v):
    hi = v.astype(BF16)
    lo = (v - hi.astype(F32)).astype(BF16)
    return hi, lo


def _dot3(a_f32, b_bf16):
    hi, mid, lo = _split3(a_f32)
    return _dot(hi, b_bf16) + _dot(mid, b_bf16) + _dot(lo, b_bf16)


def _dot3_left(a_bf16, b_f32):
    hi, mid, lo = _split3(b_f32)
    return _dot(a_bf16, hi) + _dot(a_bf16, mid) + _dot(a_bf16, lo)


def _sigmoid(x):
    return 0.5 * jnp.tanh(0.5 * x) + 0.5


def _softplus(x):
    return jnp.maximum(x, 0.0) + jnp.log(1.0 + jnp.exp(-jnp.abs(x)))


def _row_ids(i, tm, shape_cols=1):
    return i * tm + lax.broadcasted_iota(jnp.int32, (tm, shape_cols), 0)


def _norm_mm(h, g, w, *, out_dtype, name, scale=None):
    lp, d = h.shape
    n = w.shape[1]
    tm = _tile(lp, 768, 8)
    tn = _tile(n, MM_TILE, 128)

    def body(h_ref, g_ref, w_ref, y_ref, u_ref):
        @pl.when(pl.program_id(1) == 0)
        def _():
            x = h_ref[...]
            r = lax.rsqrt(jnp.mean(x * x, axis=-1, keepdims=True) + NORM_EPS)
            u_ref[...] = (x * r * g_ref[...]).astype(BF16)
        y = _dot(u_ref[...], w_ref[...])
        if scale is not None:
            y = y * scale
        y_ref[...] = y.astype(out_dtype)

    return pl.pallas_call(
        body, name=name, grid=(lp // tm, n // tn),
        in_specs=[pl.BlockSpec((tm, d), lambda i, j: (i, 0)), pl.BlockSpec((1, d), lambda i, j: (0, 0)),
                  pl.BlockSpec((d, tn), lambda i, j: (0, j))],
        out_specs=[pl.BlockSpec((tm, tn), lambda i, j: (i, j)), pl.BlockSpec((tm, d), lambda i, j: (i, 0))],
        out_shape=[jax.ShapeDtypeStruct((lp, n), out_dtype), jax.ShapeDtypeStruct((lp, d), BF16)],
        compiler_params=_cparams(("arbitrary", "arbitrary")),
    )(h, g, w)


def _mm_res(a, w, h, *, pf, name):
    lp, k = a.shape
    d = w.shape[1]
    tm = _tile(lp, 768, 8)

    def body(a_ref, w_ref, h_ref, o_ref):
        y = _dot(a_ref[...].astype(BF16), w_ref[...])
        rows = _row_ids(pl.program_id(0), tm)
        o_ref[...] = h_ref[...] + jnp.where(rows >= pf, y, 0.0)

    return pl.pallas_call(
        body, name=name, grid=(lp // tm,),
        in_specs=[pl.BlockSpec((tm, k), lambda i: (i, 0)), pl.BlockSpec((k, d), lambda i: (0, 0)),
                  pl.BlockSpec((tm, d), lambda i: (i, 0))],
        out_specs=pl.BlockSpec((tm, d), lambda i: (i, 0)),
        out_shape=jax.ShapeDtypeStruct((lp, d), F32),
        compiler_params=_cparams(("arbitrary",)),
    )(a, w, h)


def _mm_nt(dy, w, *, out_dtype, pf, name):
    lp, n = dy.shape
    k = w.shape[0]
    tm = _tile(lp, 768, 8)
    tk = _tile(k, MM_TILE, 128)

    def body(dy_ref, w_ref, o_ref):
        y = _dot_nt(dy_ref[...].astype(BF16), w_ref[...])
        rows = _row_ids(pl.program_id(0), tm)
        o_ref[...] = jnp.where(rows >= pf, y, 0.0).astype(out_dtype)

    return pl.pallas_call(
        body, name=name, grid=(lp // tm, k // tk),
        in_specs=[pl.BlockSpec((tm, n), lambda i, j: (i, 0)), pl.BlockSpec((tk, n), lambda i, j: (j, 0))],
        out_specs=pl.BlockSpec((tm, tk), lambda i, j: (i, j)),
        out_shape=jax.ShapeDtypeStruct((lp, k), out_dtype),
        compiler_params=_cparams(("arbitrary", "arbitrary")),
    )(dy, w)


def _mm_nt_normbwd(dy, w, h, g, dh_in, *, pf, name, scale=None):
    lp, n = dy.shape
    d = w.shape[0]
    tm = _tile(lp, 768, 8)
    tn = _tile(n, MM_TILE, 128)
    nj = n // tn

    def body(dy_ref, w_ref, h_ref, g_ref, dhin_ref, dh_ref, dg_ref, acc_ref):
        i, j = pl.program_id(0), pl.program_id(1)

        @pl.when(j == 0)
        def _():
            acc_ref[...] = jnp.zeros_like(acc_ref)

        @pl.when((i == 0) & (j == 0))
        def _():
            dg_ref[...] = jnp.zeros_like(dg_ref)

        acc_ref[...] += _dot_nt(dy_ref[...].astype(BF16), w_ref[...])

        @pl.when(j == nj - 1)
        def _():
            du = acc_ref[...]
            if scale is not None:
                du = du * scale
            x = h_ref[...]
            r = lax.rsqrt(jnp.mean(x * x, axis=-1, keepdims=True) + NORM_EPS)
            xhat = x * r
            dg_ref[...] += jnp.sum(du * xhat, axis=0, keepdims=True)
            dxh = du * g_ref[...]
            dx = r * (dxh - xhat * jnp.mean(dxh * xhat, axis=-1, keepdims=True))
            rows = _row_ids(i, tm)
            dh_ref[...] = jnp.where(rows >= pf, dhin_ref[...] + dx, 0.0)

    return pl.pallas_call(
        body, name=name, grid=(lp // tm, nj),
        in_specs=[pl.BlockSpec((tm, tn), lambda i, j: (i, j)), pl.BlockSpec((d, tn), lambda i, j: (0, j)),
                  pl.BlockSpec((tm, d), lambda i, j: (i, 0)), pl.BlockSpec((1, d), lambda i, j: (0, 0)),
                  pl.BlockSpec((tm, d), lambda i, j: (i, 0))],
        out_specs=[pl.BlockSpec((tm, d), lambda i, j: (i, 0)), pl.BlockSpec((1, d), lambda i, j: (0, 0))],
        out_shape=[jax.ShapeDtypeStruct((lp, d), F32), jax.ShapeDtypeStruct((1, d), F32)],
        scratch_shapes=[pltpu.VMEM((tm, d), F32)],
        compiler_params=_cparams(("arbitrary", "arbitrary")),
    )(dy, w, h, g, dh_in)


def _mm_tn(a, b, *, name, scale=None):
    lp, k = a.shape
    n = b.shape[1]
    tm = _tile(lp, 768, 8)
    tk = _tile(k, MM_TILE, 128)
    tn = _tile(n, MM_TILE, 128)
    nm = lp // tm

    def body(a_ref, b_ref, o_ref, acc_ref):
        m = pl.program_id(2)

        @pl.when(m == 0)
        def _():
            acc_ref[...] = jnp.zeros_like(acc_ref)

        acc_ref[...] += _dot_tn(a_ref[...].astype(BF16), b_ref[...].astype(BF16))

        @pl.when(m == nm - 1)
        def _():
            o_ref[...] = acc_ref[...] if scale is None else acc_ref[...] * scale

    return pl.pallas_call(
        body, name=name, grid=(k // tk, n // tn, nm),
        in_specs=[pl.BlockSpec((tm, tk), lambda i, j, m: (m, i)), pl.BlockSpec((tm, tn), lambda i, j, m: (m, j))],
        out_specs=pl.BlockSpec((tk, tn), lambda i, j, m: (i, j)),
        out_shape=jax.ShapeDtypeStruct((k, n), F32),
        scratch_shapes=[pltpu.VMEM((tk, tn), F32)],
        compiler_params=_cparams(("arbitrary", "arbitrary", "arbitrary")),
    )(a, b)


HALO = 8


def _conv_taps(ext_ref, w_ref, tm, width, cols=slice(None)):
    return [ext_ref[pl.ds(HALO - (width - 1 - k), tm), cols] for k in range(width)]


def _causal_conv(ext_ref, x_ref, w_ref, b_ref, tm, width):
    @pl.when(pl.program_id(1) == 0)
    def _():
        ext_ref[0:HALO, :] = jnp.zeros((HALO, ext_ref.shape[1]), F32)

    ext_ref[HALO:, :] = x_ref[...]
    taps = _conv_taps(ext_ref, w_ref, tm, width)
    acc = b_ref[...] + taps[0] * w_ref[0:1, :]
    for k in range(1, width):
        acc = acc + taps[k] * w_ref[k:k + 1, :]
    return acc, taps


def _ssd_conv_fwd(zx, cw, cb, *, cfg, name):
    lp, cd, di = cfg.LP, cfg.CONV_DIM, cfg.DI
    tc = _tile(math.gcd(di, cd), 512, 128)
    tm = _tile(lp, 768, 8)
    off = di // tc

    def body(x_ref, w_ref, b_ref, o_ref, ext_ref):
        acc, _ = _causal_conv(ext_ref, x_ref, w_ref, b_ref, tm, SSM_CONV)
        rows = _row_ids(pl.program_id(1), tm)
        o_ref[...] = jnp.where(rows >= cfg.PF, acc * _sigmoid(acc), 0.0)
        ext_ref[0:HALO, :] = x_ref[tm - HALO:tm, :]

    return pl.pallas_call(
        body, name=name, grid=(cd // tc, lp // tm),
        in_specs=[pl.BlockSpec((tm, tc), lambda j, i: (i, j + off)), pl.BlockSpec((SSM_CONV, tc), lambda j, i: (0, j)),
                  pl.BlockSpec((1, tc), lambda j, i: (0, j))],
        out_specs=pl.BlockSpec((tm, tc), lambda j, i: (i, j)),
        out_shape=jax.ShapeDtypeStruct((lp, cd), F32),
        scratch_shapes=[pltpu.VMEM((tm + HALO, tc), F32)],
        compiler_params=_cparams(("arbitrary", "arbitrary")),
    )(zx, cw, cb)


def _ssd_conv_bwd_pre(zx, dxbc, cw, cb, *, cfg, name):
    lp, cd, di = cfg.LP, cfg.CONV_DIM, cfg.DI
    tc = _tile(math.gcd(di, cd), 512, 128)
    tm = _tile(lp, 768, 8)
    off = di // tc

    def body(x_ref, d_ref, w_ref, b_ref, dc_ref, dw_ref, db_ref, ext_ref):
        i = pl.program_id(1)
        c, taps = _causal_conv(ext_ref, x_ref, w_ref, b_ref, tm, SSM_CONV)
        sg = _sigmoid(c)
        rows = _row_ids(i, tm)
        dc = jnp.where(rows >= cfg.PF, d_ref[...] * sg * (1.0 + c * (1.0 - sg)), 0.0)
        dc_ref[...] = dc

        @pl.when(i == 0)
        def _():
            dw_ref[...] = jnp.zeros_like(dw_ref)
            db_ref[...] = jnp.zeros_like(db_ref)

        db_ref[...] += jnp.sum(dc, axis=0, keepdims=True)
        for k in range(SSM_CONV):
            dw_ref[k:k + 1, :] += jnp.sum(dc * taps[k], axis=0, keepdims=True)
        ext_ref[0:HALO, :] = x_ref[tm - HALO:tm, :]

    return pl.pallas_call(
        body, name=name, grid=(cd // tc, lp // tm),
        in_specs=[pl.BlockSpec((tm, tc), lambda j, i: (i, j + off)), pl.BlockSpec((tm, tc), lambda j, i: (i, j)),
                  pl.BlockSpec((SSM_CONV, tc), lambda j, i: (0, j)), pl.BlockSpec((1, tc), lambda j, i: (0, j))],
        out_specs=[pl.BlockSpec((tm, tc), lambda j, i: (i, j)), pl.BlockSpec((SSM_CONV, tc), lambda j, i: (0, j)),
                   pl.BlockSpec((1, tc), lambda j, i: (0, j))],
        out_shape=[jax.ShapeDtypeStruct((lp, cd), F32), jax.ShapeDtypeStruct((SSM_CONV, cd), F32),
                   jax.ShapeDtypeStruct((1, cd), F32)],
        scratch_shapes=[pltpu.VMEM((tm + HALO, tc), F32)],
        compiler_params=_cparams(("arbitrary", "arbitrary")),
    )(zx, dxbc, cw, cb)


def _ffn_conv_fwd(up, cw, cb, *, cfg, name):
    lp, dff = cfg.LP, cfg.DFF
    tc = 2 * FFN_IL
    tm = _tile(lp, 768, 8)

    def body(x_ref, w_ref, b_ref, o_ref, ext_ref):
        hc, _ = _causal_conv(ext_ref, x_ref, w_ref, b_ref, tm, FFN_CONV)
        gc, vc = hc[:, :FFN_IL], hc[:, FFN_IL:]
        rows = _row_ids(pl.program_id(1), tm)
        o_ref[...] = jnp.where(rows >= cfg.PF, gc * _sigmoid(gc) * vc, 0.0).astype(BF16)
        ext_ref[0:HALO, :] = x_ref[tm - HALO:tm, :]

    return pl.pallas_call(
        body, name=name, grid=(dff // FFN_IL, lp // tm),
        in_specs=[pl.BlockSpec((tm, tc), lambda j, i: (i, j)), pl.BlockSpec((FFN_CONV, tc), lambda j, i: (0, j)),
                  pl.BlockSpec((1, tc), lambda j, i: (0, j))],
        out_specs=pl.BlockSpec((tm, FFN_IL), lambda j, i: (i, j)),
        out_shape=jax.ShapeDtypeStruct((lp, dff), BF16),
        scratch_shapes=[pltpu.VMEM((tm + HALO, tc), F32)],
        compiler_params=_cparams(("arbitrary", "arbitrary")),
    )(up, cw, cb)


def _ffn_conv_bwd_pre(up, dact, cw, cb, *, cfg, name):
    lp, dff = cfg.LP, cfg.DFF
    tc = 2 * FFN_IL
    tm = _tile(lp, 768, 8)

    def body(x_ref, d_ref, w_ref, b_ref, dc_ref, dw_ref, db_ref, ext_ref):
        i = pl.program_id(1)
        hc, taps = _causal_conv(ext_ref, x_ref, w_ref, b_ref, tm, FFN_CONV)
        gc, vc = hc[:, :FFN_IL], hc[:, FFN_IL:]
        sg = _sigmoid(gc)
        rows = _row_ids(i, tm)
        da = jnp.where(rows >= cfg.PF, d_ref[...].astype(F32), 0.0)
        dg = da * vc * sg * (1.0 + gc * (1.0 - sg))
        dv = da * gc * sg
        dc_ref[:, :FFN_IL] = dg
        dc_ref[:, FFN_IL:] = dv

        @pl.when(i == 0)
        def _():
            dw_ref[...] = jnp.zeros_like(dw_ref)
            db_ref[...] = jnp.zeros_like(db_ref)

        db_ref[:, :FFN_IL] += jnp.sum(dg, axis=0, keepdims=True)
        db_ref[:, FFN_IL:] += jnp.sum(dv, axis=0, keepdims=True)
        for k in range(FFN_CONV):
            dw_ref[k:k + 1, :FFN_IL] += jnp.sum(dg * taps[k][:, :FFN_IL], axis=0, keepdims=True)
            dw_ref[k:k + 1, FFN_IL:] += jnp.sum(dv * taps[k][:, FFN_IL:], axis=0, keepdims=True)
        ext_ref[0:HALO, :] = x_ref[tm - HALO:tm, :]

    return pl.pallas_call(
        body, name=name, grid=(dff // FFN_IL, lp // tm),
        in_specs=[pl.BlockSpec((tm, tc), lambda j, i: (i, j)), pl.BlockSpec((tm, FFN_IL), lambda j, i: (i, j)),
                  pl.BlockSpec((FFN_CONV, tc), lambda j, i: (0, j)), pl.BlockSpec((1, tc), lambda j, i: (0, j))],
        out_specs=[pl.BlockSpec((tm, tc), lambda j, i: (i, j)), pl.BlockSpec((FFN_CONV, tc), lambda j, i: (0, j)),
                   pl.BlockSpec((1, tc), lambda j, i: (0, j))],
        out_shape=[jax.ShapeDtypeStruct((lp, 2 * dff), F32), jax.ShapeDtypeStruct((FFN_CONV, 2 * dff), F32),
                   jax.ShapeDtypeStruct((1, 2 * dff), F32)],
        scratch_shapes=[pltpu.VMEM((tm + HALO, tc), F32)],
        compiler_params=_cparams(("arbitrary", "arbitrary")),
    )(up, dact, cw, cb)


def _head_select(lane, head):
    return ((lane >= head * SSM_HEAD_DIM) & (lane < (head + 1) * SSM_HEAD_DIM)).astype(BF16)


def _ssd_terms(x, bm, cm, dtr, bias, alog, dsk, valid, cfg):
    q, hg, gw, p = SSM_CHUNK, cfg.HG, cfg.GW, SSM_HEAD_DIM
    t = {}
    e_mat = _head_select(lax.broadcasted_iota(jnp.int32, (hg, gw), 1), lax.broadcasted_iota(jnp.int32, (hg, gw), 0))
    li = lax.broadcasted_iota(jnp.int32, (q, q), 0)
    si = lax.broadcasted_iota(jnp.int32, (q, q), 1)
    tri = li >= si
    tril = tri.astype(BF16)
    triu = (li <= si).astype(BF16)
    pre = dtr + bias
    dtv = jnp.where(valid, _softplus(pre), 0.0)
    a_head = -jnp.exp(alog)
    a = dtv * a_head
    cs = _dot3_left(tril, a)
    hi, mid, lo = _split3(a)
    cst = _dot_tn(hi, triu) + _dot_tn(mid, triu) + _dot_tn(lo, triu)
    cs_last = jnp.sum(a, axis=0, keepdims=True)
    dte = jnp.exp(jnp.minimum(cs_last - cs, 0.0))
    ecs = jnp.exp(cs)
    t.update(e_mat=e_mat, tri=tri, tril=tril, triu=triu, pre=pre, dtv=dtv, a_head=a_head, cs=cs, cst=cst,
             dec=jnp.exp(cs_last), dte=dte, ecs=ecs)
    t["dtv_x"] = _dot3(dtv, e_mat)
    t["ecs_x"] = _dot3(ecs, e_mat)
    t["dte_x"] = _dot3(dte, e_mat)
    t["dec_x"] = _dot3(t["dec"], e_mat)
    t["d_x"] = _dot3(dsk, e_mat)
    t["xdt"] = x * t["dtv_x"]
    t["gm"] = _dot_nt(cm.astype(BF16), bm.astype(BF16))
    return t


def _ssd_decay(t, e):
    diff = t["cs"][:, e:e + 1] - t["cst"][e:e + 1, :]
    return jnp.where(t["tri"], jnp.exp(jnp.minimum(diff, 0.0)), 0.0)


def _head_mask(b):
    lane = lax.broadcasted_iota(jnp.int32, (1, 2 * SSM_HEAD_DIM), 1)
    return (lane >= SSM_HEAD_DIM * b) & (lane < SSM_HEAD_DIM * (b + 1))


def _ssd_specs(cfg):
    q, g, gw, ns, hg, di = SSM_CHUNK, SSM_GROUPS, cfg.GW, SSM_STATE, cfg.HG, cfg.DI
    return dict(
        x=pl.BlockSpec((q, gw), lambda j, c: (c, j)),
        b=pl.BlockSpec((q, ns), lambda j, c: (c, di // ns + j)),
        c=pl.BlockSpec((q, ns), lambda j, c: (c, di // ns + g + j)),
        z=pl.BlockSpec((q, gw), lambda j, c: (c, j)),
        dtr=pl.BlockSpec((None, q, hg), lambda j, c: (j, c, 0)),
        prm=pl.BlockSpec((None, 8, hg), lambda j, c: (j, 0, 0)),
        gate=pl.BlockSpec((1, gw), lambda j, c: (0, j)),
        st=pl.BlockSpec((None, None, ns, gw), lambda j, c: (c, j, 0, 0)),
    )


def _ssd_fwd(xbc, zx, dtr, prm, gate_g, *, cfg, name):
    q, g, gw, ns, hg, p = SSM_CHUNK, SSM_GROUPS, cfg.GW, SSM_STATE, cfg.HG, SSM_HEAD_DIM
    nc = cfg.LP // q
    sp = _ssd_specs(cfg)

    def body(x_ref, b_ref, c_ref, z_ref, dtr_ref, prm_ref, gg_ref, y_ref, hn_ref, st_ref, s_ref):
        c = pl.program_id(1)

        @pl.when(c == 0)
        def _():
            s_ref[...] = jnp.zeros((ns, gw), F32)

        s_prev = s_ref[...]
        st_ref[...] = s_prev
        x, bm, cm = x_ref[...], b_ref[...], c_ref[...]
        valid = _row_ids(c, q) >= cfg.PF
        t = _ssd_terms(x, bm, cm, dtr_ref[...], prm_ref[0:1, :], prm_ref[1:2, :], prm_ref[2:3, :], valid, cfg)
        parts = []
        for pr in range(hg // 2):
            xp = t["xdt"][:, 2 * p * pr:2 * p * (pr + 1)]
            acc = None
            for b in range(2):
                m = (t["gm"] * _ssd_decay(t, 2 * pr + b)).astype(BF16)
                d = _dot(m, jnp.where(_head_mask(b), xp, 0.0).astype(BF16))
                acc = d if acc is None else acc + d
            parts.append(acc)
        y_diag = jnp.concatenate(parts, axis=1)
        y_off = _dot(cm.astype(BF16), s_prev.astype(BF16)) * t["ecs_x"]
        y = y_diag + y_off + x * t["d_x"]
        y_ref[...] = y
        s_ref[...] = s_prev * t["dec_x"] + _dot_tn(bm.astype(BF16), (t["xdt"] * t["dte_x"]).astype(BF16))
        z = z_ref[...]
        hgate = y * z * _sigmoid(z)
        r = lax.rsqrt(jnp.mean(hgate * hgate, axis=-1, keepdims=True) + NORM_EPS)
        hn_ref[...] = (hgate * r * gg_ref[...]).astype(BF16)

    return pl.pallas_call(
        body, name=name, grid=(g, nc),
        in_specs=[sp["x"], sp["b"], sp["c"], sp["z"], sp["dtr"], sp["prm"], sp["gate"]],
        out_specs=[sp["x"], sp["x"], sp["st"]],
        out_shape=[jax.ShapeDtypeStruct((cfg.LP, cfg.DI), F32), jax.ShapeDtypeStruct((cfg.LP, cfg.DI), BF16),
                   jax.ShapeDtypeStruct((nc, g, ns, gw), F32)],
        scratch_shapes=[pltpu.VMEM((ns, gw), F32)],
        compiler_params=_cparams(("arbitrary", "arbitrary")),
    )(xbc, xbc, xbc, zx, dtr, prm, gate_g)


def _ssd_bwd(xbc, zx, dtr, prm, gate_g, y, states, dhn, *, cfg, name):
    q, g, gw, ns, hg, p = SSM_CHUNK, SSM_GROUPS, cfg.GW, SSM_STATE, cfg.HG, SSM_HEAD_DIM
    nc = cfg.LP // q
    sp = _ssd_specs(cfg)
    rev = lambda spec: pl.BlockSpec(spec.block_shape, (lambda f: (lambda j, c: f(j, nc - 1 - c)))(spec.index_map))
    bc_spec = pl.BlockSpec((q, ns), lambda j, c: (nc - 1 - c, j))

    def body(x_ref, b_ref, c_ref, z_ref, dtr_ref, prm_ref, gg_ref, y_ref, st_ref, dhn_ref,
             dx_ref, db_ref, dc_ref, dz_ref, ddtr_ref, dprm_ref, dgg_ref, ds_ref):
        ci = pl.program_id(1)
        c = nc - 1 - ci

        @pl.when(ci == 0)
        def _():
            ds_ref[...] = jnp.zeros((ns, gw), F32)
            dprm_ref[...] = jnp.zeros((8, hg), F32)
            dgg_ref[...] = jnp.zeros((8, gw), F32)

        ds_next = ds_ref[...]
        s_prev = st_ref[...]
        x, bm, cm = x_ref[...], b_ref[...], c_ref[...]
        valid = _row_ids(c, q) >= cfg.PF
        t = _ssd_terms(x, bm, cm, dtr_ref[...], prm_ref[0:1, :], prm_ref[1:2, :], prm_ref[2:3, :], valid, cfg)
        et_mat = _head_select(lax.broadcasted_iota(jnp.int32, (gw, hg), 0), lax.broadcasted_iota(jnp.int32, (gw, hg), 1))
        heads = lambda v: _dot3(v, et_mat)
        yv, z = y_ref[...], z_ref[...]
        sz = _sigmoid(z)
        silu = z * sz
        hgate = yv * silu
        r = lax.rsqrt(jnp.mean(hgate * hgate, axis=-1, keepdims=True) + NORM_EPS)
        hhat = hgate * r
        dhn = dhn_ref[...]
        dgg = jnp.sum(dhn * hhat, axis=0, keepdims=True)
        dhh = dhn * gg_ref[...]
        dhgate = r * (dhh - hhat * jnp.mean(dhh * hhat, axis=-1, keepdims=True))
        dy = dhgate * silu
        dz_ref[...] = dhgate * yv * sz * (1.0 + z * (1.0 - sz))
        bmb, cmb = bm.astype(BF16), cm.astype(BF16)
        xdt = t["xdt"]
        dye = dy * t["ecs_x"]
        dxdt_state = _dot(bmb, ds_next.astype(BF16)) * t["dte_x"]
        dg_acc = None
        dparts = []
        li = lax.broadcasted_iota(jnp.int32, (q, q), 0)
        si = lax.broadcasted_iota(jnp.int32, (q, q), 1)
        head_id = lax.broadcasted_iota(jnp.int32, (1, hg), 1)
        da = None
        for pr in range(hg // 2):
            sl = slice(2 * p * pr, 2 * p * (pr + 1))
            xp, dyp = xdt[:, sl], dy[:, sl]
            acc = None
            for b in range(2):
                lm = _ssd_decay(t, 2 * pr + b)
                m = (t["gm"] * lm).astype(BF16)
                dym = jnp.where(_head_mask(b), dyp, 0.0).astype(BF16)
                d = _dot_tn(m, dym)
                acc = d if acc is None else acc + d
                dm = _dot_nt(dym, xp.astype(BF16)) * lm
                dg_acc = dm if dg_acc is None else dg_acc + dm
                corner = _dot(t["triu"], (dm * t["gm"]).astype(BF16))
                da_e = jnp.sum(jnp.where(si < li, corner, 0.0), axis=1, keepdims=True)
                da_e = da_e * (head_id == 2 * pr + b).astype(F32)
                da = da_e if da is None else da + da_e
            dparts.append(acc)
        dxdt = jnp.concatenate(dparts, axis=1) + dxdt_state
        dgb = dg_acc.astype(BF16)
        dc_ref[...] = _dot(dgb, bmb) + _dot_nt(dye.astype(BF16), s_prev.astype(BF16))
        xde = (xdt * t["dte_x"]).astype(BF16)
        db_ref[...] = _dot_tn(dgb, cmb) + _dot_nt(xde, ds_next.astype(BF16))
        ds_ref[...] = ds_next * t["dec_x"] + _dot_tn(cmb, dye.astype(BF16))
        y_off = _dot(cmb, s_prev.astype(BF16)) * t["ecs_x"]
        tril_strict = (li > si).astype(BF16)
        da = da + _dot3_left(t["triu"], heads(dy * y_off)) + _dot3_left(tril_strict, heads(xdt * dxdt_state)) \
            + t["dec"] * heads(jnp.sum(ds_next * s_prev, axis=0, keepdims=True))
        ddtv = da * t["a_head"] + heads(dxdt * x)
        ddtr = jnp.where(valid, ddtv * _sigmoid(t["pre"]), 0.0)
        ddtr_ref[...] = ddtr
        dx_ref[...] = dxdt * t["dtv_x"] + dy * t["d_x"]
        dalog = jnp.sum(da * t["dtv"], axis=0, keepdims=True) * t["a_head"]
        dprm_ref[0:1, :] += jnp.sum(ddtr, axis=0, keepdims=True)
        dprm_ref[1:2, :] += dalog
        dprm_ref[2:3, :] += heads(jnp.sum(dy * x, axis=0, keepdims=True))
        dgg_ref[0:1, :] += dgg

    return pl.pallas_call(
        body, name=name, grid=(g, nc),
        in_specs=[rev(sp["x"]), rev(sp["b"]), rev(sp["c"]), rev(sp["z"]), rev(sp["dtr"]), sp["prm"], sp["gate"],
                  rev(sp["x"]), rev(sp["st"]), rev(sp["x"])],
        out_specs=[rev(sp["x"]), bc_spec, bc_spec, rev(sp["x"]), rev(sp["dtr"]),
                   pl.BlockSpec((None, 8, hg), lambda j, c: (j, 0, 0)), pl.BlockSpec((None, 8, gw), lambda j, c: (j, 0, 0))],
        out_shape=[jax.ShapeDtypeStruct((cfg.LP, cfg.DI), F32), jax.ShapeDtypeStruct((cfg.LP, g * ns), F32),
                   jax.ShapeDtypeStruct((cfg.LP, g * ns), F32), jax.ShapeDtypeStruct((cfg.LP, cfg.DI), F32),
                   jax.ShapeDtypeStruct((g, cfg.LP, hg), F32), jax.ShapeDtypeStruct((g, 8, hg), F32),
                   jax.ShapeDtypeStruct((g, 8, gw), F32)],
        scratch_shapes=[pltpu.VMEM((ns, gw), F32)],
        compiler_params=_cparams(("arbitrary", "arbitrary")),
    )(xbc, xbc, xbc, zx, dtr, prm, gate_g, y, states, dhn)


ATT_STRIP = 32
ATT_NT = 2
ATT_DEAD = -120.0
ATT_NEG = -1e9
LOG2E = 1.4426950408889634
LN2 = 0.6931471805599453


def _att_init(u_ref, bias_ref, qb, cfg):
    t = ATT_T
    rows = lax.broadcasted_iota(jnp.int32, (t, t), 0)
    cols = lax.broadcasted_iota(jnp.int32, (t, t), 1)
    u_ref[...] = (rows > cols).astype(BF16)
    pad = cols < cfg.PF
    bias_ref[0] = jnp.zeros((t, t), F32)
    bias_ref[1] = jnp.where((cols >= rows) | (pad & (qb == 0)), ATT_NEG, 0.0)
    bias_ref[2] = jnp.where(pad, ATT_NEG, 0.0)
    bias_ref[3] = jnp.full((t, t), ATT_NEG, F32)


def _att_tile(kv_ref, kb, qb):
    t = ATT_T
    off = pl.multiple_of(jnp.maximum(kb, 0) * t, t)
    k = kv_ref[pl.ds(off, t), 0:2 * SB_HEAD_DIM]
    v = kv_ref[pl.ds(off, t), 2 * SB_HEAD_DIM:]
    kind = jnp.where(kb < 0, 3, jnp.where(kb == qb, 1, jnp.where(kb == 0, 2, 0)))
    return k, v, kind


def _att_trips(qb):
    n = qb + ATT_NT
    return lax.shift_right_logical(n, 1) if ATT_NT == 2 else lax.div(n, ATT_NT)


def _att_walk(qb, trip, carry_ref):
    def cond(st):
        j, alive = st
        return (j < _att_trips(qb)) & alive

    def step(st):
        j, _ = st
        trip(j, None)
        return j + 1, jnp.max(carry_ref[...]) > ATT_DEAD

    lax.while_loop(cond, step, (0, True))


def _att_carry(carry_ref, rsum_ref, u, b):
    carry = carry_ref[b]
    for up in range(u):
        carry = carry + rsum_ref[2 * up + b]
    return carry


def _att_scores_strip(s_ref, bias_ref, kind, b, r0):
    sl = pl.ds(r0, ATT_STRIP)
    s = s_ref[b, sl, :] + bias_ref[kind, sl, :]
    l1p = jnp.log2(1.0 + jnp.exp2(jnp.abs(s) * (-LOG2E))) * LN2
    ls = jnp.minimum(s, 0.0) - l1p
    return ls, ls - s


def _attn_fwd(q, kv, *, cfg, name):
    t, lp, d, rs, nch = ATT_T, cfg.LP, cfg.D, ATT_STRIP, 2 * ATT_NT
    hp, nq = d // (2 * SB_HEAD_DIM), lp // ATT_T
    scale = SB_HEAD_DIM ** -0.5

    def body(q_ref, kv_ref, o_ref, u_ref, bias_ref, s_ref, hi_ref, lo_ref, cum_ref, w_ref, cb_ref, acc_ref, carry_ref,
             rsum_ref):
        qb = pl.program_id(1)
        _att_init(u_ref, bias_ref, qb, cfg)
        lo_half = lax.broadcasted_iota(jnp.int32, (1, 2 * SB_HEAD_DIM), 1) < SB_HEAD_DIM
        q2 = q_ref[...] * scale
        qms = [jnp.where(lo_half, q2, 0).astype(BF16), jnp.where(lo_half, 0, q2).astype(BF16)]
        acc_ref[...] = jnp.zeros_like(acc_ref)
        carry_ref[...] = jnp.zeros_like(carry_ref)

        def trip(j, _):
            tiles = [_att_tile(kv_ref, qb - ATT_NT * j - u, qb) for u in range(ATT_NT)]
            chains = [(u, b) for u in range(ATT_NT) for b in range(2)]
            for c, (u, b) in enumerate(chains):
                s_ref[c] = _dot_nt(qms[b], tiles[u][0])
            for c, (u, b) in enumerate(chains):
                for r0 in range(0, t, rs):
                    sl = pl.ds(r0, rs)
                    ls, lk = _att_scores_strip(s_ref, bias_ref, tiles[u][2], c, r0)
                    s_ref[c, sl, :] = ls
                    hi = lk.astype(BF16)
                    hi_ref[c, sl, :] = hi
                    lo_ref[c, sl, :] = (lk - hi.astype(F32)).astype(BF16)
                    rsum_ref[c, sl, :] = jnp.sum(lk, axis=1, keepdims=True)
            for c in range(len(chains)):
                cum_ref[c] = _dot(hi_ref[c], u_ref[...]) + _dot(lo_ref[c], u_ref[...])
            for c, (u, b) in enumerate(chains):
                cb_ref[c] = jnp.broadcast_to(_att_carry(carry_ref, rsum_ref, u, b), (t, t))
            for c, (u, b) in enumerate(chains):
                for r0 in range(0, t, rs):
                    sl = pl.ds(r0, rs)
                    x = s_ref[c, sl, :] + cum_ref[c, sl, :] + cb_ref[c, sl, :]
                    w_ref[c, sl, :] = jnp.exp2(x * LOG2E).astype(BF16)
            for c, (u, b) in enumerate(chains):
                acc_ref[b] += _dot(w_ref[c], tiles[u][1])
                carry_ref[b] += rsum_ref[c]

        _att_walk(qb, trip, carry_ref)
        o_ref[...] = jnp.where(lo_half, acc_ref[0], acc_ref[1])

    return pl.pallas_call(
        body, name=name, grid=(hp, nq),
        in_specs=[pl.BlockSpec((t, 128), lambda h, i: (i, h)), pl.BlockSpec((lp, 256), lambda h, i: (0, h))],
        out_specs=pl.BlockSpec((t, 128), lambda h, i: (i, h)),
        out_shape=jax.ShapeDtypeStruct((lp, d), F32),
        scratch_shapes=[pltpu.VMEM((t, t), BF16), pltpu.VMEM((4, t, t), F32), pltpu.VMEM((nch, t, t), F32),
                        pltpu.VMEM((nch, t, t), BF16), pltpu.VMEM((nch, t, t), BF16), pltpu.VMEM((nch, t, t), F32),
                        pltpu.VMEM((nch, t, t), BF16), pltpu.VMEM((nch, t, t), F32),
                        pltpu.VMEM((2, t, 2 * SB_HEAD_DIM), F32), pltpu.VMEM((2, t, 1), F32), pltpu.VMEM((nch, t, 1), F32)],
        compiler_params=_cparams(("arbitrary", "arbitrary")),
    )(q, kv)


def _attn_bwd(q, kv, o, do, *, cfg, name):
    t, lp, d, rs, nch = ATT_T, cfg.LP, cfg.D, ATT_STRIP, 2 * ATT_NT
    hp, nq = d // (2 * SB_HEAD_DIM), lp // ATT_T
    scale = SB_HEAD_DIM ** -0.5

    def body(q_ref, kv_ref, o_ref, do_ref, dq_ref, dkv_ref, u_ref, bias_ref, s_ref, dw_ref, hi_ref, lo_ref, cum_ref,
             w_ref, cb_ref, acc_ref, carry_ref, ecarry_ref, etot_ref, rsum_ref, ersum_ref):
        qb = pl.program_id(1)

        @pl.when(qb == 0)
        def _():
            dkv_ref[...] = jnp.zeros_like(dkv_ref)

        _att_init(u_ref, bias_ref, qb, cfg)
        lo_half = lax.broadcasted_iota(jnp.int32, (1, 2 * SB_HEAD_DIM), 1) < SB_HEAD_DIM
        q2 = q_ref[...] * scale
        do2 = do_ref[...]
        qms = [jnp.where(lo_half, q2, 0).astype(BF16), jnp.where(lo_half, 0, q2).astype(BF16)]
        doms = [jnp.where(lo_half, do2, 0).astype(BF16), jnp.where(lo_half, 0, do2).astype(BF16)]
        prod = do2.astype(F32) * o_ref[...]
        etot_ref[0] = jnp.sum(jnp.where(lo_half, prod, 0.0), axis=1, keepdims=True)
        etot_ref[1] = jnp.sum(jnp.where(lo_half, 0.0, prod), axis=1, keepdims=True)
        acc_ref[...] = jnp.zeros_like(acc_ref)
        carry_ref[...] = jnp.zeros_like(carry_ref)
        ecarry_ref[...] = jnp.zeros_like(ecarry_ref)

        def trip(j, _):
            kbs = [qb - ATT_NT * j - u for u in range(ATT_NT)]
            tiles = [_att_tile(kv_ref, kb, qb) for kb in kbs]
            chains = [(u, b) for u in range(ATT_NT) for b in range(2)]
            for c, (u, b) in enumerate(chains):
                s_ref[c] = _dot_nt(qms[b], tiles[u][0])
                dw_ref[c] = _dot_nt(doms[b], tiles[u][1])
            for c, (u, b) in enumerate(chains):
                for r0 in range(0, t, rs):
                    sl = pl.ds(r0, rs)
                    ls, lk = _att_scores_strip(s_ref, bias_ref, tiles[u][2], c, r0)
                    s_ref[c, sl, :] = ls
                    hi = lk.astype(BF16)
                    hi_ref[c, sl, :] = hi
                    lo_ref[c, sl, :] = (lk - hi.astype(F32)).astype(BF16)
                    rsum_ref[c, sl, :] = jnp.sum(lk, axis=1, keepdims=True)
            for c in range(nch):
                cum_ref[c] = _dot(hi_ref[c], u_ref[...]) + _dot(lo_ref[c], u_ref[...])
            for c, (u, b) in enumerate(chains):
                cb_ref[c] = jnp.broadcast_to(_att_carry(carry_ref, rsum_ref, u, b), (t, t))
            for c, (u, b) in enumerate(chains):
                for r0 in range(0, t, rs):
                    sl = pl.ds(r0, rs)
                    wb = jnp.exp2((s_ref[c, sl, :] + cum_ref[c, sl, :] + cb_ref[c, sl, :]) * LOG2E).astype(BF16)
                    w_ref[c, sl, :] = wb
                    e = wb.astype(F32) * dw_ref[c, sl, :]
                    dw_ref[c, sl, :] = e
                    hi = e.astype(BF16)
                    hi_ref[c, sl, :] = hi
                    lo_ref[c, sl, :] = (e - hi.astype(F32)).astype(BF16)
                    ersum_ref[c, sl, :] = jnp.sum(e, axis=1, keepdims=True)
            for c in range(nch):
                cum_ref[c] = _dot(hi_ref[c], u_ref[...]) + _dot(lo_ref[c], u_ref[...])
            for c, (u, b) in enumerate(chains):
                cb_ref[c] = jnp.broadcast_to(etot_ref[b] - _att_carry(ecarry_ref, ersum_ref, u, b), (t, t))
            for c, (u, b) in enumerate(chains):
                for r0 in range(0, t, rs):
                    sl = pl.ds(r0, rs)
                    e = dw_ref[c, sl, :]
                    e_before = cb_ref[c, sl, :] - (e + cum_ref[c, sl, :])
                    sig = jnp.exp2(s_ref[c, sl, :] * LOG2E)
                    hi_ref[c, sl, :] = (e * (1.0 - sig) - sig * e_before).astype(BF16)
            for u in range(ATT_NT):
                off = pl.multiple_of(jnp.maximum(kbs[u], 0) * t, t)
                c0, c1 = 2 * u, 2 * u + 1
                acc_ref[0] += _dot(hi_ref[c0], tiles[u][0])
                acc_ref[1] += _dot(hi_ref[c1], tiles[u][0])
                dkv_ref[pl.ds(off, t), 0:2 * SB_HEAD_DIM] += _dot_tn(hi_ref[c0], qms[0]) + _dot_tn(hi_ref[c1], qms[1])
                dkv_ref[pl.ds(off, t), 2 * SB_HEAD_DIM:] += _dot_tn(w_ref[c0], doms[0]) + _dot_tn(w_ref[c1], doms[1])
                for b in range(2):
                    carry_ref[b] += rsum_ref[2 * u + b]
                    ecarry_ref[b] += ersum_ref[2 * u + b]

        _att_walk(qb, trip, carry_ref)
        dq_ref[...] = jnp.where(lo_half, acc_ref[0], acc_ref[1]) * scale

    big = lambda dt: pltpu.VMEM((nch, t, t), dt)
    col = lambda n: pltpu.VMEM((n, t, 1), F32)
    return pl.pallas_call(
        body, name=name, grid=(hp, nq),
        in_specs=[pl.BlockSpec((t, 128), lambda h, i: (i, h)), pl.BlockSpec((lp, 256), lambda h, i: (0, h)),
                  pl.BlockSpec((t, 128), lambda h, i: (i, h)), pl.BlockSpec((t, 128), lambda h, i: (i, h))],
        out_specs=[pl.BlockSpec((t, 128), lambda h, i: (i, h)), pl.BlockSpec((lp, 256), lambda h, i: (0, h))],
        out_shape=[jax.ShapeDtypeStruct((lp, d), F32), jax.ShapeDtypeStruct((lp, 2 * d), F32)],
        scratch_shapes=[pltpu.VMEM((t, t), BF16), pltpu.VMEM((4, t, t), F32), big(F32), big(F32), big(BF16), big(BF16),
                        big(F32), big(BF16), big(F32), pltpu.VMEM((2, t, 2 * SB_HEAD_DIM), F32), col(2), col(2), col(2),
                        col(nch), col(nch)],
        compiler_params=_cparams(("arbitrary", "arbitrary")),
    )(q, kv, o, do)


def _loss_head(h, g, target, *, cfg, name):
    lp, d = h.shape
    tm = SSM_CHUNK
    first = (cfg.PF + N_META) // tm
    assert (cfg.PF + N_META) % tm == 0

    def body(h_ref, g_ref, t_ref, loss_ref, dh_ref, dg_ref):
        i = pl.program_id(0)

        @pl.when(i == 0)
        def _():
            loss_ref[...] = jnp.zeros_like(loss_ref)
            dg_ref[...] = jnp.zeros_like(dg_ref)

        x = h_ref[...]
        r = lax.rsqrt(jnp.mean(x * x, axis=-1, keepdims=True) + NORM_EPS)
        xhat = x * r
        live = i >= first
        diff = jnp.where(live, xhat * g_ref[...] - t_ref[...], 0.0)
        loss_ref[...] += 0.5 * jnp.sum(jnp.mean(diff * diff, axis=-1, keepdims=True))
        dy = diff * (1.0 / d)
        dg_ref[...] += jnp.sum(dy * xhat, axis=0, keepdims=True)
        dxh = dy * g_ref[...]
        dh_ref[...] = r * (dxh - xhat * jnp.mean(dxh * xhat, axis=-1, keepdims=True))

    return pl.pallas_call(
        body, name=name, grid=(lp // tm,),
        in_specs=[pl.BlockSpec((tm, d), lambda i: (i, 0)), pl.BlockSpec((1, d), lambda i: (0, 0)),
                  pl.BlockSpec((tm, d), lambda i: (jnp.maximum(i - first, 0), 0))],
        out_specs=[pl.BlockSpec((8, 128), lambda i: (0, 0)), pl.BlockSpec((tm, d), lambda i: (i, 0)),
                   pl.BlockSpec((1, d), lambda i: (0, 0))],
        out_shape=[jax.ShapeDtypeStruct((8, 128), F32), jax.ShapeDtypeStruct((lp, d), F32),
                   jax.ShapeDtypeStruct((1, d), F32)],
        compiler_params=_cparams(("arbitrary",)),
    )(h, g, target)


def _conv_bwd_input(dc, cw, *, width, name):
    lp, cd = dc.shape
    tc = _tile(cd, 512, 128)
    tm = _tile(lp, 768, 8)
    ni = lp // tm

    def body(d_ref, w_ref, o_ref, ext_ref):
        @pl.when(pl.program_id(1) == 0)
        def _():
            ext_ref[tm:, :] = jnp.zeros((HALO, tc), F32)

        ext_ref[0:tm, :] = d_ref[...]
        acc = ext_ref[pl.ds(width - 1, tm), :] * w_ref[0:1, :]
        for k in range(1, width):
            acc = acc + ext_ref[pl.ds(width - 1 - k, tm), :] * w_ref[k:k + 1, :]
        o_ref[...] = acc.astype(BF16)
        ext_ref[tm:, :] = d_ref[0:HALO, :]

    return pl.pallas_call(
        body, name=name, grid=(cd // tc, ni),
        in_specs=[pl.BlockSpec((tm, tc), lambda j, i: (ni - 1 - i, j)), pl.BlockSpec((width, tc), lambda j, i: (0, j))],
        out_specs=pl.BlockSpec((tm, tc), lambda j, i: (ni - 1 - i, j)),
        out_shape=jax.ShapeDtypeStruct((lp, cd), BF16),
        scratch_shapes=[pltpu.VMEM((tm + HALO, tc), F32)],
        compiler_params=_cparams(("arbitrary", "arbitrary")),
    )(dc, cw)


MATMUL_WEIGHTS = ("ssd_w_in", "ssd_w_out", "w_kv", "sb_w_q", "sb_w_o", "ffn_w_up", "ffn_w_down")


def _il(a, cfg):
    nb = cfg.DFF // FFN_IL
    lead = a.shape[:-1]
    a = a.reshape(lead + (2, nb, FFN_IL))
    return jnp.swapaxes(a, -3, -2).reshape(lead + (2 * cfg.DFF,))


def _unil(a, cfg):
    nb = cfg.DFF // FFN_IL
    lead = a.shape[:-1]
    a = a.reshape(lead + (nb, 2, FFN_IL))
    return jnp.swapaxes(a, -3, -2).reshape(lead + (2 * cfg.DFF,))


def _pair(a, cfg):
    hp = cfg.D // 128
    lead = a.shape[:-1]
    return jnp.swapaxes(a.reshape(lead + (2, hp, 128)), -3, -2).reshape(lead + (2 * cfg.D,))


def _unpair(a, cfg):
    hp = cfg.D // 128
    lead = a.shape[:-1]
    return jnp.swapaxes(a.reshape(lead + (hp, 2, 128)), -3, -2).reshape(lead + (2 * cfg.D,))


def _prepare(full, cfg):
    d, di, cd, h = cfg.D, cfg.DI, cfg.CONV_DIM, cfg.H
    w_in = full["ssd_w_in"][0].astype(BF16)
    prm = jnp.stack([full["ssd_dt_bias"][0], full["ssd_a_log"][0], full["ssd_d_skip"][0]]).astype(F32)
    prm = jnp.pad(prm.reshape(3, SSM_GROUPS, cfg.HG).transpose(1, 0, 2), ((0, 0), (0, 5), (0, 0)))
    p = dict(
        meta=full["meta_tokens"].astype(F32),
        ssd_norm=full["ssd_norm"].astype(F32).reshape(1, d),
        w_in_main=w_in[:, :di + cd],
        w_in_dt=jnp.pad(w_in[:, di + cd:], ((0, 0), (0, 128 - h))),
        conv_w=full["ssd_conv_w"][0].astype(F32), conv_b=full["ssd_conv_b"].astype(F32).reshape(1, cd),
        prm=prm, gate=full["ssd_gate_norm"].astype(F32).reshape(1, di),
        w_out=full["ssd_w_out"][0].astype(BF16),
        kv_norm=full["kv_norm"].astype(F32).reshape(1, d), w_kv=_pair(full["w_kv"].astype(BF16), cfg),
        sb_norm=full["sb_norm"].astype(F32).reshape(1, d),
        w_q=full["sb_w_q"][0].astype(BF16), w_o=full["sb_w_o"][0].astype(BF16),
        final_norm=full["final_norm"].astype(F32).reshape(1, d),
    )
    for i in range(2):
        p[f"ffn_norm{i}"] = full["ffn_norm"][i].astype(F32).reshape(1, d)
        p[f"w_up{i}"] = _il(full["ffn_w_up"][i].astype(BF16), cfg)
        p[f"fcw{i}"] = _il(full["ffn_conv_w"][i].astype(F32), cfg)
        p[f"fcb{i}"] = _il(full["ffn_conv_b"][i].astype(F32), cfg).reshape(1, 2 * cfg.DFF)
        p[f"w_down{i}"] = full["ffn_w_down"][i].astype(BF16)
    return p


def _ffn_fwd(h, p, i, cfg):
    up, u = _norm_mm(h, p[f"ffn_norm{i}"], p[f"w_up{i}"], out_dtype=F32, name=f"ffn{i}_up")
    act = _ffn_conv_fwd(up, p[f"fcw{i}"], p[f"fcb{i}"], cfg=cfg, name=f"ffn{i}_conv")
    return _mm_res(act, p[f"w_down{i}"], h, pf=cfg.PF, name=f"ffn{i}_down"), (h, u, up, act)


def _ffn_bwd(dh, saved, p, i, cfg, grads):
    h, u, up, act = saved
    dact = _mm_nt(dh, p[f"w_down{i}"], out_dtype=F32, pf=cfg.PF, name=f"ffn{i}_dact")
    grads[f"w_down{i}"] = _mm_tn(act, dh, name=f"ffn{i}_dwdown")
    dhc, grads[f"fcw{i}"], grads[f"fcb{i}"] = _ffn_conv_bwd_pre(up, dact, p[f"fcw{i}"], p[f"fcb{i}"], cfg=cfg,
                                                                 name=f"ffn{i}_dconv")
    dup = _conv_bwd_input(dhc, p[f"fcw{i}"], width=FFN_CONV, name=f"ffn{i}_dup")
    grads[f"w_up{i}"] = _mm_tn(u, dup, name=f"ffn{i}_dwup")
    dh, grads[f"ffn_norm{i}"] = _mm_nt_normbwd(dup, p[f"w_up{i}"], h, p[f"ffn_norm{i}"], dh, pf=cfg.PF,
                                               name=f"ffn{i}_dh")
    return dh


def _local_step(x, target, p, cfg):
    d, di, cd, h_, lp, pf = cfg.D, cfg.DI, cfg.CONV_DIM, cfg.H, cfg.LP, cfg.PF
    g = {}
    h0 = jnp.concatenate([jnp.zeros((pf, d), F32), p["meta"], x], axis=0)
    zx, u1 = _norm_mm(h0, p["ssd_norm"], p["w_in_main"], out_dtype=F32, name="ssd_in")
    dt_raw, _ = _norm_mm(h0, p["ssd_norm"], p["w_in_dt"], out_dtype=F32, name="ssd_in_dt")
    dtr = dt_raw[:, :h_].reshape(lp, SSM_GROUPS, cfg.HG).transpose(1, 0, 2)
    xbc = _ssd_conv_fwd(zx, p["conv_w"], p["conv_b"], cfg=cfg, name="ssd_conv")
    y, hn, states = _ssd_fwd(xbc, zx, dtr, p["prm"], p["gate"], cfg=cfg, name="ssd_scan")
    h1 = _mm_res(hn, p["w_out"], h0, pf=pf, name="ssd_out")
    h2, ffn0 = _ffn_fwd(h1, p, 0, cfg)
    kv, ukv = _norm_mm(h2, p["kv_norm"], p["w_kv"], out_dtype=BF16, name="kv_proj")
    q, uq = _norm_mm(h2, p["sb_norm"], p["w_q"], out_dtype=BF16, name="q_proj")
    o = _attn_fwd(q, kv, cfg=cfg, name="attn_fwd")
    h3 = _mm_res(o, p["w_o"], h2, pf=pf, name="attn_out")
    h4, ffn1 = _ffn_fwd(h3, p, 1, cfg)
    loss8, dh, g["final_norm"] = _loss_head(h4, p["final_norm"], target, cfg=cfg, name="loss_head")
    dh = _ffn_bwd(dh, ffn1, p, 1, cfg, g)
    do = _mm_nt(dh, p["w_o"], out_dtype=BF16, pf=pf, name="attn_do")
    g["w_o"] = _mm_tn(o, dh, name="attn_dwo")
    dq, dkv = _attn_bwd(q, kv, o, do, cfg=cfg, name="attn_bwd")
    g["w_q"] = _mm_tn(uq, dq, name="attn_dwq")
    g["w_kv"] = _mm_tn(ukv, dkv, name="attn_dwkv")
    dh, g["sb_norm"] = _mm_nt_normbwd(dq, p["w_q"], h2, p["sb_norm"], dh, pf=pf, name="attn_dhq")
    dh, g["kv_norm"] = _mm_nt_normbwd(dkv, p["w_kv"], h2, p["kv_norm"], dh, pf=pf, name="attn_dhkv")
    dh = _ffn_bwd(dh, ffn0, p, 0, cfg, g)
    dhn = _mm_nt(dh, p["w_out"], out_dtype=F32, pf=pf, name="ssd_dhn")
    g["w_out"] = _mm_tn(hn, dh, name="ssd_dwout")
    dx, db, dc, dz, ddtr, dprm, dgate = _ssd_bwd(xbc, zx, dtr, p["prm"], p["gate"], y, states, dhn, cfg=cfg,
                                                 name="ssd_scan_bwd")
    dconv, g["conv_w"], g["conv_b"] = _ssd_conv_bwd_pre(zx, jnp.concatenate([dx, db, dc], axis=1), p["conv_w"],
                                                        p["conv_b"], cfg=cfg, name="ssd_dconv")
    dxpre = _conv_bwd_input(dconv, p["conv_w"], width=SSM_CONV, name="ssd_dxpre")
    dzx = jnp.concatenate([dz.astype(BF16), dxpre], axis=1)
    ddt = jnp.pad(ddtr.transpose(1, 0, 2).reshape(lp, h_), ((0, 0), (0, 128 - h_)))
    dw_main = _mm_tn(u1, dzx, name="ssd_dwin")
    dw_dt = _mm_tn(u1, ddt, name="ssd_dwin_dt")
    dh, gn1 = _mm_nt_normbwd(dzx, p["w_in_main"], h0, p["ssd_norm"], dh, pf=pf, name="ssd_dh")
    dh, gn2 = _mm_nt_normbwd(ddt, p["w_in_dt"], h0, p["ssd_norm"], dh, pf=pf, name="ssd_dh_dt")
    heads = lambda r: dprm[:, r, :].reshape(1, h_)
    out = dict(
        meta_tokens=dh[pf:pf + N_META], ssd_norm=gn1 + gn2,
        ssd_w_in=jnp.concatenate([dw_main, dw_dt[:, :h_]], axis=1)[None],
        ssd_conv_w=g["conv_w"][None], ssd_conv_b=g["conv_b"],
        ssd_dt_bias=heads(0), ssd_a_log=heads(1), ssd_d_skip=heads(2),
        ssd_gate_norm=dgate[:, 0, :].reshape(1, di), ssd_w_out=g["w_out"][None],
        kv_norm=g["kv_norm"].reshape(d), w_kv=_unpair(g["w_kv"], cfg),
        sb_norm=g["sb_norm"], sb_w_q=g["w_q"][None], sb_w_o=g["w_o"][None],
        ffn_norm=jnp.concatenate([g["ffn_norm0"], g["ffn_norm1"]], axis=0),
        ffn_w_up=jnp.stack([_unil(g["w_up0"], cfg), _unil(g["w_up1"], cfg)]),
        ffn_conv_w=jnp.stack([_unil(g["fcw0"], cfg), _unil(g["fcw1"], cfg)]),
        ffn_conv_b=jnp.concatenate([_unil(g["fcb0"], cfg), _unil(g["fcb1"], cfg)], axis=0),
        ffn_w_down=jnp.stack([g["w_down0"], g["w_down1"]]),
        final_norm=g["final_norm"].reshape(d),
    )
    return loss8[0, 0], dh[pf + N_META:], out


MESH_AXES = ("x", "y", "c")
HBM_SPEC = pl.BlockSpec(memory_space=pltpu.HBM)


def _all_gather(blk, *, name):
    r, w = blk.shape

    def body(x_ref, out_ref, send_sems, recv_sems, local_sem):
        x, y, c = lax.axis_index("x"), lax.axis_index("y"), lax.axis_index("c")
        me, sibling = (x, y, c), (x, y, 1 - c)
        chips = [(1 - x, y), (x, 1 - y), (1 - x, 1 - y)]

        def slot(px, py, pc):
            return out_ref.at[4 * px + 2 * py + pc]

        def copy(k, block, to, src=None):
            return pltpu.make_async_remote_copy(
                src_ref=slot(*block) if src is None else src, dst_ref=slot(*block),
                send_sem=send_sems.at[k], recv_sem=recv_sems.at[k], device_id=to, device_id_type=pl.DeviceIdType.MESH)

        mine = pltpu.make_async_copy(x_ref, slot(*me), local_sem)
        mine.start()
        first = [copy(0, me, sibling, src=x_ref)]
        first += [copy(1 + j, me, (*chip, c), src=x_ref) for j, chip in enumerate(chips)]
        for cp in first:
            cp.start()
        passed = [copy(4 + j, (*chip, c), sibling) for j, chip in enumerate(chips)]
        for j, chip in enumerate(chips):
            copy(1 + j, (*chip, c), me).wait_recv()
            passed[j].start()
        copy(0, sibling, me).wait_recv()
        for j, chip in enumerate(chips):
            copy(4 + j, (*chip, 1 - c), me).wait_recv()
        for cp in first + passed:
            cp.wait_send()
        mine.wait()

    return pl.pallas_call(
        body, name=name, out_shape=jax.ShapeDtypeStruct((N_DEV, r, w), blk.dtype),
        in_specs=[HBM_SPEC], out_specs=HBM_SPEC,
        scratch_shapes=[pltpu.SemaphoreType.DMA((7,)), pltpu.SemaphoreType.DMA((7,)), pltpu.SemaphoreType.DMA],
    )(blk)


def _all_to_all(buf, *, name):
    n, r, w = buf.shape

    def body(x_ref, out_ref, send_sems, recv_sems, local_sem):
        x, y, c = lax.axis_index("x"), lax.axis_index("y"), lax.axis_index("c")
        me_id = 4 * x + 2 * y + c
        mine = pltpu.make_async_copy(x_ref.at[me_id], out_ref.at[me_id], local_sem)
        mine.start()
        copies = []
        for k in range(1, N_DEV):
            fx, fy, fc = (k >> 2) & 1, (k >> 1) & 1, k & 1
            px = 1 - x if fx else x
            py = 1 - y if fy else y
            pc = 1 - c if fc else c
            copies.append(pltpu.make_async_remote_copy(
                src_ref=x_ref.at[4 * px + 2 * py + pc], dst_ref=out_ref.at[me_id],
                send_sem=send_sems.at[k - 1], recv_sem=recv_sems.at[k - 1],
                device_id=(px, py, pc), device_id_type=pl.DeviceIdType.MESH))
        for cp in copies:
            cp.start()
        for cp in copies:
            cp.wait()
        mine.wait()

    return pl.pallas_call(
        body, name=name, out_shape=jax.ShapeDtypeStruct((n, r, w), buf.dtype),
        in_specs=[HBM_SPEC], out_specs=HBM_SPEC,
        scratch_shapes=[pltpu.SemaphoreType.DMA((7,)), pltpu.SemaphoreType.DMA((7,)), pltpu.SemaphoreType.DMA],
    )(buf)


def _sum_slots(buf, *, name):
    n, r, w = buf.shape
    tr = _tile(r, 1024, 16)

    def body(x_ref, o_ref):
        acc = x_ref[0].astype(F32)
        for s in range(1, n):
            acc = acc + x_ref[s].astype(F32)
        o_ref[...] = acc

    return pl.pallas_call(
        body, name=name, grid=(r // tr,),
        in_specs=[pl.BlockSpec((n, tr, w), lambda i: (0, i, 0))], out_specs=pl.BlockSpec((tr, w), lambda i: (i, 0)),
        out_shape=jax.ShapeDtypeStruct((r, w), F32), compiler_params=_cparams(("arbitrary",)),
    )(buf)


def _adamw(w, g, m, v, *, name):
    r, lanes = w.shape
    tr = _tile(r, ADAM_ROWS, 8)

    def body(w_ref, g_ref, m_ref, v_ref, d_ref, nm_ref, nv_ref):
        gg = g_ref[...]
        nm = ADAM_B1 * m_ref[...] + (1.0 - ADAM_B1) * gg
        nv = ADAM_B2 * v_ref[...] + (1.0 - ADAM_B2) * (gg * gg)
        m_hat = nm / (1.0 - ADAM_B1 ** ADAM_STEP)
        v_hat = nv / (1.0 - ADAM_B2 ** ADAM_STEP)
        d_ref[...] = -ADAM_LR * (m_hat / (jnp.sqrt(v_hat) + ADAM_EPS) + ADAM_WD * w_ref[...])
        nm_ref[...] = nm
        nv_ref[...] = nv

    spec = pl.BlockSpec((tr, lanes), lambda i: (i, 0))
    shp = jax.ShapeDtypeStruct((r, lanes), F32)
    return pl.pallas_call(body, name=name, grid=(r // tr,), in_specs=[spec] * 4, out_specs=[spec] * 3,
                          out_shape=[shp] * 3, compiler_params=_cparams(("arbitrary",)))(w, g, m, v)


PACK_QUANTUM = 16 * 128
ADAM_ROWS = 2048
INPUT_NAMES = ("x", "meta_tokens", "ssd_norm", "ssd_w_in", "ssd_conv_w", "ssd_conv_b", "ssd_dt_bias", "ssd_a_log",
               "ssd_d_skip", "ssd_gate_norm", "ssd_w_out", "kv_norm", "w_kv", "sb_norm", "sb_w_q", "sb_w_o", "ffn_norm",
               "ffn_w_up", "ffn_conv_w", "ffn_conv_b", "ffn_w_down", "final_norm")
WEIGHT_NAMES = INPUT_NAMES[1:]
SHARD_AXIS = dict(meta_tokens=1, ssd_norm=1, ssd_w_in=2, ssd_conv_w=2, ssd_conv_b=1, ssd_gate_norm=1, ssd_w_out=1, w_kv=1,
                  sb_w_q=1, sb_w_o=1, ffn_w_up=2, ffn_conv_w=2, ffn_w_down=1)
SMALL_SHARDED = ("meta_tokens", "ssd_norm", "ssd_conv_w", "ssd_conv_b", "ssd_gate_norm", "ffn_conv_w")
SHARDED = MATMUL_WEIGHTS + SMALL_SHARDED
REPLICATED = tuple(n for n in WEIGHT_NAMES if n not in SHARD_AXIS)


def _rows(shape):
    n = math.prod(shape)
    return -(-n // PACK_QUANTUM) * (PACK_QUANTUM // 128)


def _pack(arrs, dtype, lead=0):
    parts = []
    for a in arrs:
        ld = a.shape[:lead]
        n = math.prod(a.shape[lead:])
        f = a.reshape(ld + (n,)).astype(dtype)
        pad = _rows(a.shape[lead:]) * 128 - n
        if pad:
            f = jnp.pad(f, [(0, 0)] * lead + [(0, pad)])
        parts.append(f.reshape(ld + (-1, 128)))
    return jnp.concatenate(parts, axis=lead)


def _unpack(flat, shapes, lead=0):
    out, r0 = [], 0
    ld = flat.shape[:lead]
    for shp in shapes:
        rows, n = _rows(shp), math.prod(shp)
        piece = lax.slice_in_dim(flat, r0, r0 + rows, axis=lead).reshape(ld + (rows * 128,))
        out.append(lax.slice_in_dim(piece, 0, n, axis=lead).reshape(ld + tuple(shp)))
        r0 += rows
    return out


def _unshard(stacked, axis):
    a = jnp.moveaxis(stacked, 0, axis)
    shp = a.shape
    return a.reshape(shp[:axis] + (shp[axis] * shp[axis + 1],) + shp[axis + 2:])


def _to_shards(full, axis):
    shp = full.shape
    a = full.reshape(shp[:axis] + (N_DEV, shp[axis] // N_DEV) + shp[axis + 1:])
    return jnp.moveaxis(a, axis, 0)


def kernel(x, meta_tokens, ssd_norm, ssd_w_in, ssd_conv_w, ssd_conv_b, ssd_dt_bias, ssd_a_log, ssd_d_skip, ssd_gate_norm, ssd_w_out, kv_norm, w_kv, sb_norm, sb_w_q, sb_w_o, ffn_norm, ffn_w_up, ffn_conv_w, ffn_conv_b, ffn_w_down, final_norm, loss_target, m_meta_tokens, m_ssd_norm, m_ssd_w_in, m_ssd_conv_w, m_ssd_conv_b, m_ssd_dt_bias, m_ssd_a_log, m_ssd_d_skip, m_ssd_gate_norm, m_ssd_w_out, m_kv_norm, m_w_kv, m_sb_norm, m_sb_w_q, m_sb_w_o, m_ffn_norm, m_ffn_w_up, m_ffn_conv_w, m_ffn_conv_b, m_ffn_w_down, m_final_norm, v_meta_tokens, v_ssd_norm, v_ssd_w_in, v_ssd_conv_w, v_ssd_conv_b, v_ssd_dt_bias, v_ssd_a_log, v_ssd_d_skip, v_ssd_gate_norm, v_ssd_w_out, v_kv_norm, v_w_kv, v_sb_norm, v_sb_w_q, v_sb_w_o, v_ffn_norm, v_ffn_w_up, v_ffn_conv_w, v_ffn_conv_b, v_ffn_w_down, v_final_norm):
    local = dict(zip(WEIGHT_NAMES, (meta_tokens, ssd_norm, ssd_w_in, ssd_conv_w, ssd_conv_b, ssd_dt_bias, ssd_a_log, ssd_d_skip, ssd_gate_norm, ssd_w_out, kv_norm, w_kv, sb_norm, sb_w_q, sb_w_o, ffn_norm, ffn_w_up, ffn_conv_w, ffn_conv_b, ffn_w_down, final_norm)))
    mom = dict(zip(WEIGHT_NAMES, (m_meta_tokens, m_ssd_norm, m_ssd_w_in, m_ssd_conv_w, m_ssd_conv_b, m_ssd_dt_bias, m_ssd_a_log, m_ssd_d_skip, m_ssd_gate_norm, m_ssd_w_out, m_kv_norm, m_w_kv, m_sb_norm, m_sb_w_q, m_sb_w_o, m_ffn_norm, m_ffn_w_up, m_ffn_conv_w, m_ffn_conv_b, m_ffn_w_down, m_final_norm)))
    var = dict(zip(WEIGHT_NAMES, (v_meta_tokens, v_ssd_norm, v_ssd_w_in, v_ssd_conv_w, v_ssd_conv_b, v_ssd_dt_bias, v_ssd_a_log, v_ssd_d_skip, v_ssd_gate_norm, v_ssd_w_out, v_kv_norm, v_w_kv, v_sb_norm, v_sb_w_q, v_sb_w_o, v_ffn_norm, v_ffn_w_up, v_ffn_conv_w, v_ffn_conv_b, v_ffn_w_down, v_final_norm)))
    seq, d = x.shape[1], x.shape[2]
    cfg = _make_cfg(d, seq, ffn_w_down.shape[1] * N_DEV)

    big = _all_gather(_pack([local[n] for n in MATMUL_WEIGHTS], BF16), name="gather_weights")
    small = _all_gather(_pack([local[n] for n in SMALL_SHARDED], F32), name="gather_small")
    full = {n: local[n] for n in REPLICATED}
    for names, buf in ((MATMUL_WEIGHTS, big), (SMALL_SHARDED, small)):
        for n, stacked in zip(names, _unpack(buf, [local[n].shape for n in names], lead=1)):
            full[n] = _unshard(stacked, SHARD_AXIS[n])

    loss, grad_x, grads = _local_step(x[0], loss_target[0], _prepare(full, cfg), cfg)

    to_peers = _pack([_to_shards(grads[n].reshape(full[n].shape), SHARD_AXIS[n]) for n in SHARDED], BF16, lead=1)
    g_sharded = _sum_slots(_all_to_all(to_peers, name="scatter_grads"), name="sum_grads")
    rep = _all_gather(_pack([grads[n].reshape(local[n].shape) for n in REPLICATED], F32), name="gather_rep_grads")
    g_rep = _sum_slots(rep, name="sum_rep_grads")
    fill = jnp.zeros(((-(g_sharded.shape[0] + g_rep.shape[0])) % ADAM_ROWS, 128), F32)
    g_flat = jnp.concatenate([g_sharded, g_rep, fill], axis=0)

    order = SHARDED + REPLICATED
    flat = lambda src: jnp.concatenate([_pack([src[n] for n in order], F32), fill], axis=0)
    delta, new_m, new_v = _adamw(flat(local), g_flat, flat(mom), flat(var), name="adamw")
    shapes = [local[n].shape for n in order]
    pick = lambda buf: dict(zip(order, _unpack(buf, shapes)))
    g_out, d_out, m_out, v_out = pick(g_flat), pick(delta), pick(new_m), pick(new_v)
    loss = lax.psum(loss, MESH_AXES)
    return (loss, grad_x[None], *[g_out[n] for n in WEIGHT_NAMES], *[d_out[n] for n in WEIGHT_NAMES],
            *[m_out[n] for n in WEIGHT_NAMES], *[v_out[n] for n in WEIGHT_NAMES])
```

```python
import collections
import functools
import math

import jax
import jax.numpy as jnp
from jax import lax
from jax.experimental import pallas as pl
from jax.experimental.pallas import tpu as pltpu

F32 = jnp.float32
BF16 = jnp.bfloat16
NORM_EPS = 1e-6
N_META = 16
SSM_HEAD_DIM = 64
SSM_GROUPS = 4
SSM_STATE = 128
SSM_CONV = 4
SSM_CHUNK = 256
SB_HEAD_DIM = 64
FFN_CONV = 3
FFN_IL = 256
ATT_T = 256
ADAM_LR, ADAM_B1, ADAM_B2, ADAM_EPS, ADAM_WD, ADAM_STEP = 0.001, 0.9, 0.999, 1e-08, 0.01, 10
N_DEV = 8
VMEM_LIMIT = 56 * 1024 * 1024
MM_TILE = 1536

Cfg = collections.namedtuple("Cfg", "D SEQ LP PF DI H HG GW CONV_DIM DFF HS")


def _make_cfg(d_model, seq, d_ff):
    pf = (-N_META) % SSM_CHUNK
    lp = pf + N_META + seq
    assert lp % SSM_CHUNK == 0 and (pf + N_META) == SSM_CHUNK
    di = 2 * d_model
    h = di // SSM_HEAD_DIM
    return Cfg(D=d_model, SEQ=seq, LP=lp, PF=pf, DI=di, H=h, HG=h // SSM_GROUPS, GW=di // SSM_GROUPS,
               CONV_DIM=di + 2 * SSM_GROUPS * SSM_STATE, DFF=d_ff, HS=d_model // SB_HEAD_DIM)


def _tile(n, pref, mult):
    t = (min(pref, n) // mult) * mult
    while t > mult and n % t:
        t -= mult
    assert t >= mult and n % t == 0, (n, pref, mult)
    return t


def _cparams(sem):
    return pltpu.CompilerParams(dimension_semantics=sem, vmem_limit_bytes=VMEM_LIMIT)


def _dot(a, b):
    return jnp.dot(a, b, preferred_element_type=F32)


def _dot_nt(a, b):
    return lax.dot_general(a, b, (((1,), (1,)), ((), ())), preferred_element_type=F32)


def _dot_tn(a, b):
    return lax.dot_general(a, b, (((0,), (0,)), ((), ())), preferred_element_type=F32)


def _split3(v):
    hi = v.astype(BF16)
    r1 = v - hi.astype(F32)
    mid = r1.astype(BF16)
    lo = (r1 - mid.astype(F32)).astype(BF16)
    return hi, mid, lo


def _split2(v):
    hi = v.astype(BF16)
    lo = (v - hi.astype(F32)).astype(BF16)
    return hi, lo


def _dot3(a_f32, b_bf16):
    hi, mid, lo = _split3(a_f32)
    return _dot(hi, b_bf16) + _dot(mid, b_bf16) + _dot(lo, b_bf16)


def _dot3_left(a_bf16, b_f32):
    hi, mid, lo = _split3(b_f32)
    return _dot(a_bf16, hi) + _dot(a_bf16, mid) + _dot(a_bf16, lo)


def _sigmoid(x):
    return 0.5 * jnp.tanh(0.5 * x) + 0.5


def _softplus(x):
    return jnp.maximum(x, 0.0) + jnp.log(1.0 + jnp.exp(-jnp.abs(x)))


def _row_ids(i, tm, shape_cols=1):
    return i * tm + lax.broadcasted_iota(jnp.int32, (tm, shape_cols), 0)


def _norm_mm(h, g, w, *, out_dtype, name, scale=None):
    lp, d = h.shape
    n = w.shape[1]
    tm = _tile(lp, 768, 8)
    tn = _tile(n, MM_TILE, 128)

    def body(h_ref, g_ref, w_ref, y_ref, u_ref):
        @pl.when(pl.program_id(1) == 0)
        def _():
            x = h_ref[...]
            r = lax.rsqrt(jnp.mean(x * x, axis=-1, keepdims=True) + NORM_EPS)
            u_ref[...] = (x * r * g_ref[...]).astype(BF16)
        y = _dot(u_ref[...], w_ref[...])
        if scale is not None:
            y = y * scale
        y_ref[...] = y.astype(out_dtype)

    return pl.pallas_call(
        body, name=name, grid=(lp // tm, n // tn),
        in_specs=[pl.BlockSpec((tm, d), lambda i, j: (i, 0)), pl.BlockSpec((1, d), lambda i, j: (0, 0)),
                  pl.BlockSpec((d, tn), lambda i, j: (0, j))],
        out_specs=[pl.BlockSpec((tm, tn), lambda i, j: (i, j)), pl.BlockSpec((tm, d), lambda i, j: (i, 0))],
        out_shape=[jax.ShapeDtypeStruct((lp, n), out_dtype), jax.ShapeDtypeStruct((lp, d), BF16)],
        compiler_params=_cparams(("arbitrary", "arbitrary")),
    )(h, g, w)


def _mm_res(a, w, h, *, pf, name):
    lp, k = a.shape
    d = w.shape[1]
    tm = _tile(lp, 768, 8)

    def body(a_ref, w_ref, h_ref, o_ref):
        y = _dot(a_ref[...].astype(BF16), w_ref[...])
        rows = _row_ids(pl.program_id(0), tm)
        o_ref[...] = h_ref[...] + jnp.where(rows >= pf, y, 0.0)

    return pl.pallas_call(
        body, name=name, grid=(lp // tm,),
        in_specs=[pl.BlockSpec((tm, k), lambda i: (i, 0)), pl.BlockSpec((k, d), lambda i: (0, 0)),
                  pl.BlockSpec((tm, d), lambda i: (i, 0))],
        out_specs=pl.BlockSpec((tm, d), lambda i: (i, 0)),
        out_shape=jax.ShapeDtypeStruct((lp, d), F32),
        compiler_params=_cparams(("arbitrary",)),
    )(a, w, h)


def _mm_nt(dy, w, *, out_dtype, pf, name):
    lp, n = dy.shape
    k = w.shape[0]
    tm = _tile(lp, 768, 8)
    tk = _tile(k, MM_TILE, 128)

    def body(dy_ref, w_ref, o_ref):
        y = _dot_nt(dy_ref[...].astype(BF16), w_ref[...])
        rows = _row_ids(pl.program_id(0), tm)
        o_ref[...] = jnp.where(rows >= pf, y, 0.0).astype(out_dtype)

    return pl.pallas_call(
        body, name=name, grid=(lp // tm, k // tk),
        in_specs=[pl.BlockSpec((tm, n), lambda i, j: (i, 0)), pl.BlockSpec((tk, n), lambda i, j: (j, 0))],
        out_specs=pl.BlockSpec((tm, tk), lambda i, j: (i, j)),
        out_shape=jax.ShapeDtypeStruct((lp, k), out_dtype),
        compiler_params=_cparams(("arbitrary", "arbitrary")),
    )(dy, w)


def _mm_nt_normbwd(dy, w, h, g, dh_in, *, pf, name, scale=None):
    lp, n = dy.shape
    d = w.shape[0]
    tm = _tile(lp, 768, 8)
    tn = _tile(n, MM_TILE, 128)
    nj = n // tn

    def body(dy_ref, w_ref, h_ref, g_ref, dhin_ref, dh_ref, dg_ref, acc_ref):
        i, j = pl.program_id(0), pl.program_id(1)

        @pl.when(j == 0)
        def _():
            acc_ref[...] = jnp.zeros_like(acc_ref)

        @pl.when((i == 0) & (j == 0))
        def _():
            dg_ref[...] = jnp.zeros_like(dg_ref)

        acc_ref[...] += _dot_nt(dy_ref[...].astype(BF16), w_ref[...])

        @pl.when(j == nj - 1)
        def _():
            du = acc_ref[...]
            if scale is not None:
                du = du * scale
            x = h_ref[...]
            r = lax.rsqrt(jnp.mean(x * x, axis=-1, keepdims=True) + NORM_EPS)
            xhat = x * r
            dg_ref[...] += jnp.sum(du * xhat, axis=0, keepdims=True)
            dxh = du * g_ref[...]
            dx = r * (dxh - xhat * jnp.mean(dxh * xhat, axis=-1, keepdims=True))
            rows = _row_ids(i, tm)
            dh_ref[...] = jnp.where(rows >= pf, dhin_ref[...] + dx, 0.0)

    return pl.pallas_call(
        body, name=name, grid=(lp // tm, nj),
        in_specs=[pl.BlockSpec((tm, tn), lambda i, j: (i, j)), pl.BlockSpec((d, tn), lambda i, j: (0, j)),
                  pl.BlockSpec((tm, d), lambda i, j: (i, 0)), pl.BlockSpec((1, d), lambda i, j: (0, 0)),
                  pl.BlockSpec((tm, d), lambda i, j: (i, 0))],
        out_specs=[pl.BlockSpec((tm, d), lambda i, j: (i, 0)), pl.BlockSpec((1, d), lambda i, j: (0, 0))],
        out_shape=[jax.ShapeDtypeStruct((lp, d), F32), jax.ShapeDtypeStruct((1, d), F32)],
        scratch_shapes=[pltpu.VMEM((tm, d), F32)],
        compiler_params=_cparams(("arbitrary", "arbitrary")),
    )(dy, w, h, g, dh_in)


def _mm_tn(a, b, *, name, scale=None):
    lp, k = a.shape
    n = b.shape[1]
    tm = _tile(lp, 768, 8)
    tk = _tile(k, MM_TILE, 128)
    tn = _tile(n, MM_TILE, 128)
    nm = lp // tm

    def body(a_ref, b_ref, o_ref, acc_ref):
        m = pl.program_id(2)

        @pl.when(m == 0)
        def _():
            acc_ref[...] = jnp.zeros_like(acc_ref)

        acc_ref[...] += _dot_tn(a_ref[...].astype(BF16), b_ref[...].astype(BF16))

        @pl.when(m == nm - 1)
        def _():
            o_ref[...] = acc_ref[...] if scale is None else acc_ref[...] * scale

    return pl.pallas_call(
        body, name=name, grid=(k // tk, n // tn, nm),
        in_specs=[pl.BlockSpec((tm, tk), lambda i, j, m: (m, i)), pl.BlockSpec((tm, tn), lambda i, j, m: (m, j))],
        out_specs=pl.BlockSpec((tk, tn), lambda i, j, m: (i, j)),
        out_shape=jax.ShapeDtypeStruct((k, n), F32),
        scratch_shapes=[pltpu.VMEM((tk, tn), F32)],
        compiler_params=_cparams(("arbitrary", "arbitrary", "arbitrary")),
    )(a, b)


HALO = 8


CONV_STRIP = 32


def _conv_load(ext_ref, x_ref):
    @pl.when(pl.program_id(1) == 0)
    def _():
        ext_ref[0:HALO, :] = jnp.zeros((HALO, ext_ref.shape[1]), F32)

    ext_ref[HALO:, :] = x_ref[...]


def _conv_strip(ext_ref, w_ref, b_ref, r0, width):
    taps = [ext_ref[pl.ds(r0 + HALO - (width - 1 - k), CONV_STRIP), :] for k in range(width)]
    acc = b_ref[...] + taps[0] * w_ref[0:1, :]
    for k in range(1, width):
        acc = acc + taps[k] * w_ref[k:k + 1, :]
    return acc, taps


def _strip_live(i, tm, r0, pf):
    return i * tm + r0 + lax.broadcasted_iota(jnp.int32, (CONV_STRIP, 1), 0) >= pf


def _fold8(x):
    out = x[0:8]
    for r in range(8, x.shape[0], 8):
        out = out + x[r:r + 8]
    return out


def _conv_grad_flush(acc_ref, dw_ref, db_ref, width, last):
    @pl.when(last)
    def _():
        for k in range(width):
            dw_ref[k:k + 1, :] = jnp.sum(acc_ref[8 * k:8 * k + 8, :], axis=0, keepdims=True)
        db_ref[...] = jnp.sum(acc_ref[8 * width:8 * width + 8, :], axis=0, keepdims=True)


def _ssd_conv_fwd(zx, cw, cb, *, cfg, name):
    lp, cd, di = cfg.LP, cfg.CONV_DIM, cfg.DI
    tc = _tile(math.gcd(di, cd), 512, 128)
    tm = _tile(lp, 768, 8)
    off = di // tc

    def body(x_ref, w_ref, b_ref, o_ref, ext_ref):
        i = pl.program_id(1)
        _conv_load(ext_ref, x_ref)
        for r0 in range(0, tm, CONV_STRIP):
            acc, _ = _conv_strip(ext_ref, w_ref, b_ref, r0, SSM_CONV)
            o_ref[pl.ds(r0, CONV_STRIP), :] = jnp.where(_strip_live(i, tm, r0, cfg.PF), acc * _sigmoid(acc), 0.0)
        ext_ref[0:HALO, :] = x_ref[tm - HALO:tm, :]

    return pl.pallas_call(
        body, name=name, grid=(cd // tc, lp // tm),
        in_specs=[pl.BlockSpec((tm, tc), lambda j, i: (i, j + off)), pl.BlockSpec((SSM_CONV, tc), lambda j, i: (0, j)),
                  pl.BlockSpec((1, tc), lambda j, i: (0, j))],
        out_specs=pl.BlockSpec((tm, tc), lambda j, i: (i, j)),
        out_shape=jax.ShapeDtypeStruct((lp, cd), F32),
        scratch_shapes=[pltpu.VMEM((tm + HALO, tc), F32)],
        compiler_params=_cparams(("arbitrary", "arbitrary")),
    )(zx, cw, cb)


def _ssd_conv_bwd_pre(zx, dxbc, cw, cb, *, cfg, name):
    lp, cd, di = cfg.LP, cfg.CONV_DIM, cfg.DI
    tc = _tile(math.gcd(di, cd), 512, 128)
    tm = _tile(lp, 768, 8)
    off = di // tc

    def body(x_ref, d_ref, w_ref, b_ref, dc_ref, dw_ref, db_ref, ext_ref, acc_ref):
        i = pl.program_id(1)

        @pl.when(i == 0)
        def _():
            acc_ref[...] = jnp.zeros_like(acc_ref)

        _conv_load(ext_ref, x_ref)
        for r0 in range(0, tm, CONV_STRIP):
            sl = pl.ds(r0, CONV_STRIP)
            c, taps = _conv_strip(ext_ref, w_ref, b_ref, r0, SSM_CONV)
            sg = _sigmoid(c)
            dc = jnp.where(_strip_live(i, tm, r0, cfg.PF), d_ref[sl, :] * sg * (1.0 + c * (1.0 - sg)), 0.0)
            dc_ref[sl, :] = dc
            for k in range(SSM_CONV):
                acc_ref[8 * k:8 * k + 8, :] += _fold8(dc * taps[k])
            acc_ref[8 * SSM_CONV:, :] += _fold8(dc)
        ext_ref[0:HALO, :] = x_ref[tm - HALO:tm, :]
        _conv_grad_flush(acc_ref, dw_ref, db_ref, SSM_CONV, i == lp // tm - 1)

    return pl.pallas_call(
        body, name=name, grid=(cd // tc, lp // tm),
        in_specs=[pl.BlockSpec((tm, tc), lambda j, i: (i, j + off)), pl.BlockSpec((tm, tc), lambda j, i: (i, j)),
                  pl.BlockSpec((SSM_CONV, tc), lambda j, i: (0, j)), pl.BlockSpec((1, tc), lambda j, i: (0, j))],
        out_specs=[pl.BlockSpec((tm, tc), lambda j, i: (i, j)), pl.BlockSpec((SSM_CONV, tc), lambda j, i: (0, j)),
                   pl.BlockSpec((1, tc), lambda j, i: (0, j))],
        out_shape=[jax.ShapeDtypeStruct((lp, cd), F32), jax.ShapeDtypeStruct((SSM_CONV, cd), F32),
                   jax.ShapeDtypeStruct((1, cd), F32)],
        scratch_shapes=[pltpu.VMEM((tm + HALO, tc), F32), pltpu.VMEM((8 * (SSM_CONV + 1), tc), F32)],
        compiler_params=_cparams(("arbitrary", "arbitrary")),
    )(zx, dxbc, cw, cb)


def _ffn_conv_fwd(up, cw, cb, *, cfg, name):
    lp, dff = cfg.LP, cfg.DFF
    tc = 2 * FFN_IL
    tm = _tile(lp, 768, 8)

    def body(x_ref, w_ref, b_ref, o_ref, ext_ref):
        i = pl.program_id(1)
        _conv_load(ext_ref, x_ref)
        for r0 in range(0, tm, CONV_STRIP):
            hc, _ = _conv_strip(ext_ref, w_ref, b_ref, r0, FFN_CONV)
            gc, vc = hc[:, :FFN_IL], hc[:, FFN_IL:]
            act = jnp.where(_strip_live(i, tm, r0, cfg.PF), gc * _sigmoid(gc) * vc, 0.0)
            o_ref[pl.ds(r0, CONV_STRIP), :] = act.astype(BF16)
        ext_ref[0:HALO, :] = x_ref[tm - HALO:tm, :]

    return pl.pallas_call(
        body, name=name, grid=(dff // FFN_IL, lp // tm),
        in_specs=[pl.BlockSpec((tm, tc), lambda j, i: (i, j)), pl.BlockSpec((FFN_CONV, tc), lambda j, i: (0, j)),
                  pl.BlockSpec((1, tc), lambda j, i: (0, j))],
        out_specs=pl.BlockSpec((tm, FFN_IL), lambda j, i: (i, j)),
        out_shape=jax.ShapeDtypeStruct((lp, dff), BF16),
        scratch_shapes=[pltpu.VMEM((tm + HALO, tc), F32)],
        compiler_params=_cparams(("arbitrary", "arbitrary")),
    )(up, cw, cb)


def _ffn_conv_bwd_pre(up, dact, cw, cb, *, cfg, name):
    lp, dff = cfg.LP, cfg.DFF
    tc = 2 * FFN_IL
    tm = _tile(lp, 768, 8)

    def body(x_ref, d_ref, w_ref, b_ref, dc_ref, dw_ref, db_ref, ext_ref, acc_ref):
        i = pl.program_id(1)

        @pl.when(i == 0)
        def _():
            acc_ref[...] = jnp.zeros_like(acc_ref)

        _conv_load(ext_ref, x_ref)
        for r0 in range(0, tm, CONV_STRIP):
            sl = pl.ds(r0, CONV_STRIP)
            hc, taps = _conv_strip(ext_ref, w_ref, b_ref, r0, FFN_CONV)
            gc, vc = hc[:, :FFN_IL], hc[:, FFN_IL:]
            sg = _sigmoid(gc)
            da = jnp.where(_strip_live(i, tm, r0, cfg.PF), d_ref[sl, :].astype(F32), 0.0)
            dc = jnp.concatenate([da * vc * sg * (1.0 + gc * (1.0 - sg)), da * gc * sg], axis=1)
            dc_ref[sl, :] = dc
            for k in range(FFN_CONV):
                acc_ref[8 * k:8 * k + 8, :] += _fold8(dc * taps[k])
            acc_ref[8 * FFN_CONV:, :] += _fold8(dc)
        ext_ref[0:HALO, :] = x_ref[tm - HALO:tm, :]
        _conv_grad_flush(acc_ref, dw_ref, db_ref, FFN_CONV, i == lp // tm - 1)

    return pl.pallas_call(
        body, name=name, grid=(dff // FFN_IL, lp // tm),
        in_specs=[pl.BlockSpec((tm, tc), lambda j, i: (i, j)), pl.BlockSpec((tm, FFN_IL), lambda j, i: (i, j)),
                  pl.BlockSpec((FFN_CONV, tc), lambda j, i: (0, j)), pl.BlockSpec((1, tc), lambda j, i: (0, j))],
        out_specs=[pl.BlockSpec((tm, tc), lambda j, i: (i, j)), pl.BlockSpec((FFN_CONV, tc), lambda j, i: (0, j)),
                   pl.BlockSpec((1, tc), lambda j, i: (0, j))],
        out_shape=[jax.ShapeDtypeStruct((lp, 2 * dff), F32), jax.ShapeDtypeStruct((FFN_CONV, 2 * dff), F32),
                   jax.ShapeDtypeStruct((1, 2 * dff), F32)],
        scratch_shapes=[pltpu.VMEM((tm + HALO, tc), F32), pltpu.VMEM((8 * (FFN_CONV + 1), tc), F32)],
        compiler_params=_cparams(("arbitrary", "arbitrary")),
    )(up, dact, cw, cb)


def _head_select(lane, head):
    return ((lane >= head * SSM_HEAD_DIM) & (lane < (head + 1) * SSM_HEAD_DIM)).astype(BF16)


def _ssd_terms(x, bm, cm, dtr, bias, alog, dsk, valid, cfg):
    q, hg, gw, p = SSM_CHUNK, cfg.HG, cfg.GW, SSM_HEAD_DIM
    t = {}
    e_mat = _head_select(lax.broadcasted_iota(jnp.int32, (hg, gw), 1), lax.broadcasted_iota(jnp.int32, (hg, gw), 0))
    li = lax.broadcasted_iota(jnp.int32, (q, q), 0)
    si = lax.broadcasted_iota(jnp.int32, (q, q), 1)
    tri = li >= si
    tril = tri.astype(BF16)
    triu = (li <= si).astype(BF16)
    pre = dtr + bias
    dtv = jnp.where(valid, _softplus(pre), 0.0)
    a_head = -jnp.exp(alog)
    a = dtv * a_head
    cs = _dot3_left(tril, a)
    hi, mid, lo = _split3(a)
    cst = _dot_tn(hi, triu) + _dot_tn(mid, triu) + _dot_tn(lo, triu)
    cs_last = jnp.sum(a, axis=0, keepdims=True)
    dte = jnp.exp(jnp.minimum(cs_last - cs, 0.0))
    ecs = jnp.exp(cs)
    t.update(e_mat=e_mat, tri=tri, tril=tril, triu=triu, pre=pre, dtv=dtv, a_head=a_head, cs=cs, cst=cst,
             dec=jnp.exp(cs_last), dte=dte, ecs=ecs)
    t["dtv_x"] = _dot3(dtv, e_mat)
    t["ecs_x"] = _dot3(ecs, e_mat)
    t["dte_x"] = _dot3(dte, e_mat)
    t["dec_x"] = _dot3(t["dec"], e_mat)
    t["d_x"] = _dot3(dsk, e_mat)
    t["xdt"] = x * t["dtv_x"]
    t["gm"] = _dot_nt(cm.astype(BF16), bm.astype(BF16))
    return t


def _ssd_decay(t, e):
    diff = t["cs"][:, e:e + 1] - t["cst"][e:e + 1, :]
    return jnp.where(t["tri"], jnp.exp(jnp.minimum(diff, 0.0)), 0.0)


def _head_mask(b):
    lane = lax.broadcasted_iota(jnp.int32, (1, 2 * SSM_HEAD_DIM), 1)
    return (lane >= SSM_HEAD_DIM * b) & (lane < SSM_HEAD_DIM * (b + 1))


def _ssd_specs(cfg):
    q, g, gw, ns, hg, di = SSM_CHUNK, SSM_GROUPS, cfg.GW, SSM_STATE, cfg.HG, cfg.DI
    return dict(
        x=pl.BlockSpec((q, gw), lambda j, c: (c, j)),
        b=pl.BlockSpec((q, ns), lambda j, c: (c, di // ns + j)),
        c=pl.BlockSpec((q, ns), lambda j, c: (c, di // ns + g + j)),
        z=pl.BlockSpec((q, gw), lambda j, c: (c, j)),
        dtr=pl.BlockSpec((None, q, hg), lambda j, c: (j, c, 0)),
        prm=pl.BlockSpec((None, 8, hg), lambda j, c: (j, 0, 0)),
        gate=pl.BlockSpec((1, gw), lambda j, c: (0, j)),
        st=pl.BlockSpec((None, None, ns, gw), lambda j, c: (c, j, 0, 0)),
    )


def _ssd_fwd(xbc, zx, dtr, prm, gate_g, *, cfg, name):
    q, g, gw, ns, hg, p = SSM_CHUNK, SSM_GROUPS, cfg.GW, SSM_STATE, cfg.HG, SSM_HEAD_DIM
    nc = cfg.LP // q
    sp = _ssd_specs(cfg)

    def body(x_ref, b_ref, c_ref, z_ref, dtr_ref, prm_ref, gg_ref, y_ref, hn_ref, st_ref, s_ref):
        c = pl.program_id(1)

        @pl.when(c == 0)
        def _():
            s_ref[...] = jnp.zeros((ns, gw), F32)

        s_prev = s_ref[...]
        st_ref[...] = s_prev
        x, bm, cm = x_ref[...], b_ref[...], c_ref[...]
        valid = _row_ids(c, q) >= cfg.PF
        t = _ssd_terms(x, bm, cm, dtr_ref[...], prm_ref[0:1, :], prm_ref[1:2, :], prm_ref[2:3, :], valid, cfg)
        parts = []
        for pr in range(hg // 2):
            xp = t["xdt"][:, 2 * p * pr:2 * p * (pr + 1)]
            acc = None
            for b in range(2):
                m = (t["gm"] * _ssd_decay(t, 2 * pr + b)).astype(BF16)
                d = _dot(m, jnp.where(_head_mask(b), xp, 0.0).astype(BF16))
                acc = d if acc is None else acc + d
            parts.append(acc)
        y_diag = jnp.concatenate(parts, axis=1)
        y_off = _dot(cm.astype(BF16), s_prev.astype(BF16)) * t["ecs_x"]
        y = y_diag + y_off + x * t["d_x"]
        y_ref[...] = y
        s_ref[...] = s_prev * t["dec_x"] + _dot_tn(bm.astype(BF16), (t["xdt"] * t["dte_x"]).astype(BF16))
        z = z_ref[...]
        hgate = y * z * _sigmoid(z)
        r = lax.rsqrt(jnp.mean(hgate * hgate, axis=-1, keepdims=True) + NORM_EPS)
        hn_ref[...] = (hgate * r * gg_ref[...]).astype(BF16)

    return pl.pallas_call(
        body, name=name, grid=(g, nc),
        in_specs=[sp["x"], sp["b"], sp["c"], sp["z"], sp["dtr"], sp["prm"], sp["gate"]],
        out_specs=[sp["x"], sp["x"], sp["st"]],
        out_shape=[jax.ShapeDtypeStruct((cfg.LP, cfg.DI), F32), jax.ShapeDtypeStruct((cfg.LP, cfg.DI), BF16),
                   jax.ShapeDtypeStruct((nc, g, ns, gw), F32)],
        scratch_shapes=[pltpu.VMEM((ns, gw), F32)],
        compiler_params=_cparams(("arbitrary", "arbitrary")),
    )(xbc, xbc, xbc, zx, dtr, prm, gate_g)


def _ssd_bwd(xbc, zx, dtr, prm, gate_g, y, states, dhn, *, cfg, name, send=None):
    q, g, gw, ns, hg, p = SSM_CHUNK, SSM_GROUPS, cfg.GW, SSM_STATE, cfg.HG, SSM_HEAD_DIM
    nc = cfg.LP // q
    sp = _ssd_specs(cfg)
    rev = lambda spec: pl.BlockSpec(spec.block_shape, (lambda f: (lambda j, c: f(j, nc - 1 - c)))(spec.index_map))
    bc_spec = pl.BlockSpec((q, ns), lambda j, c: (nc - 1 - c, j))

    def body(x_ref, b_ref, c_ref, z_ref, dtr_ref, prm_ref, gg_ref, y_ref, st_ref, dhn_ref,
             dx_ref, db_ref, dc_ref, dz_ref, ddtr_ref, dprm_ref, dgg_ref, ds_ref):
        ci = pl.program_id(1)
        c = nc - 1 - ci

        @pl.when(ci == 0)
        def _():
            ds_ref[...] = jnp.zeros((ns, gw), F32)
            dprm_ref[...] = jnp.zeros((8, hg), F32)
            dgg_ref[...] = jnp.zeros((8, gw), F32)

        ds_next = ds_ref[...]
        s_prev = st_ref[...]
        x, bm, cm = x_ref[...], b_ref[...], c_ref[...]
        valid = _row_ids(c, q) >= cfg.PF
        t = _ssd_terms(x, bm, cm, dtr_ref[...], prm_ref[0:1, :], prm_ref[1:2, :], prm_ref[2:3, :], valid, cfg)
        et_mat = _head_select(lax.broadcasted_iota(jnp.int32, (gw, hg), 0), lax.broadcasted_iota(jnp.int32, (gw, hg), 1))
        heads = lambda v: _dot3(v, et_mat)
        yv, z = y_ref[...], z_ref[...]
        sz = _sigmoid(z)
        silu = z * sz
        hgate = yv * silu
        r = lax.rsqrt(jnp.mean(hgate * hgate, axis=-1, keepdims=True) + NORM_EPS)
        hhat = hgate * r
        dhn = dhn_ref[...]
        dgg = jnp.sum(dhn * hhat, axis=0, keepdims=True)
        dhh = dhn * gg_ref[...]
        dhgate = r * (dhh - hhat * jnp.mean(dhh * hhat, axis=-1, keepdims=True))
        dy = dhgate * silu
        dz_ref[...] = dhgate * yv * sz * (1.0 + z * (1.0 - sz))
        bmb, cmb = bm.astype(BF16), cm.astype(BF16)
        xdt = t["xdt"]
        dye = dy * t["ecs_x"]
        dxdt_state = _dot(bmb, ds_next.astype(BF16)) * t["dte_x"]
        dg_acc = None
        dparts = []
        li = lax.broadcasted_iota(jnp.int32, (q, q), 0)
        si = lax.broadcasted_iota(jnp.int32, (q, q), 1)
        head_id = lax.broadcasted_iota(jnp.int32, (1, hg), 1)
        da = None
        for pr in range(hg // 2):
            sl = slice(2 * p * pr, 2 * p * (pr + 1))
            xp, dyp = xdt[:, sl], dy[:, sl]
            acc = None
            for b in range(2):
                lm = _ssd_decay(t, 2 * pr + b)
                m = (t["gm"] * lm).astype(BF16)
                dym = jnp.where(_head_mask(b), dyp, 0.0).astype(BF16)
                d = _dot_tn(m, dym)
                acc = d if acc is None else acc + d
                dm = _dot_nt(dym, xp.astype(BF16)) * lm
                dg_acc = dm if dg_acc is None else dg_acc + dm
                corner = _dot(t["triu"], (dm * t["gm"]).astype(BF16))
                da_e = jnp.sum(jnp.where(si < li, corner, 0.0), axis=1, keepdims=True)
                da_e = da_e * (head_id == 2 * pr + b).astype(F32)
                da = da_e if da is None else da + da_e
            dparts.append(acc)
        dxdt = jnp.concatenate(dparts, axis=1) + dxdt_state
        dgb = dg_acc.astype(BF16)
        dc_ref[...] = _dot(dgb, bmb) + _dot_nt(dye.astype(BF16), s_prev.astype(BF16))
        xde = (xdt * t["dte_x"]).astype(BF16)
        db_ref[...] = _dot_tn(dgb, cmb) + _dot_nt(xde, ds_next.astype(BF16))
        ds_ref[...] = ds_next * t["dec_x"] + _dot_tn(cmb, dye.astype(BF16))
        y_off = _dot(cmb, s_prev.astype(BF16)) * t["ecs_x"]
        tril_strict = (li > si).astype(BF16)
        da = da + _dot3_left(t["triu"], heads(dy * y_off)) + _dot3_left(tril_strict, heads(xdt * dxdt_state)) \
            + t["dec"] * heads(jnp.sum(ds_next * s_prev, axis=0, keepdims=True))
        ddtv = da * t["a_head"] + heads(dxdt * x)
        ddtr = jnp.where(valid, ddtv * _sigmoid(t["pre"]), 0.0)
        ddtr_ref[...] = ddtr
        dx_ref[...] = dxdt * t["dtv_x"] + dy * t["d_x"]
        dalog = jnp.sum(da * t["dtv"], axis=0, keepdims=True) * t["a_head"]
        dprm_ref[0:1, :] += jnp.sum(ddtr, axis=0, keepdims=True)
        dprm_ref[1:2, :] += dalog
        dprm_ref[2:3, :] += heads(jnp.sum(dy * x, axis=0, keepdims=True))
        dgg_ref[0:1, :] += dgg

    in_specs = [rev(sp["x"]), rev(sp["b"]), rev(sp["c"]), rev(sp["z"]), rev(sp["dtr"]), sp["prm"], sp["gate"],
                rev(sp["x"]), rev(sp["st"]), rev(sp["x"])]
    out_specs = [rev(sp["x"]), bc_spec, bc_spec, rev(sp["x"]), rev(sp["dtr"]),
                 pl.BlockSpec((None, 8, hg), lambda j, c: (j, 0, 0)), pl.BlockSpec((None, 8, gw), lambda j, c: (j, 0, 0))]
    out_shape = [jax.ShapeDtypeStruct((cfg.LP, cfg.DI), F32), jax.ShapeDtypeStruct((cfg.LP, g * ns), F32),
                 jax.ShapeDtypeStruct((cfg.LP, g * ns), F32), jax.ShapeDtypeStruct((cfg.LP, cfg.DI), F32),
                 jax.ShapeDtypeStruct((g, cfg.LP, hg), F32), jax.ShapeDtypeStruct((g, 8, hg), F32),
                 jax.ShapeDtypeStruct((g, 8, gw), F32)]
    scratch = [pltpu.VMEM((ns, gw), F32)]
    args = (xbc, xbc, xbc, zx, dtr, prm, gate_g, y, states, dhn)
    if send is None:
        kernel_body = body
    else:
        n_in, n_out = len(in_specs), len(out_specs)

        def kernel_body(*refs):
            ins, send_ref = refs[:n_in], refs[n_in]
            outs, recv_ref = refs[n_in + 1:n_in + 1 + n_out], refs[n_in + 1 + n_out]
            ds_ref, send_sems, recv_sems, local_sem = refs[n_in + n_out + 2:]
            copies = lambda: _all_to_all_copies(send_ref, recv_ref, send_sems, recv_sems, local_sem)

            @pl.when((pl.program_id(0) == 0) & (pl.program_id(1) == 0))
            def _():
                for cp in copies():
                    cp.start()

            body(*ins, *outs, ds_ref)

            @pl.when((pl.program_id(0) == g - 1) & (pl.program_id(1) == nc - 1))
            def _():
                for cp in copies():
                    cp.wait()

        in_specs, out_specs = in_specs + [HBM_SPEC], out_specs + [HBM_SPEC]
        out_shape = out_shape + [jax.ShapeDtypeStruct(send.shape, send.dtype)]
        scratch = scratch + [pltpu.SemaphoreType.DMA((N_DEV - 1,)), pltpu.SemaphoreType.DMA((N_DEV - 1,)),
                             pltpu.SemaphoreType.DMA]
        args = args + (send,)
    return pl.pallas_call(
        kernel_body, name=name, grid=(g, nc), in_specs=in_specs, out_specs=out_specs, out_shape=out_shape,
        scratch_shapes=scratch, compiler_params=_cparams(("arbitrary", "arbitrary")),
    )(*args)


ATT_STRIP = 32
ATT_NT = 2
ATT_DEAD = -120.0
ATT_NEG = -1e9
LOG2E = 1.4426950408889634
LN2 = 0.6931471805599453


def _att_init(u_ref, bias_ref, qb, cfg):
    t = ATT_T
    rows = lax.broadcasted_iota(jnp.int32, (t, t), 0)
    cols = lax.broadcasted_iota(jnp.int32, (t, t), 1)
    u_ref[...] = (rows > cols).astype(BF16)
    pad = cols < cfg.PF
    bias_ref[0] = jnp.zeros((t, t), F32)
    bias_ref[1] = jnp.where((cols >= rows) | (pad & (qb == 0)), ATT_NEG, 0.0)
    bias_ref[2] = jnp.where(pad, ATT_NEG, 0.0)
    bias_ref[3] = jnp.full((t, t), ATT_NEG, F32)


def _att_tile(kv_ref, kb, qb):
    t = ATT_T
    off = pl.multiple_of(jnp.maximum(kb, 0) * t, t)
    k = kv_ref[pl.ds(off, t), 0:2 * SB_HEAD_DIM]
    v = kv_ref[pl.ds(off, t), 2 * SB_HEAD_DIM:]
    kind = jnp.where(kb < 0, 3, jnp.where(kb == qb, 1, jnp.where(kb == 0, 2, 0)))
    return k, v, kind


def _att_trips(qb):
    n = qb + ATT_NT
    return lax.shift_right_logical(n, 1) if ATT_NT == 2 else lax.div(n, ATT_NT)


def _att_walk(qb, trip, carry_ref):
    def cond(st):
        j, alive = st
        return (j < _att_trips(qb)) & alive

    def step(st):
        j, _ = st
        trip(j, None)
        return j + 1, jnp.max(carry_ref[...]) > ATT_DEAD

    lax.while_loop(cond, step, (0, True))


def _att_carry(carry_ref, rsum_ref, u, b):
    carry = carry_ref[b]
    for up in range(u):
        carry = carry + rsum_ref[2 * up + b]
    return carry


def _att_scores_strip(s_ref, bias_ref, kind, b, r0):
    sl = pl.ds(r0, ATT_STRIP)
    s = s_ref[b, sl, :] + bias_ref[kind, sl, :]
    l1p = jnp.log2(1.0 + jnp.exp2(jnp.abs(s) * (-LOG2E))) * LN2
    ls = jnp.minimum(s, 0.0) - l1p
    return ls, ls - s


def _attn_fwd(q, kv, *, cfg, name):
    t, lp, d, rs, nch = ATT_T, cfg.LP, cfg.D, ATT_STRIP, 2 * ATT_NT
    hp, nq = d // (2 * SB_HEAD_DIM), lp // ATT_T
    scale = SB_HEAD_DIM ** -0.5

    def body(q_ref, kv_ref, o_ref, u_ref, bias_ref, s_ref, hi_ref, lo_ref, cum_ref, w_ref, cb_ref, acc_ref, carry_ref,
             rsum_ref):
        qb = pl.program_id(1)
        _att_init(u_ref, bias_ref, qb, cfg)
        lo_half = lax.broadcasted_iota(jnp.int32, (1, 2 * SB_HEAD_DIM), 1) < SB_HEAD_DIM
        q2 = q_ref[...] * scale
        qms = [jnp.where(lo_half, q2, 0).astype(BF16), jnp.where(lo_half, 0, q2).astype(BF16)]
        acc_ref[...] = jnp.zeros_like(acc_ref)
        carry_ref[...] = jnp.zeros_like(carry_ref)

        def trip(j, _):
            tiles = [_att_tile(kv_ref, qb - ATT_NT * j - u, qb) for u in range(ATT_NT)]
            chains = [(u, b) for u in range(ATT_NT) for b in range(2)]
            for c, (u, b) in enumerate(chains):
                s_ref[c] = _dot_nt(qms[b], tiles[u][0])
            for c, (u, b) in enumerate(chains):
                for r0 in range(0, t, rs):
                    sl = pl.ds(r0, rs)
                    ls, lk = _att_scores_strip(s_ref, bias_ref, tiles[u][2], c, r0)
                    s_ref[c, sl, :] = ls
                    hi = lk.astype(BF16)
                    hi_ref[c, sl, :] = hi
                    lo_ref[c, sl, :] = (lk - hi.astype(F32)).astype(BF16)
                    rsum_ref[c, sl, :] = jnp.sum(lk, axis=1, keepdims=True)
            for c in range(len(chains)):
                cum_ref[c] = _dot(hi_ref[c], u_ref[...]) + _dot(lo_ref[c], u_ref[...])
            for c, (u, b) in enumerate(chains):
                cb_ref[c] = jnp.broadcast_to(_att_carry(carry_ref, rsum_ref, u, b), (t, t))
            for c, (u, b) in enumerate(chains):
                for r0 in range(0, t, rs):
                    sl = pl.ds(r0, rs)
                    x = s_ref[c, sl, :] + cum_ref[c, sl, :] + cb_ref[c, sl, :]
                    w_ref[c, sl, :] = jnp.exp2(x * LOG2E).astype(BF16)
            for c, (u, b) in enumerate(chains):
                acc_ref[b] += _dot(w_ref[c], tiles[u][1])
                carry_ref[b] += rsum_ref[c]

        _att_walk(qb, trip, carry_ref)
        o_ref[...] = jnp.where(lo_half, acc_ref[0], acc_ref[1])

    return pl.pallas_call(
        body, name=name, grid=(hp, nq),
        in_specs=[pl.BlockSpec((t, 128), lambda h, i: (i, h)), pl.BlockSpec((lp, 256), lambda h, i: (0, h))],
        out_specs=pl.BlockSpec((t, 128), lambda h, i: (i, h)),
        out_shape=jax.ShapeDtypeStruct((lp, d), F32),
        scratch_shapes=[pltpu.VMEM((t, t), BF16), pltpu.VMEM((4, t, t), F32), pltpu.VMEM((nch, t, t), F32),
                        pltpu.VMEM((nch, t, t), BF16), pltpu.VMEM((nch, t, t), BF16), pltpu.VMEM((nch, t, t), F32),
                        pltpu.VMEM((nch, t, t), BF16), pltpu.VMEM((nch, t, t), F32),
                        pltpu.VMEM((2, t, 2 * SB_HEAD_DIM), F32), pltpu.VMEM((2, t, 1), F32), pltpu.VMEM((nch, t, 1), F32)],
        compiler_params=_cparams(("arbitrary", "arbitrary")),
    )(q, kv)


def _attn_bwd(q, kv, o, do, *, cfg, name):
    t, lp, d, rs, nch = ATT_T, cfg.LP, cfg.D, ATT_STRIP, 2 * ATT_NT
    hp, nq = d // (2 * SB_HEAD_DIM), lp // ATT_T
    scale = SB_HEAD_DIM ** -0.5

    def body(q_ref, kv_ref, o_ref, do_ref, dq_ref, dkv_ref, u_ref, bias_ref, s_ref, dw_ref, hi_ref, lo_ref, cum_ref,
             w_ref, cb_ref, acc_ref, carry_ref, ecarry_ref, etot_ref, rsum_ref, ersum_ref):
        qb = pl.program_id(1)

        @pl.when(qb == 0)
        def _():
            dkv_ref[...] = jnp.zeros_like(dkv_ref)

        _att_init(u_ref, bias_ref, qb, cfg)
        lo_half = lax.broadcasted_iota(jnp.int32, (1, 2 * SB_HEAD_DIM), 1) < SB_HEAD_DIM
        q2 = q_ref[...] * scale
        do2 = do_ref[...]
        qms = [jnp.where(lo_half, q2, 0).astype(BF16), jnp.where(lo_half, 0, q2).astype(BF16)]
        doms = [jnp.where(lo_half, do2, 0).astype(BF16), jnp.where(lo_half, 0, do2).astype(BF16)]
        prod = do2.astype(F32) * o_ref[...]
        etot_ref[0] = jnp.sum(jnp.where(lo_half, prod, 0.0), axis=1, keepdims=True)
        etot_ref[1] = jnp.sum(jnp.where(lo_half, 0.0, prod), axis=1, keepdims=True)
        acc_ref[...] = jnp.zeros_like(acc_ref)
        carry_ref[...] = jnp.zeros_like(carry_ref)
        ecarry_ref[...] = jnp.zeros_like(ecarry_ref)

        def trip(j, _):
            kbs = [qb - ATT_NT * j - u for u in range(ATT_NT)]
            tiles = [_att_tile(kv_ref, kb, qb) for kb in kbs]
            chains = [(u, b) for u in range(ATT_NT) for b in range(2)]
            for c, (u, b) in enumerate(chains):
                s_ref[c] = _dot_nt(qms[b], tiles[u][0])
                dw_ref[c] = _dot_nt(doms[b], tiles[u][1])
            for c, (u, b) in enumerate(chains):
                for r0 in range(0, t, rs):
                    sl = pl.ds(r0, rs)
                    ls, lk = _att_scores_strip(s_ref, bias_ref, tiles[u][2], c, r0)
                    s_ref[c, sl, :] = ls
                    hi = lk.astype(BF16)
                    hi_ref[c, sl, :] = hi
                    lo_ref[c, sl, :] = (lk - hi.astype(F32)).astype(BF16)
                    rsum_ref[c, sl, :] = jnp.sum(lk, axis=1, keepdims=True)
            for c in range(nch):
                cum_ref[c] = _dot(hi_ref[c], u_ref[...]) + _dot(lo_ref[c], u_ref[...])
            for c, (u, b) in enumerate(chains):
                cb_ref[c] = jnp.broadcast_to(_att_carry(carry_ref, rsum_ref, u, b), (t, t))
            for c, (u, b) in enumerate(chains):
                for r0 in range(0, t, rs):
                    sl = pl.ds(r0, rs)
                    wb = jnp.exp2((s_ref[c, sl, :] + cum_ref[c, sl, :] + cb_ref[c, sl, :]) * LOG2E).astype(BF16)
                    w_ref[c, sl, :] = wb
                    e = wb.astype(F32) * dw_ref[c, sl, :]
                    dw_ref[c, sl, :] = e
                    hi = e.astype(BF16)
                    hi_ref[c, sl, :] = hi
                    lo_ref[c, sl, :] = (e - hi.astype(F32)).astype(BF16)
                    ersum_ref[c, sl, :] = jnp.sum(e, axis=1, keepdims=True)
            for c in range(nch):
                cum_ref[c] = _dot(hi_ref[c], u_ref[...]) + _dot(lo_ref[c], u_ref[...])
            for c, (u, b) in enumerate(chains):
                cb_ref[c] = jnp.broadcast_to(etot_ref[b] - _att_carry(ecarry_ref, ersum_ref, u, b), (t, t))
            for c, (u, b) in enumerate(chains):
                for r0 in range(0, t, rs):
                    sl = pl.ds(r0, rs)
                    e = dw_ref[c, sl, :]
                    e_before = cb_ref[c, sl, :] - (e + cum_ref[c, sl, :])
                    sig = jnp.exp2(s_ref[c, sl, :] * LOG2E)
                    hi_ref[c, sl, :] = (e * (1.0 - sig) - sig * e_before).astype(BF16)
            for u in range(ATT_NT):
                off = pl.multiple_of(jnp.maximum(kbs[u], 0) * t, t)
                c0, c1 = 2 * u, 2 * u + 1
                acc_ref[0] += _dot(hi_ref[c0], tiles[u][0])
                acc_ref[1] += _dot(hi_ref[c1], tiles[u][0])
                dkv_ref[pl.ds(off, t), 0:2 * SB_HEAD_DIM] += _dot_tn(hi_ref[c0], qms[0]) + _dot_tn(hi_ref[c1], qms[1])
                dkv_ref[pl.ds(off, t), 2 * SB_HEAD_DIM:] += _dot_tn(w_ref[c0], doms[0]) + _dot_tn(w_ref[c1], doms[1])
                for b in range(2):
                    carry_ref[b] += rsum_ref[2 * u + b]
                    ecarry_ref[b] += ersum_ref[2 * u + b]

        _att_walk(qb, trip, carry_ref)
        dq_ref[...] = jnp.where(lo_half, acc_ref[0], acc_ref[1]) * scale

    big = lambda dt: pltpu.VMEM((nch, t, t), dt)
    col = lambda n: pltpu.VMEM((n, t, 1), F32)
    return pl.pallas_call(
        body, name=name, grid=(hp, nq),
        in_specs=[pl.BlockSpec((t, 128), lambda h, i: (i, h)), pl.BlockSpec((lp, 256), lambda h, i: (0, h)),
                  pl.BlockSpec((t, 128), lambda h, i: (i, h)), pl.BlockSpec((t, 128), lambda h, i: (i, h))],
        out_specs=[pl.BlockSpec((t, 128), lambda h, i: (i, h)), pl.BlockSpec((lp, 256), lambda h, i: (0, h))],
        out_shape=[jax.ShapeDtypeStruct((lp, d), F32), jax.ShapeDtypeStruct((lp, 2 * d), F32)],
        scratch_shapes=[pltpu.VMEM((t, t), BF16), pltpu.VMEM((4, t, t), F32), big(F32), big(F32), big(BF16), big(BF16),
                        big(F32), big(BF16), big(F32), pltpu.VMEM((2, t, 2 * SB_HEAD_DIM), F32), col(2), col(2), col(2),
                        col(nch), col(nch)],
        compiler_params=_cparams(("arbitrary", "arbitrary")),
    )(q, kv, o, do)


def _loss_head(h, g, target, *, cfg, name):
    lp, d = h.shape
    tm = SSM_CHUNK
    first = (cfg.PF + N_META) // tm
    assert (cfg.PF + N_META) % tm == 0

    def body(h_ref, g_ref, t_ref, loss_ref, dh_ref, dg_ref):
        i = pl.program_id(0)

        @pl.when(i == 0)
        def _():
            loss_ref[...] = jnp.zeros_like(loss_ref)
            dg_ref[...] = jnp.zeros_like(dg_ref)

        x = h_ref[...]
        r = lax.rsqrt(jnp.mean(x * x, axis=-1, keepdims=True) + NORM_EPS)
        xhat = x * r
        live = i >= first
        diff = jnp.where(live, xhat * g_ref[...] - t_ref[...], 0.0)
        loss_ref[...] += 0.5 * jnp.sum(jnp.mean(diff * diff, axis=-1, keepdims=True))
        dy = diff * (1.0 / d)
        dg_ref[...] += jnp.sum(dy * xhat, axis=0, keepdims=True)
        dxh = dy * g_ref[...]
        dh_ref[...] = r * (dxh - xhat * jnp.mean(dxh * xhat, axis=-1, keepdims=True))

    return pl.pallas_call(
        body, name=name, grid=(lp // tm,),
        in_specs=[pl.BlockSpec((tm, d), lambda i: (i, 0)), pl.BlockSpec((1, d), lambda i: (0, 0)),
                  pl.BlockSpec((tm, d), lambda i: (jnp.maximum(i - first, 0), 0))],
        out_specs=[pl.BlockSpec((8, 128), lambda i: (0, 0)), pl.BlockSpec((tm, d), lambda i: (i, 0)),
                   pl.BlockSpec((1, d), lambda i: (0, 0))],
        out_shape=[jax.ShapeDtypeStruct((8, 128), F32), jax.ShapeDtypeStruct((lp, d), F32),
                   jax.ShapeDtypeStruct((1, d), F32)],
        compiler_params=_cparams(("arbitrary",)),
    )(h, g, target)


def _conv_bwd_input(dc, cw, *, width, name):
    lp, cd = dc.shape
    tc = _tile(cd, 512, 128)
    tm = _tile(lp, 768, 8)
    ni = lp // tm

    def body(d_ref, w_ref, o_ref, ext_ref):
        @pl.when(pl.program_id(1) == 0)
        def _():
            ext_ref[tm:, :] = jnp.zeros((HALO, tc), F32)

        ext_ref[0:tm, :] = d_ref[...]
        for r0 in range(0, tm, CONV_STRIP):
            acc = ext_ref[pl.ds(r0 + width - 1, CONV_STRIP), :] * w_ref[0:1, :]
            for k in range(1, width):
                acc = acc + ext_ref[pl.ds(r0 + width - 1 - k, CONV_STRIP), :] * w_ref[k:k + 1, :]
            o_ref[pl.ds(r0, CONV_STRIP), :] = acc.astype(BF16)
        ext_ref[tm:, :] = d_ref[0:HALO, :]

    return pl.pallas_call(
        body, name=name, grid=(cd // tc, ni),
        in_specs=[pl.BlockSpec((tm, tc), lambda j, i: (ni - 1 - i, j)), pl.BlockSpec((width, tc), lambda j, i: (0, j))],
        out_specs=pl.BlockSpec((tm, tc), lambda j, i: (ni - 1 - i, j)),
        out_shape=jax.ShapeDtypeStruct((lp, cd), BF16),
        scratch_shapes=[pltpu.VMEM((tm + HALO, tc), F32)],
        compiler_params=_cparams(("arbitrary", "arbitrary")),
    )(dc, cw)


MATMUL_WEIGHTS = ("ssd_w_in", "ssd_w_out", "w_kv", "sb_w_q", "sb_w_o", "ffn_w_up", "ffn_w_down")


def _il(a, cfg):
    nb = cfg.DFF // FFN_IL
    lead = a.shape[:-1]
    a = a.reshape(lead + (2, nb, FFN_IL))
    return jnp.swapaxes(a, -3, -2).reshape(lead + (2 * cfg.DFF,))


def _unil(a, cfg):
    nb = cfg.DFF // FFN_IL
    lead = a.shape[:-1]
    a = a.reshape(lead + (nb, 2, FFN_IL))
    return jnp.swapaxes(a, -3, -2).reshape(lead + (2 * cfg.DFF,))


def _pair(a, cfg):
    hp = cfg.D // 128
    lead = a.shape[:-1]
    return jnp.swapaxes(a.reshape(lead + (2, hp, 128)), -3, -2).reshape(lead + (2 * cfg.D,))


def _unpair(a, cfg):
    hp = cfg.D // 128
    lead = a.shape[:-1]
    return jnp.swapaxes(a.reshape(lead + (hp, 2, 128)), -3, -2).reshape(lead + (2 * cfg.D,))


def _prepare(full, cfg):
    d, di, cd, h = cfg.D, cfg.DI, cfg.CONV_DIM, cfg.H
    w_in = full["ssd_w_in"][0].astype(BF16)
    prm = jnp.stack([full["ssd_dt_bias"][0], full["ssd_a_log"][0], full["ssd_d_skip"][0]]).astype(F32)
    prm = jnp.pad(prm.reshape(3, SSM_GROUPS, cfg.HG).transpose(1, 0, 2), ((0, 0), (0, 5), (0, 0)))
    p = dict(
        meta=full["meta_tokens"].astype(F32),
        ssd_norm=full["ssd_norm"].astype(F32).reshape(1, d),
        w_in_main=w_in[:, :di + cd],
        w_in_dt=jnp.pad(w_in[:, di + cd:], ((0, 0), (0, 128 - h))),
        conv_w=full["ssd_conv_w"][0].astype(F32), conv_b=full["ssd_conv_b"].astype(F32).reshape(1, cd),
        prm=prm, gate=full["ssd_gate_norm"].astype(F32).reshape(1, di),
        w_out=full["ssd_w_out"][0].astype(BF16),
        kv_norm=full["kv_norm"].astype(F32).reshape(1, d), w_kv=_pair(full["w_kv"].astype(BF16), cfg),
        sb_norm=full["sb_norm"].astype(F32).reshape(1, d),
        w_q=full["sb_w_q"][0].astype(BF16), w_o=full["sb_w_o"][0].astype(BF16),
        final_norm=full["final_norm"].astype(F32).reshape(1, d),
    )
    for i in range(2):
        p[f"ffn_norm{i}"] = full["ffn_norm"][i].astype(F32).reshape(1, d)
        p[f"w_up{i}"] = _il(full["ffn_w_up"][i].astype(BF16), cfg)
        p[f"fcw{i}"] = _il(full["ffn_conv_w"][i].astype(F32), cfg)
        p[f"fcb{i}"] = _il(full["ffn_conv_b"][i].astype(F32), cfg).reshape(1, 2 * cfg.DFF)
        p[f"w_down{i}"] = full["ffn_w_down"][i].astype(BF16)
    return p


def _ffn_fwd(h, p, i, cfg):
    up, u = _norm_mm(h, p[f"ffn_norm{i}"], p[f"w_up{i}"], out_dtype=F32, name=f"ffn{i}_up")
    act = _ffn_conv_fwd(up, p[f"fcw{i}"], p[f"fcb{i}"], cfg=cfg, name=f"ffn{i}_conv")
    return _mm_res(act, p[f"w_down{i}"], h, pf=cfg.PF, name=f"ffn{i}_down"), (h, u, up, act)


def _ffn_bwd(dh, saved, p, i, cfg, grads):
    h, u, up, act = saved
    dact = _mm_nt(dh, p[f"w_down{i}"], out_dtype=F32, pf=cfg.PF, name=f"ffn{i}_dact")
    grads[f"w_down{i}"] = _mm_tn(act, dh, name=f"ffn{i}_dwdown")
    dhc, grads[f"fcw{i}"], grads[f"fcb{i}"] = _ffn_conv_bwd_pre(up, dact, p[f"fcw{i}"], p[f"fcb{i}"], cfg=cfg,
                                                                 name=f"ffn{i}_dconv")
    dup = _conv_bwd_input(dhc, p[f"fcw{i}"], width=FFN_CONV, name=f"ffn{i}_dup")
    grads[f"w_up{i}"] = _mm_tn(u, dup, name=f"ffn{i}_dwup")
    dh, grads[f"ffn_norm{i}"] = _mm_nt_normbwd(dup, p[f"w_up{i}"], h, p[f"ffn_norm{i}"], dh, pf=cfg.PF,
                                               name=f"ffn{i}_dh")
    return dh


EARLY_GRADS = ("ssd_w_out", "w_kv", "sb_w_q", "sb_w_o", "ffn_w_up", "ffn_conv_w", "ffn_w_down")


def _local_step(x, target, p, cfg, exchange=None):
    d, di, cd, h_, lp, pf = cfg.D, cfg.DI, cfg.CONV_DIM, cfg.H, cfg.LP, cfg.PF
    g = {}
    h0 = jnp.concatenate([jnp.zeros((pf, d), F32), p["meta"], x], axis=0)
    zx, u1 = _norm_mm(h0, p["ssd_norm"], p["w_in_main"], out_dtype=F32, name="ssd_in")
    dt_raw, _ = _norm_mm(h0, p["ssd_norm"], p["w_in_dt"], out_dtype=F32, name="ssd_in_dt")
    dtr = dt_raw[:, :h_].reshape(lp, SSM_GROUPS, cfg.HG).transpose(1, 0, 2)
    xbc = _ssd_conv_fwd(zx, p["conv_w"], p["conv_b"], cfg=cfg, name="ssd_conv")
    y, hn, states = _ssd_fwd(xbc, zx, dtr, p["prm"], p["gate"], cfg=cfg, name="ssd_scan")
    h1 = _mm_res(hn, p["w_out"], h0, pf=pf, name="ssd_out")
    h2, ffn0 = _ffn_fwd(h1, p, 0, cfg)
    kv, ukv = _norm_mm(h2, p["kv_norm"], p["w_kv"], out_dtype=BF16, name="kv_proj")
    q, uq = _norm_mm(h2, p["sb_norm"], p["w_q"], out_dtype=BF16, name="q_proj")
    o = _attn_fwd(q, kv, cfg=cfg, name="attn_fwd")
    h3 = _mm_res(o, p["w_o"], h2, pf=pf, name="attn_out")
    h4, ffn1 = _ffn_fwd(h3, p, 1, cfg)
    loss8, dh, g["final_norm"] = _loss_head(h4, p["final_norm"], target, cfg=cfg, name="loss_head")
    dh = _ffn_bwd(dh, ffn1, p, 1, cfg, g)
    do = _mm_nt(dh, p["w_o"], out_dtype=BF16, pf=pf, name="attn_do")
    g["w_o"] = _mm_tn(o, dh, name="attn_dwo")
    dq, dkv = _attn_bwd(q, kv, o, do, cfg=cfg, name="attn_bwd")
    g["w_q"] = _mm_tn(uq, dq, name="attn_dwq")
    g["w_kv"] = _mm_tn(ukv, dkv, name="attn_dwkv")
    dh, g["sb_norm"] = _mm_nt_normbwd(dq, p["w_q"], h2, p["sb_norm"], dh, pf=pf, name="attn_dhq")
    dh, g["kv_norm"] = _mm_nt_normbwd(dkv, p["w_kv"], h2, p["kv_norm"], dh, pf=pf, name="attn_dhkv")
    dh = _ffn_bwd(dh, ffn0, p, 0, cfg, g)
    dhn = _mm_nt(dh, p["w_out"], out_dtype=F32, pf=pf, name="ssd_dhn")
    g["w_out"] = _mm_tn(hn, dh, name="ssd_dwout")
    done = dict(
        ssd_w_out=g["w_out"][None], w_kv=_unpair(g["w_kv"], cfg), sb_w_q=g["w_q"][None], sb_w_o=g["w_o"][None],
        ffn_w_up=jnp.stack([_unil(g["w_up0"], cfg), _unil(g["w_up1"], cfg)]),
        ffn_conv_w=jnp.stack([_unil(g["fcw0"], cfg), _unil(g["fcw1"], cfg)]),
        ffn_w_down=jnp.stack([g["w_down0"], g["w_down1"]]),
    )
    res = _ssd_bwd(xbc, zx, dtr, p["prm"], p["gate"], y, states, dhn, cfg=cfg, name="ssd_scan_bwd",
                   send=None if exchange is None else exchange(done))
    dx, db, dc, dz, ddtr, dprm, dgate = res[:7]
    dconv, g["conv_w"], g["conv_b"] = _ssd_conv_bwd_pre(zx, jnp.concatenate([dx, db, dc], axis=1), p["conv_w"],
                                                        p["conv_b"], cfg=cfg, name="ssd_dconv")
    dxpre = _conv_bwd_input(dconv, p["conv_w"], width=SSM_CONV, name="ssd_dxpre")
    dzx = jnp.concatenate([dz.astype(BF16), dxpre], axis=1)
    ddt = jnp.pad(ddtr.transpose(1, 0, 2).reshape(lp, h_), ((0, 0), (0, 128 - h_)))
    dw_main = _mm_tn(u1, dzx, name="ssd_dwin")
    dw_dt = _mm_tn(u1, ddt, name="ssd_dwin_dt")
    dh, gn1 = _mm_nt_normbwd(dzx, p["w_in_main"], h0, p["ssd_norm"], dh, pf=pf, name="ssd_dh")
    dh, gn2 = _mm_nt_normbwd(ddt, p["w_in_dt"], h0, p["ssd_norm"], dh, pf=pf, name="ssd_dh_dt")
    heads = lambda r: dprm[:, r, :].reshape(1, h_)
    out = dict(
        done,
        meta_tokens=dh[pf:pf + N_META], ssd_norm=gn1 + gn2,
        ssd_w_in=jnp.concatenate([dw_main, dw_dt[:, :h_]], axis=1)[None],
        ssd_conv_w=g["conv_w"][None], ssd_conv_b=g["conv_b"],
        ssd_dt_bias=heads(0), ssd_a_log=heads(1), ssd_d_skip=heads(2),
        ssd_gate_norm=dgate[:, 0, :].reshape(1, di),
        kv_norm=g["kv_norm"].reshape(d), sb_norm=g["sb_norm"],
        ffn_norm=jnp.concatenate([g["ffn_norm0"], g["ffn_norm1"]], axis=0),
        ffn_conv_b=jnp.concatenate([_unil(g["fcb0"], cfg), _unil(g["fcb1"], cfg)], axis=0),
        final_norm=g["final_norm"].reshape(d),
    )
    return loss8[0, 0], dh[pf + N_META:], out, (res[7] if exchange is not None else None)


MESH_AXES = ("x", "y", "c")
HBM_SPEC = pl.BlockSpec(memory_space=pltpu.HBM)


def _all_gather(blk, *, name):
    r, w = blk.shape

    def body(x_ref, out_ref, send_sems, recv_sems, local_sem):
        x, y, c = lax.axis_index("x"), lax.axis_index("y"), lax.axis_index("c")
        me, sibling = (x, y, c), (x, y, 1 - c)
        chips = [(1 - x, y), (x, 1 - y), (1 - x, 1 - y)]

        def slot(px, py, pc):
            return out_ref.at[4 * px + 2 * py + pc]

        def copy(k, block, to, src=None):
            return pltpu.make_async_remote_copy(
                src_ref=slot(*block) if src is None else src, dst_ref=slot(*block),
                send_sem=send_sems.at[k], recv_sem=recv_sems.at[k], device_id=to, device_id_type=pl.DeviceIdType.MESH)

        mine = pltpu.make_async_copy(x_ref, slot(*me), local_sem)
        mine.start()
        first = [copy(0, me, sibling, src=x_ref)]
        first += [copy(1 + j, me, (*chip, c), src=x_ref) for j, chip in enumerate(chips)]
        for cp in first:
            cp.start()
        passed = [copy(4 + j, (*chip, c), sibling) for j, chip in enumerate(chips)]
        for j, chip in enumerate(chips):
            copy(1 + j, (*chip, c), me).wait_recv()
            passed[j].start()
        copy(0, sibling, me).wait_recv()
        for j, chip in enumerate(chips):
            copy(4 + j, (*chip, 1 - c), me).wait_recv()
        for cp in first + passed:
            cp.wait_send()
        mine.wait()

    return pl.pallas_call(
        body, name=name, out_shape=jax.ShapeDtypeStruct((N_DEV, r, w), blk.dtype),
        in_specs=[HBM_SPEC], out_specs=HBM_SPEC,
        scratch_shapes=[pltpu.SemaphoreType.DMA((7,)), pltpu.SemaphoreType.DMA((7,)), pltpu.SemaphoreType.DMA],
    )(blk)


def _all_to_all_copies(x_ref, out_ref, send_sems, recv_sems, local_sem):
    x, y, c = lax.axis_index("x"), lax.axis_index("y"), lax.axis_index("c")
    me_id = 4 * x + 2 * y + c
    copies = [pltpu.make_async_copy(x_ref.at[me_id], out_ref.at[me_id], local_sem)]
    for k in range(1, N_DEV):
        fx, fy, fc = (k >> 2) & 1, (k >> 1) & 1, k & 1
        px = 1 - x if fx else x
        py = 1 - y if fy else y
        pc = 1 - c if fc else c
        copies.append(pltpu.make_async_remote_copy(
            src_ref=x_ref.at[4 * px + 2 * py + pc], dst_ref=out_ref.at[me_id],
            send_sem=send_sems.at[k - 1], recv_sem=recv_sems.at[k - 1],
            device_id=(px, py, pc), device_id_type=pl.DeviceIdType.MESH))
    return copies


def _all_to_all(buf, *, name):
    n, r, w = buf.shape

    def body(x_ref, out_ref, send_sems, recv_sems, local_sem):
        copies = _all_to_all_copies(x_ref, out_ref, send_sems, recv_sems, local_sem)
        for cp in copies:
            cp.start()
        for cp in copies:
            cp.wait()

    return pl.pallas_call(
        body, name=name, out_shape=jax.ShapeDtypeStruct((n, r, w), buf.dtype),
        in_specs=[HBM_SPEC], out_specs=HBM_SPEC,
        scratch_shapes=[pltpu.SemaphoreType.DMA((7,)), pltpu.SemaphoreType.DMA((7,)), pltpu.SemaphoreType.DMA],
    )(buf)


def _sum_slots(buf, *, name):
    n, r, w = buf.shape
    tr = _tile(r, 1024, 16)

    def body(x_ref, o_ref):
        acc = x_ref[0].astype(F32)
        for s in range(1, n):
            acc = acc + x_ref[s].astype(F32)
        o_ref[...] = acc

    return pl.pallas_call(
        body, name=name, grid=(r // tr,),
        in_specs=[pl.BlockSpec((n, tr, w), lambda i: (0, i, 0))], out_specs=pl.BlockSpec((tr, w), lambda i: (i, 0)),
        out_shape=jax.ShapeDtypeStruct((r, w), F32), compiler_params=_cparams(("arbitrary",)),
    )(buf)


def _adamw(w, g, m, v, *, name):
    r, lanes = w.shape
    tr = _tile(r, ADAM_ROWS, 8)

    def body(w_ref, g_ref, m_ref, v_ref, d_ref, nm_ref, nv_ref):
        gg = g_ref[...]
        nm = ADAM_B1 * m_ref[...] + (1.0 - ADAM_B1) * gg
        nv = ADAM_B2 * v_ref[...] + (1.0 - ADAM_B2) * (gg * gg)
        m_hat = nm / (1.0 - ADAM_B1 ** ADAM_STEP)
        v_hat = nv / (1.0 - ADAM_B2 ** ADAM_STEP)
        d_ref[...] = -ADAM_LR * (m_hat / (jnp.sqrt(v_hat) + ADAM_EPS) + ADAM_WD * w_ref[...])
        nm_ref[...] = nm
        nv_ref[...] = nv

    spec = pl.BlockSpec((tr, lanes), lambda i: (i, 0))
    shp = jax.ShapeDtypeStruct((r, lanes), F32)
    return pl.pallas_call(body, name=name, grid=(r // tr,), in_specs=[spec] * 4, out_specs=[spec] * 3,
                          out_shape=[shp] * 3, compiler_params=_cparams(("arbitrary",)))(w, g, m, v)


PACK_QUANTUM = 16 * 128
ADAM_ROWS = 2048
INPUT_NAMES = ("x", "meta_tokens", "ssd_norm", "ssd_w_in", "ssd_conv_w", "ssd_conv_b", "ssd_dt_bias", "ssd_a_log",
               "ssd_d_skip", "ssd_gate_norm", "ssd_w_out", "kv_norm", "w_kv", "sb_norm", "sb_w_q", "sb_w_o", "ffn_norm",
               "ffn_w_up", "ffn_conv_w", "ffn_conv_b", "ffn_w_down", "final_norm")
WEIGHT_NAMES = INPUT_NAMES[1:]
SHARD_AXIS = dict(meta_tokens=1, ssd_norm=1, ssd_w_in=2, ssd_conv_w=2, ssd_conv_b=1, ssd_gate_norm=1, ssd_w_out=1, w_kv=1,
                  sb_w_q=1, sb_w_o=1, ffn_w_up=2, ffn_conv_w=2, ffn_w_down=1)
SMALL_SHARDED = ("meta_tokens", "ssd_norm", "ssd_conv_w", "ssd_conv_b", "ssd_gate_norm", "ffn_conv_w")
SHARDED = MATMUL_WEIGHTS + SMALL_SHARDED
REPLICATED = tuple(n for n in WEIGHT_NAMES if n not in SHARD_AXIS)


def _rows(shape):
    n = math.prod(shape)
    return -(-n // PACK_QUANTUM) * (PACK_QUANTUM // 128)


def _pack(arrs, dtype, lead=0):
    parts = []
    for a in arrs:
        ld = a.shape[:lead]
        n = math.prod(a.shape[lead:])
        f = a.reshape(ld + (n,)).astype(dtype)
        pad = _rows(a.shape[lead:]) * 128 - n
        if pad:
            f = jnp.pad(f, [(0, 0)] * lead + [(0, pad)])
        parts.append(f.reshape(ld + (-1, 128)))
    return jnp.concatenate(parts, axis=lead)


def _unpack(flat, shapes, lead=0):
    out, r0 = [], 0
    ld = flat.shape[:lead]
    for shp in shapes:
        rows, n = _rows(shp), math.prod(shp)
        piece = lax.slice_in_dim(flat, r0, r0 + rows, axis=lead).reshape(ld + (rows * 128,))
        out.append(lax.slice_in_dim(piece, 0, n, axis=lead).reshape(ld + tuple(shp)))
        r0 += rows
    return out


def _unshard(stacked, axis):
    a = jnp.moveaxis(stacked, 0, axis)
    shp = a.shape
    return a.reshape(shp[:axis] + (shp[axis] * shp[axis + 1],) + shp[axis + 2:])


def _to_shards(full, axis):
    shp = full.shape
    a = full.reshape(shp[:axis] + (N_DEV, shp[axis] // N_DEV) + shp[axis + 1:])
    return jnp.moveaxis(a, axis, 0)


def kernel(x, meta_tokens, ssd_norm, ssd_w_in, ssd_conv_w, ssd_conv_b, ssd_dt_bias, ssd_a_log, ssd_d_skip, ssd_gate_norm, ssd_w_out, kv_norm, w_kv, sb_norm, sb_w_q, sb_w_o, ffn_norm, ffn_w_up, ffn_conv_w, ffn_conv_b, ffn_w_down, final_norm, loss_target, m_meta_tokens, m_ssd_norm, m_ssd_w_in, m_ssd_conv_w, m_ssd_conv_b, m_ssd_dt_bias, m_ssd_a_log, m_ssd_d_skip, m_ssd_gate_norm, m_ssd_w_out, m_kv_norm, m_w_kv, m_sb_norm, m_sb_w_q, m_sb_w_o, m_ffn_norm, m_ffn_w_up, m_ffn_conv_w, m_ffn_conv_b, m_ffn_w_down, m_final_norm, v_meta_tokens, v_ssd_norm, v_ssd_w_in, v_ssd_conv_w, v_ssd_conv_b, v_ssd_dt_bias, v_ssd_a_log, v_ssd_d_skip, v_ssd_gate_norm, v_ssd_w_out, v_kv_norm, v_w_kv, v_sb_norm, v_sb_w_q, v_sb_w_o, v_ffn_norm, v_ffn_w_up, v_ffn_conv_w, v_ffn_conv_b, v_ffn_w_down, v_final_norm):
    local = dict(zip(WEIGHT_NAMES, (meta_tokens, ssd_norm, ssd_w_in, ssd_conv_w, ssd_conv_b, ssd_dt_bias, ssd_a_log, ssd_d_skip, ssd_gate_norm, ssd_w_out, kv_norm, w_kv, sb_norm, sb_w_q, sb_w_o, ffn_norm, ffn_w_up, ffn_conv_w, ffn_conv_b, ffn_w_down, final_norm)))
    mom = dict(zip(WEIGHT_NAMES, (m_meta_tokens, m_ssd_norm, m_ssd_w_in, m_ssd_conv_w, m_ssd_conv_b, m_ssd_dt_bias, m_ssd_a_log, m_ssd_d_skip, m_ssd_gate_norm, m_ssd_w_out, m_kv_norm, m_w_kv, m_sb_norm, m_sb_w_q, m_sb_w_o, m_ffn_norm, m_ffn_w_up, m_ffn_conv_w, m_ffn_conv_b, m_ffn_w_down, m_final_norm)))
    var = dict(zip(WEIGHT_NAMES, (v_meta_tokens, v_ssd_norm, v_ssd_w_in, v_ssd_conv_w, v_ssd_conv_b, v_ssd_dt_bias, v_ssd_a_log, v_ssd_d_skip, v_ssd_gate_norm, v_ssd_w_out, v_kv_norm, v_w_kv, v_sb_norm, v_sb_w_q, v_sb_w_o, v_ffn_norm, v_ffn_w_up, v_ffn_conv_w, v_ffn_conv_b, v_ffn_w_down, v_final_norm)))
    seq, d = x.shape[1], x.shape[2]
    cfg = _make_cfg(d, seq, ffn_w_down.shape[1] * N_DEV)

    big = _all_gather(_pack([local[n] for n in MATMUL_WEIGHTS], BF16), name="gather_weights")
    small = _all_gather(_pack([local[n] for n in SMALL_SHARDED], F32), name="gather_small")
    full = {n: local[n] for n in REPLICATED}
    for names, buf in ((MATMUL_WEIGHTS, big), (SMALL_SHARDED, small)):
        for n, stacked in zip(names, _unpack(buf, [local[n].shape for n in names], lead=1)):
            full[n] = _unshard(stacked, SHARD_AXIS[n])

    late = tuple(n for n in SHARDED if n not in EARLY_GRADS)
    for_peers = lambda gr, names: _pack([_to_shards(gr[n].reshape(full[n].shape), SHARD_AXIS[n]) for n in names], BF16,
                                        lead=1)
    loss, grad_x, grads, got_early = _local_step(x[0], loss_target[0], _prepare(full, cfg), cfg,
                                                 exchange=lambda done: for_peers(done, EARLY_GRADS))
    g_early = _sum_slots(got_early, name="sum_grads_early")
    g_late = _sum_slots(_all_to_all(for_peers(grads, late), name="scatter_grads"), name="sum_grads")
    rep = _all_gather(_pack([grads[n].reshape(local[n].shape) for n in REPLICATED], F32), name="gather_rep_grads")
    g_rep = _sum_slots(rep, name="sum_rep_grads")
    fill = jnp.zeros(((-(g_early.shape[0] + g_late.shape[0] + g_rep.shape[0])) % ADAM_ROWS, 128), F32)
    g_flat = jnp.concatenate([g_early, g_late, g_rep, fill], axis=0)

    order = EARLY_GRADS + late + REPLICATED
    flat = lambda src: jnp.concatenate([_pack([src[n] for n in order], F32), fill], axis=0)
    delta, new_m, new_v = _adamw(flat(local), g_flat, flat(mom), flat(var), name="adamw")
    shapes = [local[n].shape for n in order]
    pick = lambda buf: dict(zip(order, _unpack(buf, shapes)))
    g_out, d_out, m_out, v_out = pick(g_flat), pick(delta), pick(new_m), pick(new_v)
    loss = lax.psum(loss, MESH_AXES)
    return (loss, grad_x[None], *[g_out[n] for n in WEIGHT_NAMES], *[d_out[n] for n in WEIGHT_NAMES],
            *[m_out[n] for n in WEIGHT_NAMES], *[v_out[n] for n in WEIGHT_NAMES])
```

```python
import collections
import functools
import math

import jax
import jax.numpy as jnp
from jax import lax
from jax.experimental import pallas as pl
from jax.experimental.pallas import tpu as pltpu

F32 = jnp.float32
BF16 = jnp.bfloat16
NORM_EPS = 1e-6
N_META = 16
SSM_HEAD_DIM = 64
SSM_GROUPS = 4
SSM_STATE = 128
SSM_CONV = 4
SSM_CHUNK = 256
SB_HEAD_DIM = 64
FFN_CONV = 3
FFN_IL = 256
ATT_T = 256
ADAM_LR, ADAM_B1, ADAM_B2, ADAM_EPS, ADAM_WD, ADAM_STEP = 0.001, 0.9, 0.999, 1e-08, 0.01, 10
N_DEV = 8
VMEM_LIMIT = 56 * 1024 * 1024
MM_TILE = 1536

Cfg = collections.namedtuple("Cfg", "D SEQ LP PF DI H HG GW CONV_DIM DFF HS")


def _make_cfg(d_model, seq, d_ff):
    pf = (-N_META) % SSM_CHUNK
    lp = pf + N_META + seq
    assert lp % SSM_CHUNK == 0 and (pf + N_META) == SSM_CHUNK
    di = 2 * d_model
    h = di // SSM_HEAD_DIM
    return Cfg(D=d_model, SEQ=seq, LP=lp, PF=pf, DI=di, H=h, HG=h // SSM_GROUPS, GW=di // SSM_GROUPS,
               CONV_DIM=di + 2 * SSM_GROUPS * SSM_STATE, DFF=d_ff, HS=d_model // SB_HEAD_DIM)


def _tile(n, pref, mult):
    t = (min(pref, n) // mult) * mult
    while t > mult and n % t:
        t -= mult
    assert t >= mult and n % t == 0, (n, pref, mult)
    return t


def _cparams(sem):
    return pltpu.CompilerParams(dimension_semantics=sem, vmem_limit_bytes=VMEM_LIMIT)


def _dot(a, b):
    return jnp.dot(a, b, preferred_element_type=F32)


def _dot_nt(a, b):
    return lax.dot_general(a, b, (((1,), (1,)), ((), ())), preferred_element_type=F32)


def _dot_tn(a, b):
    return lax.dot_general(a, b, (((0,), (0,)), ((), ())), preferred_element_type=F32)


def _split3(v):
    hi = v.astype(BF16)
    r1 = v - hi.astype(F32)
    mid = r1.astype(BF16)
    lo = (r1 - mid.astype(F32)).astype(BF16)
    return hi, mid, lo


def _split2(v):
    hi = v.astype(BF16)
    lo = (v - hi.astype(F32)).astype(BF16)
    return hi, lo


def _dot3(a_f32, b_bf16):
    hi, mid, lo = _split3(a_f32)
    return _dot(hi, b_bf16) + _dot(mid, b_bf16) + _dot(lo, b_bf16)


def _dot3_left(a_bf16, b_f32):
    hi, mid, lo = _split3(b_f32)
    return _dot(a_bf16, hi) + _dot(a_bf16, mid) + _dot(a_bf16, lo)


def _sigmoid(x):
    return 0.5 * jnp.tanh(0.5 * x) + 0.5


def _softplus(x):
    return jnp.maximum(x, 0.0) + jnp.log(1.0 + jnp.exp(-jnp.abs(x)))


def _row_ids(i, tm, shape_cols=1):
    return i * tm + lax.broadcasted_iota(jnp.int32, (tm, shape_cols), 0)


def _norm_mm(h, g, w, *, out_dtype, name, scale=None):
    lp, d = h.shape
    n = w.shape[1]
    tm = _tile(lp, 768, 8)
    tn = _tile(n, MM_TILE, 128)

    def body(h_ref, g_ref, w_ref, y_ref, u_ref):
        @pl.when(pl.program_id(1) == 0)
        def _():
            x = h_ref[...]
            r = lax.rsqrt(jnp.mean(x * x, axis=-1, keepdims=True) + NORM_EPS)
            u_ref[...] = (x * r * g_ref[...]).astype(BF16)
        y = _dot(u_ref[...], w_ref[...])
        if scale is not None:
            y = y * scale
        y_ref[...] = y.astype(out_dtype)

    return pl.pallas_call(
        body, name=name, grid=(lp // tm, n // tn),
        in_specs=[pl.BlockSpec((tm, d), lambda i, j: (i, 0)), pl.BlockSpec((1, d), lambda i, j: (0, 0)),
                  pl.BlockSpec((d, tn), lambda i, j: (0, j))],
        out_specs=[pl.BlockSpec((tm, tn), lambda i, j: (i, j)), pl.BlockSpec((tm, d), lambda i, j: (i, 0))],
        out_shape=[jax.ShapeDtypeStruct((lp, n), out_dtype), jax.ShapeDtypeStruct((lp, d), BF16)],
        compiler_params=_cparams(("arbitrary", "arbitrary")),
    )(h, g, w)


def _mm_res(a, w, h, *, pf, name):
    lp, k = a.shape
    d = w.shape[1]
    tm = _tile(lp, 768, 8)

    def body(a_ref, w_ref, h_ref, o_ref):
        y = _dot(a_ref[...].astype(BF16), w_ref[...])
        rows = _row_ids(pl.program_id(0), tm)
        o_ref[...] = h_ref[...] + jnp.where(rows >= pf, y, 0.0)

    return pl.pallas_call(
        body, name=name, grid=(lp // tm,),
        in_specs=[pl.BlockSpec((tm, k), lambda i: (i, 0)), pl.BlockSpec((k, d), lambda i: (0, 0)),
                  pl.BlockSpec((tm, d), lambda i: (i, 0))],
        out_specs=pl.BlockSpec((tm, d), lambda i: (i, 0)),
        out_shape=jax.ShapeDtypeStruct((lp, d), F32),
        compiler_params=_cparams(("arbitrary",)),
    )(a, w, h)


def _mm_nt(dy, w, *, out_dtype, pf, name):
    lp, n = dy.shape
    k = w.shape[0]
    tm = _tile(lp, 768, 8)
    tk = _tile(k, MM_TILE, 128)

    def body(dy_ref, w_ref, o_ref):
        y = _dot_nt(dy_ref[...].astype(BF16), w_ref[...])
        rows = _row_ids(pl.program_id(0), tm)
        o_ref[...] = jnp.where(rows >= pf, y, 0.0).astype(out_dtype)

    return pl.pallas_call(
        body, name=name, grid=(lp // tm, k // tk),
        in_specs=[pl.BlockSpec((tm, n), lambda i, j: (i, 0)), pl.BlockSpec((tk, n), lambda i, j: (j, 0))],
        out_specs=pl.BlockSpec((tm, tk), lambda i, j: (i, j)),
        out_shape=jax.ShapeDtypeStruct((lp, k), out_dtype),
        compiler_params=_cparams(("arbitrary", "arbitrary")),
    )(dy, w)


def _mm_nt_normbwd(dy, w, h, g, dh_in, *, pf, name, scale=None):
    lp, n = dy.shape
    d = w.shape[0]
    tm = _tile(lp, 768, 8)
    tn = _tile(n, MM_TILE, 128)
    nj = n // tn

    def body(dy_ref, w_ref, h_ref, g_ref, dhin_ref, dh_ref, dg_ref, acc_ref):
        i, j = pl.program_id(0), pl.program_id(1)

        @pl.when(j == 0)
        def _():
            acc_ref[...] = jnp.zeros_like(acc_ref)

        @pl.when((i == 0) & (j == 0))
        def _():
            dg_ref[...] = jnp.zeros_like(dg_ref)

        acc_ref[...] += _dot_nt(dy_ref[...].astype(BF16), w_ref[...])

        @pl.when(j == nj - 1)
        def _():
            du = acc_ref[...]
            if scale is not None:
                du = du * scale
            x = h_ref[...]
            r = lax.rsqrt(jnp.mean(x * x, axis=-1, keepdims=True) + NORM_EPS)
            xhat = x * r
            dg_ref[...] += jnp.sum(du * xhat, axis=0, keepdims=True)
            dxh = du * g_ref[...]
            dx = r * (dxh - xhat * jnp.mean(dxh * xhat, axis=-1, keepdims=True))
            rows = _row_ids(i, tm)
            dh_ref[...] = jnp.where(rows >= pf, dhin_ref[...] + dx, 0.0)

    return pl.pallas_call(
        body, name=name, grid=(lp // tm, nj),
        in_specs=[pl.BlockSpec((tm, tn), lambda i, j: (i, j)), pl.BlockSpec((d, tn), lambda i, j: (0, j)),
                  pl.BlockSpec((tm, d), lambda i, j: (i, 0)), pl.BlockSpec((1, d), lambda i, j: (0, 0)),
                  pl.BlockSpec((tm, d), lambda i, j: (i, 0))],
        out_specs=[pl.BlockSpec((tm, d), lambda i, j: (i, 0)), pl.BlockSpec((1, d), lambda i, j: (0, 0))],
        out_shape=[jax.ShapeDtypeStruct((lp, d), F32), jax.ShapeDtypeStruct((1, d), F32)],
        scratch_shapes=[pltpu.VMEM((tm, d), F32)],
        compiler_params=_cparams(("arbitrary", "arbitrary")),
    )(dy, w, h, g, dh_in)


def _mm_tn(a, b, *, name, scale=None):
    lp, k = a.shape
    n = b.shape[1]
    tm = _tile(lp, 768, 8)
    tk = _tile(k, MM_TILE, 128)
    tn = _tile(n, MM_TILE, 128)
    nm = lp // tm

    def body(a_ref, b_ref, o_ref, acc_ref):
        m = pl.program_id(2)

        @pl.when(m == 0)
        def _():
            acc_ref[...] = jnp.zeros_like(acc_ref)

        acc_ref[...] += _dot_tn(a_ref[...].astype(BF16), b_ref[...].astype(BF16))

        @pl.when(m == nm - 1)
        def _():
            o_ref[...] = acc_ref[...] if scale is None else acc_ref[...] * scale

    return pl.pallas_call(
        body, name=name, grid=(k // tk, n // tn, nm),
        in_specs=[pl.BlockSpec((tm, tk), lambda i, j, m: (m, i)), pl.BlockSpec((tm, tn), lambda i, j, m: (m, j))],
        out_specs=pl.BlockSpec((tk, tn), lambda i, j, m: (i, j)),
        out_shape=jax.ShapeDtypeStruct((k, n), F32),
        scratch_shapes=[pltpu.VMEM((tk, tn), F32)],
        compiler_params=_cparams(("arbitrary", "arbitrary", "arbitrary")),
    )(a, b)


HALO = 8


CONV_STRIP = 32


def _conv_load(ext_ref, x_ref):
    @pl.when(pl.program_id(1) == 0)
    def _():
        ext_ref[0:HALO, :] = jnp.zeros((HALO, ext_ref.shape[1]), F32)

    ext_ref[HALO:, :] = x_ref[...]


def _conv_strip(ext_ref, w_ref, b_ref, r0, width):
    taps = [ext_ref[pl.ds(r0 + HALO - (width - 1 - k), CONV_STRIP), :] for k in range(width)]
    acc = b_ref[...] + taps[0] * w_ref[0:1, :]
    for k in range(1, width):
        acc = acc + taps[k] * w_ref[k:k + 1, :]
    return acc, taps


def _strip_live(i, tm, r0, pf):
    return i * tm + r0 + lax.broadcasted_iota(jnp.int32, (CONV_STRIP, 1), 0) >= pf


def _fold8(x):
    out = x[0:8]
    for r in range(8, x.shape[0], 8):
        out = out + x[r:r + 8]
    return out


def _conv_grad_flush(acc_ref, dw_ref, db_ref, width, last):
    @pl.when(last)
    def _():
        for k in range(width):
            dw_ref[k:k + 1, :] = jnp.sum(acc_ref[8 * k:8 * k + 8, :], axis=0, keepdims=True)
        db_ref[...] = jnp.sum(acc_ref[8 * width:8 * width + 8, :], axis=0, keepdims=True)


def _ssd_conv_fwd(zx, cw, cb, *, cfg, name):
    lp, cd, di = cfg.LP, cfg.CONV_DIM, cfg.DI
    tc = _tile(math.gcd(di, cd), 512, 128)
    tm = _tile(lp, 768, 8)
    off = di // tc

    def body(x_ref, w_ref, b_ref, o_ref, ext_ref):
        i = pl.program_id(1)
        _conv_load(ext_ref, x_ref)
        for r0 in range(0, tm, CONV_STRIP):
            acc, _ = _conv_strip(ext_ref, w_ref, b_ref, r0, SSM_CONV)
            o_ref[pl.ds(r0, CONV_STRIP), :] = jnp.where(_strip_live(i, tm, r0, cfg.PF), acc * _sigmoid(acc), 0.0)
        ext_ref[0:HALO, :] = x_ref[tm - HALO:tm, :]

    return pl.pallas_call(
        body, name=name, grid=(cd // tc, lp // tm),
        in_specs=[pl.BlockSpec((tm, tc), lambda j, i: (i, j + off)), pl.BlockSpec((SSM_CONV, tc), lambda j, i: (0, j)),
                  pl.BlockSpec((1, tc), lambda j, i: (0, j))],
        out_specs=pl.BlockSpec((tm, tc), lambda j, i: (i, j)),
        out_shape=jax.ShapeDtypeStruct((lp, cd), F32),
        scratch_shapes=[pltpu.VMEM((tm + HALO, tc), F32)],
        compiler_params=_cparams(("arbitrary", "arbitrary")),
    )(zx, cw, cb)


def _ssd_conv_bwd_pre(zx, dxbc, cw, cb, *, cfg, name):
    lp, cd, di = cfg.LP, cfg.CONV_DIM, cfg.DI
    tc = _tile(math.gcd(di, cd), 512, 128)
    tm = _tile(lp, 768, 8)
    off = di // tc

    def body(x_ref, d_ref, w_ref, b_ref, dc_ref, dw_ref, db_ref, ext_ref, acc_ref):
        i = pl.program_id(1)

        @pl.when(i == 0)
        def _():
            acc_ref[...] = jnp.zeros_like(acc_ref)

        _conv_load(ext_ref, x_ref)
        for r0 in range(0, tm, CONV_STRIP):
            sl = pl.ds(r0, CONV_STRIP)
            c, taps = _conv_strip(ext_ref, w_ref, b_ref, r0, SSM_CONV)
            sg = _sigmoid(c)
            dc = jnp.where(_strip_live(i, tm, r0, cfg.PF), d_ref[sl, :] * sg * (1.0 + c * (1.0 - sg)), 0.0)
            dc_ref[sl, :] = dc
            for k in range(SSM_CONV):
                acc_ref[8 * k:8 * k + 8, :] += _fold8(dc * taps[k])
            acc_ref[8 * SSM_CONV:, :] += _fold8(dc)
        ext_ref[0:HALO, :] = x_ref[tm - HALO:tm, :]
        _conv_grad_flush(acc_ref, dw_ref, db_ref, SSM_CONV, i == lp // tm - 1)

    return pl.pallas_call(
        body, name=name, grid=(cd // tc, lp // tm),
        in_specs=[pl.BlockSpec((tm, tc), lambda j, i: (i, j + off)), pl.BlockSpec((tm, tc), lambda j, i: (i, j)),
                  pl.BlockSpec((SSM_CONV, tc), lambda j, i: (0, j)), pl.BlockSpec((1, tc), lambda j, i: (0, j))],
        out_specs=[pl.BlockSpec((tm, tc), lambda j, i: (i, j)), pl.BlockSpec((SSM_CONV, tc), lambda j, i: (0, j)),
                   pl.BlockSpec((1, tc), lambda j, i: (0, j))],
        out_shape=[jax.ShapeDtypeStruct((lp, cd), F32), jax.ShapeDtypeStruct((SSM_CONV, cd), F32),
                   jax.ShapeDtypeStruct((1, cd), F32)],
        scratch_shapes=[pltpu.VMEM((tm + HALO, tc), F32), pltpu.VMEM((8 * (SSM_CONV + 1), tc), F32)],
        compiler_params=_cparams(("arbitrary", "arbitrary")),
    )(zx, dxbc, cw, cb)


def _ffn_conv_fwd(up, cw, cb, *, cfg, name):
    lp, dff = cfg.LP, cfg.DFF
    tc = 2 * FFN_IL
    tm = _tile(lp, 768, 8)

    def body(x_ref, w_ref, b_ref, o_ref, ext_ref):
        i = pl.program_id(1)
        _conv_load(ext_ref, x_ref)
        for r0 in range(0, tm, CONV_STRIP):
            hc, _ = _conv_strip(ext_ref, w_ref, b_ref, r0, FFN_CONV)
            gc, vc = hc[:, :FFN_IL], hc[:, FFN_IL:]
            act = jnp.where(_strip_live(i, tm, r0, cfg.PF), gc * _sigmoid(gc) * vc, 0.0)
            o_ref[pl.ds(r0, CONV_STRIP), :] = act.astype(BF16)
        ext_ref[0:HALO, :] = x_ref[tm - HALO:tm, :]

    return pl.pallas_call(
        body, name=name, grid=(dff // FFN_IL, lp // tm),
        in_specs=[pl.BlockSpec((tm, tc), lambda j, i: (i, j)), pl.BlockSpec((FFN_CONV, tc), lambda j, i: (0, j)),
                  pl.BlockSpec((1, tc), lambda j, i: (0, j))],
        out_specs=pl.BlockSpec((tm, FFN_IL), lambda j, i: (i, j)),
        out_shape=jax.ShapeDtypeStruct((lp, dff), BF16),
        scratch_shapes=[pltpu.VMEM((tm + HALO, tc), F32)],
        compiler_params=_cparams(("arbitrary", "arbitrary")),
    )(up, cw, cb)


def _ffn_conv_bwd_pre(up, dact, cw, cb, *, cfg, name):
    lp, dff = cfg.LP, cfg.DFF
    tc = 2 * FFN_IL
    tm = _tile(lp, 768, 8)

    def body(x_ref, d_ref, w_ref, b_ref, dc_ref, dw_ref, db_ref, ext_ref, acc_ref):
        i = pl.program_id(1)

        @pl.when(i == 0)
        def _():
            acc_ref[...] = jnp.zeros_like(acc_ref)

        _conv_load(ext_ref, x_ref)
        for r0 in range(0, tm, CONV_STRIP):
            sl = pl.ds(r0, CONV_STRIP)
            hc, taps = _conv_strip(ext_ref, w_ref, b_ref, r0, FFN_CONV)
            gc, vc = hc[:, :FFN_IL], hc[:, FFN_IL:]
            sg = _sigmoid(gc)
            da = jnp.where(_strip_live(i, tm, r0, cfg.PF), d_ref[sl, :].astype(F32), 0.0)
            dc = jnp.concatenate([da * vc * sg * (1.0 + gc * (1.0 - sg)), da * gc * sg], axis=1)
            dc_ref[sl, :] = dc
            for k in range(FFN_CONV):
                acc_ref[8 * k:8 * k + 8, :] += _fold8(dc * taps[k])
            acc_ref[8 * FFN_CONV:, :] += _fold8(dc)
        ext_ref[0:HALO, :] = x_ref[tm - HALO:tm, :]
        _conv_grad_flush(acc_ref, dw_ref, db_ref, FFN_CONV, i == lp // tm - 1)

    return pl.pallas_call(
        body, name=name, grid=(dff // FFN_IL, lp // tm),
        in_specs=[pl.BlockSpec((tm, tc), lambda j, i: (i, j)), pl.BlockSpec((tm, FFN_IL), lambda j, i: (i, j)),
                  pl.BlockSpec((FFN_CONV, tc), lambda j, i: (0, j)), pl.BlockSpec((1, tc), lambda j, i: (0, j))],
        out_specs=[pl.BlockSpec((tm, tc), lambda j, i: (i, j)), pl.BlockSpec((FFN_CONV, tc), lambda j, i: (0, j)),
                   pl.BlockSpec((1, tc), lambda j, i: (0, j))],
        out_shape=[jax.ShapeDtypeStruct((lp, 2 * dff), F32), jax.ShapeDtypeStruct((FFN_CONV, 2 * dff), F32),
                   jax.ShapeDtypeStruct((1, 2 * dff), F32)],
        scratch_shapes=[pltpu.VMEM((tm + HALO, tc), F32), pltpu.VMEM((8 * (FFN_CONV + 1), tc), F32)],
        compiler_params=_cparams(("arbitrary", "arbitrary")),
    )(up, dact, cw, cb)


def _head_select(lane, head):
    return ((lane >= head * SSM_HEAD_DIM) & (lane < (head + 1) * SSM_HEAD_DIM)).astype(BF16)


def _ssd_terms(x, bm, cm, dtr, bias, alog, dsk, valid, cfg):
    q, hg, gw, p = SSM_CHUNK, cfg.HG, cfg.GW, SSM_HEAD_DIM
    t = {}
    e_mat = _head_select(lax.broadcasted_iota(jnp.int32, (hg, gw), 1), lax.broadcasted_iota(jnp.int32, (hg, gw), 0))
    li = lax.broadcasted_iota(jnp.int32, (q, q), 0)
    si = lax.broadcasted_iota(jnp.int32, (q, q), 1)
    tri = li >= si
    tril = tri.astype(BF16)
    triu = (li <= si).astype(BF16)
    pre = dtr + bias
    dtv = jnp.where(valid, _softplus(pre), 0.0)
    a_head = -jnp.exp(alog)
    a = dtv * a_head
    cs = _dot3_left(tril, a)
    hi, mid, lo = _split3(a)
    cst = _dot_tn(hi, triu) + _dot_tn(mid, triu) + _dot_tn(lo, triu)
    cs_last = jnp.sum(a, axis=0, keepdims=True)
    dte = jnp.exp(jnp.minimum(cs_last - cs, 0.0))
    ecs = jnp.exp(cs)
    t.update(e_mat=e_mat, tri=tri, tril=tril, triu=triu, pre=pre, dtv=dtv, a_head=a_head, cs=cs, cst=cst,
             dec=jnp.exp(cs_last), dte=dte, ecs=ecs)
    t["dtv_x"] = _dot3(dtv, e_mat)
    t["ecs_x"] = _dot3(ecs, e_mat)
    t["dte_x"] = _dot3(dte, e_mat)
    t["dec_x"] = _dot3(t["dec"], e_mat)
    t["d_x"] = _dot3(dsk, e_mat)
    t["xdt"] = x * t["dtv_x"]
    t["gm"] = _dot_nt(cm.astype(BF16), bm.astype(BF16))
    return t


def _ssd_decay(t, e):
    diff = t["cs"][:, e:e + 1] - t["cst"][e:e + 1, :]
    return jnp.where(t["tri"], jnp.exp(jnp.minimum(diff, 0.0)), 0.0)


def _head_mask(b):
    lane = lax.broadcasted_iota(jnp.int32, (1, 2 * SSM_HEAD_DIM), 1)
    return (lane >= SSM_HEAD_DIM * b) & (lane < SSM_HEAD_DIM * (b + 1))


def _ssd_specs(cfg):
    q, g, gw, ns, hg, di = SSM_CHUNK, SSM_GROUPS, cfg.GW, SSM_STATE, cfg.HG, cfg.DI
    return dict(
        x=pl.BlockSpec((q, gw), lambda j, c: (c, j)),
        b=pl.BlockSpec((q, ns), lambda j, c: (c, di // ns + j)),
        c=pl.BlockSpec((q, ns), lambda j, c: (c, di // ns + g + j)),
        z=pl.BlockSpec((q, gw), lambda j, c: (c, j)),
        dtr=pl.BlockSpec((None, q, hg), lambda j, c: (j, c, 0)),
        prm=pl.BlockSpec((None, 8, hg), lambda j, c: (j, 0, 0)),
        gate=pl.BlockSpec((1, gw), lambda j, c: (0, j)),
        st=pl.BlockSpec((None, None, ns, gw), lambda j, c: (c, j, 0, 0)),
    )


GATHER_FORWARD_STEPS = 16


def _ssd_fwd(xbc, zx, dtr, prm, gate_g, *, cfg, name, gather=None):
    q, g, gw, ns, hg, p = SSM_CHUNK, SSM_GROUPS, cfg.GW, SSM_STATE, cfg.HG, SSM_HEAD_DIM
    nc = cfg.LP // q
    sp = _ssd_specs(cfg)

    def body(x_ref, b_ref, c_ref, z_ref, dtr_ref, prm_ref, gg_ref, y_ref, hn_ref, st_ref, s_ref):
        c = pl.program_id(1)

        @pl.when(c == 0)
        def _():
            s_ref[...] = jnp.zeros((ns, gw), F32)

        s_prev = s_ref[...]
        st_ref[...] = s_prev
        x, bm, cm = x_ref[...], b_ref[...], c_ref[...]
        valid = _row_ids(c, q) >= cfg.PF
        t = _ssd_terms(x, bm, cm, dtr_ref[...], prm_ref[0:1, :], prm_ref[1:2, :], prm_ref[2:3, :], valid, cfg)
        parts = []
        for pr in range(hg // 2):
            xp = t["xdt"][:, 2 * p * pr:2 * p * (pr + 1)]
            acc = None
            for b in range(2):
                m = (t["gm"] * _ssd_decay(t, 2 * pr + b)).astype(BF16)
                d = _dot(m, jnp.where(_head_mask(b), xp, 0.0).astype(BF16))
                acc = d if acc is None else acc + d
            parts.append(acc)
        y_diag = jnp.concatenate(parts, axis=1)
        y_off = _dot(cm.astype(BF16), s_prev.astype(BF16)) * t["ecs_x"]
        y = y_diag + y_off + x * t["d_x"]
        y_ref[...] = y
        s_ref[...] = s_prev * t["dec_x"] + _dot_tn(bm.astype(BF16), (t["xdt"] * t["dte_x"]).astype(BF16))
        z = z_ref[...]
        hgate = y * z * _sigmoid(z)
        r = lax.rsqrt(jnp.mean(hgate * hgate, axis=-1, keepdims=True) + NORM_EPS)
        hn_ref[...] = (hgate * r * gg_ref[...]).astype(BF16)

    in_specs = [sp["x"], sp["b"], sp["c"], sp["z"], sp["dtr"], sp["prm"], sp["gate"]]
    out_specs = [sp["x"], sp["x"], sp["st"]]
    out_shape = [jax.ShapeDtypeStruct((cfg.LP, cfg.DI), F32), jax.ShapeDtypeStruct((cfg.LP, cfg.DI), BF16),
                 jax.ShapeDtypeStruct((nc, g, ns, gw), F32)]
    scratch = [pltpu.VMEM((ns, gw), F32)]
    args = (xbc, xbc, xbc, zx, dtr, prm, gate_g)
    if gather is None:
        kernel_body = body
    else:
        n_in, n_out = len(in_specs), len(out_specs)
        fwd_c = max(nc - GATHER_FORWARD_STEPS, 0)

        def kernel_body(*refs):
            ins, blk_ref = refs[:n_in], refs[n_in]
            outs, all_ref = refs[n_in + 1:n_in + 1 + n_out], refs[n_in + 1 + n_out]
            s_ref = refs[n_in + n_out + 2]
            sems = refs[n_in + n_out + 3:]
            j, c = pl.program_id(0), pl.program_id(1)

            @pl.when((j == 0) & (c == 0))
            def _():
                _Gather(blk_ref, all_ref, *sems).start()

            body(*ins, *outs, s_ref)

            @pl.when((j == g - 1) & (c == fwd_c))
            def _():
                _Gather(blk_ref, all_ref, *sems).forward()

            @pl.when((j == g - 1) & (c == nc - 1))
            def _():
                _Gather(blk_ref, all_ref, *sems).finish()

        in_specs, out_specs = in_specs + [HBM_SPEC], out_specs + [HBM_SPEC]
        out_shape = out_shape + [jax.ShapeDtypeStruct((N_DEV,) + gather.shape, gather.dtype)]
        scratch = scratch + list(GATHER_SEMS)
        args = args + (gather,)
    return pl.pallas_call(
        kernel_body, name=name, grid=(g, nc), in_specs=in_specs, out_specs=out_specs, out_shape=out_shape,
        scratch_shapes=scratch, compiler_params=_cparams(("arbitrary", "arbitrary")),
    )(*args)


def _ssd_bwd(xbc, zx, dtr, prm, gate_g, y, states, dhn, *, cfg, name, send=None):
    q, g, gw, ns, hg, p = SSM_CHUNK, SSM_GROUPS, cfg.GW, SSM_STATE, cfg.HG, SSM_HEAD_DIM
    nc = cfg.LP // q
    sp = _ssd_specs(cfg)
    rev = lambda spec: pl.BlockSpec(spec.block_shape, (lambda f: (lambda j, c: f(j, nc - 1 - c)))(spec.index_map))
    bc_spec = pl.BlockSpec((q, ns), lambda j, c: (nc - 1 - c, j))

    def body(x_ref, b_ref, c_ref, z_ref, dtr_ref, prm_ref, gg_ref, y_ref, st_ref, dhn_ref,
             dx_ref, db_ref, dc_ref, dz_ref, ddtr_ref, dprm_ref, dgg_ref, ds_ref):
        ci = pl.program_id(1)
        c = nc - 1 - ci

        @pl.when(ci == 0)
        def _():
            ds_ref[...] = jnp.zeros((ns, gw), F32)
            dprm_ref[...] = jnp.zeros((8, hg), F32)
            dgg_ref[...] = jnp.zeros((8, gw), F32)

        ds_next = ds_ref[...]
        s_prev = st_ref[...]
        x, bm, cm = x_ref[...], b_ref[...], c_ref[...]
        valid = _row_ids(c, q) >= cfg.PF
        t = _ssd_terms(x, bm, cm, dtr_ref[...], prm_ref[0:1, :], prm_ref[1:2, :], prm_ref[2:3, :], valid, cfg)
        et_mat = _head_select(lax.broadcasted_iota(jnp.int32, (gw, hg), 0), lax.broadcasted_iota(jnp.int32, (gw, hg), 1))
        heads = lambda v: _dot3(v, et_mat)
        yv, z = y_ref[...], z_ref[...]
        sz = _sigmoid(z)
        silu = z * sz
        hgate = yv * silu
        r = lax.rsqrt(jnp.mean(hgate * hgate, axis=-1, keepdims=True) + NORM_EPS)
        hhat = hgate * r
        dhn = dhn_ref[...]
        dgg = jnp.sum(dhn * hhat, axis=0, keepdims=True)
        dhh = dhn * gg_ref[...]
        dhgate = r * (dhh - hhat * jnp.mean(dhh * hhat, axis=-1, keepdims=True))
        dy = dhgate * silu
        dz_ref[...] = dhgate * yv * sz * (1.0 + z * (1.0 - sz))
        bmb, cmb = bm.astype(BF16), cm.astype(BF16)
        xdt = t["xdt"]
        dye = dy * t["ecs_x"]
        dxdt_state = _dot(bmb, ds_next.astype(BF16)) * t["dte_x"]
        dg_acc = None
        dparts = []
        li = lax.broadcasted_iota(jnp.int32, (q, q), 0)
        si = lax.broadcasted_iota(jnp.int32, (q, q), 1)
        head_id = lax.broadcasted_iota(jnp.int32, (1, hg), 1)
        da = None
        for pr in range(hg // 2):
            sl = slice(2 * p * pr, 2 * p * (pr + 1))
            xp, dyp = xdt[:, sl], dy[:, sl]
            acc = None
            for b in range(2):
                lm = _ssd_decay(t, 2 * pr + b)
                m = (t["gm"] * lm).astype(BF16)
                dym = jnp.where(_head_mask(b), dyp, 0.0).astype(BF16)
                d = _dot_tn(m, dym)
                acc = d if acc is None else acc + d
                dm = _dot_nt(dym, xp.astype(BF16)) * lm
                dg_acc = dm if dg_acc is None else dg_acc + dm
                corner = _dot(t["triu"], (dm * t["gm"]).astype(BF16))
                da_e = jnp.sum(jnp.where(si < li, corner, 0.0), axis=1, keepdims=True)
                da_e = da_e * (head_id == 2 * pr + b).astype(F32)
                da = da_e if da is None else da + da_e
            dparts.append(acc)
        dxdt = jnp.concatenate(dparts, axis=1) + dxdt_state
        dgb = dg_acc.astype(BF16)
        dc_ref[...] = _dot(dgb, bmb) + _dot_nt(dye.astype(BF16), s_prev.astype(BF16))
        xde = (xdt * t["dte_x"]).astype(BF16)
        db_ref[...] = _dot_tn(dgb, cmb) + _dot_nt(xde, ds_next.astype(BF16))
        ds_ref[...] = ds_next * t["dec_x"] + _dot_tn(cmb, dye.astype(BF16))
        y_off = _dot(cmb, s_prev.astype(BF16)) * t["ecs_x"]
        tril_strict = (li > si).astype(BF16)
        da = da + _dot3_left(t["triu"], heads(dy * y_off)) + _dot3_left(tril_strict, heads(xdt * dxdt_state)) \
            + t["dec"] * heads(jnp.sum(ds_next * s_prev, axis=0, keepdims=True))
        ddtv = da * t["a_head"] + heads(dxdt * x)
        ddtr = jnp.where(valid, ddtv * _sigmoid(t["pre"]), 0.0)
        ddtr_ref[...] = ddtr
        dx_ref[...] = dxdt * t["dtv_x"] + dy * t["d_x"]
        dalog = jnp.sum(da * t["dtv"], axis=0, keepdims=True) * t["a_head"]
        dprm_ref[0:1, :] += jnp.sum(ddtr, axis=0, keepdims=True)
        dprm_ref[1:2, :] += dalog
        dprm_ref[2:3, :] += heads(jnp.sum(dy * x, axis=0, keepdims=True))
        dgg_ref[0:1, :] += dgg

    in_specs = [rev(sp["x"]), rev(sp["b"]), rev(sp["c"]), rev(sp["z"]), rev(sp["dtr"]), sp["prm"], sp["gate"],
                rev(sp["x"]), rev(sp["st"]), rev(sp["x"])]
    out_specs = [rev(sp["x"]), bc_spec, bc_spec, rev(sp["x"]), rev(sp["dtr"]),
                 pl.BlockSpec((None, 8, hg), lambda j, c: (j, 0, 0)), pl.BlockSpec((None, 8, gw), lambda j, c: (j, 0, 0))]
    out_shape = [jax.ShapeDtypeStruct((cfg.LP, cfg.DI), F32), jax.ShapeDtypeStruct((cfg.LP, g * ns), F32),
                 jax.ShapeDtypeStruct((cfg.LP, g * ns), F32), jax.ShapeDtypeStruct((cfg.LP, cfg.DI), F32),
                 jax.ShapeDtypeStruct((g, cfg.LP, hg), F32), jax.ShapeDtypeStruct((g, 8, hg), F32),
                 jax.ShapeDtypeStruct((g, 8, gw), F32)]
    scratch = [pltpu.VMEM((ns, gw), F32)]
    args = (xbc, xbc, xbc, zx, dtr, prm, gate_g, y, states, dhn)
    if send is None:
        kernel_body = body
    else:
        n_in, n_out = len(in_specs), len(out_specs)

        def kernel_body(*refs):
            ins, send_ref = refs[:n_in], refs[n_in]
            outs, recv_ref = refs[n_in + 1:n_in + 1 + n_out], refs[n_in + 1 + n_out]
            ds_ref, send_sems, recv_sems, local_sem = refs[n_in + n_out + 2:]
            copies = lambda: _all_to_all_copies(send_ref, recv_ref, send_sems, recv_sems, local_sem)

            @pl.when((pl.program_id(0) == 0) & (pl.program_id(1) == 0))
            def _():
                for cp in copies():
                    cp.start()

            body(*ins, *outs, ds_ref)

            @pl.when((pl.program_id(0) == g - 1) & (pl.program_id(1) == nc - 1))
            def _():
                for cp in copies():
                    cp.wait()

        in_specs, out_specs = in_specs + [HBM_SPEC], out_specs + [HBM_SPEC]
        out_shape = out_shape + [jax.ShapeDtypeStruct(send.shape, send.dtype)]
        scratch = scratch + [pltpu.SemaphoreType.DMA((N_DEV - 1,)), pltpu.SemaphoreType.DMA((N_DEV - 1,)),
                             pltpu.SemaphoreType.DMA]
        args = args + (send,)
    return pl.pallas_call(
        kernel_body, name=name, grid=(g, nc), in_specs=in_specs, out_specs=out_specs, out_shape=out_shape,
        scratch_shapes=scratch, compiler_params=_cparams(("arbitrary", "arbitrary")),
    )(*args)


ATT_STRIP = 32
ATT_NT = 2
ATT_DEAD = -120.0
ATT_NEG = -1e9
LOG2E = 1.4426950408889634
LN2 = 0.6931471805599453


def _att_init(u_ref, bias_ref, qb, cfg):
    @pl.when(qb == 0)
    def _():
        t = ATT_T
        rows = lax.broadcasted_iota(jnp.int32, (t, t), 0)
        cols = lax.broadcasted_iota(jnp.int32, (t, t), 1)
        u_ref[...] = (rows > cols).astype(BF16)
        pad = cols < cfg.PF
        bias_ref[0] = jnp.zeros((t, t), F32)
        bias_ref[1] = jnp.where(cols >= rows, ATT_NEG, 0.0)
        bias_ref[2] = jnp.where(pad, ATT_NEG, 0.0)
        bias_ref[3] = jnp.full((t, t), ATT_NEG, F32)
        bias_ref[4] = jnp.where((cols >= rows) | pad, ATT_NEG, 0.0)


def _att_tile(kv_ref, kb, qb):
    t = ATT_T
    off = pl.multiple_of(jnp.maximum(kb, 0) * t, t)
    k = kv_ref[pl.ds(off, t), 0:2 * SB_HEAD_DIM]
    v = kv_ref[pl.ds(off, t), 2 * SB_HEAD_DIM:]
    kind = jnp.where(kb < 0, 3, jnp.where(kb == qb, jnp.where(qb == 0, 4, 1), jnp.where(kb == 0, 2, 0)))
    return k, v, kind


def _att_trips(qb):
    n = qb + ATT_NT
    return lax.shift_right_logical(n, 1) if ATT_NT == 2 else lax.div(n, ATT_NT)


def _att_walk(qb, trip, carry_ref):
    def cond(st):
        j, alive = st
        return (j < _att_trips(qb)) & alive

    def step(st):
        j, _ = st
        trip(j, None)
        return j + 1, jnp.max(carry_ref[...]) > ATT_DEAD

    lax.while_loop(cond, step, (0, True))


def _att_carry(carry_ref, rsum_ref, u, b):
    carry = carry_ref[b]
    for up in range(u):
        carry = carry + rsum_ref[2 * up + b]
    return carry


def _att_scores_strip(s_ref, bias_ref, kind, b, r0):
    sl = pl.ds(r0, ATT_STRIP)
    s = s_ref[b, sl, :] + bias_ref[kind, sl, :]
    l1p = jnp.log2(1.0 + jnp.exp2(jnp.abs(s) * (-LOG2E))) * LN2
    ls = jnp.minimum(s, 0.0) - l1p
    return ls, ls - s


def _attn_fwd(q, kv, *, cfg, name):
    t, lp, d, rs, nch = ATT_T, cfg.LP, cfg.D, ATT_STRIP, 2 * ATT_NT
    hp, nq = d // (2 * SB_HEAD_DIM), lp // ATT_T
    scale = SB_HEAD_DIM ** -0.5

    def body(q_ref, kv_ref, o_ref, u_ref, bias_ref, s_ref, hi_ref, lo_ref, cum_ref, w_ref, cb_ref, acc_ref, carry_ref,
             rsum_ref):
        qb = pl.program_id(1)
        _att_init(u_ref, bias_ref, qb, cfg)
        lo_half = lax.broadcasted_iota(jnp.int32, (1, 2 * SB_HEAD_DIM), 1) < SB_HEAD_DIM
        q2 = q_ref[...] * scale
        qms = [jnp.where(lo_half, q2, 0).astype(BF16), jnp.where(lo_half, 0, q2).astype(BF16)]
        acc_ref[...] = jnp.zeros_like(acc_ref)
        carry_ref[...] = jnp.zeros_like(carry_ref)

        def trip(j, _):
            tiles = [_att_tile(kv_ref, qb - ATT_NT * j - u, qb) for u in range(ATT_NT)]
            chains = [(u, b) for u in range(ATT_NT) for b in range(2)]
            for c, (u, b) in enumerate(chains):
                s_ref[c] = _dot_nt(qms[b], tiles[u][0])
            for c, (u, b) in enumerate(chains):
                for r0 in range(0, t, rs):
                    sl = pl.ds(r0, rs)
                    ls, lk = _att_scores_strip(s_ref, bias_ref, tiles[u][2], c, r0)
                    s_ref[c, sl, :] = ls
                    hi = lk.astype(BF16)
                    hi_ref[c, sl, :] = hi
                    lo_ref[c, sl, :] = (lk - hi.astype(F32)).astype(BF16)
                    rsum_ref[c, sl, :] = jnp.sum(lk, axis=1, keepdims=True)
            for c in range(len(chains)):
                cum_ref[c] = _dot(hi_ref[c], u_ref[...]) + _dot(lo_ref[c], u_ref[...])
            for c, (u, b) in enumerate(chains):
                cb_ref[c] = jnp.broadcast_to(_att_carry(carry_ref, rsum_ref, u, b), (t, t))
            for c, (u, b) in enumerate(chains):
                for r0 in range(0, t, rs):
                    sl = pl.ds(r0, rs)
                    x = s_ref[c, sl, :] + cum_ref[c, sl, :] + cb_ref[c, sl, :]
                    w_ref[c, sl, :] = jnp.exp2(x * LOG2E).astype(BF16)
            for c, (u, b) in enumerate(chains):
                acc_ref[b] += _dot(w_ref[c], tiles[u][1])
                carry_ref[b] += rsum_ref[c]

        _att_walk(qb, trip, carry_ref)
        o_ref[...] = jnp.where(lo_half, acc_ref[0], acc_ref[1])

    return pl.pallas_call(
        body, name=name, grid=(hp, nq),
        in_specs=[pl.BlockSpec((t, 128), lambda h, i: (i, h)), pl.BlockSpec((lp, 256), lambda h, i: (0, h))],
        out_specs=pl.BlockSpec((t, 128), lambda h, i: (i, h)),
        out_shape=jax.ShapeDtypeStruct((lp, d), F32),
        scratch_shapes=[pltpu.VMEM((t, t), BF16), pltpu.VMEM((5, t, t), F32), pltpu.VMEM((nch, t, t), F32),
                        pltpu.VMEM((nch, t, t), BF16), pltpu.VMEM((nch, t, t), BF16), pltpu.VMEM((nch, t, t), F32),
                        pltpu.VMEM((nch, t, t), BF16), pltpu.VMEM((nch, t, t), F32),
                        pltpu.VMEM((2, t, 2 * SB_HEAD_DIM), F32), pltpu.VMEM((2, t, 1), F32), pltpu.VMEM((nch, t, 1), F32)],
        compiler_params=_cparams(("arbitrary", "arbitrary")),
    )(q, kv)


def _attn_bwd(q, kv, o, do, *, cfg, name):
    t, lp, d, rs, nch = ATT_T, cfg.LP, cfg.D, ATT_STRIP, 2 * ATT_NT
    hp, nq = d // (2 * SB_HEAD_DIM), lp // ATT_T
    scale = SB_HEAD_DIM ** -0.5

    def body(q_ref, kv_ref, o_ref, do_ref, dq_ref, dkv_ref, u_ref, bias_ref, s_ref, dw_ref, hi_ref, lo_ref, cum_ref,
             w_ref, cb_ref, acc_ref, carry_ref, ecarry_ref, etot_ref, rsum_ref, ersum_ref):
        qb = pl.program_id(1)

        @pl.when(qb == 0)
        def _():
            dkv_ref[...] = jnp.zeros_like(dkv_ref)

        _att_init(u_ref, bias_ref, qb, cfg)
        lo_half = lax.broadcasted_iota(jnp.int32, (1, 2 * SB_HEAD_DIM), 1) < SB_HEAD_DIM
        q2 = q_ref[...] * scale
        do2 = do_ref[...]
        qms = [jnp.where(lo_half, q2, 0).astype(BF16), jnp.where(lo_half, 0, q2).astype(BF16)]
        doms = [jnp.where(lo_half, do2, 0).astype(BF16), jnp.where(lo_half, 0, do2).astype(BF16)]
        prod = do2.astype(F32) * o_ref[...]
        etot_ref[0] = jnp.sum(jnp.where(lo_half, prod, 0.0), axis=1, keepdims=True)
        etot_ref[1] = jnp.sum(jnp.where(lo_half, 0.0, prod), axis=1, keepdims=True)
        acc_ref[...] = jnp.zeros_like(acc_ref)
        carry_ref[...] = jnp.zeros_like(carry_ref)
        ecarry_ref[...] = jnp.zeros_like(ecarry_ref)

        def trip(j, _):
            kbs = [qb - ATT_NT * j - u for u in range(ATT_NT)]
            tiles = [_att_tile(kv_ref, kb, qb) for kb in kbs]
            chains = [(u, b) for u in range(ATT_NT) for b in range(2)]
            for c, (u, b) in enumerate(chains):
                s_ref[c] = _dot_nt(qms[b], tiles[u][0])
                dw_ref[c] = _dot_nt(doms[b], tiles[u][1])
            for c, (u, b) in enumerate(chains):
                for r0 in range(0, t, rs):
                    sl = pl.ds(r0, rs)
                    ls, lk = _att_scores_strip(s_ref, bias_ref, tiles[u][2], c, r0)
                    s_ref[c, sl, :] = ls
                    hi = lk.astype(BF16)
                    hi_ref[c, sl, :] = hi
                    lo_ref[c, sl, :] = (lk - hi.astype(F32)).astype(BF16)
                    rsum_ref[c, sl, :] = jnp.sum(lk, axis=1, keepdims=True)
            for c in range(nch):
                cum_ref[c] = _dot(hi_ref[c], u_ref[...]) + _dot(lo_ref[c], u_ref[...])
            for c, (u, b) in enumerate(chains):
                cb_ref[c] = jnp.broadcast_to(_att_carry(carry_ref, rsum_ref, u, b), (t, t))
            for c, (u, b) in enumerate(chains):
                for r0 in range(0, t, rs):
                    sl = pl.ds(r0, rs)
                    wb = jnp.exp2((s_ref[c, sl, :] + cum_ref[c, sl, :] + cb_ref[c, sl, :]) * LOG2E).astype(BF16)
                    w_ref[c, sl, :] = wb
                    e = wb.astype(F32) * dw_ref[c, sl, :]
                    dw_ref[c, sl, :] = e
                    hi = e.astype(BF16)
                    hi_ref[c, sl, :] = hi
                    lo_ref[c, sl, :] = (e - hi.astype(F32)).astype(BF16)
                    ersum_ref[c, sl, :] = jnp.sum(e, axis=1, keepdims=True)
            for c in range(nch):
                cum_ref[c] = _dot(hi_ref[c], u_ref[...]) + _dot(lo_ref[c], u_ref[...])
            for c, (u, b) in enumerate(chains):
                cb_ref[c] = jnp.broadcast_to(etot_ref[b] - _att_carry(ecarry_ref, ersum_ref, u, b), (t, t))
            for c, (u, b) in enumerate(chains):
                for r0 in range(0, t, rs):
                    sl = pl.ds(r0, rs)
                    e = dw_ref[c, sl, :]
                    e_before = cb_ref[c, sl, :] - (e + cum_ref[c, sl, :])
                    sig = jnp.exp2(s_ref[c, sl, :] * LOG2E)
                    hi_ref[c, sl, :] = (e * (1.0 - sig) - sig * e_before).astype(BF16)
            for u in range(ATT_NT):
                off = pl.multiple_of(jnp.maximum(kbs[u], 0) * t, t)
                c0, c1 = 2 * u, 2 * u + 1
                acc_ref[0] += _dot(hi_ref[c0], tiles[u][0])
                acc_ref[1] += _dot(hi_ref[c1], tiles[u][0])
                dkv_ref[pl.ds(off, t), 0:2 * SB_HEAD_DIM] += _dot_tn(hi_ref[c0], qms[0]) + _dot_tn(hi_ref[c1], qms[1])
                dkv_ref[pl.ds(off, t), 2 * SB_HEAD_DIM:] += _dot_tn(w_ref[c0], doms[0]) + _dot_tn(w_ref[c1], doms[1])
                for b in range(2):
                    carry_ref[b] += rsum_ref[2 * u + b]
                    ecarry_ref[b] += ersum_ref[2 * u + b]

        _att_walk(qb, trip, carry_ref)
        dq_ref[...] = jnp.where(lo_half, acc_ref[0], acc_ref[1]) * scale

    big = lambda dt: pltpu.VMEM((nch, t, t), dt)
    col = lambda n: pltpu.VMEM((n, t, 1), F32)
    return pl.pallas_call(
        body, name=name, grid=(hp, nq),
        in_specs=[pl.BlockSpec((t, 128), lambda h, i: (i, h)), pl.BlockSpec((lp, 256), lambda h, i: (0, h)),
                  pl.BlockSpec((t, 128), lambda h, i: (i, h)), pl.BlockSpec((t, 128), lambda h, i: (i, h))],
        out_specs=[pl.BlockSpec((t, 128), lambda h, i: (i, h)), pl.BlockSpec((lp, 256), lambda h, i: (0, h))],
        out_shape=[jax.ShapeDtypeStruct((lp, d), F32), jax.ShapeDtypeStruct((lp, 2 * d), F32)],
        scratch_shapes=[pltpu.VMEM((t, t), BF16), pltpu.VMEM((5, t, t), F32), big(F32), big(F32), big(BF16), big(BF16),
                        big(F32), big(BF16), big(F32), pltpu.VMEM((2, t, 2 * SB_HEAD_DIM), F32), col(2), col(2), col(2),
                        col(nch), col(nch)],
        compiler_params=_cparams(("arbitrary", "arbitrary")),
    )(q, kv, o, do)


def _loss_head(h, g, target, *, cfg, name):
    lp, d = h.shape
    tm = SSM_CHUNK
    first = (cfg.PF + N_META) // tm
    assert (cfg.PF + N_META) % tm == 0

    def body(h_ref, g_ref, t_ref, loss_ref, dh_ref, dg_ref):
        i = pl.program_id(0)

        @pl.when(i == 0)
        def _():
            loss_ref[...] = jnp.zeros_like(loss_ref)
            dg_ref[...] = jnp.zeros_like(dg_ref)

        x = h_ref[...]
        r = lax.rsqrt(jnp.mean(x * x, axis=-1, keepdims=True) + NORM_EPS)
        xhat = x * r
        live = i >= first
        diff = jnp.where(live, xhat * g_ref[...] - t_ref[...], 0.0)
        loss_ref[...] += 0.5 * jnp.sum(jnp.mean(diff * diff, axis=-1, keepdims=True))
        dy = diff * (1.0 / d)
        dg_ref[...] += jnp.sum(dy * xhat, axis=0, keepdims=True)
        dxh = dy * g_ref[...]
        dh_ref[...] = r * (dxh - xhat * jnp.mean(dxh * xhat, axis=-1, keepdims=True))

    return pl.pallas_call(
        body, name=name, grid=(lp // tm,),
        in_specs=[pl.BlockSpec((tm, d), lambda i: (i, 0)), pl.BlockSpec((1, d), lambda i: (0, 0)),
                  pl.BlockSpec((tm, d), lambda i: (jnp.maximum(i - first, 0), 0))],
        out_specs=[pl.BlockSpec((8, 128), lambda i: (0, 0)), pl.BlockSpec((tm, d), lambda i: (i, 0)),
                   pl.BlockSpec((1, d), lambda i: (0, 0))],
        out_shape=[jax.ShapeDtypeStruct((8, 128), F32), jax.ShapeDtypeStruct((lp, d), F32),
                   jax.ShapeDtypeStruct((1, d), F32)],
        compiler_params=_cparams(("arbitrary",)),
    )(h, g, target)


def _conv_bwd_input(dc, cw, *, width, name):
    lp, cd = dc.shape
    tc = _tile(cd, 512, 128)
    tm = _tile(lp, 768, 8)
    ni = lp // tm

    def body(d_ref, w_ref, o_ref, ext_ref):
        @pl.when(pl.program_id(1) == 0)
        def _():
            ext_ref[tm:, :] = jnp.zeros((HALO, tc), F32)

        ext_ref[0:tm, :] = d_ref[...]
        for r0 in range(0, tm, CONV_STRIP):
            acc = ext_ref[pl.ds(r0 + width - 1, CONV_STRIP), :] * w_ref[0:1, :]
            for k in range(1, width):
                acc = acc + ext_ref[pl.ds(r0 + width - 1 - k, CONV_STRIP), :] * w_ref[k:k + 1, :]
            o_ref[pl.ds(r0, CONV_STRIP), :] = acc.astype(BF16)
        ext_ref[tm:, :] = d_ref[0:HALO, :]

    return pl.pallas_call(
        body, name=name, grid=(cd // tc, ni),
        in_specs=[pl.BlockSpec((tm, tc), lambda j, i: (ni - 1 - i, j)), pl.BlockSpec((width, tc), lambda j, i: (0, j))],
        out_specs=pl.BlockSpec((tm, tc), lambda j, i: (ni - 1 - i, j)),
        out_shape=jax.ShapeDtypeStruct((lp, cd), BF16),
        scratch_shapes=[pltpu.VMEM((tm + HALO, tc), F32)],
        compiler_params=_cparams(("arbitrary", "arbitrary")),
    )(dc, cw)


MATMUL_WEIGHTS = ("ssd_w_in", "ssd_w_out", "w_kv", "sb_w_q", "sb_w_o", "ffn_w_up", "ffn_w_down")


def _il(a, cfg):
    nb = cfg.DFF // FFN_IL
    lead = a.shape[:-1]
    a = a.reshape(lead + (2, nb, FFN_IL))
    return jnp.swapaxes(a, -3, -2).reshape(lead + (2 * cfg.DFF,))


def _unil(a, cfg):
    nb = cfg.DFF // FFN_IL
    lead = a.shape[:-1]
    a = a.reshape(lead + (nb, 2, FFN_IL))
    return jnp.swapaxes(a, -3, -2).reshape(lead + (2 * cfg.DFF,))


def _pair(a, cfg):
    hp = cfg.D // 128
    lead = a.shape[:-1]
    return jnp.swapaxes(a.reshape(lead + (2, hp, 128)), -3, -2).reshape(lead + (2 * cfg.D,))


def _unpair(a, cfg):
    hp = cfg.D // 128
    lead = a.shape[:-1]
    return jnp.swapaxes(a.reshape(lead + (hp, 2, 128)), -3, -2).reshape(lead + (2 * cfg.D,))


def _prepare(full, cfg):
    d, di, cd, h = cfg.D, cfg.DI, cfg.CONV_DIM, cfg.H
    w_in = full["ssd_w_in"][0].astype(BF16)
    prm = jnp.stack([full["ssd_dt_bias"][0], full["ssd_a_log"][0], full["ssd_d_skip"][0]]).astype(F32)
    prm = jnp.pad(prm.reshape(3, SSM_GROUPS, cfg.HG).transpose(1, 0, 2), ((0, 0), (0, 5), (0, 0)))
    p = dict(
        meta=full["meta_tokens"].astype(F32),
        ssd_norm=full["ssd_norm"].astype(F32).reshape(1, d),
        w_in_main=w_in[:, :di + cd],
        w_in_dt=jnp.pad(w_in[:, di + cd:], ((0, 0), (0, 128 - h))),
        conv_w=full["ssd_conv_w"][0].astype(F32), conv_b=full["ssd_conv_b"].astype(F32).reshape(1, cd),
        prm=prm, gate=full["ssd_gate_norm"].astype(F32).reshape(1, di),
        w_out=full["ssd_w_out"][0].astype(BF16),
        kv_norm=full["kv_norm"].astype(F32).reshape(1, d), sb_norm=full["sb_norm"].astype(F32).reshape(1, d),
        final_norm=full["final_norm"].astype(F32).reshape(1, d),
    )
    for i in range(2):
        p[f"ffn_norm{i}"] = full["ffn_norm"][i].astype(F32).reshape(1, d)
        p[f"fcw{i}"] = _il(full["ffn_conv_w"][i].astype(F32), cfg)
        p[f"fcb{i}"] = _il(full["ffn_conv_b"][i].astype(F32), cfg).reshape(1, 2 * cfg.DFF)
    if all(n in full for n in LATE_WEIGHTS):
        p.update(_prepare_late(full, cfg))
    return p


LATE_WEIGHTS = ("w_kv", "sb_w_q", "sb_w_o", "ffn_w_up", "ffn_w_down")


def _prepare_late(full, cfg):
    p = dict(w_kv=_pair(full["w_kv"].astype(BF16), cfg), w_q=full["sb_w_q"][0].astype(BF16),
             w_o=full["sb_w_o"][0].astype(BF16))
    for i in range(2):
        p[f"w_up{i}"] = _il(full["ffn_w_up"][i].astype(BF16), cfg)
        p[f"w_down{i}"] = full["ffn_w_down"][i].astype(BF16)
    return p


def _ffn_fwd(h, p, i, cfg):
    up, u = _norm_mm(h, p[f"ffn_norm{i}"], p[f"w_up{i}"], out_dtype=F32, name=f"ffn{i}_up")
    act = _ffn_conv_fwd(up, p[f"fcw{i}"], p[f"fcb{i}"], cfg=cfg, name=f"ffn{i}_conv")
    return _mm_res(act, p[f"w_down{i}"], h, pf=cfg.PF, name=f"ffn{i}_down"), (h, u, up, act)


def _ffn_bwd(dh, saved, p, i, cfg, grads):
    h, u, up, act = saved
    dact = _mm_nt(dh, p[f"w_down{i}"], out_dtype=F32, pf=cfg.PF, name=f"ffn{i}_dact")
    grads[f"w_down{i}"] = _mm_tn(act, dh, name=f"ffn{i}_dwdown")
    dhc, grads[f"fcw{i}"], grads[f"fcb{i}"] = _ffn_conv_bwd_pre(up, dact, p[f"fcw{i}"], p[f"fcb{i}"], cfg=cfg,
                                                                 name=f"ffn{i}_dconv")
    dup = _conv_bwd_input(dhc, p[f"fcw{i}"], width=FFN_CONV, name=f"ffn{i}_dup")
    grads[f"w_up{i}"] = _mm_tn(u, dup, name=f"ffn{i}_dwup")
    dh, grads[f"ffn_norm{i}"] = _mm_nt_normbwd(dup, p[f"w_up{i}"], h, p[f"ffn_norm{i}"], dh, pf=cfg.PF,
                                               name=f"ffn{i}_dh")
    return dh


EARLY_GRADS = ("ssd_w_out", "w_kv", "sb_w_q", "sb_w_o", "ffn_w_up", "ffn_conv_w", "ffn_w_down")


def _local_step(x, target, p, cfg, exchange=None, late_weights=None):
    d, di, cd, h_, lp, pf = cfg.D, cfg.DI, cfg.CONV_DIM, cfg.H, cfg.LP, cfg.PF
    g = {}
    h0 = jnp.concatenate([jnp.zeros((pf, d), F32), p["meta"], x], axis=0)
    zx, u1 = _norm_mm(h0, p["ssd_norm"], p["w_in_main"], out_dtype=F32, name="ssd_in")
    dt_raw, _ = _norm_mm(h0, p["ssd_norm"], p["w_in_dt"], out_dtype=F32, name="ssd_in_dt")
    dtr = dt_raw[:, :h_].reshape(lp, SSM_GROUPS, cfg.HG).transpose(1, 0, 2)
    xbc = _ssd_conv_fwd(zx, p["conv_w"], p["conv_b"], cfg=cfg, name="ssd_conv")
    if late_weights is None:
        y, hn, states = _ssd_fwd(xbc, zx, dtr, p["prm"], p["gate"], cfg=cfg, name="ssd_scan")
    else:
        block, finish = late_weights
        y, hn, states, gathered = _ssd_fwd(xbc, zx, dtr, p["prm"], p["gate"], cfg=cfg, name="ssd_scan", gather=block)
        p = dict(p, **finish(gathered))
    h1 = _mm_res(hn, p["w_out"], h0, pf=pf, name="ssd_out")
    h2, ffn0 = _ffn_fwd(h1, p, 0, cfg)
    kv, ukv = _norm_mm(h2, p["kv_norm"], p["w_kv"], out_dtype=BF16, name="kv_proj")
    q, uq = _norm_mm(h2, p["sb_norm"], p["w_q"], out_dtype=BF16, name="q_proj")
    o = _attn_fwd(q, kv, cfg=cfg, name="attn_fwd")
    h3 = _mm_res(o, p["w_o"], h2, pf=pf, name="attn_out")
    h4, ffn1 = _ffn_fwd(h3, p, 1, cfg)
    loss8, dh, g["final_norm"] = _loss_head(h4, p["final_norm"], target, cfg=cfg, name="loss_head")
    dh = _ffn_bwd(dh, ffn1, p, 1, cfg, g)
    do = _mm_nt(dh, p["w_o"], out_dtype=BF16, pf=pf, name="attn_do")
    g["w_o"] = _mm_tn(o, dh, name="attn_dwo")
    dq, dkv = _attn_bwd(q, kv, o, do, cfg=cfg, name="attn_bwd")
    g["w_q"] = _mm_tn(uq, dq, name="attn_dwq")
    g["w_kv"] = _mm_tn(ukv, dkv, name="attn_dwkv")
    dh, g["sb_norm"] = _mm_nt_normbwd(dq, p["w_q"], h2, p["sb_norm"], dh, pf=pf, name="attn_dhq")
    dh, g["kv_norm"] = _mm_nt_normbwd(dkv, p["w_kv"], h2, p["kv_norm"], dh, pf=pf, name="attn_dhkv")
    dh = _ffn_bwd(dh, ffn0, p, 0, cfg, g)
    dhn = _mm_nt(dh, p["w_out"], out_dtype=F32, pf=pf, name="ssd_dhn")
    g["w_out"] = _mm_tn(hn, dh, name="ssd_dwout")
    done = dict(
        ssd_w_out=g["w_out"][None], w_kv=_unpair(g["w_kv"], cfg), sb_w_q=g["w_q"][None], sb_w_o=g["w_o"][None],
        ffn_w_up=jnp.stack([_unil(g["w_up0"], cfg), _unil(g["w_up1"], cfg)]),
        ffn_conv_w=jnp.stack([_unil(g["fcw0"], cfg), _unil(g["fcw1"], cfg)]),
        ffn_w_down=jnp.stack([g["w_down0"], g["w_down1"]]),
    )
    res = _ssd_bwd(xbc, zx, dtr, p["prm"], p["gate"], y, states, dhn, cfg=cfg, name="ssd_scan_bwd",
                   send=None if exchange is None else exchange(done))
    dx, db, dc, dz, ddtr, dprm, dgate = res[:7]
    dconv, g["conv_w"], g["conv_b"] = _ssd_conv_bwd_pre(zx, jnp.concatenate([dx, db, dc], axis=1), p["conv_w"],
                                                        p["conv_b"], cfg=cfg, name="ssd_dconv")
    dxpre = _conv_bwd_input(dconv, p["conv_w"], width=SSM_CONV, name="ssd_dxpre")
    dzx = jnp.concatenate([dz.astype(BF16), dxpre], axis=1)
    ddt = jnp.pad(ddtr.transpose(1, 0, 2).reshape(lp, h_), ((0, 0), (0, 128 - h_)))
    dw_main = _mm_tn(u1, dzx, name="ssd_dwin")
    dw_dt = _mm_tn(u1, ddt, name="ssd_dwin_dt")
    dh, gn1 = _mm_nt_normbwd(dzx, p["w_in_main"], h0, p["ssd_norm"], dh, pf=pf, name="ssd_dh")
    dh, gn2 = _mm_nt_normbwd(ddt, p["w_in_dt"], h0, p["ssd_norm"], dh, pf=pf, name="ssd_dh_dt")
    heads = lambda r: dprm[:, r, :].reshape(1, h_)
    out = dict(
        done,
        meta_tokens=dh[pf:pf + N_META], ssd_norm=gn1 + gn2,
        ssd_w_in=jnp.concatenate([dw_main, dw_dt[:, :h_]], axis=1)[None],
        ssd_conv_w=g["conv_w"][None], ssd_conv_b=g["conv_b"],
        ssd_dt_bias=heads(0), ssd_a_log=heads(1), ssd_d_skip=heads(2),
        ssd_gate_norm=dgate[:, 0, :].reshape(1, di),
        kv_norm=g["kv_norm"].reshape(d), sb_norm=g["sb_norm"],
        ffn_norm=jnp.concatenate([g["ffn_norm0"], g["ffn_norm1"]], axis=0),
        ffn_conv_b=jnp.concatenate([_unil(g["fcb0"], cfg), _unil(g["fcb1"], cfg)], axis=0),
        final_norm=g["final_norm"].reshape(d),
    )
    return loss8[0, 0], dh[pf + N_META:], out, (res[7] if exchange is not None else None)


MESH_AXES = ("x", "y", "c")
HBM_SPEC = pl.BlockSpec(memory_space=pltpu.HBM)


class _Gather:
    def __init__(self, x_ref, out_ref, send_sems, recv_sems, local_sem):
        x, y, c = lax.axis_index("x"), lax.axis_index("y"), lax.axis_index("c")
        me, sibling = (x, y, c), (x, y, 1 - c)
        chips = [(1 - x, y), (x, 1 - y), (1 - x, 1 - y)]

        def slot(px, py, pc):
            return out_ref.at[4 * px + 2 * py + pc]

        def copy(k, block, to, src=None):
            return pltpu.make_async_remote_copy(
                src_ref=slot(*block) if src is None else src, dst_ref=slot(*block),
                send_sem=send_sems.at[k], recv_sem=recv_sems.at[k], device_id=to, device_id_type=pl.DeviceIdType.MESH)

        self.mine = lambda: pltpu.make_async_copy(x_ref, slot(*me), local_sem)
        self.first = lambda: [copy(0, me, sibling, src=x_ref)] + [copy(1 + j, me, (*ch, c), src=x_ref)
                                                                  for j, ch in enumerate(chips)]
        self.passed = lambda: [copy(4 + j, (*ch, c), sibling) for j, ch in enumerate(chips)]
        self.over_ici = lambda: [copy(1 + j, (*ch, c), me) for j, ch in enumerate(chips)]
        self.from_sibling = lambda: [copy(0, sibling, me)] + [copy(4 + j, (*ch, 1 - c), me) for j, ch in enumerate(chips)]

    def start(self):
        self.mine().start()
        for cp in self.first():
            cp.start()

    def forward(self):
        for arrived, onward in zip(self.over_ici(), self.passed()):
            arrived.wait_recv()
            onward.start()

    def finish(self):
        for cp in self.from_sibling():
            cp.wait_recv()
        for cp in self.first() + self.passed():
            cp.wait_send()
        self.mine().wait()


GATHER_SEMS = [pltpu.SemaphoreType.DMA((7,)), pltpu.SemaphoreType.DMA((7,)), pltpu.SemaphoreType.DMA]


def _all_gather(blk, *, name):
    r, w = blk.shape

    def body(x_ref, out_ref, send_sems, recv_sems, local_sem):
        ag = _Gather(x_ref, out_ref, send_sems, recv_sems, local_sem)
        ag.start()
        ag.forward()
        ag.finish()

    return pl.pallas_call(
        body, name=name, out_shape=jax.ShapeDtypeStruct((N_DEV, r, w), blk.dtype),
        in_specs=[HBM_SPEC], out_specs=HBM_SPEC, scratch_shapes=list(GATHER_SEMS),
    )(blk)


def _all_to_all_copies(x_ref, out_ref, send_sems, recv_sems, local_sem):
    x, y, c = lax.axis_index("x"), lax.axis_index("y"), lax.axis_index("c")
    me_id = 4 * x + 2 * y + c
    copies = [pltpu.make_async_copy(x_ref.at[me_id], out_ref.at[me_id], local_sem)]
    for k in range(1, N_DEV):
        fx, fy, fc = (k >> 2) & 1, (k >> 1) & 1, k & 1
        px = 1 - x if fx else x
        py = 1 - y if fy else y
        pc = 1 - c if fc else c
        copies.append(pltpu.make_async_remote_copy(
            src_ref=x_ref.at[4 * px + 2 * py + pc], dst_ref=out_ref.at[me_id],
            send_sem=send_sems.at[k - 1], recv_sem=recv_sems.at[k - 1],
            device_id=(px, py, pc), device_id_type=pl.DeviceIdType.MESH))
    return copies


def _all_to_all(buf, *, name):
    n, r, w = buf.shape

    def body(x_ref, out_ref, send_sems, recv_sems, local_sem):
        copies = _all_to_all_copies(x_ref, out_ref, send_sems, recv_sems, local_sem)
        for cp in copies:
            cp.start()
        for cp in copies:
            cp.wait()

    return pl.pallas_call(
        body, name=name, out_shape=jax.ShapeDtypeStruct((n, r, w), buf.dtype),
        in_specs=[HBM_SPEC], out_specs=HBM_SPEC,
        scratch_shapes=[pltpu.SemaphoreType.DMA((7,)), pltpu.SemaphoreType.DMA((7,)), pltpu.SemaphoreType.DMA],
    )(buf)


def _sum_slots(buf, *, name):
    n, r, w = buf.shape
    tr = _tile(r, 1024, 16)

    def body(x_ref, o_ref):
        acc = x_ref[0].astype(F32)
        for s in range(1, n):
            acc = acc + x_ref[s].astype(F32)
        o_ref[...] = acc

    return pl.pallas_call(
        body, name=name, grid=(r // tr,),
        in_specs=[pl.BlockSpec((n, tr, w), lambda i: (0, i, 0))], out_specs=pl.BlockSpec((tr, w), lambda i: (i, 0)),
        out_shape=jax.ShapeDtypeStruct((r, w), F32), compiler_params=_cparams(("arbitrary",)),
    )(buf)


def _adamw(w, g, m, v, *, name):
    r, lanes = w.shape
    tr = _tile(r, ADAM_ROWS, 8)

    def body(w_ref, g_ref, m_ref, v_ref, d_ref, nm_ref, nv_ref):
        gg = g_ref[...]
        nm = ADAM_B1 * m_ref[...] + (1.0 - ADAM_B1) * gg
        nv = ADAM_B2 * v_ref[...] + (1.0 - ADAM_B2) * (gg * gg)
        m_hat = nm / (1.0 - ADAM_B1 ** ADAM_STEP)
        v_hat = nv / (1.0 - ADAM_B2 ** ADAM_STEP)
        d_ref[...] = -ADAM_LR * (m_hat / (jnp.sqrt(v_hat) + ADAM_EPS) + ADAM_WD * w_ref[...])
        nm_ref[...] = nm
        nv_ref[...] = nv

    spec = pl.BlockSpec((tr, lanes), lambda i: (i, 0))
    shp = jax.ShapeDtypeStruct((r, lanes), F32)
    return pl.pallas_call(body, name=name, grid=(r // tr,), in_specs=[spec] * 4, out_specs=[spec] * 3,
                          out_shape=[shp] * 3, compiler_params=_cparams(("arbitrary",)))(w, g, m, v)


PACK_QUANTUM = 16 * 128
ADAM_ROWS = 2048
INPUT_NAMES = ("x", "meta_tokens", "ssd_norm", "ssd_w_in", "ssd_conv_w", "ssd_conv_b", "ssd_dt_bias", "ssd_a_log",
               "ssd_d_skip", "ssd_gate_norm", "ssd_w_out", "kv_norm", "w_kv", "sb_norm", "sb_w_q", "sb_w_o", "ffn_norm",
               "ffn_w_up", "ffn_conv_w", "ffn_conv_b", "ffn_w_down", "final_norm")
WEIGHT_NAMES = INPUT_NAMES[1:]
SHARD_AXIS = dict(meta_tokens=1, ssd_norm=1, ssd_w_in=2, ssd_conv_w=2, ssd_conv_b=1, ssd_gate_norm=1, ssd_w_out=1, w_kv=1,
                  sb_w_q=1, sb_w_o=1, ffn_w_up=2, ffn_conv_w=2, ffn_w_down=1)
SMALL_SHARDED = ("meta_tokens", "ssd_norm", "ssd_conv_w", "ssd_conv_b", "ssd_gate_norm", "ffn_conv_w")
SHARDED = MATMUL_WEIGHTS + SMALL_SHARDED
REPLICATED = tuple(n for n in WEIGHT_NAMES if n not in SHARD_AXIS)


def _rows(shape):
    n = math.prod(shape)
    return -(-n // PACK_QUANTUM) * (PACK_QUANTUM // 128)


def _pack(arrs, dtype, lead=0):
    parts = []
    for a in arrs:
        ld = a.shape[:lead]
        n = math.prod(a.shape[lead:])
        f = a.reshape(ld + (n,)).astype(dtype)
        pad = _rows(a.shape[lead:]) * 128 - n
        if pad:
            f = jnp.pad(f, [(0, 0)] * lead + [(0, pad)])
        parts.append(f.reshape(ld + (-1, 128)))
    return jnp.concatenate(parts, axis=lead)


def _unpack(flat, shapes, lead=0):
    out, r0 = [], 0
    ld = flat.shape[:lead]
    for shp in shapes:
        rows, n = _rows(shp), math.prod(shp)
        piece = lax.slice_in_dim(flat, r0, r0 + rows, axis=lead).reshape(ld + (rows * 128,))
        out.append(lax.slice_in_dim(piece, 0, n, axis=lead).reshape(ld + tuple(shp)))
        r0 += rows
    return out


def _unshard(stacked, axis):
    a = jnp.moveaxis(stacked, 0, axis)
    shp = a.shape
    return a.reshape(shp[:axis] + (shp[axis] * shp[axis + 1],) + shp[axis + 2:])


def _to_shards(full, axis):
    shp = full.shape
    a = full.reshape(shp[:axis] + (N_DEV, shp[axis] // N_DEV) + shp[axis + 1:])
    return jnp.moveaxis(a, axis, 0)


def kernel(x, meta_tokens, ssd_norm, ssd_w_in, ssd_conv_w, ssd_conv_b, ssd_dt_bias, ssd_a_log, ssd_d_skip, ssd_gate_norm, ssd_w_out, kv_norm, w_kv, sb_norm, sb_w_q, sb_w_o, ffn_norm, ffn_w_up, ffn_conv_w, ffn_conv_b, ffn_w_down, final_norm, loss_target, m_meta_tokens, m_ssd_norm, m_ssd_w_in, m_ssd_conv_w, m_ssd_conv_b, m_ssd_dt_bias, m_ssd_a_log, m_ssd_d_skip, m_ssd_gate_norm, m_ssd_w_out, m_kv_norm, m_w_kv, m_sb_norm, m_sb_w_q, m_sb_w_o, m_ffn_norm, m_ffn_w_up, m_ffn_conv_w, m_ffn_conv_b, m_ffn_w_down, m_final_norm, v_meta_tokens, v_ssd_norm, v_ssd_w_in, v_ssd_conv_w, v_ssd_conv_b, v_ssd_dt_bias, v_ssd_a_log, v_ssd_d_skip, v_ssd_gate_norm, v_ssd_w_out, v_kv_norm, v_w_kv, v_sb_norm, v_sb_w_q, v_sb_w_o, v_ffn_norm, v_ffn_w_up, v_ffn_conv_w, v_ffn_conv_b, v_ffn_w_down, v_final_norm):
    local = dict(zip(WEIGHT_NAMES, (meta_tokens, ssd_norm, ssd_w_in, ssd_conv_w, ssd_conv_b, ssd_dt_bias, ssd_a_log, ssd_d_skip, ssd_gate_norm, ssd_w_out, kv_norm, w_kv, sb_norm, sb_w_q, sb_w_o, ffn_norm, ffn_w_up, ffn_conv_w, ffn_conv_b, ffn_w_down, final_norm)))
    mom = dict(zip(WEIGHT_NAMES, (m_meta_tokens, m_ssd_norm, m_ssd_w_in, m_ssd_conv_w, m_ssd_conv_b, m_ssd_dt_bias, m_ssd_a_log, m_ssd_d_skip, m_ssd_gate_norm, m_ssd_w_out, m_kv_norm, m_w_kv, m_sb_norm, m_sb_w_q, m_sb_w_o, m_ffn_norm, m_ffn_w_up, m_ffn_conv_w, m_ffn_conv_b, m_ffn_w_down, m_final_norm)))
    var = dict(zip(WEIGHT_NAMES, (v_meta_tokens, v_ssd_norm, v_ssd_w_in, v_ssd_conv_w, v_ssd_conv_b, v_ssd_dt_bias, v_ssd_a_log, v_ssd_d_skip, v_ssd_gate_norm, v_ssd_w_out, v_kv_norm, v_w_kv, v_sb_norm, v_sb_w_q, v_sb_w_o, v_ffn_norm, v_ffn_w_up, v_ffn_conv_w, v_ffn_conv_b, v_ffn_w_down, v_final_norm)))
    seq, d = x.shape[1], x.shape[2]
    cfg = _make_cfg(d, seq, ffn_w_down.shape[1] * N_DEV)

    first = tuple(n for n in MATMUL_WEIGHTS if n not in LATE_WEIGHTS)
    big = _all_gather(_pack([local[n] for n in first], BF16), name="gather_weights")
    small = _all_gather(_pack([local[n] for n in SMALL_SHARDED], F32), name="gather_small")

    def unpacked(names, buf):
        return {n: _unshard(stacked, SHARD_AXIS[n])
                for n, stacked in zip(names, _unpack(buf, [local[n].shape for n in names], lead=1))}

    full = {n: local[n] for n in REPLICATED}
    full.update(unpacked(first, big))
    full.update(unpacked(SMALL_SHARDED, small))
    late_weights = (_pack([local[n] for n in LATE_WEIGHTS], BF16),
                    lambda gathered: _prepare_late(unpacked(LATE_WEIGHTS, gathered), cfg))
    full_shape = {n: tuple(s * (N_DEV if a == SHARD_AXIS.get(n) else 1) for a, s in enumerate(local[n].shape))
                  for n in WEIGHT_NAMES}

    late = tuple(n for n in SHARDED if n not in EARLY_GRADS)
    for_peers = lambda gr, names: _pack([_to_shards(gr[n].reshape(full_shape[n]), SHARD_AXIS[n]) for n in names], BF16,
                                        lead=1)
    loss, grad_x, grads, got_early = _local_step(x[0], loss_target[0], _prepare(full, cfg), cfg,
                                                 exchange=lambda done: for_peers(done, EARLY_GRADS),
                                                 late_weights=late_weights)
    g_early = _sum_slots(got_early, name="sum_grads_early")
    g_late = _sum_slots(_all_to_all(for_peers(grads, late), name="scatter_grads"), name="sum_grads")
    rep = _all_gather(_pack([grads[n].reshape(local[n].shape) for n in REPLICATED], F32), name="gather_rep_grads")
    g_rep = _sum_slots(rep, name="sum_rep_grads")
    fill = jnp.zeros(((-(g_early.shape[0] + g_late.shape[0] + g_rep.shape[0])) % ADAM_ROWS, 128), F32)
    g_flat = jnp.concatenate([g_early, g_late, g_rep, fill], axis=0)

    order = EARLY_GRADS + late + REPLICATED
    flat = lambda src: jnp.concatenate([_pack([src[n] for n in order], F32), fill], axis=0)
    delta, new_m, new_v = _adamw(flat(local), g_flat, flat(mom), flat(var), name="adamw")
    shapes = [local[n].shape for n in order]
    pick = lambda buf: dict(zip(order, _unpack(buf, shapes)))
    g_out, d_out, m_out, v_out = pick(g_flat), pick(delta), pick(new_m), pick(new_v)
    loss = lax.psum(loss, MESH_AXES)
    return (loss, grad_x[None], *[g_out[n] for n in WEIGHT_NAMES], *[d_out[n] for n in WEIGHT_NAMES],
            *[m_out[n] for n in WEIGHT_NAMES], *[v_out[n] for n in WEIGHT_NAMES])
```

```python
import collections
import functools
import math

import jax
import jax.numpy as jnp
from jax import lax
from jax.experimental import pallas as pl
from jax.experimental.pallas import tpu as pltpu

F32 = jnp.float32
BF16 = jnp.bfloat16
NORM_EPS = 1e-6
N_META = 16
SSM_HEAD_DIM = 64
SSM_GROUPS = 4
SSM_STATE = 128
SSM_CONV = 4
SSM_CHUNK = 256
SB_HEAD_DIM = 64
FFN_CONV = 3
FFN_IL = 256
DT_PAD = 256
ATT_T = 256
ADAM_LR, ADAM_B1, ADAM_B2, ADAM_EPS, ADAM_WD, ADAM_STEP = 0.001, 0.9, 0.999, 1e-08, 0.01, 10
N_DEV = 8
VMEM_LIMIT = 56 * 1024 * 1024
MM_TILE = 1536

Cfg = collections.namedtuple("Cfg", "D SEQ LP PF DI H HG GW CONV_DIM DFF HS")


def _make_cfg(d_model, seq, d_ff):
    pf = (-N_META) % SSM_CHUNK
    lp = pf + N_META + seq
    assert lp % SSM_CHUNK == 0 and (pf + N_META) == SSM_CHUNK
    di = 2 * d_model
    h = di // SSM_HEAD_DIM
    return Cfg(D=d_model, SEQ=seq, LP=lp, PF=pf, DI=di, H=h, HG=h // SSM_GROUPS, GW=di // SSM_GROUPS,
               CONV_DIM=di + 2 * SSM_GROUPS * SSM_STATE, DFF=d_ff, HS=d_model // SB_HEAD_DIM)


def _tile(n, pref, mult):
    t = (min(pref, n) // mult) * mult
    while t > mult and n % t:
        t -= mult
    assert t >= mult and n % t == 0, (n, pref, mult)
    return t


def _cparams(sem):
    return pltpu.CompilerParams(dimension_semantics=sem, vmem_limit_bytes=VMEM_LIMIT)


def _dot(a, b):
    return jnp.dot(a, b, preferred_element_type=F32)


def _dot_nt(a, b):
    return lax.dot_general(a, b, (((1,), (1,)), ((), ())), preferred_element_type=F32)


def _dot_tn(a, b):
    return lax.dot_general(a, b, (((0,), (0,)), ((), ())), preferred_element_type=F32)


def _split3(v):
    hi = v.astype(BF16)
    r1 = v - hi.astype(F32)
    mid = r1.astype(BF16)
    lo = (r1 - mid.astype(F32)).astype(BF16)
    return hi, mid, lo


def _split2(v):
    hi = v.astype(BF16)
    lo = (v - hi.astype(F32)).astype(BF16)
    return hi, lo


def _dot3(a_f32, b_bf16):
    hi, mid, lo = _split3(a_f32)
    return _dot(hi, b_bf16) + _dot(mid, b_bf16) + _dot(lo, b_bf16)


def _dot3_left(a_bf16, b_f32):
    hi, mid, lo = _split3(b_f32)
    return _dot(a_bf16, hi) + _dot(a_bf16, mid) + _dot(a_bf16, lo)


def _sigmoid(x):
    return 0.5 * jnp.tanh(0.5 * x) + 0.5


def _softplus(x):
    return jnp.maximum(x, 0.0) + jnp.log(1.0 + jnp.exp(-jnp.abs(x)))


def _row_ids(i, tm, shape_cols=1):
    return i * tm + lax.broadcasted_iota(jnp.int32, (tm, shape_cols), 0)


def _norm_mm(h, g, w, *, out_dtype, name, scale=None):
    lp, d = h.shape
    n = w.shape[1]
    tm = _tile(lp, 768, 8)
    tn = _tile(n, MM_TILE, 128)

    def body(h_ref, g_ref, w_ref, y_ref, u_ref):
        @pl.when(pl.program_id(1) == 0)
        def _():
            x = h_ref[...]
            r = lax.rsqrt(jnp.mean(x * x, axis=-1, keepdims=True) + NORM_EPS)
            u_ref[...] = (x * r * g_ref[...]).astype(BF16)
        y = _dot(u_ref[...], w_ref[...])
        if scale is not None:
            y = y * scale
        y_ref[...] = y.astype(out_dtype)

    return pl.pallas_call(
        body, name=name, grid=(lp // tm, n // tn),
        in_specs=[pl.BlockSpec((tm, d), lambda i, j: (i, 0)), pl.BlockSpec((1, d), lambda i, j: (0, 0)),
                  pl.BlockSpec((d, tn), lambda i, j: (0, j))],
        out_specs=[pl.BlockSpec((tm, tn), lambda i, j: (i, j)), pl.BlockSpec((tm, d), lambda i, j: (i, 0))],
        out_shape=[jax.ShapeDtypeStruct((lp, n), out_dtype), jax.ShapeDtypeStruct((lp, d), BF16)],
        compiler_params=_cparams(("arbitrary", "arbitrary")),
    )(h, g, w)


def _mm_res(a, w, h, *, pf, name):
    lp, k = a.shape
    d = w.shape[1]
    tm = _tile(lp, 768, 8)

    def body(a_ref, w_ref, h_ref, o_ref):
        y = _dot(a_ref[...].astype(BF16), w_ref[...])
        rows = _row_ids(pl.program_id(0), tm)
        o_ref[...] = h_ref[...] + jnp.where(rows >= pf, y, 0.0)

    return pl.pallas_call(
        body, name=name, grid=(lp // tm,),
        in_specs=[pl.BlockSpec((tm, k), lambda i: (i, 0)), pl.BlockSpec((k, d), lambda i: (0, 0)),
                  pl.BlockSpec((tm, d), lambda i: (i, 0))],
        out_specs=pl.BlockSpec((tm, d), lambda i: (i, 0)),
        out_shape=jax.ShapeDtypeStruct((lp, d), F32),
        compiler_params=_cparams(("arbitrary",)),
    )(a, w, h)


def _mm_nt(dy, w, *, out_dtype, pf, name):
    lp, n = dy.shape
    k = w.shape[0]
    tm = _tile(lp, 768, 8)
    tk = _tile(k, MM_TILE, 128)

    def body(dy_ref, w_ref, o_ref):
        y = _dot_nt(dy_ref[...].astype(BF16), w_ref[...])
        rows = _row_ids(pl.program_id(0), tm)
        o_ref[...] = jnp.where(rows >= pf, y, 0.0).astype(out_dtype)

    return pl.pallas_call(
        body, name=name, grid=(lp // tm, k // tk),
        in_specs=[pl.BlockSpec((tm, n), lambda i, j: (i, 0)), pl.BlockSpec((tk, n), lambda i, j: (j, 0))],
        out_specs=pl.BlockSpec((tm, tk), lambda i, j: (i, j)),
        out_shape=jax.ShapeDtypeStruct((lp, k), out_dtype),
        compiler_params=_cparams(("arbitrary", "arbitrary")),
    )(dy, w)


def _mm_nt_normbwd(dy, w, h, g, dh_in, *, pf, name, scale=None):
    lp, n = dy.shape
    d = w.shape[0]
    tm = _tile(lp, 768, 8)
    tn = _tile(n, MM_TILE, 128)
    nj = n // tn

    def body(dy_ref, w_ref, h_ref, g_ref, dhin_ref, dh_ref, dg_ref, acc_ref):
        i, j = pl.program_id(0), pl.program_id(1)

        @pl.when(j == 0)
        def _():
            acc_ref[...] = jnp.zeros_like(acc_ref)

        @pl.when((i == 0) & (j == 0))
        def _():
            dg_ref[...] = jnp.zeros_like(dg_ref)

        acc_ref[...] += _dot_nt(dy_ref[...].astype(BF16), w_ref[...])

        @pl.when(j == nj - 1)
        def _():
            du = acc_ref[...]
            if scale is not None:
                du = du * scale
            x = h_ref[...]
            r = lax.rsqrt(jnp.mean(x * x, axis=-1, keepdims=True) + NORM_EPS)
            xhat = x * r
            dg_ref[...] += jnp.sum(du * xhat, axis=0, keepdims=True)
            dxh = du * g_ref[...]
            dx = r * (dxh - xhat * jnp.mean(dxh * xhat, axis=-1, keepdims=True))
            rows = _row_ids(i, tm)
            dh_ref[...] = jnp.where(rows >= pf, dhin_ref[...] + dx, 0.0)

    return pl.pallas_call(
        body, name=name, grid=(lp // tm, nj),
        in_specs=[pl.BlockSpec((tm, tn), lambda i, j: (i, j)), pl.BlockSpec((d, tn), lambda i, j: (0, j)),
                  pl.BlockSpec((tm, d), lambda i, j: (i, 0)), pl.BlockSpec((1, d), lambda i, j: (0, 0)),
                  pl.BlockSpec((tm, d), lambda i, j: (i, 0))],
        out_specs=[pl.BlockSpec((tm, d), lambda i, j: (i, 0)), pl.BlockSpec((1, d), lambda i, j: (0, 0))],
        out_shape=[jax.ShapeDtypeStruct((lp, d), F32), jax.ShapeDtypeStruct((1, d), F32)],
        scratch_shapes=[pltpu.VMEM((tm, d), F32)],
        compiler_params=_cparams(("arbitrary", "arbitrary")),
    )(dy, w, h, g, dh_in)


def _mm_tn(a, b, *, name, scale=None):
    lp, k = a.shape
    n = b.shape[1]
    tm = _tile(lp, 768, 8)
    tk = _tile(k, MM_TILE, 128)
    tn = _tile(n, MM_TILE, 128)
    nm = lp // tm

    def body(a_ref, b_ref, o_ref, acc_ref):
        m = pl.program_id(2)

        @pl.when(m == 0)
        def _():
            acc_ref[...] = jnp.zeros_like(acc_ref)

        acc_ref[...] += _dot_tn(a_ref[...].astype(BF16), b_ref[...].astype(BF16))

        @pl.when(m == nm - 1)
        def _():
            o_ref[...] = acc_ref[...] if scale is None else acc_ref[...] * scale

    return pl.pallas_call(
        body, name=name, grid=(k // tk, n // tn, nm),
        in_specs=[pl.BlockSpec((tm, tk), lambda i, j, m: (m, i)), pl.BlockSpec((tm, tn), lambda i, j, m: (m, j))],
        out_specs=pl.BlockSpec((tk, tn), lambda i, j, m: (i, j)),
        out_shape=jax.ShapeDtypeStruct((k, n), F32),
        scratch_shapes=[pltpu.VMEM((tk, tn), F32)],
        compiler_params=_cparams(("arbitrary", "arbitrary", "arbitrary")),
    )(a, b)


HALO = 8


CONV_STRIP = 32


def _conv_load(ext_ref, x_ref):
    @pl.when(pl.program_id(1) == 0)
    def _():
        ext_ref[0:HALO, :] = jnp.zeros((HALO, ext_ref.shape[1]), F32)

    ext_ref[HALO:, :] = x_ref[...]


def _conv_strip(ext_ref, w_ref, b_ref, r0, width):
    taps = [ext_ref[pl.ds(r0 + HALO - (width - 1 - k), CONV_STRIP), :] for k in range(width)]
    acc = b_ref[...] + taps[0] * w_ref[0:1, :]
    for k in range(1, width):
        acc = acc + taps[k] * w_ref[k:k + 1, :]
    return acc, taps


def _strip_live(i, tm, r0, pf):
    return i * tm + r0 + lax.broadcasted_iota(jnp.int32, (CONV_STRIP, 1), 0) >= pf


def _fold8(x):
    out = x[0:8]
    for r in range(8, x.shape[0], 8):
        out = out + x[r:r + 8]
    return out


def _conv_grad_flush(acc_ref, dw_ref, db_ref, width, last):
    @pl.when(last)
    def _():
        for k in range(width):
            dw_ref[k:k + 1, :] = jnp.sum(acc_ref[8 * k:8 * k + 8, :], axis=0, keepdims=True)
        db_ref[...] = jnp.sum(acc_ref[8 * width:8 * width + 8, :], axis=0, keepdims=True)


def _ssd_conv_fwd(zx, cw, cb, *, cfg, name):
    lp, cd, di = cfg.LP, cfg.CONV_DIM, cfg.DI
    tc = _tile(math.gcd(di, cd), 512, 128)
    tm = _tile(lp, 768, 8)
    off = di // tc

    def body(x_ref, w_ref, b_ref, o_ref, ext_ref):
        i = pl.program_id(1)
        _conv_load(ext_ref, x_ref)
        for r0 in range(0, tm, CONV_STRIP):
            acc, _ = _conv_strip(ext_ref, w_ref, b_ref, r0, SSM_CONV)
            o_ref[pl.ds(r0, CONV_STRIP), :] = jnp.where(_strip_live(i, tm, r0, cfg.PF), acc * _sigmoid(acc), 0.0)
        ext_ref[0:HALO, :] = x_ref[tm - HALO:tm, :]

    return pl.pallas_call(
        body, name=name, grid=(cd // tc, lp // tm),
        in_specs=[pl.BlockSpec((tm, tc), lambda j, i: (i, j + off)), pl.BlockSpec((SSM_CONV, tc), lambda j, i: (0, j)),
                  pl.BlockSpec((1, tc), lambda j, i: (0, j))],
        out_specs=pl.BlockSpec((tm, tc), lambda j, i: (i, j)),
        out_shape=jax.ShapeDtypeStruct((lp, cd), F32),
        scratch_shapes=[pltpu.VMEM((tm + HALO, tc), F32)],
        compiler_params=_cparams(("arbitrary", "arbitrary")),
    )(zx, cw, cb)


def _ssd_conv_bwd_pre(zx, dxbc, cw, cb, *, cfg, name):
    lp, cd, di = cfg.LP, cfg.CONV_DIM, cfg.DI
    tc = _tile(math.gcd(di, cd), 512, 128)
    tm = _tile(lp, 768, 8)
    off = di // tc

    def body(x_ref, d_ref, w_ref, b_ref, dc_ref, dw_ref, db_ref, ext_ref, acc_ref):
        i = pl.program_id(1)

        @pl.when(i == 0)
        def _():
            acc_ref[...] = jnp.zeros_like(acc_ref)

        _conv_load(ext_ref, x_ref)
        for r0 in range(0, tm, CONV_STRIP):
            sl = pl.ds(r0, CONV_STRIP)
            c, taps = _conv_strip(ext_ref, w_ref, b_ref, r0, SSM_CONV)
            sg = _sigmoid(c)
            dc = jnp.where(_strip_live(i, tm, r0, cfg.PF), d_ref[sl, :] * sg * (1.0 + c * (1.0 - sg)), 0.0)
            dc_ref[sl, :] = dc
            for k in range(SSM_CONV):
                acc_ref[8 * k:8 * k + 8, :] += _fold8(dc * taps[k])
            acc_ref[8 * SSM_CONV:, :] += _fold8(dc)
        ext_ref[0:HALO, :] = x_ref[tm - HALO:tm, :]
        _conv_grad_flush(acc_ref, dw_ref, db_ref, SSM_CONV, i == lp // tm - 1)

    return pl.pallas_call(
        body, name=name, grid=(cd // tc, lp // tm),
        in_specs=[pl.BlockSpec((tm, tc), lambda j, i: (i, j + off)), pl.BlockSpec((tm, tc), lambda j, i: (i, j)),
                  pl.BlockSpec((SSM_CONV, tc), lambda j, i: (0, j)), pl.BlockSpec((1, tc), lambda j, i: (0, j))],
        out_specs=[pl.BlockSpec((tm, tc), lambda j, i: (i, j)), pl.BlockSpec((SSM_CONV, tc), lambda j, i: (0, j)),
                   pl.BlockSpec((1, tc), lambda j, i: (0, j))],
        out_shape=[jax.ShapeDtypeStruct((lp, cd), F32), jax.ShapeDtypeStruct((SSM_CONV, cd), F32),
                   jax.ShapeDtypeStruct((1, cd), F32)],
        scratch_shapes=[pltpu.VMEM((tm + HALO, tc), F32), pltpu.VMEM((8 * (SSM_CONV + 1), tc), F32)],
        compiler_params=_cparams(("arbitrary", "arbitrary")),
    )(zx, dxbc, cw, cb)


def _ffn_conv_fwd(up, cw, cb, *, cfg, name):
    lp, dff = cfg.LP, cfg.DFF
    tc = 2 * FFN_IL
    tm = _tile(lp, 768, 8)

    def body(x_ref, w_ref, b_ref, o_ref, ext_ref):
        i = pl.program_id(1)
        _conv_load(ext_ref, x_ref)
        for r0 in range(0, tm, CONV_STRIP):
            hc, _ = _conv_strip(ext_ref, w_ref, b_ref, r0, FFN_CONV)
            gc, vc = hc[:, :FFN_IL], hc[:, FFN_IL:]
            act = jnp.where(_strip_live(i, tm, r0, cfg.PF), gc * _sigmoid(gc) * vc, 0.0)
            o_ref[pl.ds(r0, CONV_STRIP), :] = act.astype(BF16)
        ext_ref[0:HALO, :] = x_ref[tm - HALO:tm, :]

    return pl.pallas_call(
        body, name=name, grid=(dff // FFN_IL, lp // tm),
        in_specs=[pl.BlockSpec((tm, tc), lambda j, i: (i, j)), pl.BlockSpec((FFN_CONV, tc), lambda j, i: (0, j)),
                  pl.BlockSpec((1, tc), lambda j, i: (0, j))],
        out_specs=pl.BlockSpec((tm, FFN_IL), lambda j, i: (i, j)),
        out_shape=jax.ShapeDtypeStruct((lp, dff), BF16),
        scratch_shapes=[pltpu.VMEM((tm + HALO, tc), F32)],
        compiler_params=_cparams(("arbitrary", "arbitrary")),
    )(up, cw, cb)


def _ffn_conv_bwd_pre(up, dact, cw, cb, *, cfg, name):
    lp, dff = cfg.LP, cfg.DFF
    tc = 2 * FFN_IL
    tm = _tile(lp, 768, 8)

    def body(x_ref, d_ref, w_ref, b_ref, dc_ref, dw_ref, db_ref, ext_ref, acc_ref):
        i = pl.program_id(1)

        @pl.when(i == 0)
        def _():
            acc_ref[...] = jnp.zeros_like(acc_ref)

        _conv_load(ext_ref, x_ref)
        for r0 in range(0, tm, CONV_STRIP):
            sl = pl.ds(r0, CONV_STRIP)
            hc, taps = _conv_strip(ext_ref, w_ref, b_ref, r0, FFN_CONV)
            gc, vc = hc[:, :FFN_IL], hc[:, FFN_IL:]
            sg = _sigmoid(gc)
            da = jnp.where(_strip_live(i, tm, r0, cfg.PF), d_ref[sl, :].astype(F32), 0.0)
            dc = jnp.concatenate([da * vc * sg * (1.0 + gc * (1.0 - sg)), da * gc * sg], axis=1)
            dc_ref[sl, :] = dc
            for k in range(FFN_CONV):
                acc_ref[8 * k:8 * k + 8, :] += _fold8(dc * taps[k])
            acc_ref[8 * FFN_CONV:, :] += _fold8(dc)
        ext_ref[0:HALO, :] = x_ref[tm - HALO:tm, :]
        _conv_grad_flush(acc_ref, dw_ref, db_ref, FFN_CONV, i == lp // tm - 1)

    return pl.pallas_call(
        body, name=name, grid=(dff // FFN_IL, lp // tm),
        in_specs=[pl.BlockSpec((tm, tc), lambda j, i: (i, j)), pl.BlockSpec((tm, FFN_IL), lambda j, i: (i, j)),
                  pl.BlockSpec((FFN_CONV, tc), lambda j, i: (0, j)), pl.BlockSpec((1, tc), lambda j, i: (0, j))],
        out_specs=[pl.BlockSpec((tm, tc), lambda j, i: (i, j)), pl.BlockSpec((FFN_CONV, tc), lambda j, i: (0, j)),
                   pl.BlockSpec((1, tc), lambda j, i: (0, j))],
        out_shape=[jax.ShapeDtypeStruct((lp, 2 * dff), F32), jax.ShapeDtypeStruct((FFN_CONV, 2 * dff), F32),
                   jax.ShapeDtypeStruct((1, 2 * dff), F32)],
        scratch_shapes=[pltpu.VMEM((tm + HALO, tc), F32), pltpu.VMEM((8 * (FFN_CONV + 1), tc), F32)],
        compiler_params=_cparams(("arbitrary", "arbitrary")),
    )(up, dact, cw, cb)


def _head_select(lane, head):
    return ((lane >= head * SSM_HEAD_DIM) & (lane < (head + 1) * SSM_HEAD_DIM)).astype(BF16)


def _ssd_terms(x, bm, cm, dtr, bias, alog, dsk, valid, cfg):
    q, hg, gw, p = SSM_CHUNK, cfg.HG, cfg.GW, SSM_HEAD_DIM
    t = {}
    e_mat = _head_select(lax.broadcasted_iota(jnp.int32, (hg, gw), 1), lax.broadcasted_iota(jnp.int32, (hg, gw), 0))
    li = lax.broadcasted_iota(jnp.int32, (q, q), 0)
    si = lax.broadcasted_iota(jnp.int32, (q, q), 1)
    tri = li >= si
    tril = tri.astype(BF16)
    triu = (li <= si).astype(BF16)
    pre = dtr + bias
    dtv = jnp.where(valid, _softplus(pre), 0.0)
    a_head = -jnp.exp(alog)
    a = dtv * a_head
    cs = _dot3_left(tril, a)
    hi, mid, lo = _split3(a)
    cst = _dot_tn(hi, triu) + _dot_tn(mid, triu) + _dot_tn(lo, triu)
    cs_last = jnp.sum(a, axis=0, keepdims=True)
    dte = jnp.exp(jnp.minimum(cs_last - cs, 0.0))
    ecs = jnp.exp(cs)
    t.update(e_mat=e_mat, tri=tri, tril=tril, triu=triu, pre=pre, dtv=dtv, a_head=a_head, cs=cs, cst=cst,
             dec=jnp.exp(cs_last), dte=dte, ecs=ecs)
    t["dtv_x"] = _dot3(dtv, e_mat)
    t["ecs_x"] = _dot3(ecs, e_mat)
    t["dte_x"] = _dot3(dte, e_mat)
    t["dec_x"] = _dot3(t["dec"], e_mat)
    t["d_x"] = _dot3(dsk, e_mat)
    t["xdt"] = x * t["dtv_x"]
    t["gm"] = _dot_nt(cm.astype(BF16), bm.astype(BF16))
    return t


def _ssd_decay(t, e):
    diff = t["cs"][:, e:e + 1] - t["cst"][e:e + 1, :]
    return jnp.where(t["tri"], jnp.exp(jnp.minimum(diff, 0.0)), 0.0)


def _head_mask(b):
    lane = lax.broadcasted_iota(jnp.int32, (1, 2 * SSM_HEAD_DIM), 1)
    return (lane >= SSM_HEAD_DIM * b) & (lane < SSM_HEAD_DIM * (b + 1))


def _ssd_specs(cfg):
    q, g, gw, ns, hg, di = SSM_CHUNK, SSM_GROUPS, cfg.GW, SSM_STATE, cfg.HG, cfg.DI
    return dict(
        x=pl.BlockSpec((q, gw), lambda j, c: (c, j)),
        b=pl.BlockSpec((q, ns), lambda j, c: (c, di // ns + j)),
        c=pl.BlockSpec((q, ns), lambda j, c: (c, di // ns + g + j)),
        z=pl.BlockSpec((q, gw), lambda j, c: (c, j)),
        dtr=pl.BlockSpec((None, q, hg), lambda j, c: (j, c, 0)),
        prm=pl.BlockSpec((None, 8, hg), lambda j, c: (j, 0, 0)),
        gate=pl.BlockSpec((1, gw), lambda j, c: (0, j)),
        st=pl.BlockSpec((None, None, ns, gw), lambda j, c: (c, j, 0, 0)),
    )


GATHER_FORWARD_STEPS = 16


def _ssd_fwd(xbc, zx, dtr, prm, gate_g, *, cfg, name, gather=None):
    q, g, gw, ns, hg, p = SSM_CHUNK, SSM_GROUPS, cfg.GW, SSM_STATE, cfg.HG, SSM_HEAD_DIM
    nc = cfg.LP // q
    sp = _ssd_specs(cfg)

    def body(x_ref, b_ref, c_ref, z_ref, dtr_ref, prm_ref, gg_ref, y_ref, hn_ref, st_ref, s_ref):
        c = pl.program_id(1)

        @pl.when(c == 0)
        def _():
            s_ref[...] = jnp.zeros((ns, gw), F32)

        s_prev = s_ref[...]
        st_ref[...] = s_prev
        x, bm, cm = x_ref[...], b_ref[...], c_ref[...]
        valid = _row_ids(c, q) >= cfg.PF
        t = _ssd_terms(x, bm, cm, dtr_ref[...], prm_ref[0:1, :], prm_ref[1:2, :], prm_ref[2:3, :], valid, cfg)
        parts = []
        for pr in range(hg // 2):
            xp = t["xdt"][:, 2 * p * pr:2 * p * (pr + 1)]
            acc = None
            for b in range(2):
                m = (t["gm"] * _ssd_decay(t, 2 * pr + b)).astype(BF16)
                d = _dot(m, jnp.where(_head_mask(b), xp, 0.0).astype(BF16))
                acc = d if acc is None else acc + d
            parts.append(acc)
        y_diag = jnp.concatenate(parts, axis=1)
        y_off = _dot(cm.astype(BF16), s_prev.astype(BF16)) * t["ecs_x"]
        y = y_diag + y_off + x * t["d_x"]
        y_ref[...] = y
        s_ref[...] = s_prev * t["dec_x"] + _dot_tn(bm.astype(BF16), (t["xdt"] * t["dte_x"]).astype(BF16))
        z = z_ref[...]
        hgate = y * z * _sigmoid(z)
        r = lax.rsqrt(jnp.mean(hgate * hgate, axis=-1, keepdims=True) + NORM_EPS)
        hn_ref[...] = (hgate * r * gg_ref[...]).astype(BF16)

    in_specs = [sp["x"], sp["b"], sp["c"], sp["z"], sp["dtr"], sp["prm"], sp["gate"]]
    out_specs = [sp["x"], sp["x"], sp["st"]]
    out_shape = [jax.ShapeDtypeStruct((cfg.LP, cfg.DI), F32), jax.ShapeDtypeStruct((cfg.LP, cfg.DI), BF16),
                 jax.ShapeDtypeStruct((nc, g, ns, gw), F32)]
    scratch = [pltpu.VMEM((ns, gw), F32)]
    args = (xbc, xbc, xbc, zx, dtr, prm, gate_g)
    if gather is None:
        kernel_body = body
    else:
        n_in, n_out = len(in_specs), len(out_specs)
        fwd_c = max(nc - GATHER_FORWARD_STEPS, 0)

        def kernel_body(*refs):
            ins, blk_ref = refs[:n_in], refs[n_in]
            outs, all_ref = refs[n_in + 1:n_in + 1 + n_out], refs[n_in + 1 + n_out]
            s_ref = refs[n_in + n_out + 2]
            sems = refs[n_in + n_out + 3:]
            j, c = pl.program_id(0), pl.program_id(1)

            @pl.when((j == 0) & (c == 0))
            def _():
                _Gather(blk_ref, all_ref, *sems).start()

            body(*ins, *outs, s_ref)

            @pl.when((j == g - 1) & (c == fwd_c))
            def _():
                _Gather(blk_ref, all_ref, *sems).forward()

            @pl.when((j == g - 1) & (c == nc - 1))
            def _():
                _Gather(blk_ref, all_ref, *sems).finish()

        in_specs, out_specs = in_specs + [HBM_SPEC], out_specs + [HBM_SPEC]
        out_shape = out_shape + [jax.ShapeDtypeStruct((N_DEV,) + gather.shape, gather.dtype)]
        scratch = scratch + list(GATHER_SEMS)
        args = args + (gather,)
    return pl.pallas_call(
        kernel_body, name=name, grid=(g, nc), in_specs=in_specs, out_specs=out_specs, out_shape=out_shape,
        scratch_shapes=scratch, compiler_params=_cparams(("arbitrary", "arbitrary")),
    )(*args)


def _ssd_bwd(xbc, zx, dtr, prm, gate_g, y, states, dhn, *, cfg, name, send=None):
    q, g, gw, ns, hg, p = SSM_CHUNK, SSM_GROUPS, cfg.GW, SSM_STATE, cfg.HG, SSM_HEAD_DIM
    nc = cfg.LP // q
    sp = _ssd_specs(cfg)
    rev = lambda spec: pl.BlockSpec(spec.block_shape, (lambda f: (lambda j, c: f(j, nc - 1 - c)))(spec.index_map))
    bc_spec = pl.BlockSpec((q, ns), lambda j, c: (nc - 1 - c, j))

    def body(x_ref, b_ref, c_ref, z_ref, dtr_ref, prm_ref, gg_ref, y_ref, st_ref, dhn_ref,
             dx_ref, db_ref, dc_ref, dz_ref, ddtr_ref, dprm_ref, dgg_ref, ds_ref):
        ci = pl.program_id(1)
        c = nc - 1 - ci

        @pl.when(ci == 0)
        def _():
            ds_ref[...] = jnp.zeros((ns, gw), F32)
            dprm_ref[...] = jnp.zeros((8, hg), F32)
            dgg_ref[...] = jnp.zeros((8, gw), F32)

        ds_next = ds_ref[...]
        s_prev = st_ref[...]
        x, bm, cm = x_ref[...], b_ref[...], c_ref[...]
        valid = _row_ids(c, q) >= cfg.PF
        t = _ssd_terms(x, bm, cm, dtr_ref[...], prm_ref[0:1, :], prm_ref[1:2, :], prm_ref[2:3, :], valid, cfg)
        et_mat = _head_select(lax.broadcasted_iota(jnp.int32, (gw, hg), 0), lax.broadcasted_iota(jnp.int32, (gw, hg), 1))
        heads = lambda v: _dot3(v, et_mat)
        yv, z = y_ref[...], z_ref[...]
        sz = _sigmoid(z)
        silu = z * sz
        hgate = yv * silu
        r = lax.rsqrt(jnp.mean(hgate * hgate, axis=-1, keepdims=True) + NORM_EPS)
        hhat = hgate * r
        dhn = dhn_ref[...]
        dgg = jnp.sum(dhn * hhat, axis=0, keepdims=True)
        dhh = dhn * gg_ref[...]
        dhgate = r * (dhh - hhat * jnp.mean(dhh * hhat, axis=-1, keepdims=True))
        dy = dhgate * silu
        dz_ref[...] = (dhgate * yv * sz * (1.0 + z * (1.0 - sz))).astype(BF16)
        bmb, cmb = bm.astype(BF16), cm.astype(BF16)
        xdt = t["xdt"]
        dye = dy * t["ecs_x"]
        dxdt_state = _dot(bmb, ds_next.astype(BF16)) * t["dte_x"]
        dg_acc = None
        dparts = []
        li = lax.broadcasted_iota(jnp.int32, (q, q), 0)
        si = lax.broadcasted_iota(jnp.int32, (q, q), 1)
        head_id = lax.broadcasted_iota(jnp.int32, (1, hg), 1)
        da = None
        for pr in range(hg // 2):
            sl = slice(2 * p * pr, 2 * p * (pr + 1))
            xp, dyp = xdt[:, sl], dy[:, sl]
            acc = None
            for b in range(2):
                lm = _ssd_decay(t, 2 * pr + b)
                m = (t["gm"] * lm).astype(BF16)
                dym = jnp.where(_head_mask(b), dyp, 0.0).astype(BF16)
                d = _dot_tn(m, dym)
                acc = d if acc is None else acc + d
                dm = _dot_nt(dym, xp.astype(BF16)) * lm
                dg_acc = dm if dg_acc is None else dg_acc + dm
                corner = _dot(t["triu"], (dm * t["gm"]).astype(BF16))
                da_e = jnp.sum(jnp.where(si < li, corner, 0.0), axis=1, keepdims=True)
                da_e = da_e * (head_id == 2 * pr + b).astype(F32)
                da = da_e if da is None else da + da_e
            dparts.append(acc)
        dxdt = jnp.concatenate(dparts, axis=1) + dxdt_state
        dgb = dg_acc.astype(BF16)
        dc_ref[...] = _dot(dgb, bmb) + _dot_nt(dye.astype(BF16), s_prev.astype(BF16))
        xde = (xdt * t["dte_x"]).astype(BF16)
        db_ref[...] = _dot_tn(dgb, cmb) + _dot_nt(xde, ds_next.astype(BF16))
        ds_ref[...] = ds_next * t["dec_x"] + _dot_tn(cmb, dye.astype(BF16))
        y_off = _dot(cmb, s_prev.astype(BF16)) * t["ecs_x"]
        tril_strict = (li > si).astype(BF16)
        da = da + _dot3_left(t["triu"], heads(dy * y_off)) + _dot3_left(tril_strict, heads(xdt * dxdt_state)) \
            + t["dec"] * heads(jnp.sum(ds_next * s_prev, axis=0, keepdims=True))
        ddtv = da * t["a_head"] + heads(dxdt * x)
        ddtr = jnp.where(valid, ddtv * _sigmoid(t["pre"]), 0.0)
        ddtr_ref[...] = ddtr
        dx_ref[...] = dxdt * t["dtv_x"] + dy * t["d_x"]
        dalog = jnp.sum(da * t["dtv"], axis=0, keepdims=True) * t["a_head"]
        dprm_ref[0:1, :] += jnp.sum(ddtr, axis=0, keepdims=True)
        dprm_ref[1:2, :] += dalog
        dprm_ref[2:3, :] += heads(jnp.sum(dy * x, axis=0, keepdims=True))
        dgg_ref[0:1, :] += dgg

    in_specs = [rev(sp["x"]), rev(sp["b"]), rev(sp["c"]), rev(sp["z"]), rev(sp["dtr"]), sp["prm"], sp["gate"],
                rev(sp["x"]), rev(sp["st"]), rev(sp["x"])]
    out_specs = [rev(sp["x"]), bc_spec, bc_spec, rev(sp["x"]), rev(sp["dtr"]),
                 pl.BlockSpec((None, 8, hg), lambda j, c: (j, 0, 0)), pl.BlockSpec((None, 8, gw), lambda j, c: (j, 0, 0))]
    out_shape = [jax.ShapeDtypeStruct((cfg.LP, cfg.DI), F32), jax.ShapeDtypeStruct((cfg.LP, g * ns), F32),
                 jax.ShapeDtypeStruct((cfg.LP, g * ns), F32), jax.ShapeDtypeStruct((cfg.LP, cfg.DI), BF16),
                 jax.ShapeDtypeStruct((g, cfg.LP, hg), F32), jax.ShapeDtypeStruct((g, 8, hg), F32),
                 jax.ShapeDtypeStruct((g, 8, gw), F32)]
    scratch = [pltpu.VMEM((ns, gw), F32)]
    args = (xbc, xbc, xbc, zx, dtr, prm, gate_g, y, states, dhn)
    if send is None:
        kernel_body = body
    else:
        n_in, n_out = len(in_specs), len(out_specs)

        def kernel_body(*refs):
            ins, send_ref = refs[:n_in], refs[n_in]
            outs, recv_ref = refs[n_in + 1:n_in + 1 + n_out], refs[n_in + 1 + n_out]
            ds_ref, send_sems, recv_sems, local_sem = refs[n_in + n_out + 2:]
            copies = lambda: _all_to_all_copies(send_ref, recv_ref, send_sems, recv_sems, local_sem)

            @pl.when((pl.program_id(0) == 0) & (pl.program_id(1) == 0))
            def _():
                for cp in copies():
                    cp.start()

            body(*ins, *outs, ds_ref)

            @pl.when((pl.program_id(0) == g - 1) & (pl.program_id(1) == nc - 1))
            def _():
                for cp in copies():
                    cp.wait()

        in_specs, out_specs = in_specs + [HBM_SPEC], out_specs + [HBM_SPEC]
        out_shape = out_shape + [jax.ShapeDtypeStruct(send.shape, send.dtype)]
        scratch = scratch + [pltpu.SemaphoreType.DMA((N_DEV - 1,)), pltpu.SemaphoreType.DMA((N_DEV - 1,)),
                             pltpu.SemaphoreType.DMA]
        args = args + (send,)
    return pl.pallas_call(
        kernel_body, name=name, grid=(g, nc), in_specs=in_specs, out_specs=out_specs, out_shape=out_shape,
        scratch_shapes=scratch, compiler_params=_cparams(("arbitrary", "arbitrary")),
    )(*args)


ATT_STRIP = 32
ATT_NT = 2
ATT_DEAD = -120.0
ATT_NEG = -1e9
LOG2E = 1.4426950408889634
LN2 = 0.6931471805599453


def _att_init(u_ref, bias_ref, qb, cfg):
    @pl.when(qb == 0)
    def _():
        t = ATT_T
        rows = lax.broadcasted_iota(jnp.int32, (t, t), 0)
        cols = lax.broadcasted_iota(jnp.int32, (t, t), 1)
        u_ref[...] = (rows > cols).astype(BF16)
        pad = cols < cfg.PF % t
        bias_ref[0] = jnp.zeros((t, t), F32)
        bias_ref[1] = jnp.where(cols >= rows, ATT_NEG, 0.0)
        bias_ref[2] = jnp.where(pad, ATT_NEG, 0.0)
        bias_ref[3] = jnp.full((t, t), ATT_NEG, F32)
        bias_ref[4] = jnp.where((cols >= rows) | pad, ATT_NEG, 0.0)


def _att_tile(kv_ref, kb, qb, pf):
    t = ATT_T
    edge = pf // t
    off = pl.multiple_of(jnp.maximum(kb, 0) * t, t)
    k = kv_ref[pl.ds(off, t), 0:2 * SB_HEAD_DIM]
    v = kv_ref[pl.ds(off, t), 2 * SB_HEAD_DIM:]
    kind = jnp.where(kb < edge, 3, jnp.where(kb == qb, jnp.where(qb == edge, 4, 1), jnp.where(kb == edge, 2, 0)))
    return k, v, kind


def _att_trips(qb):
    n = qb + ATT_NT
    return lax.shift_right_logical(n, 1) if ATT_NT == 2 else lax.div(n, ATT_NT)


def _att_walk(qb, trip, carry_ref):
    def cond(st):
        j, alive = st
        return (j < _att_trips(qb)) & alive

    def step(st):
        j, _ = st
        trip(j, None)
        return j + 1, jnp.max(carry_ref[...]) > ATT_DEAD

    lax.while_loop(cond, step, (0, True))


def _att_carry(carry_ref, rsum_ref, u, b):
    carry = carry_ref[b]
    for up in range(u):
        carry = carry + rsum_ref[2 * up + b]
    return carry


def _att_scores_strip(s_ref, bias_ref, kind, b, r0):
    sl = pl.ds(r0, ATT_STRIP)
    s = s_ref[b, sl, :] + bias_ref[kind, sl, :]
    l1p = jnp.log2(1.0 + jnp.exp2(jnp.abs(s) * (-LOG2E))) * LN2
    ls = jnp.minimum(s, 0.0) - l1p
    return ls, ls - s


def _attn_fwd(q, kv, *, cfg, name):
    t, lp, d, rs, nch = ATT_T, cfg.LP, cfg.D, ATT_STRIP, 2 * ATT_NT
    hp, nq = d // (2 * SB_HEAD_DIM), lp // ATT_T
    scale = SB_HEAD_DIM ** -0.5

    def body(q_ref, kv_ref, o_ref, u_ref, bias_ref, s_ref, hi_ref, lo_ref, cum_ref, w_ref, cb_ref, acc_ref, carry_ref,
             rsum_ref):
        qb = pl.program_id(1)
        _att_init(u_ref, bias_ref, qb, cfg)
        lo_half = lax.broadcasted_iota(jnp.int32, (1, 2 * SB_HEAD_DIM), 1) < SB_HEAD_DIM
        q2 = q_ref[...] * scale
        qms = [jnp.where(lo_half, q2, 0).astype(BF16), jnp.where(lo_half, 0, q2).astype(BF16)]
        acc_ref[...] = jnp.zeros_like(acc_ref)
        carry_ref[...] = jnp.zeros_like(carry_ref)

        def trip(j, _):
            tiles = [_att_tile(kv_ref, qb - ATT_NT * j - u, qb, cfg.PF) for u in range(ATT_NT)]
            chains = [(u, b) for u in range(ATT_NT) for b in range(2)]
            for c, (u, b) in enumerate(chains):
                s_ref[c] = _dot_nt(qms[b], tiles[u][0])
            for c, (u, b) in enumerate(chains):
                for r0 in range(0, t, rs):
                    sl = pl.ds(r0, rs)
                    ls, lk = _att_scores_strip(s_ref, bias_ref, tiles[u][2], c, r0)
                    s_ref[c, sl, :] = ls
                    hi = lk.astype(BF16)
                    hi_ref[c, sl, :] = hi
                    lo_ref[c, sl, :] = (lk - hi.astype(F32)).astype(BF16)
                    rsum_ref[c, sl, :] = jnp.sum(lk, axis=1, keepdims=True)
            for c in range(len(chains)):
                cum_ref[c] = _dot(hi_ref[c], u_ref[...]) + _dot(lo_ref[c], u_ref[...])
            for c, (u, b) in enumerate(chains):
                cb_ref[c] = jnp.broadcast_to(_att_carry(carry_ref, rsum_ref, u, b), (t, t))
            for c, (u, b) in enumerate(chains):
                for r0 in range(0, t, rs):
                    sl = pl.ds(r0, rs)
                    x = s_ref[c, sl, :] + cum_ref[c, sl, :] + cb_ref[c, sl, :]
                    w_ref[c, sl, :] = jnp.exp2(x * LOG2E).astype(BF16)
            for c, (u, b) in enumerate(chains):
                acc_ref[b] += _dot(w_ref[c], tiles[u][1])
                carry_ref[b] += rsum_ref[c]

        _att_walk(qb, trip, carry_ref)
        o_ref[...] = jnp.where(lo_half, acc_ref[0], acc_ref[1])

    return pl.pallas_call(
        body, name=name, grid=(hp, nq),
        in_specs=[pl.BlockSpec((t, 128), lambda h, i: (i, h)), pl.BlockSpec((lp, 256), lambda h, i: (0, h))],
        out_specs=pl.BlockSpec((t, 128), lambda h, i: (i, h)),
        out_shape=jax.ShapeDtypeStruct((lp, d), F32),
        scratch_shapes=[pltpu.VMEM((t, t), BF16), pltpu.VMEM((5, t, t), F32), pltpu.VMEM((nch, t, t), F32),
                        pltpu.VMEM((nch, t, t), BF16), pltpu.VMEM((nch, t, t), BF16), pltpu.VMEM((nch, t, t), F32),
                        pltpu.VMEM((nch, t, t), BF16), pltpu.VMEM((nch, t, t), F32),
                        pltpu.VMEM((2, t, 2 * SB_HEAD_DIM), F32), pltpu.VMEM((2, t, 1), F32), pltpu.VMEM((nch, t, 1), F32)],
        compiler_params=_cparams(("arbitrary", "arbitrary")),
    )(q, kv)


def _attn_bwd(q, kv, o, do, *, cfg, name):
    t, lp, d, rs, nch = ATT_T, cfg.LP, cfg.D, ATT_STRIP, 2 * ATT_NT
    hp, nq = d // (2 * SB_HEAD_DIM), lp // ATT_T
    scale = SB_HEAD_DIM ** -0.5

    def body(q_ref, kv_ref, o_ref, do_ref, dq_ref, dkv_ref, u_ref, bias_ref, s_ref, dw_ref, hi_ref, lo_ref, cum_ref,
             w_ref, cb_ref, acc_ref, carry_ref, ecarry_ref, etot_ref, rsum_ref, ersum_ref):
        qb = pl.program_id(1)

        @pl.when(qb == 0)
        def _():
            dkv_ref[...] = jnp.zeros_like(dkv_ref)

        _att_init(u_ref, bias_ref, qb, cfg)
        lo_half = lax.broadcasted_iota(jnp.int32, (1, 2 * SB_HEAD_DIM), 1) < SB_HEAD_DIM
        q2 = q_ref[...] * scale
        do2 = do_ref[...]
        qms = [jnp.where(lo_half, q2, 0).astype(BF16), jnp.where(lo_half, 0, q2).astype(BF16)]
        doms = [jnp.where(lo_half, do2, 0).astype(BF16), jnp.where(lo_half, 0, do2).astype(BF16)]
        prod = do2.astype(F32) * o_ref[...]
        etot_ref[0] = jnp.sum(jnp.where(lo_half, prod, 0.0), axis=1, keepdims=True)
        etot_ref[1] = jnp.sum(jnp.where(lo_half, 0.0, prod), axis=1, keepdims=True)
        acc_ref[...] = jnp.zeros_like(acc_ref)
        carry_ref[...] = jnp.zeros_like(carry_ref)
        ecarry_ref[...] = jnp.zeros_like(ecarry_ref)

        def trip(j, _):
            kbs = [qb - ATT_NT * j - u for u in range(ATT_NT)]
            tiles = [_att_tile(kv_ref, kb, qb, cfg.PF) for kb in kbs]
            chains = [(u, b) for u in range(ATT_NT) for b in range(2)]
            for c, (u, b) in enumerate(chains):
                s_ref[c] = _dot_nt(qms[b], tiles[u][0])
                dw_ref[c] = _dot_nt(doms[b], tiles[u][1])
            for c, (u, b) in enumerate(chains):
                for r0 in range(0, t, rs):
                    sl = pl.ds(r0, rs)
                    ls, lk = _att_scores_strip(s_ref, bias_ref, tiles[u][2], c, r0)
                    s_ref[c, sl, :] = ls
                    hi = lk.astype(BF16)
                    hi_ref[c, sl, :] = hi
                    lo_ref[c, sl, :] = (lk - hi.astype(F32)).astype(BF16)
                    rsum_ref[c, sl, :] = jnp.sum(lk, axis=1, keepdims=True)
            for c in range(nch):
                cum_ref[c] = _dot(hi_ref[c], u_ref[...]) + _dot(lo_ref[c], u_ref[...])
            for c, (u, b) in enumerate(chains):
                cb_ref[c] = jnp.broadcast_to(_att_carry(carry_ref, rsum_ref, u, b), (t, t))
            for c, (u, b) in enumerate(chains):
                for r0 in range(0, t, rs):
                    sl = pl.ds(r0, rs)
                    wb = jnp.exp2((s_ref[c, sl, :] + cum_ref[c, sl, :] + cb_ref[c, sl, :]) * LOG2E).astype(BF16)
                    w_ref[c, sl, :] = wb
                    e = wb.astype(F32) * dw_ref[c, sl, :]
                    dw_ref[c, sl, :] = e
                    hi = e.astype(BF16)
                    hi_ref[c, sl, :] = hi
                    lo_ref[c, sl, :] = (e - hi.astype(F32)).astype(BF16)
                    ersum_ref[c, sl, :] = jnp.sum(e, axis=1, keepdims=True)
            for c in range(nch):
                cum_ref[c] = _dot(hi_ref[c], u_ref[...]) + _dot(lo_ref[c], u_ref[...])
            for c, (u, b) in enumerate(chains):
                cb_ref[c] = jnp.broadcast_to(etot_ref[b] - _att_carry(ecarry_ref, ersum_ref, u, b), (t, t))
            for c, (u, b) in enumerate(chains):
                for r0 in range(0, t, rs):
                    sl = pl.ds(r0, rs)
                    e = dw_ref[c, sl, :]
                    e_before = cb_ref[c, sl, :] - (e + cum_ref[c, sl, :])
                    sig = jnp.exp2(s_ref[c, sl, :] * LOG2E)
                    hi_ref[c, sl, :] = (e * (1.0 - sig) - sig * e_before).astype(BF16)
            for u in range(ATT_NT):
                off = pl.multiple_of(jnp.maximum(kbs[u], 0) * t, t)
                c0, c1 = 2 * u, 2 * u + 1
                acc_ref[0] += _dot(hi_ref[c0], tiles[u][0])
                acc_ref[1] += _dot(hi_ref[c1], tiles[u][0])
                dkv_ref[pl.ds(off, t), 0:2 * SB_HEAD_DIM] += _dot_tn(hi_ref[c0], qms[0]) + _dot_tn(hi_ref[c1], qms[1])
                dkv_ref[pl.ds(off, t), 2 * SB_HEAD_DIM:] += _dot_tn(w_ref[c0], doms[0]) + _dot_tn(w_ref[c1], doms[1])
                for b in range(2):
                    carry_ref[b] += rsum_ref[2 * u + b]
                    ecarry_ref[b] += ersum_ref[2 * u + b]

        _att_walk(qb, trip, carry_ref)
        dq_ref[...] = jnp.where(lo_half, acc_ref[0], acc_ref[1]) * scale

    big = lambda dt: pltpu.VMEM((nch, t, t), dt)
    col = lambda n: pltpu.VMEM((n, t, 1), F32)
    return pl.pallas_call(
        body, name=name, grid=(hp, nq),
        in_specs=[pl.BlockSpec((t, 128), lambda h, i: (i, h)), pl.BlockSpec((lp, 256), lambda h, i: (0, h)),
                  pl.BlockSpec((t, 128), lambda h, i: (i, h)), pl.BlockSpec((t, 128), lambda h, i: (i, h))],
        out_specs=[pl.BlockSpec((t, 128), lambda h, i: (i, h)), pl.BlockSpec((lp, 256), lambda h, i: (0, h))],
        out_shape=[jax.ShapeDtypeStruct((lp, d), F32), jax.ShapeDtypeStruct((lp, 2 * d), F32)],
        scratch_shapes=[pltpu.VMEM((t, t), BF16), pltpu.VMEM((5, t, t), F32), big(F32), big(F32), big(BF16), big(BF16),
                        big(F32), big(BF16), big(F32), pltpu.VMEM((2, t, 2 * SB_HEAD_DIM), F32), col(2), col(2), col(2),
                        col(nch), col(nch)],
        compiler_params=_cparams(("arbitrary", "arbitrary")),
    )(q, kv, o, do)


def _loss_head(h, g, target, *, cfg, name):
    lp, d = h.shape
    tm = SSM_CHUNK
    first = (cfg.PF + N_META) // tm
    assert (cfg.PF + N_META) % tm == 0

    def body(h_ref, g_ref, t_ref, loss_ref, dh_ref, dg_ref):
        i = pl.program_id(0)

        @pl.when(i == 0)
        def _():
            loss_ref[...] = jnp.zeros_like(loss_ref)
            dg_ref[...] = jnp.zeros_like(dg_ref)

        x = h_ref[...]
        r = lax.rsqrt(jnp.mean(x * x, axis=-1, keepdims=True) + NORM_EPS)
        xhat = x * r
        live = i >= first
        diff = jnp.where(live, xhat * g_ref[...] - t_ref[...], 0.0)
        loss_ref[...] += 0.5 * jnp.sum(jnp.mean(diff * diff, axis=-1, keepdims=True))
        dy = diff * (1.0 / d)
        dg_ref[...] += jnp.sum(dy * xhat, axis=0, keepdims=True)
        dxh = dy * g_ref[...]
        dh_ref[...] = r * (dxh - xhat * jnp.mean(dxh * xhat, axis=-1, keepdims=True))

    return pl.pallas_call(
        body, name=name, grid=(lp // tm,),
        in_specs=[pl.BlockSpec((tm, d), lambda i: (i, 0)), pl.BlockSpec((1, d), lambda i: (0, 0)),
                  pl.BlockSpec((tm, d), lambda i: (jnp.maximum(i - first, 0), 0))],
        out_specs=[pl.BlockSpec((8, 128), lambda i: (0, 0)), pl.BlockSpec((tm, d), lambda i: (i, 0)),
                   pl.BlockSpec((1, d), lambda i: (0, 0))],
        out_shape=[jax.ShapeDtypeStruct((8, 128), F32), jax.ShapeDtypeStruct((lp, d), F32),
                   jax.ShapeDtypeStruct((1, d), F32)],
        compiler_params=_cparams(("arbitrary",)),
    )(h, g, target)


def _conv_bwd_input(dc, cw, *, width, name):
    lp, cd = dc.shape
    tc = _tile(cd, 512, 128)
    tm = _tile(lp, 768, 8)
    ni = lp // tm

    def body(d_ref, w_ref, o_ref, ext_ref):
        @pl.when(pl.program_id(1) == 0)
        def _():
            ext_ref[tm:, :] = jnp.zeros((HALO, tc), F32)

        ext_ref[0:tm, :] = d_ref[...]
        for r0 in range(0, tm, CONV_STRIP):
            acc = ext_ref[pl.ds(r0 + width - 1, CONV_STRIP), :] * w_ref[0:1, :]
            for k in range(1, width):
                acc = acc + ext_ref[pl.ds(r0 + width - 1 - k, CONV_STRIP), :] * w_ref[k:k + 1, :]
            o_ref[pl.ds(r0, CONV_STRIP), :] = acc.astype(BF16)
        ext_ref[tm:, :] = d_ref[0:HALO, :]

    return pl.pallas_call(
        body, name=name, grid=(cd // tc, ni),
        in_specs=[pl.BlockSpec((tm, tc), lambda j, i: (ni - 1 - i, j)), pl.BlockSpec((width, tc), lambda j, i: (0, j))],
        out_specs=pl.BlockSpec((tm, tc), lambda j, i: (ni - 1 - i, j)),
        out_shape=jax.ShapeDtypeStruct((lp, cd), BF16),
        scratch_shapes=[pltpu.VMEM((tm + HALO, tc), F32)],
        compiler_params=_cparams(("arbitrary", "arbitrary")),
    )(dc, cw)


MATMUL_WEIGHTS = ("ssd_w_in", "ssd_w_out", "w_kv", "sb_w_q", "sb_w_o", "ffn_w_up", "ffn_w_down")


def _il(a, cfg):
    nb = cfg.DFF // FFN_IL
    lead = a.shape[:-1]
    a = a.reshape(lead + (2, nb, FFN_IL))
    return jnp.swapaxes(a, -3, -2).reshape(lead + (2 * cfg.DFF,))


def _unil(a, cfg):
    nb = cfg.DFF // FFN_IL
    lead = a.shape[:-1]
    a = a.reshape(lead + (nb, 2, FFN_IL))
    return jnp.swapaxes(a, -3, -2).reshape(lead + (2 * cfg.DFF,))


def _pair(a, cfg):
    hp = cfg.D // 128
    lead = a.shape[:-1]
    return jnp.swapaxes(a.reshape(lead + (2, hp, 128)), -3, -2).reshape(lead + (2 * cfg.D,))


def _unpair(a, cfg):
    hp = cfg.D // 128
    lead = a.shape[:-1]
    return jnp.swapaxes(a.reshape(lead + (hp, 2, 128)), -3, -2).reshape(lead + (2 * cfg.D,))


def _prepare(full, cfg):
    d, di, cd, h = cfg.D, cfg.DI, cfg.CONV_DIM, cfg.H
    w_in = full["ssd_w_in"][0].astype(BF16)
    prm = jnp.stack([full["ssd_dt_bias"][0], full["ssd_a_log"][0], full["ssd_d_skip"][0]]).astype(F32)
    prm = jnp.pad(prm.reshape(3, SSM_GROUPS, cfg.HG).transpose(1, 0, 2), ((0, 0), (0, 5), (0, 0)))
    p = dict(
        meta=full["meta_tokens"].astype(F32),
        ssd_norm=full["ssd_norm"].astype(F32).reshape(1, d),
        w_in=jnp.pad(w_in, ((0, 0), (0, DT_PAD - h))),
        conv_w=full["ssd_conv_w"][0].astype(F32), conv_b=full["ssd_conv_b"].astype(F32).reshape(1, cd),
        prm=prm, gate=full["ssd_gate_norm"].astype(F32).reshape(1, di),
        w_out=full["ssd_w_out"][0].astype(BF16),
        kv_norm=full["kv_norm"].astype(F32).reshape(1, d), sb_norm=full["sb_norm"].astype(F32).reshape(1, d),
        final_norm=full["final_norm"].astype(F32).reshape(1, d),
    )
    for i in range(2):
        p[f"ffn_norm{i}"] = full["ffn_norm"][i].astype(F32).reshape(1, d)
        p[f"fcw{i}"] = _il(full["ffn_conv_w"][i].astype(F32), cfg)
        p[f"fcb{i}"] = _il(full["ffn_conv_b"][i].astype(F32), cfg).reshape(1, 2 * cfg.DFF)
    if all(n in full for n in LATE_WEIGHTS):
        p.update(_prepare_late(full, cfg))
    return p


LATE_WEIGHTS = ("w_kv", "sb_w_q", "sb_w_o", "ffn_w_up", "ffn_w_down")


def _prepare_late(full, cfg):
    p = dict(w_kv=_pair(full["w_kv"].astype(BF16), cfg), w_q=full["sb_w_q"][0].astype(BF16),
             w_o=full["sb_w_o"][0].astype(BF16))
    for i in range(2):
        p[f"w_up{i}"] = _il(full["ffn_w_up"][i].astype(BF16), cfg)
        p[f"w_down{i}"] = full["ffn_w_down"][i].astype(BF16)
    return p


def _ffn_fwd(h, p, i, cfg):
    up, u = _norm_mm(h, p[f"ffn_norm{i}"], p[f"w_up{i}"], out_dtype=F32, name=f"ffn{i}_up")
    act = _ffn_conv_fwd(up, p[f"fcw{i}"], p[f"fcb{i}"], cfg=cfg, name=f"ffn{i}_conv")
    return _mm_res(act, p[f"w_down{i}"], h, pf=cfg.PF, name=f"ffn{i}_down"), (h, u, up, act)


def _ffn_bwd(dh, saved, p, i, cfg, grads):
    h, u, up, act = saved
    dact = _mm_nt(dh, p[f"w_down{i}"], out_dtype=F32, pf=cfg.PF, name=f"ffn{i}_dact")
    grads[f"w_down{i}"] = _mm_tn(act, dh, name=f"ffn{i}_dwdown")
    dhc, grads[f"fcw{i}"], grads[f"fcb{i}"] = _ffn_conv_bwd_pre(up, dact, p[f"fcw{i}"], p[f"fcb{i}"], cfg=cfg,
                                                                 name=f"ffn{i}_dconv")
    dup = _conv_bwd_input(dhc, p[f"fcw{i}"], width=FFN_CONV, name=f"ffn{i}_dup")
    grads[f"w_up{i}"] = _mm_tn(u, dup, name=f"ffn{i}_dwup")
    dh, grads[f"ffn_norm{i}"] = _mm_nt_normbwd(dup, p[f"w_up{i}"], h, p[f"ffn_norm{i}"], dh, pf=cfg.PF,
                                               name=f"ffn{i}_dh")
    return dh


EARLY_GRADS = ("ssd_w_out", "w_kv", "sb_w_q", "sb_w_o", "ffn_w_up", "ffn_conv_w", "ffn_w_down")


def _local_step(x, target, p, cfg, exchange=None, late_weights=None):
    d, di, cd, h_, lp, pf = cfg.D, cfg.DI, cfg.CONV_DIM, cfg.H, cfg.LP, cfg.PF
    g = {}
    h0 = jnp.concatenate([jnp.zeros((pf, d), F32), p["meta"], x], axis=0)
    zx, u1 = _norm_mm(h0, p["ssd_norm"], p["w_in"], out_dtype=F32, name="ssd_in")
    dtr = zx[:, di + cd:di + cd + h_].reshape(lp, SSM_GROUPS, cfg.HG).transpose(1, 0, 2)
    xbc = _ssd_conv_fwd(zx, p["conv_w"], p["conv_b"], cfg=cfg, name="ssd_conv")
    if late_weights is None:
        y, hn, states = _ssd_fwd(xbc, zx, dtr, p["prm"], p["gate"], cfg=cfg, name="ssd_scan")
    else:
        block, finish = late_weights
        y, hn, states, gathered = _ssd_fwd(xbc, zx, dtr, p["prm"], p["gate"], cfg=cfg, name="ssd_scan", gather=block)
        p = dict(p, **finish(gathered))
    h1 = _mm_res(hn, p["w_out"], h0, pf=pf, name="ssd_out")
    h2, ffn0 = _ffn_fwd(h1, p, 0, cfg)
    kv, ukv = _norm_mm(h2, p["kv_norm"], p["w_kv"], out_dtype=BF16, name="kv_proj")
    q, uq = _norm_mm(h2, p["sb_norm"], p["w_q"], out_dtype=BF16, name="q_proj")
    o = _attn_fwd(q, kv, cfg=cfg, name="attn_fwd")
    h3 = _mm_res(o, p["w_o"], h2, pf=pf, name="attn_out")
    h4, ffn1 = _ffn_fwd(h3, p, 1, cfg)
    loss8, dh, g["final_norm"] = _loss_head(h4, p["final_norm"], target, cfg=cfg, name="loss_head")
    dh = _ffn_bwd(dh, ffn1, p, 1, cfg, g)
    do = _mm_nt(dh, p["w_o"], out_dtype=BF16, pf=pf, name="attn_do")
    g["w_o"] = _mm_tn(o, dh, name="attn_dwo")
    dq, dkv = _attn_bwd(q, kv, o, do, cfg=cfg, name="attn_bwd")
    g["w_q"] = _mm_tn(uq, dq, name="attn_dwq")
    g["w_kv"] = _mm_tn(ukv, dkv, name="attn_dwkv")
    dh, g["sb_norm"] = _mm_nt_normbwd(dq, p["w_q"], h2, p["sb_norm"], dh, pf=pf, name="attn_dhq")
    dh, g["kv_norm"] = _mm_nt_normbwd(dkv, p["w_kv"], h2, p["kv_norm"], dh, pf=pf, name="attn_dhkv")
    dh = _ffn_bwd(dh, ffn0, p, 0, cfg, g)
    dhn = _mm_nt(dh, p["w_out"], out_dtype=F32, pf=pf, name="ssd_dhn")
    g["w_out"] = _mm_tn(hn, dh, name="ssd_dwout")
    done = dict(
        ssd_w_out=g["w_out"][None], w_kv=_unpair(g["w_kv"], cfg), sb_w_q=g["w_q"][None], sb_w_o=g["w_o"][None],
        ffn_w_up=jnp.stack([_unil(g["w_up0"], cfg), _unil(g["w_up1"], cfg)]),
        ffn_conv_w=jnp.stack([_unil(g["fcw0"], cfg), _unil(g["fcw1"], cfg)]),
        ffn_w_down=jnp.stack([g["w_down0"], g["w_down1"]]),
    )
    res = _ssd_bwd(xbc, zx, dtr, p["prm"], p["gate"], y, states, dhn, cfg=cfg, name="ssd_scan_bwd",
                   send=None if exchange is None else exchange(done))
    dx, db, dc, dz, ddtr, dprm, dgate = res[:7]
    dconv, g["conv_w"], g["conv_b"] = _ssd_conv_bwd_pre(zx, jnp.concatenate([dx, db, dc], axis=1), p["conv_w"],
                                                        p["conv_b"], cfg=cfg, name="ssd_dconv")
    dxpre = _conv_bwd_input(dconv, p["conv_w"], width=SSM_CONV, name="ssd_dxpre")
    ddt = jnp.pad(ddtr.transpose(1, 0, 2).reshape(lp, h_), ((0, 0), (0, DT_PAD - h_))).astype(BF16)
    dzx = jnp.concatenate([dz, dxpre, ddt], axis=1)
    dw_in = _mm_tn(u1, dzx, name="ssd_dwin")
    dh, g["ssd_norm"] = _mm_nt_normbwd(dzx, p["w_in"], h0, p["ssd_norm"], dh, pf=pf, name="ssd_dh")
    heads = lambda r: dprm[:, r, :].reshape(1, h_)
    out = dict(
        done,
        meta_tokens=dh[pf:pf + N_META], ssd_norm=g["ssd_norm"],
        ssd_w_in=dw_in[:, :di + cd + h_][None],
        ssd_conv_w=g["conv_w"][None], ssd_conv_b=g["conv_b"],
        ssd_dt_bias=heads(0), ssd_a_log=heads(1), ssd_d_skip=heads(2),
        ssd_gate_norm=dgate[:, 0, :].reshape(1, di),
        kv_norm=g["kv_norm"].reshape(d), sb_norm=g["sb_norm"],
        ffn_norm=jnp.concatenate([g["ffn_norm0"], g["ffn_norm1"]], axis=0),
        ffn_conv_b=jnp.concatenate([_unil(g["fcb0"], cfg), _unil(g["fcb1"], cfg)], axis=0),
        final_norm=g["final_norm"].reshape(d),
    )
    return loss8[0, 0], dh[pf + N_META:], out, (res[7] if exchange is not None else None)


MESH_AXES = ("x", "y", "c")
HBM_SPEC = pl.BlockSpec(memory_space=pltpu.HBM)


class _Gather:
    def __init__(self, x_ref, out_ref, send_sems, recv_sems, local_sem):
        x, y, c = lax.axis_index("x"), lax.axis_index("y"), lax.axis_index("c")
        me, sibling = (x, y, c), (x, y, 1 - c)
        chips = [(1 - x, y), (x, 1 - y), (1 - x, 1 - y)]

        def slot(px, py, pc):
            return out_ref.at[4 * px + 2 * py + pc]

        def copy(k, block, to, src=None):
            return pltpu.make_async_remote_copy(
                src_ref=slot(*block) if src is None else src, dst_ref=slot(*block),
                send_sem=send_sems.at[k], recv_sem=recv_sems.at[k], device_id=to, device_id_type=pl.DeviceIdType.MESH)

        self.mine = lambda: pltpu.make_async_copy(x_ref, slot(*me), local_sem)
        self.first = lambda: [copy(0, me, sibling, src=x_ref)] + [copy(1 + j, me, (*ch, c), src=x_ref)
                                                                  for j, ch in enumerate(chips)]
        self.passed = lambda: [copy(4 + j, (*ch, c), sibling) for j, ch in enumerate(chips)]
        self.over_ici = lambda: [copy(1 + j, (*ch, c), me) for j, ch in enumerate(chips)]
        self.from_sibling = lambda: [copy(0, sibling, me)] + [copy(4 + j, (*ch, 1 - c), me) for j, ch in enumerate(chips)]

    def start(self):
        self.mine().start()
        for cp in self.first():
            cp.start()

    def forward(self):
        for arrived, onward in zip(self.over_ici(), self.passed()):
            arrived.wait_recv()
            onward.start()

    def finish(self):
        for cp in self.from_sibling():
            cp.wait_recv()
        for cp in self.first() + self.passed():
            cp.wait_send()
        self.mine().wait()


GATHER_SEMS = [pltpu.SemaphoreType.DMA((7,)), pltpu.SemaphoreType.DMA((7,)), pltpu.SemaphoreType.DMA]


def _all_gather(blk, *, name):
    r, w = blk.shape

    def body(x_ref, out_ref, send_sems, recv_sems, local_sem):
        ag = _Gather(x_ref, out_ref, send_sems, recv_sems, local_sem)
        ag.start()
        ag.forward()
        ag.finish()

    return pl.pallas_call(
        body, name=name, out_shape=jax.ShapeDtypeStruct((N_DEV, r, w), blk.dtype),
        in_specs=[HBM_SPEC], out_specs=HBM_SPEC, scratch_shapes=list(GATHER_SEMS),
    )(blk)


def _all_to_all_copies(x_ref, out_ref, send_sems, recv_sems, local_sem):
    x, y, c = lax.axis_index("x"), lax.axis_index("y"), lax.axis_index("c")
    me_id = 4 * x + 2 * y + c
    copies = [pltpu.make_async_copy(x_ref.at[me_id], out_ref.at[me_id], local_sem)]
    for k in range(1, N_DEV):
        fx, fy, fc = (k >> 2) & 1, (k >> 1) & 1, k & 1
        px = 1 - x if fx else x
        py = 1 - y if fy else y
        pc = 1 - c if fc else c
        copies.append(pltpu.make_async_remote_copy(
            src_ref=x_ref.at[4 * px + 2 * py + pc], dst_ref=out_ref.at[me_id],
            send_sem=send_sems.at[k - 1], recv_sem=recv_sems.at[k - 1],
            device_id=(px, py, pc), device_id_type=pl.DeviceIdType.MESH))
    return copies


def _all_to_all(buf, *, name):
    n, r, w = buf.shape

    def body(x_ref, out_ref, send_sems, recv_sems, local_sem):
        copies = _all_to_all_copies(x_ref, out_ref, send_sems, recv_sems, local_sem)
        for cp in copies:
            cp.start()
        for cp in copies:
            cp.wait()

    return pl.pallas_call(
        body, name=name, out_shape=jax.ShapeDtypeStruct((n, r, w), buf.dtype),
        in_specs=[HBM_SPEC], out_specs=HBM_SPEC,
        scratch_shapes=[pltpu.SemaphoreType.DMA((7,)), pltpu.SemaphoreType.DMA((7,)), pltpu.SemaphoreType.DMA],
    )(buf)


def _sum_slots(buf, *, name):
    n, r, w = buf.shape
    tr = _tile(r, 1024, 16)

    def body(x_ref, o_ref):
        acc = x_ref[0].astype(F32)
        for s in range(1, n):
            acc = acc + x_ref[s].astype(F32)
        o_ref[...] = acc

    return pl.pallas_call(
        body, name=name, grid=(r // tr,),
        in_specs=[pl.BlockSpec((n, tr, w), lambda i: (0, i, 0))], out_specs=pl.BlockSpec((tr, w), lambda i: (i, 0)),
        out_shape=jax.ShapeDtypeStruct((r, w), F32), compiler_params=_cparams(("arbitrary",)),
    )(buf)


def _adamw(w, g, m, v, *, name):
    r, lanes = w.shape
    tr = _tile(r, ADAM_ROWS, 8)

    def body(w_ref, g_ref, m_ref, v_ref, d_ref, nm_ref, nv_ref):
        gg = g_ref[...]
        nm = ADAM_B1 * m_ref[...] + (1.0 - ADAM_B1) * gg
        nv = ADAM_B2 * v_ref[...] + (1.0 - ADAM_B2) * (gg * gg)
        m_hat = nm / (1.0 - ADAM_B1 ** ADAM_STEP)
        v_hat = nv / (1.0 - ADAM_B2 ** ADAM_STEP)
        d_ref[...] = -ADAM_LR * (m_hat / (jnp.sqrt(v_hat) + ADAM_EPS) + ADAM_WD * w_ref[...])
        nm_ref[...] = nm
        nv_ref[...] = nv

    spec = pl.BlockSpec((tr, lanes), lambda i: (i, 0))
    shp = jax.ShapeDtypeStruct((r, lanes), F32)
    return pl.pallas_call(body, name=name, grid=(r // tr,), in_specs=[spec] * 4, out_specs=[spec] * 3,
                          out_shape=[shp] * 3, compiler_params=_cparams(("arbitrary",)))(w, g, m, v)


PACK_QUANTUM = 16 * 128
ADAM_ROWS = 2048
INPUT_NAMES = ("x", "meta_tokens", "ssd_norm", "ssd_w_in", "ssd_conv_w", "ssd_conv_b", "ssd_dt_bias", "ssd_a_log",
               "ssd_d_skip", "ssd_gate_norm", "ssd_w_out", "kv_norm", "w_kv", "sb_norm", "sb_w_q", "sb_w_o", "ffn_norm",
               "ffn_w_up", "ffn_conv_w", "ffn_conv_b", "ffn_w_down", "final_norm")
WEIGHT_NAMES = INPUT_NAMES[1:]
SHARD_AXIS = dict(meta_tokens=1, ssd_norm=1, ssd_w_in=2, ssd_conv_w=2, ssd_conv_b=1, ssd_gate_norm=1, ssd_w_out=1, w_kv=1,
                  sb_w_q=1, sb_w_o=1, ffn_w_up=2, ffn_conv_w=2, ffn_w_down=1)
SMALL_SHARDED = ("meta_tokens", "ssd_norm", "ssd_conv_w", "ssd_conv_b", "ssd_gate_norm", "ffn_conv_w")
SHARDED = MATMUL_WEIGHTS + SMALL_SHARDED
REPLICATED = tuple(n for n in WEIGHT_NAMES if n not in SHARD_AXIS)


def _rows(shape):
    n = math.prod(shape)
    return -(-n // PACK_QUANTUM) * (PACK_QUANTUM // 128)


def _pack(arrs, dtype, lead=0):
    parts = []
    for a in arrs:
        ld = a.shape[:lead]
        n = math.prod(a.shape[lead:])
        f = a.reshape(ld + (n,)).astype(dtype)
        pad = _rows(a.shape[lead:]) * 128 - n
        if pad:
            f = jnp.pad(f, [(0, 0)] * lead + [(0, pad)])
        parts.append(f.reshape(ld + (-1, 128)))
    return jnp.concatenate(parts, axis=lead)


def _unpack(flat, shapes, lead=0):
    out, r0 = [], 0
    ld = flat.shape[:lead]
    for shp in shapes:
        rows, n = _rows(shp), math.prod(shp)
        piece = lax.slice_in_dim(flat, r0, r0 + rows, axis=lead).reshape(ld + (rows * 128,))
        out.append(lax.slice_in_dim(piece, 0, n, axis=lead).reshape(ld + tuple(shp)))
        r0 += rows
    return out


def _unshard(stacked, axis):
    a = jnp.moveaxis(stacked, 0, axis)
    shp = a.shape
    return a.reshape(shp[:axis] + (shp[axis] * shp[axis + 1],) + shp[axis + 2:])


def _to_shards(full, axis):
    shp = full.shape
    a = full.reshape(shp[:axis] + (N_DEV, shp[axis] // N_DEV) + shp[axis + 1:])
    return jnp.moveaxis(a, axis, 0)


def kernel(x, meta_tokens, ssd_norm, ssd_w_in, ssd_conv_w, ssd_conv_b, ssd_dt_bias, ssd_a_log, ssd_d_skip, ssd_gate_norm, ssd_w_out, kv_norm, w_kv, sb_norm, sb_w_q, sb_w_o, ffn_norm, ffn_w_up, ffn_conv_w, ffn_conv_b, ffn_w_down, final_norm, loss_target, m_meta_tokens, m_ssd_norm, m_ssd_w_in, m_ssd_conv_w, m_ssd_conv_b, m_ssd_dt_bias, m_ssd_a_log, m_ssd_d_skip, m_ssd_gate_norm, m_ssd_w_out, m_kv_norm, m_w_kv, m_sb_norm, m_sb_w_q, m_sb_w_o, m_ffn_norm, m_ffn_w_up, m_ffn_conv_w, m_ffn_conv_b, m_ffn_w_down, m_final_norm, v_meta_tokens, v_ssd_norm, v_ssd_w_in, v_ssd_conv_w, v_ssd_conv_b, v_ssd_dt_bias, v_ssd_a_log, v_ssd_d_skip, v_ssd_gate_norm, v_ssd_w_out, v_kv_norm, v_w_kv, v_sb_norm, v_sb_w_q, v_sb_w_o, v_ffn_norm, v_ffn_w_up, v_ffn_conv_w, v_ffn_conv_b, v_ffn_w_down, v_final_norm):
    local = dict(zip(WEIGHT_NAMES, (meta_tokens, ssd_norm, ssd_w_in, ssd_conv_w, ssd_conv_b, ssd_dt_bias, ssd_a_log, ssd_d_skip, ssd_gate_norm, ssd_w_out, kv_norm, w_kv, sb_norm, sb_w_q, sb_w_o, ffn_norm, ffn_w_up, ffn_conv_w, ffn_conv_b, ffn_w_down, final_norm)))
    mom = dict(zip(WEIGHT_NAMES, (m_meta_tokens, m_ssd_norm, m_ssd_w_in, m_ssd_conv_w, m_ssd_conv_b, m_ssd_dt_bias, m_ssd_a_log, m_ssd_d_skip, m_ssd_gate_norm, m_ssd_w_out, m_kv_norm, m_w_kv, m_sb_norm, m_sb_w_q, m_sb_w_o, m_ffn_norm, m_ffn_w_up, m_ffn_conv_w, m_ffn_conv_b, m_ffn_w_down, m_final_norm)))
    var = dict(zip(WEIGHT_NAMES, (v_meta_tokens, v_ssd_norm, v_ssd_w_in, v_ssd_conv_w, v_ssd_conv_b, v_ssd_dt_bias, v_ssd_a_log, v_ssd_d_skip, v_ssd_gate_norm, v_ssd_w_out, v_kv_norm, v_w_kv, v_sb_norm, v_sb_w_q, v_sb_w_o, v_ffn_norm, v_ffn_w_up, v_ffn_conv_w, v_ffn_conv_b, v_ffn_w_down, v_final_norm)))
    seq, d = x.shape[1], x.shape[2]
    cfg = _make_cfg(d, seq, ffn_w_down.shape[1] * N_DEV)

    first = tuple(n for n in MATMUL_WEIGHTS if n not in LATE_WEIGHTS)
    big = _all_gather(_pack([local[n] for n in first], BF16), name="gather_weights")
    small = _all_gather(_pack([local[n] for n in SMALL_SHARDED], F32), name="gather_small")

    def unpacked(names, buf):
        return {n: _unshard(stacked, SHARD_AXIS[n])
                for n, stacked in zip(names, _unpack(buf, [local[n].shape for n in names], lead=1))}

    full = {n: local[n] for n in REPLICATED}
    full.update(unpacked(first, big))
    full.update(unpacked(SMALL_SHARDED, small))
    late_weights = (_pack([local[n] for n in LATE_WEIGHTS], BF16),
                    lambda gathered: _prepare_late(unpacked(LATE_WEIGHTS, gathered), cfg))
    full_shape = {n: tuple(s * (N_DEV if a == SHARD_AXIS.get(n) else 1) for a, s in enumerate(local[n].shape))
                  for n in WEIGHT_NAMES}

    late = tuple(n for n in SHARDED if n not in EARLY_GRADS)
    for_peers = lambda gr, names: _pack([_to_shards(gr[n].reshape(full_shape[n]), SHARD_AXIS[n]) for n in names], BF16,
                                        lead=1)
    loss, grad_x, grads, got_early = _local_step(x[0], loss_target[0], _prepare(full, cfg), cfg,
                                                 exchange=lambda done: for_peers(done, EARLY_GRADS),
                                                 late_weights=late_weights)
    g_early = _sum_slots(got_early, name="sum_grads_early")
    g_late = _sum_slots(_all_to_all(for_peers(grads, late), name="scatter_grads"), name="sum_grads")
    rep = _all_gather(_pack([grads[n].reshape(local[n].shape) for n in REPLICATED], F32), name="gather_rep_grads")
    g_rep = _sum_slots(rep, name="sum_rep_grads")
    fill = jnp.zeros(((-(g_early.shape[0] + g_late.shape[0] + g_rep.shape[0])) % ADAM_ROWS, 128), F32)
    g_flat = jnp.concatenate([g_early, g_late, g_rep, fill], axis=0)

    order = EARLY_GRADS + late + REPLICATED
    flat = lambda src: jnp.concatenate([_pack([src[n] for n in order], F32), fill], axis=0)
    delta, new_m, new_v = _adamw(flat(local), g_flat, flat(mom), flat(var), name="adamw")
    shapes = [local[n].shape for n in order]
    pick = lambda buf: dict(zip(order, _unpack(buf, shapes)))
    g_out, d_out, m_out, v_out = pick(g_flat), pick(delta), pick(new_m), pick(new_v)
    loss = lax.psum(loss, MESH_AXES)
    return (loss, grad_x[None], *[g_out[n] for n in WEIGHT_NAMES], *[d_out[n] for n in WEIGHT_NAMES],
            *[m_out[n] for n in WEIGHT_NAMES], *[v_out[n] for n in WEIGHT_NAMES])
```

```python
import collections
import functools
import math

import jax
import jax.numpy as jnp
from jax import lax
from jax.experimental import pallas as pl
from jax.experimental.pallas import tpu as pltpu

F32 = jnp.float32
BF16 = jnp.bfloat16
NORM_EPS = 1e-6
N_META = 16
SSM_HEAD_DIM = 64
SSM_GROUPS = 4
SSM_STATE = 128
SSM_CONV = 4
SSM_CHUNK = 256
SB_HEAD_DIM = 64
FFN_CONV = 3
FFN_IL = 256
DT_PAD = 256
ATT_T = 256
ADAM_LR, ADAM_B1, ADAM_B2, ADAM_EPS, ADAM_WD, ADAM_STEP = 0.001, 0.9, 0.999, 1e-08, 0.01, 10
N_DEV = 8
VMEM_LIMIT = 56 * 1024 * 1024
MM_TILE = 1536

Cfg = collections.namedtuple("Cfg", "D SEQ LP PF DI H HG GW CONV_DIM DFF HS")


def _make_cfg(d_model, seq, d_ff):
    pf = (-N_META) % SSM_CHUNK
    lp = pf + N_META + seq
    assert lp % SSM_CHUNK == 0 and (pf + N_META) == SSM_CHUNK
    di = 2 * d_model
    h = di // SSM_HEAD_DIM
    return Cfg(D=d_model, SEQ=seq, LP=lp, PF=pf, DI=di, H=h, HG=h // SSM_GROUPS, GW=di // SSM_GROUPS,
               CONV_DIM=di + 2 * SSM_GROUPS * SSM_STATE, DFF=d_ff, HS=d_model // SB_HEAD_DIM)


def _tile(n, pref, mult):
    t = (min(pref, n) // mult) * mult
    while t > mult and n % t:
        t -= mult
    assert t >= mult and n % t == 0, (n, pref, mult)
    return t


def _cparams(sem):
    return pltpu.CompilerParams(dimension_semantics=sem, vmem_limit_bytes=VMEM_LIMIT)


def _dot(a, b):
    return jnp.dot(a, b, preferred_element_type=F32)


def _dot_nt(a, b):
    return lax.dot_general(a, b, (((1,), (1,)), ((), ())), preferred_element_type=F32)


def _dot_tn(a, b):
    return lax.dot_general(a, b, (((0,), (0,)), ((), ())), preferred_element_type=F32)


def _split3(v):
    hi = v.astype(BF16)
    r1 = v - hi.astype(F32)
    mid = r1.astype(BF16)
    lo = (r1 - mid.astype(F32)).astype(BF16)
    return hi, mid, lo


def _split2(v):
    hi = v.astype(BF16)
    lo = (v - hi.astype(F32)).astype(BF16)
    return hi, lo


def _dot3(a_f32, b_bf16):
    hi, mid, lo = _split3(a_f32)
    return _dot(hi, b_bf16) + _dot(mid, b_bf16) + _dot(lo, b_bf16)


def _dot3_left(a_bf16, b_f32):
    hi, mid, lo = _split3(b_f32)
    return _dot(a_bf16, hi) + _dot(a_bf16, mid) + _dot(a_bf16, lo)


def _sigmoid(x):
    return 0.5 * jnp.tanh(0.5 * x) + 0.5


def _softplus(x):
    return jnp.maximum(x, 0.0) + jnp.log(1.0 + jnp.exp(-jnp.abs(x)))


def _row_ids(i, tm, shape_cols=1):
    return i * tm + lax.broadcasted_iota(jnp.int32, (tm, shape_cols), 0)


def _norm_mm(h, g, w, *, out_dtype, name, scale=None):
    lp, d = h.shape
    n = w.shape[1]
    tm = _tile(lp, 768, 8)
    tn = _tile(n, MM_TILE, 128)

    def body(h_ref, g_ref, w_ref, y_ref, u_ref):
        @pl.when(pl.program_id(1) == 0)
        def _():
            x = h_ref[...]
            r = lax.rsqrt(jnp.mean(x * x, axis=-1, keepdims=True) + NORM_EPS)
            u_ref[...] = (x * r * g_ref[...]).astype(BF16)
        y = _dot(u_ref[...], w_ref[...])
        if scale is not None:
            y = y * scale
        y_ref[...] = y.astype(out_dtype)

    return pl.pallas_call(
        body, name=name, grid=(lp // tm, n // tn),
        in_specs=[pl.BlockSpec((tm, d), lambda i, j: (i, 0)), pl.BlockSpec((1, d), lambda i, j: (0, 0)),
                  pl.BlockSpec((d, tn), lambda i, j: (0, j))],
        out_specs=[pl.BlockSpec((tm, tn), lambda i, j: (i, j)), pl.BlockSpec((tm, d), lambda i, j: (i, 0))],
        out_shape=[jax.ShapeDtypeStruct((lp, n), out_dtype), jax.ShapeDtypeStruct((lp, d), BF16)],
        compiler_params=_cparams(("arbitrary", "arbitrary")),
    )(h, g, w)


def _mm_res(a, w, h, *, pf, name):
    lp, k = a.shape
    d = w.shape[1]
    tm = _tile(lp, 768, 8)

    def body(a_ref, w_ref, h_ref, o_ref):
        y = _dot(a_ref[...].astype(BF16), w_ref[...])
        rows = _row_ids(pl.program_id(0), tm)
        o_ref[...] = h_ref[...] + jnp.where(rows >= pf, y, 0.0)

    return pl.pallas_call(
        body, name=name, grid=(lp // tm,),
        in_specs=[pl.BlockSpec((tm, k), lambda i: (i, 0)), pl.BlockSpec((k, d), lambda i: (0, 0)),
                  pl.BlockSpec((tm, d), lambda i: (i, 0))],
        out_specs=pl.BlockSpec((tm, d), lambda i: (i, 0)),
        out_shape=jax.ShapeDtypeStruct((lp, d), F32),
        compiler_params=_cparams(("arbitrary",)),
    )(a, w, h)


def _mm_nt(dy, w, *, out_dtype, pf, name):
    lp, n = dy.shape
    k = w.shape[0]
    tm = _tile(lp, 768, 8)
    tk = _tile(k, MM_TILE, 128)

    def body(dy_ref, w_ref, o_ref):
        y = _dot_nt(dy_ref[...].astype(BF16), w_ref[...])
        rows = _row_ids(pl.program_id(0), tm)
        o_ref[...] = jnp.where(rows >= pf, y, 0.0).astype(out_dtype)

    return pl.pallas_call(
        body, name=name, grid=(lp // tm, k // tk),
        in_specs=[pl.BlockSpec((tm, n), lambda i, j: (i, 0)), pl.BlockSpec((tk, n), lambda i, j: (j, 0))],
        out_specs=pl.BlockSpec((tm, tk), lambda i, j: (i, j)),
        out_shape=jax.ShapeDtypeStruct((lp, k), out_dtype),
        compiler_params=_cparams(("arbitrary", "arbitrary")),
    )(dy, w)


def _mm_nt_normbwd(dy, w, h, g, dh_in, *, pf, name, scale=None):
    lp, n = dy.shape
    d = w.shape[0]
    tm = _tile(lp, 768, 8)
    tn = _tile(n, MM_TILE, 128)
    nj = n // tn

    def body(dy_ref, w_ref, h_ref, g_ref, dhin_ref, dh_ref, dg_ref, acc_ref):
        i, j = pl.program_id(0), pl.program_id(1)

        @pl.when(j == 0)
        def _():
            acc_ref[...] = jnp.zeros_like(acc_ref)

        @pl.when((i == 0) & (j == 0))
        def _():
            dg_ref[...] = jnp.zeros_like(dg_ref)

        acc_ref[...] += _dot_nt(dy_ref[...].astype(BF16), w_ref[...])

        @pl.when(j == nj - 1)
        def _():
            du = acc_ref[...]
            if scale is not None:
                du = du * scale
            x = h_ref[...]
            r = lax.rsqrt(jnp.mean(x * x, axis=-1, keepdims=True) + NORM_EPS)
            xhat = x * r
            dg_ref[...] += jnp.sum(du * xhat, axis=0, keepdims=True)
            dxh = du * g_ref[...]
            dx = r * (dxh - xhat * jnp.mean(dxh * xhat, axis=-1, keepdims=True))
            rows = _row_ids(i, tm)
            dh_ref[...] = jnp.where(rows >= pf, dhin_ref[...] + dx, 0.0)

    return pl.pallas_call(
        body, name=name, grid=(lp // tm, nj),
        in_specs=[pl.BlockSpec((tm, tn), lambda i, j: (i, j)), pl.BlockSpec((d, tn), lambda i, j: (0, j)),
                  pl.BlockSpec((tm, d), lambda i, j: (i, 0)), pl.BlockSpec((1, d), lambda i, j: (0, 0)),
                  pl.BlockSpec((tm, d), lambda i, j: (i, 0))],
        out_specs=[pl.BlockSpec((tm, d), lambda i, j: (i, 0)), pl.BlockSpec((1, d), lambda i, j: (0, 0))],
        out_shape=[jax.ShapeDtypeStruct((lp, d), F32), jax.ShapeDtypeStruct((1, d), F32)],
        scratch_shapes=[pltpu.VMEM((tm, d), F32)],
        compiler_params=_cparams(("arbitrary", "arbitrary")),
    )(dy, w, h, g, dh_in)


def _mm_tn(a, b, *, name, scale=None):
    lp, k = a.shape
    n = b.shape[1]
    tm = _tile(lp, 768, 8)
    tk = _tile(k, MM_TILE, 128)
    tn = _tile(n, MM_TILE, 128)
    nm = lp // tm

    def body(a_ref, b_ref, o_ref, acc_ref):
        m = pl.program_id(2)

        @pl.when(m == 0)
        def _():
            acc_ref[...] = jnp.zeros_like(acc_ref)

        acc_ref[...] += _dot_tn(a_ref[...].astype(BF16), b_ref[...].astype(BF16))

        @pl.when(m == nm - 1)
        def _():
            o_ref[...] = acc_ref[...] if scale is None else acc_ref[...] * scale

    return pl.pallas_call(
        body, name=name, grid=(k // tk, n // tn, nm),
        in_specs=[pl.BlockSpec((tm, tk), lambda i, j, m: (m, i)), pl.BlockSpec((tm, tn), lambda i, j, m: (m, j))],
        out_specs=pl.BlockSpec((tk, tn), lambda i, j, m: (i, j)),
        out_shape=jax.ShapeDtypeStruct((k, n), F32),
        scratch_shapes=[pltpu.VMEM((tk, tn), F32)],
        compiler_params=_cparams(("arbitrary", "arbitrary", "arbitrary")),
    )(a, b)


HALO = 8


CONV_STRIP = 32


def _conv_load(ext_ref, x_ref):
    @pl.when(pl.program_id(1) == 0)
    def _():
        ext_ref[0:HALO, :] = jnp.zeros((HALO, ext_ref.shape[1]), F32)

    ext_ref[HALO:, :] = x_ref[...].astype(F32)


def _conv_strip(ext_ref, w_ref, b_ref, r0, width):
    taps = [ext_ref[pl.ds(r0 + HALO - (width - 1 - k), CONV_STRIP), :] for k in range(width)]
    acc = b_ref[...] + taps[0] * w_ref[0:1, :]
    for k in range(1, width):
        acc = acc + taps[k] * w_ref[k:k + 1, :]
    return acc, taps


def _strip_live(i, tm, r0, pf):
    return i * tm + r0 + lax.broadcasted_iota(jnp.int32, (CONV_STRIP, 1), 0) >= pf


def _fold8(x):
    out = x[0:8]
    for r in range(8, x.shape[0], 8):
        out = out + x[r:r + 8]
    return out


def _conv_grad_flush(acc_ref, dw_ref, db_ref, width, last):
    @pl.when(last)
    def _():
        for k in range(width):
            dw_ref[k:k + 1, :] = jnp.sum(acc_ref[8 * k:8 * k + 8, :], axis=0, keepdims=True)
        db_ref[...] = jnp.sum(acc_ref[8 * width:8 * width + 8, :], axis=0, keepdims=True)


def _ssd_conv_fwd(zx, cw, cb, *, cfg, name):
    lp, cd, di = cfg.LP, cfg.CONV_DIM, cfg.DI
    tc = _tile(math.gcd(di, cd), 512, 128)
    tm = _tile(lp, 768, 8)
    off = di // tc

    def body(x_ref, w_ref, b_ref, o_ref, ext_ref):
        i = pl.program_id(1)
        _conv_load(ext_ref, x_ref)
        for r0 in range(0, tm, CONV_STRIP):
            acc, _ = _conv_strip(ext_ref, w_ref, b_ref, r0, SSM_CONV)
            o_ref[pl.ds(r0, CONV_STRIP), :] = jnp.where(_strip_live(i, tm, r0, cfg.PF), acc * _sigmoid(acc), 0.0)
        ext_ref[0:HALO, :] = ext_ref[tm:tm + HALO, :]

    return pl.pallas_call(
        body, name=name, grid=(cd // tc, lp // tm),
        in_specs=[pl.BlockSpec((tm, tc), lambda j, i: (i, j + off)), pl.BlockSpec((SSM_CONV, tc), lambda j, i: (0, j)),
                  pl.BlockSpec((1, tc), lambda j, i: (0, j))],
        out_specs=pl.BlockSpec((tm, tc), lambda j, i: (i, j)),
        out_shape=jax.ShapeDtypeStruct((lp, cd), F32),
        scratch_shapes=[pltpu.VMEM((tm + HALO, tc), F32)],
        compiler_params=_cparams(("arbitrary", "arbitrary")),
    )(zx, cw, cb)


def _ssd_conv_bwd_pre(zx, dxbc, cw, cb, *, cfg, name):
    lp, cd, di = cfg.LP, cfg.CONV_DIM, cfg.DI
    tc = _tile(math.gcd(di, cd), 512, 128)
    tm = _tile(lp, 768, 8)
    off = di // tc

    def body(x_ref, d_ref, w_ref, b_ref, dc_ref, dw_ref, db_ref, ext_ref, acc_ref):
        i = pl.program_id(1)

        @pl.when(i == 0)
        def _():
            acc_ref[...] = jnp.zeros_like(acc_ref)

        _conv_load(ext_ref, x_ref)
        for r0 in range(0, tm, CONV_STRIP):
            sl = pl.ds(r0, CONV_STRIP)
            c, taps = _conv_strip(ext_ref, w_ref, b_ref, r0, SSM_CONV)
            sg = _sigmoid(c)
            dc = jnp.where(_strip_live(i, tm, r0, cfg.PF), d_ref[sl, :].astype(F32) * sg * (1.0 + c * (1.0 - sg)), 0.0)
            dc_ref[sl, :] = dc.astype(BF16)
            for k in range(SSM_CONV):
                acc_ref[8 * k:8 * k + 8, :] += _fold8(dc * taps[k])
            acc_ref[8 * SSM_CONV:, :] += _fold8(dc)
        ext_ref[0:HALO, :] = ext_ref[tm:tm + HALO, :]
        _conv_grad_flush(acc_ref, dw_ref, db_ref, SSM_CONV, i == lp // tm - 1)

    return pl.pallas_call(
        body, name=name, grid=(cd // tc, lp // tm),
        in_specs=[pl.BlockSpec((tm, tc), lambda j, i: (i, j + off)), pl.BlockSpec((tm, tc), lambda j, i: (i, j)),
                  pl.BlockSpec((SSM_CONV, tc), lambda j, i: (0, j)), pl.BlockSpec((1, tc), lambda j, i: (0, j))],
        out_specs=[pl.BlockSpec((tm, tc), lambda j, i: (i, j)), pl.BlockSpec((SSM_CONV, tc), lambda j, i: (0, j)),
                   pl.BlockSpec((1, tc), lambda j, i: (0, j))],
        out_shape=[jax.ShapeDtypeStruct((lp, cd), BF16), jax.ShapeDtypeStruct((SSM_CONV, cd), F32),
                   jax.ShapeDtypeStruct((1, cd), F32)],
        scratch_shapes=[pltpu.VMEM((tm + HALO, tc), F32), pltpu.VMEM((8 * (SSM_CONV + 1), tc), F32)],
        compiler_params=_cparams(("arbitrary", "arbitrary")),
    )(zx, dxbc, cw, cb)


def _ffn_conv_fwd(up, cw, cb, *, cfg, name):
    lp, dff = cfg.LP, cfg.DFF
    tc = 2 * FFN_IL
    tm = _tile(lp, 768, 8)

    def body(x_ref, w_ref, b_ref, o_ref, ext_ref):
        i = pl.program_id(1)
        _conv_load(ext_ref, x_ref)
        for r0 in range(0, tm, CONV_STRIP):
            hc, _ = _conv_strip(ext_ref, w_ref, b_ref, r0, FFN_CONV)
            gc, vc = hc[:, :FFN_IL], hc[:, FFN_IL:]
            act = jnp.where(_strip_live(i, tm, r0, cfg.PF), gc * _sigmoid(gc) * vc, 0.0)
            o_ref[pl.ds(r0, CONV_STRIP), :] = act.astype(BF16)
        ext_ref[0:HALO, :] = ext_ref[tm:tm + HALO, :]

    return pl.pallas_call(
        body, name=name, grid=(dff // FFN_IL, lp // tm),
        in_specs=[pl.BlockSpec((tm, tc), lambda j, i: (i, j)), pl.BlockSpec((FFN_CONV, tc), lambda j, i: (0, j)),
                  pl.BlockSpec((1, tc), lambda j, i: (0, j))],
        out_specs=pl.BlockSpec((tm, FFN_IL), lambda j, i: (i, j)),
        out_shape=jax.ShapeDtypeStruct((lp, dff), BF16),
        scratch_shapes=[pltpu.VMEM((tm + HALO, tc), F32)],
        compiler_params=_cparams(("arbitrary", "arbitrary")),
    )(up, cw, cb)


def _ffn_conv_bwd_pre(up, dact, cw, cb, *, cfg, name):
    lp, dff = cfg.LP, cfg.DFF
    tc = 2 * FFN_IL
    tm = _tile(lp, 768, 8)

    def body(x_ref, d_ref, w_ref, b_ref, dc_ref, dw_ref, db_ref, ext_ref, acc_ref):
        i = pl.program_id(1)

        @pl.when(i == 0)
        def _():
            acc_ref[...] = jnp.zeros_like(acc_ref)

        _conv_load(ext_ref, x_ref)
        for r0 in range(0, tm, CONV_STRIP):
            sl = pl.ds(r0, CONV_STRIP)
            hc, taps = _conv_strip(ext_ref, w_ref, b_ref, r0, FFN_CONV)
            gc, vc = hc[:, :FFN_IL], hc[:, FFN_IL:]
            sg = _sigmoid(gc)
            da = jnp.where(_strip_live(i, tm, r0, cfg.PF), d_ref[sl, :].astype(F32), 0.0)
            dc = jnp.concatenate([da * vc * sg * (1.0 + gc * (1.0 - sg)), da * gc * sg], axis=1)
            dc_ref[sl, :] = dc.astype(BF16)
            for k in range(FFN_CONV):
                acc_ref[8 * k:8 * k + 8, :] += _fold8(dc * taps[k])
            acc_ref[8 * FFN_CONV:, :] += _fold8(dc)
        ext_ref[0:HALO, :] = ext_ref[tm:tm + HALO, :]
        _conv_grad_flush(acc_ref, dw_ref, db_ref, FFN_CONV, i == lp // tm - 1)

    return pl.pallas_call(
        body, name=name, grid=(dff // FFN_IL, lp // tm),
        in_specs=[pl.BlockSpec((tm, tc), lambda j, i: (i, j)), pl.BlockSpec((tm, FFN_IL), lambda j, i: (i, j)),
                  pl.BlockSpec((FFN_CONV, tc), lambda j, i: (0, j)), pl.BlockSpec((1, tc), lambda j, i: (0, j))],
        out_specs=[pl.BlockSpec((tm, tc), lambda j, i: (i, j)), pl.BlockSpec((FFN_CONV, tc), lambda j, i: (0, j)),
                   pl.BlockSpec((1, tc), lambda j, i: (0, j))],
        out_shape=[jax.ShapeDtypeStruct((lp, 2 * dff), BF16), jax.ShapeDtypeStruct((FFN_CONV, 2 * dff), F32),
                   jax.ShapeDtypeStruct((1, 2 * dff), F32)],
        scratch_shapes=[pltpu.VMEM((tm + HALO, tc), F32), pltpu.VMEM((8 * (FFN_CONV + 1), tc), F32)],
        compiler_params=_cparams(("arbitrary", "arbitrary")),
    )(up, dact, cw, cb)


def _head_select(lane, head):
    return ((lane >= head * SSM_HEAD_DIM) & (lane < (head + 1) * SSM_HEAD_DIM)).astype(BF16)


def _ssd_terms(x, bm, cm, dtr, bias, alog, dsk, valid, cfg):
    q, hg, gw, p = SSM_CHUNK, cfg.HG, cfg.GW, SSM_HEAD_DIM
    t = {}
    e_mat = _head_select(lax.broadcasted_iota(jnp.int32, (hg, gw), 1), lax.broadcasted_iota(jnp.int32, (hg, gw), 0))
    li = lax.broadcasted_iota(jnp.int32, (q, q), 0)
    si = lax.broadcasted_iota(jnp.int32, (q, q), 1)
    tri = li >= si
    tril = tri.astype(BF16)
    triu = (li <= si).astype(BF16)
    pre = dtr + bias
    dtv = jnp.where(valid, _softplus(pre), 0.0)
    a_head = -jnp.exp(alog)
    a = dtv * a_head
    cs = _dot3_left(tril, a)
    hi, mid, lo = _split3(a)
    cst = _dot_tn(hi, triu) + _dot_tn(mid, triu) + _dot_tn(lo, triu)
    cs_last = jnp.sum(a, axis=0, keepdims=True)
    dte = jnp.exp(jnp.minimum(cs_last - cs, 0.0))
    ecs = jnp.exp(cs)
    t.update(e_mat=e_mat, tri=tri, tril=tril, triu=triu, pre=pre, dtv=dtv, a_head=a_head, cs=cs, cst=cst,
             dec=jnp.exp(cs_last), dte=dte, ecs=ecs)
    t["dtv_x"] = _dot3(dtv, e_mat)
    t["ecs_x"] = _dot3(ecs, e_mat)
    t["dte_x"] = _dot3(dte, e_mat)
    t["dec_x"] = _dot3(t["dec"], e_mat)
    t["d_x"] = _dot3(dsk, e_mat)
    t["xdt"] = x * t["dtv_x"]
    t["gm"] = _dot_nt(cm.astype(BF16), bm.astype(BF16))
    return t


def _ssd_decay(t, e):
    diff = t["cs"][:, e:e + 1] - t["cst"][e:e + 1, :]
    return jnp.where(t["tri"], jnp.exp(jnp.minimum(diff, 0.0)), 0.0)


def _head_mask(b):
    lane = lax.broadcasted_iota(jnp.int32, (1, 2 * SSM_HEAD_DIM), 1)
    return (lane >= SSM_HEAD_DIM * b) & (lane < SSM_HEAD_DIM * (b + 1))


def _ssd_specs(cfg):
    q, g, gw, ns, hg, di = SSM_CHUNK, SSM_GROUPS, cfg.GW, SSM_STATE, cfg.HG, cfg.DI
    return dict(
        x=pl.BlockSpec((q, gw), lambda j, c: (c, j)),
        b=pl.BlockSpec((q, ns), lambda j, c: (c, di // ns + j)),
        c=pl.BlockSpec((q, ns), lambda j, c: (c, di // ns + g + j)),
        z=pl.BlockSpec((q, gw), lambda j, c: (c, j)),
        dtr=pl.BlockSpec((None, q, hg), lambda j, c: (j, c, 0)),
        prm=pl.BlockSpec((None, 8, hg), lambda j, c: (j, 0, 0)),
        gate=pl.BlockSpec((1, gw), lambda j, c: (0, j)),
        st=pl.BlockSpec((None, None, ns, gw), lambda j, c: (c, j, 0, 0)),
    )


GATHER_FORWARD_STEPS = 16


def _ssd_fwd(xbc, zx, dtr, prm, gate_g, *, cfg, name, gather=None):
    q, g, gw, ns, hg, p = SSM_CHUNK, SSM_GROUPS, cfg.GW, SSM_STATE, cfg.HG, SSM_HEAD_DIM
    nc = cfg.LP // q
    sp = _ssd_specs(cfg)

    def body(x_ref, b_ref, c_ref, z_ref, dtr_ref, prm_ref, gg_ref, y_ref, hn_ref, st_ref, s_ref):
        c = pl.program_id(1)

        @pl.when(c == 0)
        def _():
            s_ref[...] = jnp.zeros((ns, gw), F32)

        s_prev = s_ref[...]
        st_ref[...] = s_prev
        x, bm, cm = x_ref[...], b_ref[...], c_ref[...]
        valid = _row_ids(c, q) >= cfg.PF
        t = _ssd_terms(x, bm, cm, dtr_ref[...], prm_ref[0:1, :], prm_ref[1:2, :], prm_ref[2:3, :], valid, cfg)
        parts = []
        for pr in range(hg // 2):
            xp = t["xdt"][:, 2 * p * pr:2 * p * (pr + 1)]
            acc = None
            for b in range(2):
                m = (t["gm"] * _ssd_decay(t, 2 * pr + b)).astype(BF16)
                d = _dot(m, jnp.where(_head_mask(b), xp, 0.0).astype(BF16))
                acc = d if acc is None else acc + d
            parts.append(acc)
        y_diag = jnp.concatenate(parts, axis=1)
        y_off = _dot(cm.astype(BF16), s_prev.astype(BF16)) * t["ecs_x"]
        y = y_diag + y_off + x * t["d_x"]
        y_ref[...] = y
        s_ref[...] = s_prev * t["dec_x"] + _dot_tn(bm.astype(BF16), (t["xdt"] * t["dte_x"]).astype(BF16))
        z = z_ref[...]
        hgate = y * z * _sigmoid(z)
        r = lax.rsqrt(jnp.mean(hgate * hgate, axis=-1, keepdims=True) + NORM_EPS)
        hn_ref[...] = (hgate * r * gg_ref[...]).astype(BF16)

    in_specs = [sp["x"], sp["b"], sp["c"], sp["z"], sp["dtr"], sp["prm"], sp["gate"]]
    out_specs = [sp["x"], sp["x"], sp["st"]]
    out_shape = [jax.ShapeDtypeStruct((cfg.LP, cfg.DI), F32), jax.ShapeDtypeStruct((cfg.LP, cfg.DI), BF16),
                 jax.ShapeDtypeStruct((nc, g, ns, gw), F32)]
    scratch = [pltpu.VMEM((ns, gw), F32)]
    args = (xbc, xbc, xbc, zx, dtr, prm, gate_g)
    if gather is None:
        kernel_body = body
    else:
        n_in, n_out = len(in_specs), len(out_specs)
        fwd_c = max(nc - GATHER_FORWARD_STEPS, 0)

        def kernel_body(*refs):
            ins, blk_ref = refs[:n_in], refs[n_in]
            outs, all_ref = refs[n_in + 1:n_in + 1 + n_out], refs[n_in + 1 + n_out]
            s_ref = refs[n_in + n_out + 2]
            sems = refs[n_in + n_out + 3:]
            j, c = pl.program_id(0), pl.program_id(1)

            @pl.when((j == 0) & (c == 0))
            def _():
                _Gather(blk_ref, all_ref, *sems).start()

            body(*ins, *outs, s_ref)

            @pl.when((j == g - 1) & (c == fwd_c))
            def _():
                _Gather(blk_ref, all_ref, *sems).forward()

            @pl.when((j == g - 1) & (c == nc - 1))
            def _():
                _Gather(blk_ref, all_ref, *sems).finish()

        in_specs, out_specs = in_specs + [HBM_SPEC], out_specs + [HBM_SPEC]
        out_shape = out_shape + [jax.ShapeDtypeStruct((N_DEV,) + gather.shape, gather.dtype)]
        scratch = scratch + list(GATHER_SEMS)
        args = args + (gather,)
    return pl.pallas_call(
        kernel_body, name=name, grid=(g, nc), in_specs=in_specs, out_specs=out_specs, out_shape=out_shape,
        scratch_shapes=scratch, compiler_params=_cparams(("arbitrary", "arbitrary")),
    )(*args)


def _ssd_bwd(xbc, zx, dtr, prm, gate_g, y, states, dhn, *, cfg, name, send=None):
    q, g, gw, ns, hg, p = SSM_CHUNK, SSM_GROUPS, cfg.GW, SSM_STATE, cfg.HG, SSM_HEAD_DIM
    nc = cfg.LP // q
    sp = _ssd_specs(cfg)
    rev = lambda spec: pl.BlockSpec(spec.block_shape, (lambda f: (lambda j, c: f(j, nc - 1 - c)))(spec.index_map))
    bc_spec = pl.BlockSpec((q, ns), lambda j, c: (nc - 1 - c, j))

    def body(x_ref, b_ref, c_ref, z_ref, dtr_ref, prm_ref, gg_ref, y_ref, st_ref, dhn_ref,
             dx_ref, db_ref, dc_ref, dz_ref, ddtr_ref, dprm_ref, dgg_ref, ds_ref):
        ci = pl.program_id(1)
        c = nc - 1 - ci

        @pl.when(ci == 0)
        def _():
            ds_ref[...] = jnp.zeros((ns, gw), F32)
            dprm_ref[...] = jnp.zeros((8, hg), F32)
            dgg_ref[...] = jnp.zeros((8, gw), F32)

        ds_next = ds_ref[...]
        s_prev = st_ref[...]
        x, bm, cm = x_ref[...], b_ref[...], c_ref[...]
        valid = _row_ids(c, q) >= cfg.PF
        t = _ssd_terms(x, bm, cm, dtr_ref[...], prm_ref[0:1, :], prm_ref[1:2, :], prm_ref[2:3, :], valid, cfg)
        et_mat = _head_select(lax.broadcasted_iota(jnp.int32, (gw, hg), 0), lax.broadcasted_iota(jnp.int32, (gw, hg), 1))
        heads = lambda v: _dot3(v, et_mat)
        yv, z = y_ref[...], z_ref[...]
        sz = _sigmoid(z)
        silu = z * sz
        hgate = yv * silu
        r = lax.rsqrt(jnp.mean(hgate * hgate, axis=-1, keepdims=True) + NORM_EPS)
        hhat = hgate * r
        dhn = dhn_ref[...].astype(F32)
        dgg = jnp.sum(dhn * hhat, axis=0, keepdims=True)
        dhh = dhn * gg_ref[...]
        dhgate = r * (dhh - hhat * jnp.mean(dhh * hhat, axis=-1, keepdims=True))
        dy = dhgate * silu
        dz_ref[...] = (dhgate * yv * sz * (1.0 + z * (1.0 - sz))).astype(BF16)
        bmb, cmb = bm.astype(BF16), cm.astype(BF16)
        xdt = t["xdt"]
        dye = dy * t["ecs_x"]
        dxdt_state = _dot(bmb, ds_next.astype(BF16)) * t["dte_x"]
        dg_acc = None
        dparts = []
        li = lax.broadcasted_iota(jnp.int32, (q, q), 0)
        si = lax.broadcasted_iota(jnp.int32, (q, q), 1)
        head_id = lax.broadcasted_iota(jnp.int32, (1, hg), 1)
        da = None
        for pr in range(hg // 2):
            sl = slice(2 * p * pr, 2 * p * (pr + 1))
            xp, dyp = xdt[:, sl], dy[:, sl]
            acc = None
            for b in range(2):
                lm = _ssd_decay(t, 2 * pr + b)
                m = (t["gm"] * lm).astype(BF16)
                dym = jnp.where(_head_mask(b), dyp, 0.0).astype(BF16)
                d = _dot_tn(m, dym)
                acc = d if acc is None else acc + d
                dm = _dot_nt(dym, xp.astype(BF16)) * lm
                dg_acc = dm if dg_acc is None else dg_acc + dm
                corner = _dot(t["triu"], (dm * t["gm"]).astype(BF16))
                da_e = jnp.sum(jnp.where(si < li, corner, 0.0), axis=1, keepdims=True)
                da_e = da_e * (head_id == 2 * pr + b).astype(F32)
                da = da_e if da is None else da + da_e
            dparts.append(acc)
        dxdt = jnp.concatenate(dparts, axis=1) + dxdt_state
        dgb = dg_acc.astype(BF16)
        dc_ref[...] = (_dot(dgb, bmb) + _dot_nt(dye.astype(BF16), s_prev.astype(BF16))).astype(BF16)
        xde = (xdt * t["dte_x"]).astype(BF16)
        db_ref[...] = (_dot_tn(dgb, cmb) + _dot_nt(xde, ds_next.astype(BF16))).astype(BF16)
        ds_ref[...] = ds_next * t["dec_x"] + _dot_tn(cmb, dye.astype(BF16))
        y_off = _dot(cmb, s_prev.astype(BF16)) * t["ecs_x"]
        tril_strict = (li > si).astype(BF16)
        da = da + _dot3_left(t["triu"], heads(dy * y_off)) + _dot3_left(tril_strict, heads(xdt * dxdt_state)) \
            + t["dec"] * heads(jnp.sum(ds_next * s_prev, axis=0, keepdims=True))
        ddtv = da * t["a_head"] + heads(dxdt * x)
        ddtr = jnp.where(valid, ddtv * _sigmoid(t["pre"]), 0.0)
        ddtr_ref[...] = ddtr
        dx_ref[...] = (dxdt * t["dtv_x"] + dy * t["d_x"]).astype(BF16)
        dalog = jnp.sum(da * t["dtv"], axis=0, keepdims=True) * t["a_head"]
        dprm_ref[0:1, :] += jnp.sum(ddtr, axis=0, keepdims=True)
        dprm_ref[1:2, :] += dalog
        dprm_ref[2:3, :] += heads(jnp.sum(dy * x, axis=0, keepdims=True))
        dgg_ref[0:1, :] += dgg

    in_specs = [rev(sp["x"]), rev(sp["b"]), rev(sp["c"]), rev(sp["z"]), rev(sp["dtr"]), sp["prm"], sp["gate"],
                rev(sp["x"]), rev(sp["st"]), rev(sp["x"])]
    out_specs = [rev(sp["x"]), bc_spec, bc_spec, rev(sp["x"]), rev(sp["dtr"]),
                 pl.BlockSpec((None, 8, hg), lambda j, c: (j, 0, 0)), pl.BlockSpec((None, 8, gw), lambda j, c: (j, 0, 0))]
    out_shape = [jax.ShapeDtypeStruct((cfg.LP, cfg.DI), BF16), jax.ShapeDtypeStruct((cfg.LP, g * ns), BF16),
                 jax.ShapeDtypeStruct((cfg.LP, g * ns), BF16), jax.ShapeDtypeStruct((cfg.LP, cfg.DI), BF16),
                 jax.ShapeDtypeStruct((g, cfg.LP, hg), F32), jax.ShapeDtypeStruct((g, 8, hg), F32),
                 jax.ShapeDtypeStruct((g, 8, gw), F32)]
    scratch = [pltpu.VMEM((ns, gw), F32)]
    args = (xbc, xbc, xbc, zx, dtr, prm, gate_g, y, states, dhn)
    if send is None:
        kernel_body = body
    else:
        n_in, n_out = len(in_specs), len(out_specs)

        def kernel_body(*refs):
            ins, send_ref = refs[:n_in], refs[n_in]
            outs, recv_ref = refs[n_in + 1:n_in + 1 + n_out], refs[n_in + 1 + n_out]
            ds_ref, send_sems, recv_sems, local_sem = refs[n_in + n_out + 2:]
            copies = lambda: _all_to_all_copies(send_ref, recv_ref, send_sems, recv_sems, local_sem)

            @pl.when((pl.program_id(0) == 0) & (pl.program_id(1) == 0))
            def _():
                for cp in copies():
                    cp.start()

            body(*ins, *outs, ds_ref)

            @pl.when((pl.program_id(0) == g - 1) & (pl.program_id(1) == nc - 1))
            def _():
                for cp in copies():
                    cp.wait()

        in_specs, out_specs = in_specs + [HBM_SPEC], out_specs + [HBM_SPEC]
        out_shape = out_shape + [jax.ShapeDtypeStruct(send.shape, send.dtype)]
        scratch = scratch + [pltpu.SemaphoreType.DMA((N_DEV - 1,)), pltpu.SemaphoreType.DMA((N_DEV - 1,)),
                             pltpu.SemaphoreType.DMA]
        args = args + (send,)
    return pl.pallas_call(
        kernel_body, name=name, grid=(g, nc), in_specs=in_specs, out_specs=out_specs, out_shape=out_shape,
        scratch_shapes=scratch, compiler_params=_cparams(("arbitrary", "arbitrary")),
    )(*args)


ATT_STRIP = 32
ATT_NT = 2
ATT_DEAD = -120.0
ATT_NEG = -1e9
LOG2E = 1.4426950408889634
LN2 = 0.6931471805599453


def _att_init(u_ref, bias_ref, qb, cfg):
    @pl.when(qb == 0)
    def _():
        t = ATT_T
        rows = lax.broadcasted_iota(jnp.int32, (t, t), 0)
        cols = lax.broadcasted_iota(jnp.int32, (t, t), 1)
        u_ref[...] = (rows > cols).astype(BF16)
        pad = cols < cfg.PF % t
        bias_ref[0] = jnp.zeros((t, t), F32)
        bias_ref[1] = jnp.where(cols >= rows, ATT_NEG, 0.0)
        bias_ref[2] = jnp.where(pad, ATT_NEG, 0.0)
        bias_ref[3] = jnp.full((t, t), ATT_NEG, F32)
        bias_ref[4] = jnp.where((cols >= rows) | pad, ATT_NEG, 0.0)


def _att_tile(kv_ref, kb, qb, pf):
    t = ATT_T
    edge = pf // t
    off = pl.multiple_of(jnp.maximum(kb, 0) * t, t)
    k = kv_ref[pl.ds(off, t), 0:2 * SB_HEAD_DIM]
    v = kv_ref[pl.ds(off, t), 2 * SB_HEAD_DIM:]
    kind = jnp.where(kb < edge, 3, jnp.where(kb == qb, jnp.where(qb == edge, 4, 1), jnp.where(kb == edge, 2, 0)))
    return k, v, kind


def _att_trips(qb):
    n = qb + ATT_NT
    return lax.shift_right_logical(n, 1) if ATT_NT == 2 else lax.div(n, ATT_NT)


def _att_walk(qb, trip, carry_ref):
    def cond(st):
        j, alive = st
        return (j < _att_trips(qb)) & alive

    def step(st):
        j, _ = st
        trip(j, None)
        return j + 1, jnp.max(carry_ref[...]) > ATT_DEAD

    lax.while_loop(cond, step, (0, True))


def _att_carry(carry_ref, rsum_ref, u, b):
    carry = carry_ref[b]
    for up in range(u):
        carry = carry + rsum_ref[2 * up + b]
    return carry


def _att_scores_strip(s_ref, bias_ref, kind, b, r0):
    sl = pl.ds(r0, ATT_STRIP)
    s = s_ref[b, sl, :] + bias_ref[kind, sl, :]
    l1p = jnp.log2(1.0 + jnp.exp2(jnp.abs(s) * (-LOG2E))) * LN2
    ls = jnp.minimum(s, 0.0) - l1p
    return ls, ls - s


def _attn_fwd(q, kv, *, cfg, name):
    t, lp, d, rs, nch = ATT_T, cfg.LP, cfg.D, ATT_STRIP, 2 * ATT_NT
    hp, nq = d // (2 * SB_HEAD_DIM), lp // ATT_T
    scale = SB_HEAD_DIM ** -0.5

    def body(q_ref, kv_ref, o_ref, u_ref, bias_ref, s_ref, hi_ref, lo_ref, cum_ref, w_ref, cb_ref, acc_ref, carry_ref,
             rsum_ref):
        qb = pl.program_id(1)
        _att_init(u_ref, bias_ref, qb, cfg)
        lo_half = lax.broadcasted_iota(jnp.int32, (1, 2 * SB_HEAD_DIM), 1) < SB_HEAD_DIM
        q2 = q_ref[...] * scale
        qms = [jnp.where(lo_half, q2, 0).astype(BF16), jnp.where(lo_half, 0, q2).astype(BF16)]
        acc_ref[...] = jnp.zeros_like(acc_ref)
        carry_ref[...] = jnp.zeros_like(carry_ref)

        def trip(j, _):
            tiles = [_att_tile(kv_ref, qb - ATT_NT * j - u, qb, cfg.PF) for u in range(ATT_NT)]
            chains = [(u, b) for u in range(ATT_NT) for b in range(2)]
            for c, (u, b) in enumerate(chains):
                s_ref[c] = _dot_nt(qms[b], tiles[u][0])
            for c, (u, b) in enumerate(chains):
                for r0 in range(0, t, rs):
                    sl = pl.ds(r0, rs)
                    ls, lk = _att_scores_strip(s_ref, bias_ref, tiles[u][2], c, r0)
                    s_ref[c, sl, :] = ls
                    hi = lk.astype(BF16)
                    hi_ref[c, sl, :] = hi
                    lo_ref[c, sl, :] = (lk - hi.astype(F32)).astype(BF16)
                    rsum_ref[c, sl, :] = jnp.sum(lk, axis=1, keepdims=True)
            for c in range(len(chains)):
                cum_ref[c] = _dot(hi_ref[c], u_ref[...]) + _dot(lo_ref[c], u_ref[...])
            for c, (u, b) in enumerate(chains):
                cb_ref[c] = jnp.broadcast_to(_att_carry(carry_ref, rsum_ref, u, b), (t, t))
            for c, (u, b) in enumerate(chains):
                for r0 in range(0, t, rs):
                    sl = pl.ds(r0, rs)
                    x = s_ref[c, sl, :] + cum_ref[c, sl, :] + cb_ref[c, sl, :]
                    w_ref[c, sl, :] = jnp.exp2(x * LOG2E).astype(BF16)
            for c, (u, b) in enumerate(chains):
                acc_ref[b] += _dot(w_ref[c], tiles[u][1])
                carry_ref[b] += rsum_ref[c]

        _att_walk(qb, trip, carry_ref)
        o_ref[...] = jnp.where(lo_half, acc_ref[0], acc_ref[1])

    return pl.pallas_call(
        body, name=name, grid=(hp, nq),
        in_specs=[pl.BlockSpec((t, 128), lambda h, i: (i, h)), pl.BlockSpec((lp, 256), lambda h, i: (0, h))],
        out_specs=pl.BlockSpec((t, 128), lambda h, i: (i, h)),
        out_shape=jax.ShapeDtypeStruct((lp, d), F32),
        scratch_shapes=[pltpu.VMEM((t, t), BF16), pltpu.VMEM((5, t, t), F32), pltpu.VMEM((nch, t, t), F32),
                        pltpu.VMEM((nch, t, t), BF16), pltpu.VMEM((nch, t, t), BF16), pltpu.VMEM((nch, t, t), F32),
                        pltpu.VMEM((nch, t, t), BF16), pltpu.VMEM((nch, t, t), F32),
                        pltpu.VMEM((2, t, 2 * SB_HEAD_DIM), F32), pltpu.VMEM((2, t, 1), F32), pltpu.VMEM((nch, t, 1), F32)],
        compiler_params=_cparams(("arbitrary", "arbitrary")),
    )(q, kv)


def _attn_bwd(q, kv, o, do, *, cfg, name):
    t, lp, d, rs, nch = ATT_T, cfg.LP, cfg.D, ATT_STRIP, 2 * ATT_NT
    hp, nq = d // (2 * SB_HEAD_DIM), lp // ATT_T
    scale = SB_HEAD_DIM ** -0.5

    def body(q_ref, kv_ref, o_ref, do_ref, dq_ref, dkv_ref, u_ref, bias_ref, s_ref, dw_ref, hi_ref, lo_ref, cum_ref,
             w_ref, cb_ref, acc_ref, carry_ref, ecarry_ref, etot_ref, rsum_ref, ersum_ref):
        qb = pl.program_id(1)

        @pl.when(qb == 0)
        def _():
            dkv_ref[...] = jnp.zeros_like(dkv_ref)

        _att_init(u_ref, bias_ref, qb, cfg)
        lo_half = lax.broadcasted_iota(jnp.int32, (1, 2 * SB_HEAD_DIM), 1) < SB_HEAD_DIM
        q2 = q_ref[...] * scale
        do2 = do_ref[...]
        qms = [jnp.where(lo_half, q2, 0).astype(BF16), jnp.where(lo_half, 0, q2).astype(BF16)]
        doms = [jnp.where(lo_half, do2, 0).astype(BF16), jnp.where(lo_half, 0, do2).astype(BF16)]
        prod = do2.astype(F32) * o_ref[...]
        etot_ref[0] = jnp.sum(jnp.where(lo_half, prod, 0.0), axis=1, keepdims=True)
        etot_ref[1] = jnp.sum(jnp.where(lo_half, 0.0, prod), axis=1, keepdims=True)
        acc_ref[...] = jnp.zeros_like(acc_ref)
        carry_ref[...] = jnp.zeros_like(carry_ref)
        ecarry_ref[...] = jnp.zeros_like(ecarry_ref)

        def trip(j, _):
            kbs = [qb - ATT_NT * j - u for u in range(ATT_NT)]
            tiles = [_att_tile(kv_ref, kb, qb, cfg.PF) for kb in kbs]
            chains = [(u, b) for u in range(ATT_NT) for b in range(2)]
            for c, (u, b) in enumerate(chains):
                s_ref[c] = _dot_nt(qms[b], tiles[u][0])
                dw_ref[c] = _dot_nt(doms[b], tiles[u][1])
            for c, (u, b) in enumerate(chains):
                for r0 in range(0, t, rs):
                    sl = pl.ds(r0, rs)
                    ls, lk = _att_scores_strip(s_ref, bias_ref, tiles[u][2], c, r0)
                    s_ref[c, sl, :] = ls
                    hi = lk.astype(BF16)
                    hi_ref[c, sl, :] = hi
                    lo_ref[c, sl, :] = (lk - hi.astype(F32)).astype(BF16)
                    rsum_ref[c, sl, :] = jnp.sum(lk, axis=1, keepdims=True)
            for c in range(nch):
                cum_ref[c] = _dot(hi_ref[c], u_ref[...]) + _dot(lo_ref[c], u_ref[...])
            for c, (u, b) in enumerate(chains):
                cb_ref[c] = jnp.broadcast_to(_att_carry(carry_ref, rsum_ref, u, b), (t, t))
            for c, (u, b) in enumerate(chains):
                for r0 in range(0, t, rs):
                    sl = pl.ds(r0, rs)
                    wb = jnp.exp2((s_ref[c, sl, :] + cum_ref[c, sl, :] + cb_ref[c, sl, :]) * LOG2E).astype(BF16)
                    w_ref[c, sl, :] = wb
                    e = wb.astype(F32) * dw_ref[c, sl, :]
                    dw_ref[c, sl, :] = e
                    hi = e.astype(BF16)
                    hi_ref[c, sl, :] = hi
                    lo_ref[c, sl, :] = (e - hi.astype(F32)).astype(BF16)
                    ersum_ref[c, sl, :] = jnp.sum(e, axis=1, keepdims=True)
            for c in range(nch):
                cum_ref[c] = _dot(hi_ref[c], u_ref[...]) + _dot(lo_ref[c], u_ref[...])
            for c, (u, b) in enumerate(chains):
                cb_ref[c] = jnp.broadcast_to(etot_ref[b] - _att_carry(ecarry_ref, ersum_ref, u, b), (t, t))
            for c, (u, b) in enumerate(chains):
                for r0 in range(0, t, rs):
                    sl = pl.ds(r0, rs)
                    e = dw_ref[c, sl, :]
                    e_before = cb_ref[c, sl, :] - (e + cum_ref[c, sl, :])
                    sig = jnp.exp2(s_ref[c, sl, :] * LOG2E)
                    hi_ref[c, sl, :] = (e * (1.0 - sig) - sig * e_before).astype(BF16)
            for u in range(ATT_NT):
                off = pl.multiple_of(jnp.maximum(kbs[u], 0) * t, t)
                c0, c1 = 2 * u, 2 * u + 1
                acc_ref[0] += _dot(hi_ref[c0], tiles[u][0])
                acc_ref[1] += _dot(hi_ref[c1], tiles[u][0])
                dkv_ref[pl.ds(off, t), 0:2 * SB_HEAD_DIM] += _dot_tn(hi_ref[c0], qms[0]) + _dot_tn(hi_ref[c1], qms[1])
                dkv_ref[pl.ds(off, t), 2 * SB_HEAD_DIM:] += _dot_tn(w_ref[c0], doms[0]) + _dot_tn(w_ref[c1], doms[1])
                for b in range(2):
                    carry_ref[b] += rsum_ref[2 * u + b]
                    ecarry_ref[b] += ersum_ref[2 * u + b]

        _att_walk(qb, trip, carry_ref)
        dq_ref[...] = jnp.where(lo_half, acc_ref[0], acc_ref[1]) * scale

    big = lambda dt: pltpu.VMEM((nch, t, t), dt)
    col = lambda n: pltpu.VMEM((n, t, 1), F32)
    return pl.pallas_call(
        body, name=name, grid=(hp, nq),
        in_specs=[pl.BlockSpec((t, 128), lambda h, i: (i, h)), pl.BlockSpec((lp, 256), lambda h, i: (0, h)),
                  pl.BlockSpec((t, 128), lambda h, i: (i, h)), pl.BlockSpec((t, 128), lambda h, i: (i, h))],
        out_specs=[pl.BlockSpec((t, 128), lambda h, i: (i, h)), pl.BlockSpec((lp, 256), lambda h, i: (0, h))],
        out_shape=[jax.ShapeDtypeStruct((lp, d), F32), jax.ShapeDtypeStruct((lp, 2 * d), F32)],
        scratch_shapes=[pltpu.VMEM((t, t), BF16), pltpu.VMEM((5, t, t), F32), big(F32), big(F32), big(BF16), big(BF16),
                        big(F32), big(BF16), big(F32), pltpu.VMEM((2, t, 2 * SB_HEAD_DIM), F32), col(2), col(2), col(2),
                        col(nch), col(nch)],
        compiler_params=_cparams(("arbitrary", "arbitrary")),
    )(q, kv, o, do)


def _loss_head(h, g, target, *, cfg, name):
    lp, d = h.shape
    tm = SSM_CHUNK
    first = (cfg.PF + N_META) // tm
    assert (cfg.PF + N_META) % tm == 0

    def body(h_ref, g_ref, t_ref, loss_ref, dh_ref, dg_ref):
        i = pl.program_id(0)

        @pl.when(i == 0)
        def _():
            loss_ref[...] = jnp.zeros_like(loss_ref)
            dg_ref[...] = jnp.zeros_like(dg_ref)

        x = h_ref[...]
        r = lax.rsqrt(jnp.mean(x * x, axis=-1, keepdims=True) + NORM_EPS)
        xhat = x * r
        live = i >= first
        diff = jnp.where(live, xhat * g_ref[...] - t_ref[...], 0.0)
        loss_ref[...] += 0.5 * jnp.sum(jnp.mean(diff * diff, axis=-1, keepdims=True))
        dy = diff * (1.0 / d)
        dg_ref[...] += jnp.sum(dy * xhat, axis=0, keepdims=True)
        dxh = dy * g_ref[...]
        dh_ref[...] = r * (dxh - xhat * jnp.mean(dxh * xhat, axis=-1, keepdims=True))

    return pl.pallas_call(
        body, name=name, grid=(lp // tm,),
        in_specs=[pl.BlockSpec((tm, d), lambda i: (i, 0)), pl.BlockSpec((1, d), lambda i: (0, 0)),
                  pl.BlockSpec((tm, d), lambda i: (jnp.maximum(i - first, 0), 0))],
        out_specs=[pl.BlockSpec((8, 128), lambda i: (0, 0)), pl.BlockSpec((tm, d), lambda i: (i, 0)),
                   pl.BlockSpec((1, d), lambda i: (0, 0))],
        out_shape=[jax.ShapeDtypeStruct((8, 128), F32), jax.ShapeDtypeStruct((lp, d), F32),
                   jax.ShapeDtypeStruct((1, d), F32)],
        compiler_params=_cparams(("arbitrary",)),
    )(h, g, target)


def _conv_bwd_input(dc, cw, *, width, name):
    lp, cd = dc.shape
    tc = _tile(cd, 512, 128)
    tm = _tile(lp, 768, 8)
    ni = lp // tm

    def body(d_ref, w_ref, o_ref, ext_ref):
        @pl.when(pl.program_id(1) == 0)
        def _():
            ext_ref[tm:, :] = jnp.zeros((HALO, tc), F32)

        ext_ref[0:tm, :] = d_ref[...].astype(F32)
        for r0 in range(0, tm, CONV_STRIP):
            acc = ext_ref[pl.ds(r0 + width - 1, CONV_STRIP), :] * w_ref[0:1, :]
            for k in range(1, width):
                acc = acc + ext_ref[pl.ds(r0 + width - 1 - k, CONV_STRIP), :] * w_ref[k:k + 1, :]
            o_ref[pl.ds(r0, CONV_STRIP), :] = acc.astype(BF16)
        ext_ref[tm:, :] = ext_ref[0:HALO, :]

    return pl.pallas_call(
        body, name=name, grid=(cd // tc, ni),
        in_specs=[pl.BlockSpec((tm, tc), lambda j, i: (ni - 1 - i, j)), pl.BlockSpec((width, tc), lambda j, i: (0, j))],
        out_specs=pl.BlockSpec((tm, tc), lambda j, i: (ni - 1 - i, j)),
        out_shape=jax.ShapeDtypeStruct((lp, cd), BF16),
        scratch_shapes=[pltpu.VMEM((tm + HALO, tc), F32)],
        compiler_params=_cparams(("arbitrary", "arbitrary")),
    )(dc, cw)


MATMUL_WEIGHTS = ("ssd_w_in", "ssd_w_out", "w_kv", "sb_w_q", "sb_w_o", "ffn_w_up", "ffn_w_down")


def _il(a, cfg):
    nb = cfg.DFF // FFN_IL
    lead = a.shape[:-1]
    a = a.reshape(lead + (2, nb, FFN_IL))
    return jnp.swapaxes(a, -3, -2).reshape(lead + (2 * cfg.DFF,))


def _unil(a, cfg):
    nb = cfg.DFF // FFN_IL
    lead = a.shape[:-1]
    a = a.reshape(lead + (nb, 2, FFN_IL))
    return jnp.swapaxes(a, -3, -2).reshape(lead + (2 * cfg.DFF,))


def _pair(a, cfg):
    hp = cfg.D // 128
    lead = a.shape[:-1]
    return jnp.swapaxes(a.reshape(lead + (2, hp, 128)), -3, -2).reshape(lead + (2 * cfg.D,))


def _unpair(a, cfg):
    hp = cfg.D // 128
    lead = a.shape[:-1]
    return jnp.swapaxes(a.reshape(lead + (hp, 2, 128)), -3, -2).reshape(lead + (2 * cfg.D,))


def _prepare(full, cfg):
    d, di, cd, h = cfg.D, cfg.DI, cfg.CONV_DIM, cfg.H
    w_in = full["ssd_w_in"][0].astype(BF16)
    prm = jnp.stack([full["ssd_dt_bias"][0], full["ssd_a_log"][0], full["ssd_d_skip"][0]]).astype(F32)
    prm = jnp.pad(prm.reshape(3, SSM_GROUPS, cfg.HG).transpose(1, 0, 2), ((0, 0), (0, 5), (0, 0)))
    p = dict(
        meta=full["meta_tokens"].astype(F32),
        ssd_norm=full["ssd_norm"].astype(F32).reshape(1, d),
        w_in=jnp.pad(w_in, ((0, 0), (0, DT_PAD - h))),
        conv_w=full["ssd_conv_w"][0].astype(F32), conv_b=full["ssd_conv_b"].astype(F32).reshape(1, cd),
        prm=prm, gate=full["ssd_gate_norm"].astype(F32).reshape(1, di),
        w_out=full["ssd_w_out"][0].astype(BF16),
        kv_norm=full["kv_norm"].astype(F32).reshape(1, d), sb_norm=full["sb_norm"].astype(F32).reshape(1, d),
        final_norm=full["final_norm"].astype(F32).reshape(1, d),
    )
    for i in range(2):
        p[f"ffn_norm{i}"] = full["ffn_norm"][i].astype(F32).reshape(1, d)
        p[f"fcw{i}"] = _il(full["ffn_conv_w"][i].astype(F32), cfg)
        p[f"fcb{i}"] = _il(full["ffn_conv_b"][i].astype(F32), cfg).reshape(1, 2 * cfg.DFF)
    if all(n in full for n in LATE_WEIGHTS):
        p.update(_prepare_late(full, cfg))
    return p


LATE_WEIGHTS = ("w_kv", "sb_w_q", "sb_w_o", "ffn_w_up", "ffn_w_down")


def _prepare_late(full, cfg):
    p = dict(w_kv=_pair(full["w_kv"].astype(BF16), cfg), w_q=full["sb_w_q"][0].astype(BF16),
             w_o=full["sb_w_o"][0].astype(BF16))
    for i in range(2):
        p[f"w_up{i}"] = _il(full["ffn_w_up"][i].astype(BF16), cfg)
        p[f"w_down{i}"] = full["ffn_w_down"][i].astype(BF16)
    return p


def _ffn_fwd(h, p, i, cfg):
    up, u = _norm_mm(h, p[f"ffn_norm{i}"], p[f"w_up{i}"], out_dtype=BF16, name=f"ffn{i}_up")
    act = _ffn_conv_fwd(up, p[f"fcw{i}"], p[f"fcb{i}"], cfg=cfg, name=f"ffn{i}_conv")
    return _mm_res(act, p[f"w_down{i}"], h, pf=cfg.PF, name=f"ffn{i}_down"), (h, u, up, act)


def _ffn_bwd(dh, saved, p, i, cfg, grads):
    h, u, up, act = saved
    dact = _mm_nt(dh, p[f"w_down{i}"], out_dtype=BF16, pf=cfg.PF, name=f"ffn{i}_dact")
    grads[f"w_down{i}"] = _mm_tn(act, dh, name=f"ffn{i}_dwdown")
    dhc, grads[f"fcw{i}"], grads[f"fcb{i}"] = _ffn_conv_bwd_pre(up, dact, p[f"fcw{i}"], p[f"fcb{i}"], cfg=cfg,
                                                                 name=f"ffn{i}_dconv")
    dup = _conv_bwd_input(dhc, p[f"fcw{i}"], width=FFN_CONV, name=f"ffn{i}_dup")
    grads[f"w_up{i}"] = _mm_tn(u, dup, name=f"ffn{i}_dwup")
    dh, grads[f"ffn_norm{i}"] = _mm_nt_normbwd(dup, p[f"w_up{i}"], h, p[f"ffn_norm{i}"], dh, pf=cfg.PF,
                                               name=f"ffn{i}_dh")
    return dh


EARLY_GRADS = ("ssd_w_out", "w_kv", "sb_w_q", "sb_w_o", "ffn_w_up", "ffn_conv_w", "ffn_w_down")


def _local_step(x, target, p, cfg, exchange=None, late_weights=None):
    d, di, cd, h_, lp, pf = cfg.D, cfg.DI, cfg.CONV_DIM, cfg.H, cfg.LP, cfg.PF
    g = {}
    h0 = jnp.concatenate([jnp.zeros((pf, d), F32), p["meta"], x], axis=0)
    zx, u1 = _norm_mm(h0, p["ssd_norm"], p["w_in"], out_dtype=F32, name="ssd_in")
    dtr = zx[:, di + cd:di + cd + h_].reshape(lp, SSM_GROUPS, cfg.HG).transpose(1, 0, 2)
    xbc = _ssd_conv_fwd(zx, p["conv_w"], p["conv_b"], cfg=cfg, name="ssd_conv")
    if late_weights is None:
        y, hn, states = _ssd_fwd(xbc, zx, dtr, p["prm"], p["gate"], cfg=cfg, name="ssd_scan")
    else:
        block, finish = late_weights
        y, hn, states, gathered = _ssd_fwd(xbc, zx, dtr, p["prm"], p["gate"], cfg=cfg, name="ssd_scan", gather=block)
        p = dict(p, **finish(gathered))
    h1 = _mm_res(hn, p["w_out"], h0, pf=pf, name="ssd_out")
    h2, ffn0 = _ffn_fwd(h1, p, 0, cfg)
    kv, ukv = _norm_mm(h2, p["kv_norm"], p["w_kv"], out_dtype=BF16, name="kv_proj")
    q, uq = _norm_mm(h2, p["sb_norm"], p["w_q"], out_dtype=BF16, name="q_proj")
    o = _attn_fwd(q, kv, cfg=cfg, name="attn_fwd")
    h3 = _mm_res(o, p["w_o"], h2, pf=pf, name="attn_out")
    h4, ffn1 = _ffn_fwd(h3, p, 1, cfg)
    loss8, dh, g["final_norm"] = _loss_head(h4, p["final_norm"], target, cfg=cfg, name="loss_head")
    dh = _ffn_bwd(dh, ffn1, p, 1, cfg, g)
    do = _mm_nt(dh, p["w_o"], out_dtype=BF16, pf=pf, name="attn_do")
    g["w_o"] = _mm_tn(o, dh, name="attn_dwo")
    dq, dkv = _attn_bwd(q, kv, o, do, cfg=cfg, name="attn_bwd")
    g["w_q"] = _mm_tn(uq, dq, name="attn_dwq")
    g["w_kv"] = _mm_tn(ukv, dkv, name="attn_dwkv")
    dh, g["sb_norm"] = _mm_nt_normbwd(dq, p["w_q"], h2, p["sb_norm"], dh, pf=pf, name="attn_dhq")
    dh, g["kv_norm"] = _mm_nt_normbwd(dkv, p["w_kv"], h2, p["kv_norm"], dh, pf=pf, name="attn_dhkv")
    dh = _ffn_bwd(dh, ffn0, p, 0, cfg, g)
    dhn = _mm_nt(dh, p["w_out"], out_dtype=BF16, pf=pf, name="ssd_dhn")
    g["w_out"] = _mm_tn(hn, dh, name="ssd_dwout")
    done = dict(
        ssd_w_out=g["w_out"][None], w_kv=_unpair(g["w_kv"], cfg), sb_w_q=g["w_q"][None], sb_w_o=g["w_o"][None],
        ffn_w_up=jnp.stack([_unil(g["w_up0"], cfg), _unil(g["w_up1"], cfg)]),
        ffn_conv_w=jnp.stack([_unil(g["fcw0"], cfg), _unil(g["fcw1"], cfg)]),
        ffn_w_down=jnp.stack([g["w_down0"], g["w_down1"]]),
    )
    res = _ssd_bwd(xbc, zx, dtr, p["prm"], p["gate"], y, states, dhn, cfg=cfg, name="ssd_scan_bwd",
                   send=None if exchange is None else exchange(done))
    dx, db, dc, dz, ddtr, dprm, dgate = res[:7]
    dconv, g["conv_w"], g["conv_b"] = _ssd_conv_bwd_pre(zx, jnp.concatenate([dx, db, dc], axis=1), p["conv_w"],
                                                        p["conv_b"], cfg=cfg, name="ssd_dconv")
    dxpre = _conv_bwd_input(dconv, p["conv_w"], width=SSM_CONV, name="ssd_dxpre")
    ddt = jnp.pad(ddtr.transpose(1, 0, 2).reshape(lp, h_), ((0, 0), (0, DT_PAD - h_))).astype(BF16)
    dzx = jnp.concatenate([dz, dxpre, ddt], axis=1)
    dw_in = _mm_tn(u1, dzx, name="ssd_dwin")
    dh, g["ssd_norm"] = _mm_nt_normbwd(dzx, p["w_in"], h0, p["ssd_norm"], dh, pf=pf, name="ssd_dh")
    heads = lambda r: dprm[:, r, :].reshape(1, h_)
    out = dict(
        done,
        meta_tokens=dh[pf:pf + N_META], ssd_norm=g["ssd_norm"],
        ssd_w_in=dw_in[:, :di + cd + h_][None],
        ssd_conv_w=g["conv_w"][None], ssd_conv_b=g["conv_b"],
        ssd_dt_bias=heads(0), ssd_a_log=heads(1), ssd_d_skip=heads(2),
        ssd_gate_norm=dgate[:, 0, :].reshape(1, di),
        kv_norm=g["kv_norm"].reshape(d), sb_norm=g["sb_norm"],
        ffn_norm=jnp.concatenate([g["ffn_norm0"], g["ffn_norm1"]], axis=0),
        ffn_conv_b=jnp.concatenate([_unil(g["fcb0"], cfg), _unil(g["fcb1"], cfg)], axis=0),
        final_norm=g["final_norm"].reshape(d),
    )
    return loss8[0, 0], dh[pf + N_META:], out, (res[7] if exchange is not None else None)


MESH_AXES = ("x", "y", "c")
HBM_SPEC = pl.BlockSpec(memory_space=pltpu.HBM)


class _Gather:
    def __init__(self, x_ref, out_ref, send_sems, recv_sems, local_sem):
        x, y, c = lax.axis_index("x"), lax.axis_index("y"), lax.axis_index("c")
        me, sibling = (x, y, c), (x, y, 1 - c)
        chips = [(1 - x, y), (x, 1 - y), (1 - x, 1 - y)]

        def slot(px, py, pc):
            return out_ref.at[4 * px + 2 * py + pc]

        def copy(k, block, to, src=None):
            return pltpu.make_async_remote_copy(
                src_ref=slot(*block) if src is None else src, dst_ref=slot(*block),
                send_sem=send_sems.at[k], recv_sem=recv_sems.at[k], device_id=to, device_id_type=pl.DeviceIdType.MESH)

        self.mine = lambda: pltpu.make_async_copy(x_ref, slot(*me), local_sem)
        self.first = lambda: [copy(0, me, sibling, src=x_ref)] + [copy(1 + j, me, (*ch, c), src=x_ref)
                                                                  for j, ch in enumerate(chips)]
        self.passed = lambda: [copy(4 + j, (*ch, c), sibling) for j, ch in enumerate(chips)]
        self.over_ici = lambda: [copy(1 + j, (*ch, c), me) for j, ch in enumerate(chips)]
        self.from_sibling = lambda: [copy(0, sibling, me)] + [copy(4 + j, (*ch, 1 - c), me) for j, ch in enumerate(chips)]

    def start(self):
        self.mine().start()
        for cp in self.first():
            cp.start()

    def forward(self):
        for arrived, onward in zip(self.over_ici(), self.passed()):
            arrived.wait_recv()
            onward.start()

    def finish(self):
        for cp in self.from_sibling():
            cp.wait_recv()
        for cp in self.first() + self.passed():
            cp.wait_send()
        self.mine().wait()


GATHER_SEMS = [pltpu.SemaphoreType.DMA((7,)), pltpu.SemaphoreType.DMA((7,)), pltpu.SemaphoreType.DMA]


def _all_gather(blk, *, name):
    r, w = blk.shape

    def body(x_ref, out_ref, send_sems, recv_sems, local_sem):
        ag = _Gather(x_ref, out_ref, send_sems, recv_sems, local_sem)
        ag.start()
        ag.forward()
        ag.finish()

    return pl.pallas_call(
        body, name=name, out_shape=jax.ShapeDtypeStruct((N_DEV, r, w), blk.dtype),
        in_specs=[HBM_SPEC], out_specs=HBM_SPEC, scratch_shapes=list(GATHER_SEMS),
    )(blk)


def _all_to_all_copies(x_ref, out_ref, send_sems, recv_sems, local_sem):
    x, y, c = lax.axis_index("x"), lax.axis_index("y"), lax.axis_index("c")
    me_id = 4 * x + 2 * y + c
    copies = [pltpu.make_async_copy(x_ref.at[me_id], out_ref.at[me_id], local_sem)]
    for k in range(1, N_DEV):
        fx, fy, fc = (k >> 2) & 1, (k >> 1) & 1, k & 1
        px = 1 - x if fx else x
        py = 1 - y if fy else y
        pc = 1 - c if fc else c
        copies.append(pltpu.make_async_remote_copy(
            src_ref=x_ref.at[4 * px + 2 * py + pc], dst_ref=out_ref.at[me_id],
            send_sem=send_sems.at[k - 1], recv_sem=recv_sems.at[k - 1],
            device_id=(px, py, pc), device_id_type=pl.DeviceIdType.MESH))
    return copies


def _all_to_all(buf, *, name):
    n, r, w = buf.shape

    def body(x_ref, out_ref, send_sems, recv_sems, local_sem):
        copies = _all_to_all_copies(x_ref, out_ref, send_sems, recv_sems, local_sem)
        for cp in copies:
            cp.start()
        for cp in copies:
            cp.wait()

    return pl.pallas_call(
        body, name=name, out_shape=jax.ShapeDtypeStruct((n, r, w), buf.dtype),
        in_specs=[HBM_SPEC], out_specs=HBM_SPEC,
        scratch_shapes=[pltpu.SemaphoreType.DMA((7,)), pltpu.SemaphoreType.DMA((7,)), pltpu.SemaphoreType.DMA],
    )(buf)


def _sum_slots(buf, *, name):
    n, r, w = buf.shape
    tr = _tile(r, 1024, 16)

    def body(x_ref, o_ref):
        acc = x_ref[0].astype(F32)
        for s in range(1, n):
            acc = acc + x_ref[s].astype(F32)
        o_ref[...] = acc

    return pl.pallas_call(
        body, name=name, grid=(r // tr,),
        in_specs=[pl.BlockSpec((n, tr, w), lambda i: (0, i, 0))], out_specs=pl.BlockSpec((tr, w), lambda i: (i, 0)),
        out_shape=jax.ShapeDtypeStruct((r, w), F32), compiler_params=_cparams(("arbitrary",)),
    )(buf)


def _adamw(w, g, m, v, *, name):
    r, lanes = w.shape
    tr = _tile(r, ADAM_ROWS, 8)

    def body(w_ref, g_ref, m_ref, v_ref, d_ref, nm_ref, nv_ref):
        gg = g_ref[...]
        nm = ADAM_B1 * m_ref[...] + (1.0 - ADAM_B1) * gg
        nv = ADAM_B2 * v_ref[...] + (1.0 - ADAM_B2) * (gg * gg)
        m_hat = nm / (1.0 - ADAM_B1 ** ADAM_STEP)
        v_hat = nv / (1.0 - ADAM_B2 ** ADAM_STEP)
        d_ref[...] = -ADAM_LR * (m_hat / (jnp.sqrt(v_hat) + ADAM_EPS) + ADAM_WD * w_ref[...])
        nm_ref[...] = nm
        nv_ref[...] = nv

    spec = pl.BlockSpec((tr, lanes), lambda i: (i, 0))
    shp = jax.ShapeDtypeStruct((r, lanes), F32)
    return pl.pallas_call(body, name=name, grid=(r // tr,), in_specs=[spec] * 4, out_specs=[spec] * 3,
                          out_shape=[shp] * 3, compiler_params=_cparams(("arbitrary",)))(w, g, m, v)


PACK_QUANTUM = 16 * 128
ADAM_ROWS = 2048
INPUT_NAMES = ("x", "meta_tokens", "ssd_norm", "ssd_w_in", "ssd_conv_w", "ssd_conv_b", "ssd_dt_bias", "ssd_a_log",
               "ssd_d_skip", "ssd_gate_norm", "ssd_w_out", "kv_norm", "w_kv", "sb_norm", "sb_w_q", "sb_w_o", "ffn_norm",
               "ffn_w_up", "ffn_conv_w", "ffn_conv_b", "ffn_w_down", "final_norm")
WEIGHT_NAMES = INPUT_NAMES[1:]
SHARD_AXIS = dict(meta_tokens=1, ssd_norm=1, ssd_w_in=2, ssd_conv_w=2, ssd_conv_b=1, ssd_gate_norm=1, ssd_w_out=1, w_kv=1,
                  sb_w_q=1, sb_w_o=1, ffn_w_up=2, ffn_conv_w=2, ffn_w_down=1)
SMALL_SHARDED = ("meta_tokens", "ssd_norm", "ssd_conv_w", "ssd_conv_b", "ssd_gate_norm", "ffn_conv_w")
SHARDED = MATMUL_WEIGHTS + SMALL_SHARDED
REPLICATED = tuple(n for n in WEIGHT_NAMES if n not in SHARD_AXIS)


def _rows(shape):
    n = math.prod(shape)
    return -(-n // PACK_QUANTUM) * (PACK_QUANTUM // 128)


def _pack(arrs, dtype, lead=0):
    parts = []
    for a in arrs:
        ld = a.shape[:lead]
        n = math.prod(a.shape[lead:])
        f = a.reshape(ld + (n,)).astype(dtype)
        pad = _rows(a.shape[lead:]) * 128 - n
        if pad:
            f = jnp.pad(f, [(0, 0)] * lead + [(0, pad)])
        parts.append(f.reshape(ld + (-1, 128)))
    return jnp.concatenate(parts, axis=lead)


def _unpack(flat, shapes, lead=0):
    out, r0 = [], 0
    ld = flat.shape[:lead]
    for shp in shapes:
        rows, n = _rows(shp), math.prod(shp)
        piece = lax.slice_in_dim(flat, r0, r0 + rows, axis=lead).reshape(ld + (rows * 128,))
        out.append(lax.slice_in_dim(piece, 0, n, axis=lead).reshape(ld + tuple(shp)))
        r0 += rows
    return out


def _unshard(stacked, axis):
    a = jnp.moveaxis(stacked, 0, axis)
    shp = a.shape
    return a.reshape(shp[:axis] + (shp[axis] * shp[axis + 1],) + shp[axis + 2:])


def _to_shards(full, axis):
    shp = full.shape
    a = full.reshape(shp[:axis] + (N_DEV, shp[axis] // N_DEV) + shp[axis + 1:])
    return jnp.moveaxis(a, axis, 0)


def kernel(x, meta_tokens, ssd_norm, ssd_w_in, ssd_conv_w, ssd_conv_b, ssd_dt_bias, ssd_a_log, ssd_d_skip, ssd_gate_norm, ssd_w_out, kv_norm, w_kv, sb_norm, sb_w_q, sb_w_o, ffn_norm, ffn_w_up, ffn_conv_w, ffn_conv_b, ffn_w_down, final_norm, loss_target, m_meta_tokens, m_ssd_norm, m_ssd_w_in, m_ssd_conv_w, m_ssd_conv_b, m_ssd_dt_bias, m_ssd_a_log, m_ssd_d_skip, m_ssd_gate_norm, m_ssd_w_out, m_kv_norm, m_w_kv, m_sb_norm, m_sb_w_q, m_sb_w_o, m_ffn_norm, m_ffn_w_up, m_ffn_conv_w, m_ffn_conv_b, m_ffn_w_down, m_final_norm, v_meta_tokens, v_ssd_norm, v_ssd_w_in, v_ssd_conv_w, v_ssd_conv_b, v_ssd_dt_bias, v_ssd_a_log, v_ssd_d_skip, v_ssd_gate_norm, v_ssd_w_out, v_kv_norm, v_w_kv, v_sb_norm, v_sb_w_q, v_sb_w_o, v_ffn_norm, v_ffn_w_up, v_ffn_conv_w, v_ffn_conv_b, v_ffn_w_down, v_final_norm):
    local = dict(zip(WEIGHT_NAMES, (meta_tokens, ssd_norm, ssd_w_in, ssd_conv_w, ssd_conv_b, ssd_dt_bias, ssd_a_log, ssd_d_skip, ssd_gate_norm, ssd_w_out, kv_norm, w_kv, sb_norm, sb_w_q, sb_w_o, ffn_norm, ffn_w_up, ffn_conv_w, ffn_conv_b, ffn_w_down, final_norm)))
    mom = dict(zip(WEIGHT_NAMES, (m_meta_tokens, m_ssd_norm, m_ssd_w_in, m_ssd_conv_w, m_ssd_conv_b, m_ssd_dt_bias, m_ssd_a_log, m_ssd_d_skip, m_ssd_gate_norm, m_ssd_w_out, m_kv_norm, m_w_kv, m_sb_norm, m_sb_w_q, m_sb_w_o, m_ffn_norm, m_ffn_w_up, m_ffn_conv_w, m_ffn_conv_b, m_ffn_w_down, m_final_norm)))
    var = dict(zip(WEIGHT_NAMES, (v_meta_tokens, v_ssd_norm, v_ssd_w_in, v_ssd_conv_w, v_ssd_conv_b, v_ssd_dt_bias, v_ssd_a_log, v_ssd_d_skip, v_ssd_gate_norm, v_ssd_w_out, v_kv_norm, v_w_kv, v_sb_norm, v_sb_w_q, v_sb_w_o, v_ffn_norm, v_ffn_w_up, v_ffn_conv_w, v_ffn_conv_b, v_ffn_w_down, v_final_norm)))
    seq, d = x.shape[1], x.shape[2]
    cfg = _make_cfg(d, seq, ffn_w_down.shape[1] * N_DEV)

    first = tuple(n for n in MATMUL_WEIGHTS if n not in LATE_WEIGHTS)
    big = _all_gather(_pack([local[n] for n in first], BF16), name="gather_weights")
    small = _all_gather(_pack([local[n] for n in SMALL_SHARDED], F32), name="gather_small")

    def unpacked(names, buf):
        return {n: _unshard(stacked, SHARD_AXIS[n])
                for n, stacked in zip(names, _unpack(buf, [local[n].shape for n in names], lead=1))}

    full = {n: local[n] for n in REPLICATED}
    full.update(unpacked(first, big))
    full.update(unpacked(SMALL_SHARDED, small))
    late_weights = (_pack([local[n] for n in LATE_WEIGHTS], BF16),
                    lambda gathered: _prepare_late(unpacked(LATE_WEIGHTS, gathered), cfg))
    full_shape = {n: tuple(s * (N_DEV if a == SHARD_AXIS.get(n) else 1) for a, s in enumerate(local[n].shape))
                  for n in WEIGHT_NAMES}

    late = tuple(n for n in SHARDED if n not in EARLY_GRADS)
    for_peers = lambda gr, names: _pack([_to_shards(gr[n].reshape(full_shape[n]), SHARD_AXIS[n]) for n in names], BF16,
                                        lead=1)
    loss, grad_x, grads, got_early = _local_step(x[0], loss_target[0], _prepare(full, cfg), cfg,
                                                 exchange=lambda done: for_peers(done, EARLY_GRADS),
                                                 late_weights=late_weights)
    g_early = _sum_slots(got_early, name="sum_grads_early")
    g_late = _sum_slots(_all_to_all(for_peers(grads, late), name="scatter_grads"), name="sum_grads")
    rep = _all_gather(_pack([grads[n].reshape(local[n].shape) for n in REPLICATED], F32), name="gather_rep_grads")
    g_rep = _sum_slots(rep, name="sum_rep_grads")
    fill = jnp.zeros(((-(g_early.shape[0] + g_late.shape[0] + g_rep.shape[0])) % ADAM_ROWS, 128), F32)
    g_flat = jnp.concatenate([g_early, g_late, g_rep, fill], axis=0)

    order = EARLY_GRADS + late + REPLICATED
    flat = lambda src: jnp.concatenate([_pack([src[n] for n in order], F32), fill], axis=0)
    delta, new_m, new_v = _adamw(flat(local), g_flat, flat(mom), flat(var), name="adamw")
    shapes = [local[n].shape for n in order]
    pick = lambda buf: dict(zip(order, _unpack(buf, shapes)))
    g_out, d_out, m_out, v_out = pick(g_flat), pick(delta), pick(new_m), pick(new_v)
    loss = lax.psum(loss, MESH_AXES)
    return (loss, grad_x[None], *[g_out[n] for n in WEIGHT_NAMES], *[d_out[n] for n in WEIGHT_NAMES],
            *[m_out[n] for n in WEIGHT_NAMES], *[v_out[n] for n in WEIGHT_NAMES])
```

```python
import collections
import functools
import math

import jax
import jax.numpy as jnp
from jax import lax
from jax.experimental import pallas as pl
from jax.experimental.pallas import tpu as pltpu

F32 = jnp.float32
BF16 = jnp.bfloat16
NORM_EPS = 1e-6
N_META = 16
SSM_HEAD_DIM = 64
SSM_GROUPS = 4
SSM_STATE = 128
SSM_CONV = 4
SSM_CHUNK = 256
SB_HEAD_DIM = 64
FFN_CONV = 3
FFN_IL = 256
DT_PAD = 256
ATT_T = 256
ADAM_LR, ADAM_B1, ADAM_B2, ADAM_EPS, ADAM_WD, ADAM_STEP = 0.001, 0.9, 0.999, 1e-08, 0.01, 10
N_DEV = 8
VMEM_LIMIT = 56 * 1024 * 1024
MM_TILE = 1792

Cfg = collections.namedtuple("Cfg", "D SEQ LP PF DI H HG GW CONV_DIM DFF HS")


def _make_cfg(d_model, seq, d_ff):
    pf = (-N_META) % SSM_CHUNK
    lp = pf + N_META + seq
    assert lp % SSM_CHUNK == 0 and (pf + N_META) == SSM_CHUNK
    di = 2 * d_model
    h = di // SSM_HEAD_DIM
    return Cfg(D=d_model, SEQ=seq, LP=lp, PF=pf, DI=di, H=h, HG=h // SSM_GROUPS, GW=di // SSM_GROUPS,
               CONV_DIM=di + 2 * SSM_GROUPS * SSM_STATE, DFF=d_ff, HS=d_model // SB_HEAD_DIM)


def _tile(n, pref, mult):
    t = (min(pref, n) // mult) * mult
    while t > mult and n % t:
        t -= mult
    assert t >= mult and n % t == 0, (n, pref, mult)
    return t


def _cparams(sem):
    return pltpu.CompilerParams(dimension_semantics=sem, vmem_limit_bytes=VMEM_LIMIT)


def _dot(a, b):
    return jnp.dot(a, b, preferred_element_type=F32)


def _dot_nt(a, b):
    return lax.dot_general(a, b, (((1,), (1,)), ((), ())), preferred_element_type=F32)


def _dot_tn(a, b):
    return lax.dot_general(a, b, (((0,), (0,)), ((), ())), preferred_element_type=F32)


def _split3(v):
    hi = v.astype(BF16)
    r1 = v - hi.astype(F32)
    mid = r1.astype(BF16)
    lo = (r1 - mid.astype(F32)).astype(BF16)
    return hi, mid, lo


def _split2(v):
    hi = v.astype(BF16)
    lo = (v - hi.astype(F32)).astype(BF16)
    return hi, lo


def _dot3(a_f32, b_bf16):
    hi, mid, lo = _split3(a_f32)
    return _dot(hi, b_bf16) + _dot(mid, b_bf16) + _dot(lo, b_bf16)


def _dot3_left(a_bf16, b_f32):
    hi, mid, lo = _split3(b_f32)
    return _dot(a_bf16, hi) + _dot(a_bf16, mid) + _dot(a_bf16, lo)


def _sigmoid(x):
    return 0.5 * jnp.tanh(0.5 * x) + 0.5


def _softplus(x):
    return jnp.maximum(x, 0.0) + jnp.log(1.0 + jnp.exp(-jnp.abs(x)))


def _row_ids(i, tm, shape_cols=1):
    return i * tm + lax.broadcasted_iota(jnp.int32, (tm, shape_cols), 0)


def _norm_mm(h, g, w, *, out_dtype, name, scale=None):
    lp, d = h.shape
    n = w.shape[1]
    tm = _tile(lp, 768, 8)
    tn = _tile(n, MM_TILE, 128)

    def body(h_ref, g_ref, w_ref, y_ref, u_ref):
        @pl.when(pl.program_id(1) == 0)
        def _():
            x = h_ref[...]
            r = lax.rsqrt(jnp.mean(x * x, axis=-1, keepdims=True) + NORM_EPS)
            u_ref[...] = (x * r * g_ref[...]).astype(BF16)
        y = _dot(u_ref[...], w_ref[...])
        if scale is not None:
            y = y * scale
        y_ref[...] = y.astype(out_dtype)

    return pl.pallas_call(
        body, name=name, grid=(lp // tm, n // tn),
        in_specs=[pl.BlockSpec((tm, d), lambda i, j: (i, 0)), pl.BlockSpec((1, d), lambda i, j: (0, 0)),
                  pl.BlockSpec((d, tn), lambda i, j: (0, j))],
        out_specs=[pl.BlockSpec((tm, tn), lambda i, j: (i, j)), pl.BlockSpec((tm, d), lambda i, j: (i, 0))],
        out_shape=[jax.ShapeDtypeStruct((lp, n), out_dtype), jax.ShapeDtypeStruct((lp, d), BF16)],
        compiler_params=_cparams(("arbitrary", "arbitrary")),
    )(h, g, w)


def _mm_res(a, w, h, *, pf, name):
    lp, k = a.shape
    d = w.shape[1]
    tm = _tile(lp, 768, 8)

    def body(a_ref, w_ref, h_ref, o_ref):
        y = _dot(a_ref[...].astype(BF16), w_ref[...])
        rows = _row_ids(pl.program_id(0), tm)
        o_ref[...] = h_ref[...] + jnp.where(rows >= pf, y, 0.0)

    return pl.pallas_call(
        body, name=name, grid=(lp // tm,),
        in_specs=[pl.BlockSpec((tm, k), lambda i: (i, 0)), pl.BlockSpec((k, d), lambda i: (0, 0)),
                  pl.BlockSpec((tm, d), lambda i: (i, 0))],
        out_specs=pl.BlockSpec((tm, d), lambda i: (i, 0)),
        out_shape=jax.ShapeDtypeStruct((lp, d), F32),
        compiler_params=_cparams(("arbitrary",)),
    )(a, w, h)


def _mm_nt(dy, w, *, out_dtype, pf, name):
    lp, n = dy.shape
    k = w.shape[0]
    tm = _tile(lp, 768, 8)
    tk = _tile(k, MM_TILE, 128)

    def body(dy_ref, w_ref, o_ref):
        y = _dot_nt(dy_ref[...].astype(BF16), w_ref[...])
        rows = _row_ids(pl.program_id(0), tm)
        o_ref[...] = jnp.where(rows >= pf, y, 0.0).astype(out_dtype)

    return pl.pallas_call(
        body, name=name, grid=(lp // tm, k // tk),
        in_specs=[pl.BlockSpec((tm, n), lambda i, j: (i, 0)), pl.BlockSpec((tk, n), lambda i, j: (j, 0))],
        out_specs=pl.BlockSpec((tm, tk), lambda i, j: (i, j)),
        out_shape=jax.ShapeDtypeStruct((lp, k), out_dtype),
        compiler_params=_cparams(("arbitrary", "arbitrary")),
    )(dy, w)


def _mm_nt_normbwd(dy, w, h, g, dh_in, *, pf, name, scale=None):
    lp, n = dy.shape
    d = w.shape[0]
    tm = _tile(lp, 768, 8)
    tn = _tile(n, MM_TILE, 128)
    nj = n // tn

    def body(dy_ref, w_ref, h_ref, g_ref, dhin_ref, dh_ref, dg_ref, acc_ref):
        i, j = pl.program_id(0), pl.program_id(1)

        @pl.when(j == 0)
        def _():
            acc_ref[...] = jnp.zeros_like(acc_ref)

        @pl.when((i == 0) & (j == 0))
        def _():
            dg_ref[...] = jnp.zeros_like(dg_ref)

        acc_ref[...] += _dot_nt(dy_ref[...].astype(BF16), w_ref[...])

        @pl.when(j == nj - 1)
        def _():
            du = acc_ref[...]
            if scale is not None:
                du = du * scale
            x = h_ref[...]
            r = lax.rsqrt(jnp.mean(x * x, axis=-1, keepdims=True) + NORM_EPS)
            xhat = x * r
            dg_ref[...] += jnp.sum(du * xhat, axis=0, keepdims=True)
            dxh = du * g_ref[...]
            dx = r * (dxh - xhat * jnp.mean(dxh * xhat, axis=-1, keepdims=True))
            rows = _row_ids(i, tm)
            dh_ref[...] = jnp.where(rows >= pf, dhin_ref[...] + dx, 0.0)

    return pl.pallas_call(
        body, name=name, grid=(lp // tm, nj),
        in_specs=[pl.BlockSpec((tm, tn), lambda i, j: (i, j)), pl.BlockSpec((d, tn), lambda i, j: (0, j)),
                  pl.BlockSpec((tm, d), lambda i, j: (i, 0)), pl.BlockSpec((1, d), lambda i, j: (0, 0)),
                  pl.BlockSpec((tm, d), lambda i, j: (i, 0))],
        out_specs=[pl.BlockSpec((tm, d), lambda i, j: (i, 0)), pl.BlockSpec((1, d), lambda i, j: (0, 0))],
        out_shape=[jax.ShapeDtypeStruct((lp, d), F32), jax.ShapeDtypeStruct((1, d), F32)],
        scratch_shapes=[pltpu.VMEM((tm, d), F32)],
        compiler_params=_cparams(("arbitrary", "arbitrary")),
    )(dy, w, h, g, dh_in)


def _mm_tn(a, b, *, name, scale=None):
    lp, k = a.shape
    n = b.shape[1]
    tm = _tile(lp, 768, 8)
    tk = _tile(k, MM_TILE, 128)
    tn = _tile(n, MM_TILE, 128)
    nm = lp // tm

    def body(a_ref, b_ref, o_ref, acc_ref):
        m = pl.program_id(2)

        @pl.when(m == 0)
        def _():
            acc_ref[...] = jnp.zeros_like(acc_ref)

        acc_ref[...] += _dot_tn(a_ref[...].astype(BF16), b_ref[...].astype(BF16))

        @pl.when(m == nm - 1)
        def _():
            o_ref[...] = acc_ref[...] if scale is None else acc_ref[...] * scale

    return pl.pallas_call(
        body, name=name, grid=(k // tk, n // tn, nm),
        in_specs=[pl.BlockSpec((tm, tk), lambda i, j, m: (m, i)), pl.BlockSpec((tm, tn), lambda i, j, m: (m, j))],
        out_specs=pl.BlockSpec((tk, tn), lambda i, j, m: (i, j)),
        out_shape=jax.ShapeDtypeStruct((k, n), F32),
        scratch_shapes=[pltpu.VMEM((tk, tn), F32)],
        compiler_params=_cparams(("arbitrary", "arbitrary", "arbitrary")),
    )(a, b)


HALO = 8


CONV_STRIP = 32


def _conv_load(ext_ref, x_ref):
    @pl.when(pl.program_id(1) == 0)
    def _():
        ext_ref[0:HALO, :] = jnp.zeros((HALO, ext_ref.shape[1]), F32)

    ext_ref[HALO:, :] = x_ref[...].astype(F32)


def _conv_strip(ext_ref, w_ref, b_ref, r0, width):
    taps = [ext_ref[pl.ds(r0 + HALO - (width - 1 - k), CONV_STRIP), :] for k in range(width)]
    acc = b_ref[...] + taps[0] * w_ref[0:1, :]
    for k in range(1, width):
        acc = acc + taps[k] * w_ref[k:k + 1, :]
    return acc, taps


def _drop_pad_rows(v, i, tm, r0, pf):
    if r0 >= pf:
        return v
    return jnp.where(i * tm + r0 + lax.broadcasted_iota(jnp.int32, (CONV_STRIP, 1), 0) >= pf, v, 0.0)


def _fold8(x):
    out = x[0:8]
    for r in range(8, x.shape[0], 8):
        out = out + x[r:r + 8]
    return out


def _conv_grad_flush(acc_ref, dw_ref, db_ref, width, last):
    @pl.when(last)
    def _():
        for k in range(width):
            dw_ref[k:k + 1, :] = jnp.sum(acc_ref[8 * k:8 * k + 8, :], axis=0, keepdims=True)
        db_ref[...] = jnp.sum(acc_ref[8 * width:8 * width + 8, :], axis=0, keepdims=True)


def _ssd_conv_fwd(zx, cw, cb, *, cfg, name):
    lp, cd, di = cfg.LP, cfg.CONV_DIM, cfg.DI
    tc = _tile(math.gcd(di, cd), 512, 128)
    tm = _tile(lp, 768, 8)
    off = di // tc

    def body(x_ref, w_ref, b_ref, o_ref, ext_ref):
        i = pl.program_id(1)
        _conv_load(ext_ref, x_ref)
        for r0 in range(0, tm, CONV_STRIP):
            acc, _ = _conv_strip(ext_ref, w_ref, b_ref, r0, SSM_CONV)
            o_ref[pl.ds(r0, CONV_STRIP), :] = _drop_pad_rows(acc * _sigmoid(acc), i, tm, r0, cfg.PF)
        ext_ref[0:HALO, :] = ext_ref[tm:tm + HALO, :]

    return pl.pallas_call(
        body, name=name, grid=(cd // tc, lp // tm),
        in_specs=[pl.BlockSpec((tm, tc), lambda j, i: (i, j + off)), pl.BlockSpec((SSM_CONV, tc), lambda j, i: (0, j)),
                  pl.BlockSpec((1, tc), lambda j, i: (0, j))],
        out_specs=pl.BlockSpec((tm, tc), lambda j, i: (i, j)),
        out_shape=jax.ShapeDtypeStruct((lp, cd), F32),
        scratch_shapes=[pltpu.VMEM((tm + HALO, tc), F32)],
        compiler_params=_cparams(("arbitrary", "arbitrary")),
    )(zx, cw, cb)


def _ssd_conv_bwd_pre(zx, dxbc, cw, cb, *, cfg, name):
    lp, cd, di = cfg.LP, cfg.CONV_DIM, cfg.DI
    tc = _tile(math.gcd(di, cd), 512, 128)
    tm = _tile(lp, 768, 8)
    off = di // tc

    def body(x_ref, d_ref, w_ref, b_ref, dc_ref, dw_ref, db_ref, ext_ref, acc_ref):
        i = pl.program_id(1)

        @pl.when(i == 0)
        def _():
            acc_ref[...] = jnp.zeros_like(acc_ref)

        _conv_load(ext_ref, x_ref)
        for r0 in range(0, tm, CONV_STRIP):
            sl = pl.ds(r0, CONV_STRIP)
            c, taps = _conv_strip(ext_ref, w_ref, b_ref, r0, SSM_CONV)
            sg = _sigmoid(c)
            dc = _drop_pad_rows(d_ref[sl, :].astype(F32) * sg * (1.0 + c * (1.0 - sg)), i, tm, r0, cfg.PF)
            dc_ref[sl, :] = dc.astype(BF16)
            for k in range(SSM_CONV):
                acc_ref[8 * k:8 * k + 8, :] += _fold8(dc * taps[k])
            acc_ref[8 * SSM_CONV:, :] += _fold8(dc)
        ext_ref[0:HALO, :] = ext_ref[tm:tm + HALO, :]
        _conv_grad_flush(acc_ref, dw_ref, db_ref, SSM_CONV, i == lp // tm - 1)

    return pl.pallas_call(
        body, name=name, grid=(cd // tc, lp // tm),
        in_specs=[pl.BlockSpec((tm, tc), lambda j, i: (i, j + off)), pl.BlockSpec((tm, tc), lambda j, i: (i, j)),
                  pl.BlockSpec((SSM_CONV, tc), lambda j, i: (0, j)), pl.BlockSpec((1, tc), lambda j, i: (0, j))],
        out_specs=[pl.BlockSpec((tm, tc), lambda j, i: (i, j)), pl.BlockSpec((SSM_CONV, tc), lambda j, i: (0, j)),
                   pl.BlockSpec((1, tc), lambda j, i: (0, j))],
        out_shape=[jax.ShapeDtypeStruct((lp, cd), BF16), jax.ShapeDtypeStruct((SSM_CONV, cd), F32),
                   jax.ShapeDtypeStruct((1, cd), F32)],
        scratch_shapes=[pltpu.VMEM((tm + HALO, tc), F32), pltpu.VMEM((8 * (SSM_CONV + 1), tc), F32)],
        compiler_params=_cparams(("arbitrary", "arbitrary")),
    )(zx, dxbc, cw, cb)


def _ffn_conv_fwd(up, cw, cb, *, cfg, name):
    lp, dff = cfg.LP, cfg.DFF
    tc = 2 * FFN_IL
    tm = _tile(lp, 768, 8)

    def body(x_ref, w_ref, b_ref, o_ref, ext_ref):
        i = pl.program_id(1)
        _conv_load(ext_ref, x_ref)
        for r0 in range(0, tm, CONV_STRIP):
            hc, _ = _conv_strip(ext_ref, w_ref, b_ref, r0, FFN_CONV)
            gc, vc = hc[:, :FFN_IL], hc[:, FFN_IL:]
            act = _drop_pad_rows(gc * _sigmoid(gc) * vc, i, tm, r0, cfg.PF)
            o_ref[pl.ds(r0, CONV_STRIP), :] = act.astype(BF16)
        ext_ref[0:HALO, :] = ext_ref[tm:tm + HALO, :]

    return pl.pallas_call(
        body, name=name, grid=(dff // FFN_IL, lp // tm),
        in_specs=[pl.BlockSpec((tm, tc), lambda j, i: (i, j)), pl.BlockSpec((FFN_CONV, tc), lambda j, i: (0, j)),
                  pl.BlockSpec((1, tc), lambda j, i: (0, j))],
        out_specs=pl.BlockSpec((tm, FFN_IL), lambda j, i: (i, j)),
        out_shape=jax.ShapeDtypeStruct((lp, dff), BF16),
        scratch_shapes=[pltpu.VMEM((tm + HALO, tc), F32)],
        compiler_params=_cparams(("arbitrary", "arbitrary")),
    )(up, cw, cb)


def _ffn_conv_bwd_pre(up, dact, cw, cb, *, cfg, name):
    lp, dff = cfg.LP, cfg.DFF
    tc = 2 * FFN_IL
    tm = _tile(lp, 768, 8)

    def body(x_ref, d_ref, w_ref, b_ref, dc_ref, dw_ref, db_ref, ext_ref, acc_ref):
        i = pl.program_id(1)

        @pl.when(i == 0)
        def _():
            acc_ref[...] = jnp.zeros_like(acc_ref)

        _conv_load(ext_ref, x_ref)
        for r0 in range(0, tm, CONV_STRIP):
            sl = pl.ds(r0, CONV_STRIP)
            hc, taps = _conv_strip(ext_ref, w_ref, b_ref, r0, FFN_CONV)
            gc, vc = hc[:, :FFN_IL], hc[:, FFN_IL:]
            sg = _sigmoid(gc)
            da = _drop_pad_rows(d_ref[sl, :].astype(F32), i, tm, r0, cfg.PF)
            dc = jnp.concatenate([da * vc * sg * (1.0 + gc * (1.0 - sg)), da * gc * sg], axis=1)
            dc_ref[sl, :] = dc.astype(BF16)
            for k in range(FFN_CONV):
                acc_ref[8 * k:8 * k + 8, :] += _fold8(dc * taps[k])
            acc_ref[8 * FFN_CONV:, :] += _fold8(dc)
        ext_ref[0:HALO, :] = ext_ref[tm:tm + HALO, :]
        _conv_grad_flush(acc_ref, dw_ref, db_ref, FFN_CONV, i == lp // tm - 1)

    return pl.pallas_call(
        body, name=name, grid=(dff // FFN_IL, lp // tm),
        in_specs=[pl.BlockSpec((tm, tc), lambda j, i: (i, j)), pl.BlockSpec((tm, FFN_IL), lambda j, i: (i, j)),
                  pl.BlockSpec((FFN_CONV, tc), lambda j, i: (0, j)), pl.BlockSpec((1, tc), lambda j, i: (0, j))],
        out_specs=[pl.BlockSpec((tm, tc), lambda j, i: (i, j)), pl.BlockSpec((FFN_CONV, tc), lambda j, i: (0, j)),
                   pl.BlockSpec((1, tc), lambda j, i: (0, j))],
        out_shape=[jax.ShapeDtypeStruct((lp, 2 * dff), BF16), jax.ShapeDtypeStruct((FFN_CONV, 2 * dff), F32),
                   jax.ShapeDtypeStruct((1, 2 * dff), F32)],
        scratch_shapes=[pltpu.VMEM((tm + HALO, tc), F32), pltpu.VMEM((8 * (FFN_CONV + 1), tc), F32)],
        compiler_params=_cparams(("arbitrary", "arbitrary")),
    )(up, dact, cw, cb)


def _head_select(lane, head):
    return ((lane >= head * SSM_HEAD_DIM) & (lane < (head + 1) * SSM_HEAD_DIM)).astype(BF16)


def _ssd_terms(x, bm, cm, dtr, bias, alog, dsk, valid, cfg):
    q, hg, gw, p = SSM_CHUNK, cfg.HG, cfg.GW, SSM_HEAD_DIM
    t = {}
    e_mat = _head_select(lax.broadcasted_iota(jnp.int32, (hg, gw), 1), lax.broadcasted_iota(jnp.int32, (hg, gw), 0))
    li = lax.broadcasted_iota(jnp.int32, (q, q), 0)
    si = lax.broadcasted_iota(jnp.int32, (q, q), 1)
    tri = li >= si
    tril = tri.astype(BF16)
    triu = (li <= si).astype(BF16)
    pre = dtr + bias
    dtv = jnp.where(valid, _softplus(pre), 0.0)
    a_head = -jnp.exp(alog)
    a = dtv * a_head
    cs = _dot3_left(tril, a)
    hi, mid, lo = _split3(a)
    cst = _dot_tn(hi, triu) + _dot_tn(mid, triu) + _dot_tn(lo, triu)
    cs_last = jnp.sum(a, axis=0, keepdims=True)
    dte = jnp.exp(jnp.minimum(cs_last - cs, 0.0))
    ecs = jnp.exp(cs)
    t.update(e_mat=e_mat, tri=tri, tril=tril, triu=triu, pre=pre, dtv=dtv, a_head=a_head, cs=cs, cst=cst,
             dec=jnp.exp(cs_last), dte=dte, ecs=ecs)
    t["dtv_x"] = _dot3(dtv, e_mat)
    t["ecs_x"] = _dot3(ecs, e_mat)
    t["dte_x"] = _dot3(dte, e_mat)
    t["dec_x"] = _dot3(t["dec"], e_mat)
    t["d_x"] = _dot3(dsk, e_mat)
    t["xdt"] = x * t["dtv_x"]
    t["gm"] = _dot_nt(cm.astype(BF16), bm.astype(BF16))
    return t


def _ssd_decay(t, e):
    diff = t["cs"][:, e:e + 1] - t["cst"][e:e + 1, :]
    return jnp.where(t["tri"], jnp.exp(jnp.minimum(diff, 0.0)), 0.0)


def _head_mask(b):
    lane = lax.broadcasted_iota(jnp.int32, (1, 2 * SSM_HEAD_DIM), 1)
    return (lane >= SSM_HEAD_DIM * b) & (lane < SSM_HEAD_DIM * (b + 1))


def _ssd_specs(cfg):
    q, g, gw, ns, hg, di = SSM_CHUNK, SSM_GROUPS, cfg.GW, SSM_STATE, cfg.HG, cfg.DI
    return dict(
        x=pl.BlockSpec((q, gw), lambda j, c: (c, j)),
        b=pl.BlockSpec((q, ns), lambda j, c: (c, di // ns + j)),
        c=pl.BlockSpec((q, ns), lambda j, c: (c, di // ns + g + j)),
        z=pl.BlockSpec((q, gw), lambda j, c: (c, j)),
        dtr=pl.BlockSpec((None, q, hg), lambda j, c: (j, c, 0)),
        prm=pl.BlockSpec((None, 8, hg), lambda j, c: (j, 0, 0)),
        gate=pl.BlockSpec((1, gw), lambda j, c: (0, j)),
        st=pl.BlockSpec((None, None, ns, gw), lambda j, c: (c, j, 0, 0)),
    )


GATHER_FORWARD_STEPS = 16


def _ssd_fwd(xbc, zx, dtr, prm, gate_g, *, cfg, name, gather=None):
    q, g, gw, ns, hg, p = SSM_CHUNK, SSM_GROUPS, cfg.GW, SSM_STATE, cfg.HG, SSM_HEAD_DIM
    nc = cfg.LP // q
    sp = _ssd_specs(cfg)

    def body(x_ref, b_ref, c_ref, z_ref, dtr_ref, prm_ref, gg_ref, y_ref, hn_ref, st_ref, s_ref):
        c = pl.program_id(1)

        @pl.when(c == 0)
        def _():
            s_ref[...] = jnp.zeros((ns, gw), F32)

        s_prev = s_ref[...]
        st_ref[...] = s_prev
        x, bm, cm = x_ref[...], b_ref[...], c_ref[...]
        valid = _row_ids(c, q) >= cfg.PF
        t = _ssd_terms(x, bm, cm, dtr_ref[...], prm_ref[0:1, :], prm_ref[1:2, :], prm_ref[2:3, :], valid, cfg)
        parts = []
        for pr in range(hg // 2):
            xp = t["xdt"][:, 2 * p * pr:2 * p * (pr + 1)]
            acc = None
            for b in range(2):
                m = (t["gm"] * _ssd_decay(t, 2 * pr + b)).astype(BF16)
                d = _dot(m, jnp.where(_head_mask(b), xp, 0.0).astype(BF16))
                acc = d if acc is None else acc + d
            parts.append(acc)
        y_diag = jnp.concatenate(parts, axis=1)
        y_off = _dot(cm.astype(BF16), s_prev.astype(BF16)) * t["ecs_x"]
        y = y_diag + y_off + x * t["d_x"]
        y_ref[...] = y
        s_ref[...] = s_prev * t["dec_x"] + _dot_tn(bm.astype(BF16), (t["xdt"] * t["dte_x"]).astype(BF16))
        z = z_ref[...]
        hgate = y * z * _sigmoid(z)
        r = lax.rsqrt(jnp.mean(hgate * hgate, axis=-1, keepdims=True) + NORM_EPS)
        hn_ref[...] = (hgate * r * gg_ref[...]).astype(BF16)

    in_specs = [sp["x"], sp["b"], sp["c"], sp["z"], sp["dtr"], sp["prm"], sp["gate"]]
    out_specs = [sp["x"], sp["x"], sp["st"]]
    out_shape = [jax.ShapeDtypeStruct((cfg.LP, cfg.DI), F32), jax.ShapeDtypeStruct((cfg.LP, cfg.DI), BF16),
                 jax.ShapeDtypeStruct((nc, g, ns, gw), F32)]
    scratch = [pltpu.VMEM((ns, gw), F32)]
    args = (xbc, xbc, xbc, zx, dtr, prm, gate_g)
    if gather is None:
        kernel_body = body
    else:
        n_in, n_out = len(in_specs), len(out_specs)
        fwd_c = max(nc - GATHER_FORWARD_STEPS, 0)

        def kernel_body(*refs):
            ins, blk_ref = refs[:n_in], refs[n_in]
            outs, all_ref = refs[n_in + 1:n_in + 1 + n_out], refs[n_in + 1 + n_out]
            s_ref = refs[n_in + n_out + 2]
            sems = refs[n_in + n_out + 3:]
            j, c = pl.program_id(0), pl.program_id(1)

            @pl.when((j == 0) & (c == 0))
            def _():
                _Gather(blk_ref, all_ref, *sems).start()

            body(*ins, *outs, s_ref)

            @pl.when((j == g - 1) & (c == fwd_c))
            def _():
                _Gather(blk_ref, all_ref, *sems).forward()

            @pl.when((j == g - 1) & (c == nc - 1))
            def _():
                _Gather(blk_ref, all_ref, *sems).finish()

        in_specs, out_specs = in_specs + [HBM_SPEC], out_specs + [HBM_SPEC]
        out_shape = out_shape + [jax.ShapeDtypeStruct((N_DEV,) + gather.shape, gather.dtype)]
        scratch = scratch + list(GATHER_SEMS)
        args = args + (gather,)
    return pl.pallas_call(
        kernel_body, name=name, grid=(g, nc), in_specs=in_specs, out_specs=out_specs, out_shape=out_shape,
        scratch_shapes=scratch, compiler_params=_cparams(("arbitrary", "arbitrary")),
    )(*args)


def _ssd_bwd(xbc, zx, dtr, prm, gate_g, y, states, dhn, *, cfg, name, send=None):
    q, g, gw, ns, hg, p = SSM_CHUNK, SSM_GROUPS, cfg.GW, SSM_STATE, cfg.HG, SSM_HEAD_DIM
    nc = cfg.LP // q
    sp = _ssd_specs(cfg)
    rev = lambda spec: pl.BlockSpec(spec.block_shape, (lambda f: (lambda j, c: f(j, nc - 1 - c)))(spec.index_map))
    bc_spec = pl.BlockSpec((q, ns), lambda j, c: (nc - 1 - c, j))

    def body(x_ref, b_ref, c_ref, z_ref, dtr_ref, prm_ref, gg_ref, y_ref, st_ref, dhn_ref,
             dx_ref, db_ref, dc_ref, dz_ref, ddtr_ref, dprm_ref, dgg_ref, ds_ref):
        ci = pl.program_id(1)
        c = nc - 1 - ci

        @pl.when(ci == 0)
        def _():
            ds_ref[...] = jnp.zeros((ns, gw), F32)
            dprm_ref[...] = jnp.zeros((8, hg), F32)
            dgg_ref[...] = jnp.zeros((8, gw), F32)

        ds_next = ds_ref[...]
        s_prev = st_ref[...]
        x, bm, cm = x_ref[...], b_ref[...], c_ref[...]
        valid = _row_ids(c, q) >= cfg.PF
        t = _ssd_terms(x, bm, cm, dtr_ref[...], prm_ref[0:1, :], prm_ref[1:2, :], prm_ref[2:3, :], valid, cfg)
        et_mat = _head_select(lax.broadcasted_iota(jnp.int32, (gw, hg), 0), lax.broadcasted_iota(jnp.int32, (gw, hg), 1))
        heads = lambda v: _dot3(v, et_mat)
        yv, z = y_ref[...], z_ref[...]
        sz = _sigmoid(z)
        silu = z * sz
        hgate = yv * silu
        r = lax.rsqrt(jnp.mean(hgate * hgate, axis=-1, keepdims=True) + NORM_EPS)
        hhat = hgate * r
        dhn = dhn_ref[...].astype(F32)
        dgg = jnp.sum(dhn * hhat, axis=0, keepdims=True)
        dhh = dhn * gg_ref[...]
        dhgate = r * (dhh - hhat * jnp.mean(dhh * hhat, axis=-1, keepdims=True))
        dy = dhgate * silu
        dz_ref[...] = (dhgate * yv * sz * (1.0 + z * (1.0 - sz))).astype(BF16)
        bmb, cmb = bm.astype(BF16), cm.astype(BF16)
        xdt = t["xdt"]
        dye = dy * t["ecs_x"]
        dxdt_state = _dot(bmb, ds_next.astype(BF16)) * t["dte_x"]
        dg_acc = None
        dparts = []
        li = lax.broadcasted_iota(jnp.int32, (q, q), 0)
        si = lax.broadcasted_iota(jnp.int32, (q, q), 1)
        head_id = lax.broadcasted_iota(jnp.int32, (1, hg), 1)
        da = None
        for pr in range(hg // 2):
            sl = slice(2 * p * pr, 2 * p * (pr + 1))
            xp, dyp = xdt[:, sl], dy[:, sl]
            acc = None
            for b in range(2):
                lm = _ssd_decay(t, 2 * pr + b)
                m = (t["gm"] * lm).astype(BF16)
                dym = jnp.where(_head_mask(b), dyp, 0.0).astype(BF16)
                d = _dot_tn(m, dym)
                acc = d if acc is None else acc + d
                dm = _dot_nt(dym, xp.astype(BF16)) * lm
                dg_acc = dm if dg_acc is None else dg_acc + dm
                corner = _dot(t["triu"], (dm * t["gm"]).astype(BF16))
                da_e = jnp.sum(jnp.where(si < li, corner, 0.0), axis=1, keepdims=True)
                da_e = da_e * (head_id == 2 * pr + b).astype(F32)
                da = da_e if da is None else da + da_e
            dparts.append(acc)
        dxdt = jnp.concatenate(dparts, axis=1) + dxdt_state
        dgb = dg_acc.astype(BF16)
        dc_ref[...] = (_dot(dgb, bmb) + _dot_nt(dye.astype(BF16), s_prev.astype(BF16))).astype(BF16)
        xde = (xdt * t["dte_x"]).astype(BF16)
        db_ref[...] = (_dot_tn(dgb, cmb) + _dot_nt(xde, ds_next.astype(BF16))).astype(BF16)
        ds_ref[...] = ds_next * t["dec_x"] + _dot_tn(cmb, dye.astype(BF16))
        y_off = _dot(cmb, s_prev.astype(BF16)) * t["ecs_x"]
        tril_strict = (li > si).astype(BF16)
        da = da + _dot3_left(t["triu"], heads(dy * y_off)) + _dot3_left(tril_strict, heads(xdt * dxdt_state)) \
            + t["dec"] * heads(jnp.sum(ds_next * s_prev, axis=0, keepdims=True))
        ddtv = da * t["a_head"] + heads(dxdt * x)
        ddtr = jnp.where(valid, ddtv * _sigmoid(t["pre"]), 0.0)
        ddtr_ref[...] = ddtr
        dx_ref[...] = (dxdt * t["dtv_x"] + dy * t["d_x"]).astype(BF16)
        dalog = jnp.sum(da * t["dtv"], axis=0, keepdims=True) * t["a_head"]
        dprm_ref[0:1, :] += jnp.sum(ddtr, axis=0, keepdims=True)
        dprm_ref[1:2, :] += dalog
        dprm_ref[2:3, :] += heads(jnp.sum(dy * x, axis=0, keepdims=True))
        dgg_ref[0:1, :] += dgg

    in_specs = [rev(sp["x"]), rev(sp["b"]), rev(sp["c"]), rev(sp["z"]), rev(sp["dtr"]), sp["prm"], sp["gate"],
                rev(sp["x"]), rev(sp["st"]), rev(sp["x"])]
    out_specs = [rev(sp["x"]), bc_spec, bc_spec, rev(sp["x"]), rev(sp["dtr"]),
                 pl.BlockSpec((None, 8, hg), lambda j, c: (j, 0, 0)), pl.BlockSpec((None, 8, gw), lambda j, c: (j, 0, 0))]
    out_shape = [jax.ShapeDtypeStruct((cfg.LP, cfg.DI), BF16), jax.ShapeDtypeStruct((cfg.LP, g * ns), BF16),
                 jax.ShapeDtypeStruct((cfg.LP, g * ns), BF16), jax.ShapeDtypeStruct((cfg.LP, cfg.DI), BF16),
                 jax.ShapeDtypeStruct((g, cfg.LP, hg), F32), jax.ShapeDtypeStruct((g, 8, hg), F32),
                 jax.ShapeDtypeStruct((g, 8, gw), F32)]
    scratch = [pltpu.VMEM((ns, gw), F32)]
    args = (xbc, xbc, xbc, zx, dtr, prm, gate_g, y, states, dhn)
    if send is None:
        kernel_body = body
    else:
        n_in, n_out = len(in_specs), len(out_specs)

        def kernel_body(*refs):
            ins, send_ref = refs[:n_in], refs[n_in]
            outs, recv_ref = refs[n_in + 1:n_in + 1 + n_out], refs[n_in + 1 + n_out]
            ds_ref, send_sems, recv_sems, local_sem = refs[n_in + n_out + 2:]
            copies = lambda: _all_to_all_copies(send_ref, recv_ref, send_sems, recv_sems, local_sem)

            @pl.when((pl.program_id(0) == 0) & (pl.program_id(1) == 0))
            def _():
                for cp in copies():
                    cp.start()

            body(*ins, *outs, ds_ref)

            @pl.when((pl.program_id(0) == g - 1) & (pl.program_id(1) == nc - 1))
            def _():
                for cp in copies():
                    cp.wait()

        in_specs, out_specs = in_specs + [HBM_SPEC], out_specs + [HBM_SPEC]
        out_shape = out_shape + [jax.ShapeDtypeStruct(send.shape, send.dtype)]
        scratch = scratch + [pltpu.SemaphoreType.DMA((N_DEV - 1,)), pltpu.SemaphoreType.DMA((N_DEV - 1,)),
                             pltpu.SemaphoreType.DMA]
        args = args + (send,)
    return pl.pallas_call(
        kernel_body, name=name, grid=(g, nc), in_specs=in_specs, out_specs=out_specs, out_shape=out_shape,
        scratch_shapes=scratch, compiler_params=_cparams(("arbitrary", "arbitrary")),
    )(*args)


ATT_STRIP = 32
ATT_NT = 2
ATT_DEAD = -120.0
ATT_NEG = -1e9
LOG2E = 1.4426950408889634
LN2 = 0.6931471805599453


def _att_init(u_ref, bias_ref, qb, cfg):
    @pl.when(qb == 0)
    def _():
        t = ATT_T
        rows = lax.broadcasted_iota(jnp.int32, (t, t), 0)
        cols = lax.broadcasted_iota(jnp.int32, (t, t), 1)
        u_ref[...] = (rows > cols).astype(BF16)
        pad = cols < cfg.PF % t
        bias_ref[0] = jnp.zeros((t, t), F32)
        bias_ref[1] = jnp.where(cols >= rows, ATT_NEG, 0.0)
        bias_ref[2] = jnp.where(pad, ATT_NEG, 0.0)
        bias_ref[3] = jnp.full((t, t), ATT_NEG, F32)
        bias_ref[4] = jnp.where((cols >= rows) | pad, ATT_NEG, 0.0)


def _att_tile(kv_ref, kb, qb, pf):
    t = ATT_T
    edge = pf // t
    off = pl.multiple_of(jnp.maximum(kb, 0) * t, t)
    k = kv_ref[pl.ds(off, t), 0:2 * SB_HEAD_DIM]
    v = kv_ref[pl.ds(off, t), 2 * SB_HEAD_DIM:]
    kind = jnp.where(kb < edge, 3, jnp.where(kb == qb, jnp.where(qb == edge, 4, 1), jnp.where(kb == edge, 2, 0)))
    return k, v, kind


def _att_trips(qb):
    n = qb + ATT_NT
    return lax.shift_right_logical(n, 1) if ATT_NT == 2 else lax.div(n, ATT_NT)


def _att_walk(qb, trip, carry_ref):
    def cond(st):
        j, alive = st
        return (j < _att_trips(qb)) & alive

    def step(st):
        j, _ = st
        trip(j, None)
        return j + 1, jnp.max(carry_ref[...]) > ATT_DEAD

    lax.while_loop(cond, step, (0, True))


def _att_carry(carry_ref, rsum_ref, u, b):
    carry = carry_ref[b]
    for up in range(u):
        carry = carry + rsum_ref[2 * up + b]
    return carry


def _att_scores_strip(s_ref, bias_ref, kind, b, r0):
    sl = pl.ds(r0, ATT_STRIP)
    s = s_ref[b, sl, :] + bias_ref[kind, sl, :]
    l1p = jnp.log2(1.0 + jnp.exp2(jnp.abs(s) * (-LOG2E))) * LN2
    ls = jnp.minimum(s, 0.0) - l1p
    return ls, ls - s


def _attn_fwd(q, kv, *, cfg, name):
    t, lp, d, rs, nch = ATT_T, cfg.LP, cfg.D, ATT_STRIP, 2 * ATT_NT
    hp, nq = d // (2 * SB_HEAD_DIM), lp // ATT_T
    scale = SB_HEAD_DIM ** -0.5

    def body(q_ref, kv_ref, o_ref, u_ref, bias_ref, s_ref, hi_ref, lo_ref, cum_ref, w_ref, cb_ref, acc_ref, carry_ref,
             rsum_ref):
        qb = pl.program_id(1)
        _att_init(u_ref, bias_ref, qb, cfg)
        lo_half = lax.broadcasted_iota(jnp.int32, (1, 2 * SB_HEAD_DIM), 1) < SB_HEAD_DIM
        q2 = q_ref[...] * scale
        qms = [jnp.where(lo_half, q2, 0).astype(BF16), jnp.where(lo_half, 0, q2).astype(BF16)]
        acc_ref[...] = jnp.zeros_like(acc_ref)
        carry_ref[...] = jnp.zeros_like(carry_ref)

        def trip(j, _):
            tiles = [_att_tile(kv_ref, qb - ATT_NT * j - u, qb, cfg.PF) for u in range(ATT_NT)]
            chains = [(u, b) for u in range(ATT_NT) for b in range(2)]
            for c, (u, b) in enumerate(chains):
                s_ref[c] = _dot_nt(qms[b], tiles[u][0])
            for c, (u, b) in enumerate(chains):
                for r0 in range(0, t, rs):
                    sl = pl.ds(r0, rs)
                    ls, lk = _att_scores_strip(s_ref, bias_ref, tiles[u][2], c, r0)
                    s_ref[c, sl, :] = ls
                    hi = lk.astype(BF16)
                    hi_ref[c, sl, :] = hi
                    lo_ref[c, sl, :] = (lk - hi.astype(F32)).astype(BF16)
                    rsum_ref[c, sl, :] = jnp.sum(lk, axis=1, keepdims=True)
            for c in range(len(chains)):
                cum_ref[c] = _dot(hi_ref[c], u_ref[...]) + _dot(lo_ref[c], u_ref[...])
            for c, (u, b) in enumerate(chains):
                cb_ref[c] = jnp.broadcast_to(_att_carry(carry_ref, rsum_ref, u, b), (t, t))
            for c, (u, b) in enumerate(chains):
                for r0 in range(0, t, rs):
                    sl = pl.ds(r0, rs)
                    x = s_ref[c, sl, :] + cum_ref[c, sl, :] + cb_ref[c, sl, :]
                    w_ref[c, sl, :] = jnp.exp2(x * LOG2E).astype(BF16)
            for c, (u, b) in enumerate(chains):
                acc_ref[b] += _dot(w_ref[c], tiles[u][1])
                carry_ref[b] += rsum_ref[c]

        _att_walk(qb, trip, carry_ref)
        o_ref[...] = jnp.where(lo_half, acc_ref[0], acc_ref[1])

    return pl.pallas_call(
        body, name=name, grid=(hp, nq),
        in_specs=[pl.BlockSpec((t, 128), lambda h, i: (i, h)), pl.BlockSpec((lp, 256), lambda h, i: (0, h))],
        out_specs=pl.BlockSpec((t, 128), lambda h, i: (i, h)),
        out_shape=jax.ShapeDtypeStruct((lp, d), F32),
        scratch_shapes=[pltpu.VMEM((t, t), BF16), pltpu.VMEM((5, t, t), F32), pltpu.VMEM((nch, t, t), F32),
                        pltpu.VMEM((nch, t, t), BF16), pltpu.VMEM((nch, t, t), BF16), pltpu.VMEM((nch, t, t), F32),
                        pltpu.VMEM((nch, t, t), BF16), pltpu.VMEM((nch, t, t), F32),
                        pltpu.VMEM((2, t, 2 * SB_HEAD_DIM), F32), pltpu.VMEM((2, t, 1), F32), pltpu.VMEM((nch, t, 1), F32)],
        compiler_params=_cparams(("arbitrary", "arbitrary")),
    )(q, kv)


def _attn_bwd(q, kv, o, do, *, cfg, name):
    t, lp, d, rs, nch = ATT_T, cfg.LP, cfg.D, ATT_STRIP, 2 * ATT_NT
    hp, nq = d // (2 * SB_HEAD_DIM), lp // ATT_T
    scale = SB_HEAD_DIM ** -0.5

    def body(q_ref, kv_ref, o_ref, do_ref, dq_ref, dkv_ref, u_ref, bias_ref, s_ref, dw_ref, hi_ref, lo_ref, cum_ref,
             w_ref, cb_ref, acc_ref, carry_ref, ecarry_ref, etot_ref, rsum_ref, ersum_ref):
        qb = pl.program_id(1)

        @pl.when(qb == 0)
        def _():
            dkv_ref[...] = jnp.zeros_like(dkv_ref)

        _att_init(u_ref, bias_ref, qb, cfg)
        lo_half = lax.broadcasted_iota(jnp.int32, (1, 2 * SB_HEAD_DIM), 1) < SB_HEAD_DIM
        q2 = q_ref[...] * scale
        do2 = do_ref[...]
        qms = [jnp.where(lo_half, q2, 0).astype(BF16), jnp.where(lo_half, 0, q2).astype(BF16)]
        doms = [jnp.where(lo_half, do2, 0).astype(BF16), jnp.where(lo_half, 0, do2).astype(BF16)]
        prod = do2.astype(F32) * o_ref[...]
        etot_ref[0] = jnp.sum(jnp.where(lo_half, prod, 0.0), axis=1, keepdims=True)
        etot_ref[1] = jnp.sum(jnp.where(lo_half, 0.0, prod), axis=1, keepdims=True)
        acc_ref[...] = jnp.zeros_like(acc_ref)
        carry_ref[...] = jnp.zeros_like(carry_ref)
        ecarry_ref[...] = jnp.zeros_like(ecarry_ref)

        def trip(j, _):
            kbs = [qb - ATT_NT * j - u for u in range(ATT_NT)]
            tiles = [_att_tile(kv_ref, kb, qb, cfg.PF) for kb in kbs]
            chains = [(u, b) for u in range(ATT_NT) for b in range(2)]
            for c, (u, b) in enumerate(chains):
                s_ref[c] = _dot_nt(qms[b], tiles[u][0])
                dw_ref[c] = _dot_nt(doms[b], tiles[u][1])
            for c, (u, b) in enumerate(chains):
                for r0 in range(0, t, rs):
                    sl = pl.ds(r0, rs)
                    ls, lk = _att_scores_strip(s_ref, bias_ref, tiles[u][2], c, r0)
                    s_ref[c, sl, :] = ls
                    hi = lk.astype(BF16)
                    hi_ref[c, sl, :] = hi
                    lo_ref[c, sl, :] = (lk - hi.astype(F32)).astype(BF16)
                    rsum_ref[c, sl, :] = jnp.sum(lk, axis=1, keepdims=True)
            for c in range(nch):
                cum_ref[c] = _dot(hi_ref[c], u_ref[...]) + _dot(lo_ref[c], u_ref[...])
            for c, (u, b) in enumerate(chains):
                cb_ref[c] = jnp.broadcast_to(_att_carry(carry_ref, rsum_ref, u, b), (t, t))
            for c, (u, b) in enumerate(chains):
                for r0 in range(0, t, rs):
                    sl = pl.ds(r0, rs)
                    wb = jnp.exp2((s_ref[c, sl, :] + cum_ref[c, sl, :] + cb_ref[c, sl, :]) * LOG2E).astype(BF16)
                    w_ref[c, sl, :] = wb
                    e = wb.astype(F32) * dw_ref[c, sl, :]
                    dw_ref[c, sl, :] = e
                    hi = e.astype(BF16)
                    hi_ref[c, sl, :] = hi
                    lo_ref[c, sl, :] = (e - hi.astype(F32)).astype(BF16)
                    ersum_ref[c, sl, :] = jnp.sum(e, axis=1, keepdims=True)
            for c in range(nch):
                cum_ref[c] = _dot(hi_ref[c], u_ref[...]) + _dot(lo_ref[c], u_ref[...])
            for c, (u, b) in enumerate(chains):
                cb_ref[c] = jnp.broadcast_to(etot_ref[b] - _att_carry(ecarry_ref, ersum_ref, u, b), (t, t))
            for c, (u, b) in enumerate(chains):
                for r0 in range(0, t, rs):
                    sl = pl.ds(r0, rs)
                    e = dw_ref[c, sl, :]
                    e_before = cb_ref[c, sl, :] - (e + cum_ref[c, sl, :])
                    sig = jnp.exp2(s_ref[c, sl, :] * LOG2E)
                    hi_ref[c, sl, :] = (e * (1.0 - sig) - sig * e_before).astype(BF16)
            for u in range(ATT_NT):
                off = pl.multiple_of(jnp.maximum(kbs[u], 0) * t, t)
                c0, c1 = 2 * u, 2 * u + 1
                acc_ref[0] += _dot(hi_ref[c0], tiles[u][0])
                acc_ref[1] += _dot(hi_ref[c1], tiles[u][0])
                dkv_ref[pl.ds(off, t), 0:2 * SB_HEAD_DIM] += _dot_tn(hi_ref[c0], qms[0]) + _dot_tn(hi_ref[c1], qms[1])
                dkv_ref[pl.ds(off, t), 2 * SB_HEAD_DIM:] += _dot_tn(w_ref[c0], doms[0]) + _dot_tn(w_ref[c1], doms[1])
                for b in range(2):
                    carry_ref[b] += rsum_ref[2 * u + b]
                    ecarry_ref[b] += ersum_ref[2 * u + b]

        _att_walk(qb, trip, carry_ref)
        dq_ref[...] = jnp.where(lo_half, acc_ref[0], acc_ref[1]) * scale

    big = lambda dt: pltpu.VMEM((nch, t, t), dt)
    col = lambda n: pltpu.VMEM((n, t, 1), F32)
    return pl.pallas_call(
        body, name=name, grid=(hp, nq),
        in_specs=[pl.BlockSpec((t, 128), lambda h, i: (i, h)), pl.BlockSpec((lp, 256), lambda h, i: (0, h)),
                  pl.BlockSpec((t, 128), lambda h, i: (i, h)), pl.BlockSpec((t, 128), lambda h, i: (i, h))],
        out_specs=[pl.BlockSpec((t, 128), lambda h, i: (i, h)), pl.BlockSpec((lp, 256), lambda h, i: (0, h))],
        out_shape=[jax.ShapeDtypeStruct((lp, d), F32), jax.ShapeDtypeStruct((lp, 2 * d), F32)],
        scratch_shapes=[pltpu.VMEM((t, t), BF16), pltpu.VMEM((5, t, t), F32), big(F32), big(F32), big(BF16), big(BF16),
                        big(F32), big(BF16), big(F32), pltpu.VMEM((2, t, 2 * SB_HEAD_DIM), F32), col(2), col(2), col(2),
                        col(nch), col(nch)],
        compiler_params=_cparams(("arbitrary", "arbitrary")),
    )(q, kv, o, do)


def _loss_head(h, g, target, *, cfg, name):
    lp, d = h.shape
    tm = SSM_CHUNK
    first = (cfg.PF + N_META) // tm
    assert (cfg.PF + N_META) % tm == 0

    def body(h_ref, g_ref, t_ref, loss_ref, dh_ref, dg_ref):
        i = pl.program_id(0)

        @pl.when(i == 0)
        def _():
            loss_ref[...] = jnp.zeros_like(loss_ref)
            dg_ref[...] = jnp.zeros_like(dg_ref)

        x = h_ref[...]
        r = lax.rsqrt(jnp.mean(x * x, axis=-1, keepdims=True) + NORM_EPS)
        xhat = x * r
        live = i >= first
        diff = jnp.where(live, xhat * g_ref[...] - t_ref[...], 0.0)
        loss_ref[...] += 0.5 * jnp.sum(jnp.mean(diff * diff, axis=-1, keepdims=True))
        dy = diff * (1.0 / d)
        dg_ref[...] += jnp.sum(dy * xhat, axis=0, keepdims=True)
        dxh = dy * g_ref[...]
        dh_ref[...] = r * (dxh - xhat * jnp.mean(dxh * xhat, axis=-1, keepdims=True))

    return pl.pallas_call(
        body, name=name, grid=(lp // tm,),
        in_specs=[pl.BlockSpec((tm, d), lambda i: (i, 0)), pl.BlockSpec((1, d), lambda i: (0, 0)),
                  pl.BlockSpec((tm, d), lambda i: (jnp.maximum(i - first, 0), 0))],
        out_specs=[pl.BlockSpec((8, 128), lambda i: (0, 0)), pl.BlockSpec((tm, d), lambda i: (i, 0)),
                   pl.BlockSpec((1, d), lambda i: (0, 0))],
        out_shape=[jax.ShapeDtypeStruct((8, 128), F32), jax.ShapeDtypeStruct((lp, d), F32),
                   jax.ShapeDtypeStruct((1, d), F32)],
        compiler_params=_cparams(("arbitrary",)),
    )(h, g, target)


def _conv_bwd_input(dc, cw, *, width, name):
    lp, cd = dc.shape
    tc = _tile(cd, 512, 128)
    tm = _tile(lp, 768, 8)
    ni = lp // tm

    def body(d_ref, w_ref, o_ref, ext_ref):
        @pl.when(pl.program_id(1) == 0)
        def _():
            ext_ref[tm:, :] = jnp.zeros((HALO, tc), F32)

        ext_ref[0:tm, :] = d_ref[...].astype(F32)
        for r0 in range(0, tm, CONV_STRIP):
            acc = ext_ref[pl.ds(r0 + width - 1, CONV_STRIP), :] * w_ref[0:1, :]
            for k in range(1, width):
                acc = acc + ext_ref[pl.ds(r0 + width - 1 - k, CONV_STRIP), :] * w_ref[k:k + 1, :]
            o_ref[pl.ds(r0, CONV_STRIP), :] = acc.astype(BF16)
        ext_ref[tm:, :] = ext_ref[0:HALO, :]

    return pl.pallas_call(
        body, name=name, grid=(cd // tc, ni),
        in_specs=[pl.BlockSpec((tm, tc), lambda j, i: (ni - 1 - i, j)), pl.BlockSpec((width, tc), lambda j, i: (0, j))],
        out_specs=pl.BlockSpec((tm, tc), lambda j, i: (ni - 1 - i, j)),
        out_shape=jax.ShapeDtypeStruct((lp, cd), BF16),
        scratch_shapes=[pltpu.VMEM((tm + HALO, tc), F32)],
        compiler_params=_cparams(("arbitrary", "arbitrary")),
    )(dc, cw)


MATMUL_WEIGHTS = ("ssd_w_in", "ssd_w_out", "w_kv", "sb_w_q", "sb_w_o", "ffn_w_up", "ffn_w_down")


def _il(a, cfg):
    nb = cfg.DFF // FFN_IL
    lead = a.shape[:-1]
    a = a.reshape(lead + (2, nb, FFN_IL))
    return jnp.swapaxes(a, -3, -2).reshape(lead + (2 * cfg.DFF,))


def _unil(a, cfg):
    nb = cfg.DFF // FFN_IL
    lead = a.shape[:-1]
    a = a.reshape(lead + (nb, 2, FFN_IL))
    return jnp.swapaxes(a, -3, -2).reshape(lead + (2 * cfg.DFF,))


def _pair(a, cfg):
    hp = cfg.D // 128
    lead = a.shape[:-1]
    return jnp.swapaxes(a.reshape(lead + (2, hp, 128)), -3, -2).reshape(lead + (2 * cfg.D,))


def _unpair(a, cfg):
    hp = cfg.D // 128
    lead = a.shape[:-1]
    return jnp.swapaxes(a.reshape(lead + (hp, 2, 128)), -3, -2).reshape(lead + (2 * cfg.D,))


def _prepare(full, cfg):
    d, di, cd, h = cfg.D, cfg.DI, cfg.CONV_DIM, cfg.H
    w_in = full["ssd_w_in"][0].astype(BF16)
    prm = jnp.stack([full["ssd_dt_bias"][0], full["ssd_a_log"][0], full["ssd_d_skip"][0]]).astype(F32)
    prm = jnp.pad(prm.reshape(3, SSM_GROUPS, cfg.HG).transpose(1, 0, 2), ((0, 0), (0, 5), (0, 0)))
    p = dict(
        meta=full["meta_tokens"].astype(F32),
        ssd_norm=full["ssd_norm"].astype(F32).reshape(1, d),
        w_in=jnp.pad(w_in, ((0, 0), (0, DT_PAD - h))),
        conv_w=full["ssd_conv_w"][0].astype(F32), conv_b=full["ssd_conv_b"].astype(F32).reshape(1, cd),
        prm=prm, gate=full["ssd_gate_norm"].astype(F32).reshape(1, di),
        w_out=full["ssd_w_out"][0].astype(BF16),
        kv_norm=full["kv_norm"].astype(F32).reshape(1, d), sb_norm=full["sb_norm"].astype(F32).reshape(1, d),
        final_norm=full["final_norm"].astype(F32).reshape(1, d),
    )
    for i in range(2):
        p[f"ffn_norm{i}"] = full["ffn_norm"][i].astype(F32).reshape(1, d)
        p[f"fcw{i}"] = _il(full["ffn_conv_w"][i].astype(F32), cfg)
        p[f"fcb{i}"] = _il(full["ffn_conv_b"][i].astype(F32), cfg).reshape(1, 2 * cfg.DFF)
    if all(n in full for n in LATE_WEIGHTS):
        p.update(_prepare_late(full, cfg))
    return p


LATE_WEIGHTS = ("w_kv", "sb_w_q", "sb_w_o", "ffn_w_up", "ffn_w_down")


def _prepare_late(full, cfg):
    p = dict(w_kv=_pair(full["w_kv"].astype(BF16), cfg), w_q=full["sb_w_q"][0].astype(BF16),
             w_o=full["sb_w_o"][0].astype(BF16))
    for i in range(2):
        p[f"w_up{i}"] = _il(full["ffn_w_up"][i].astype(BF16), cfg)
        p[f"w_down{i}"] = full["ffn_w_down"][i].astype(BF16)
    return p


def _ffn_fwd(h, p, i, cfg):
    up, u = _norm_mm(h, p[f"ffn_norm{i}"], p[f"w_up{i}"], out_dtype=BF16, name=f"ffn{i}_up")
    act = _ffn_conv_fwd(up, p[f"fcw{i}"], p[f"fcb{i}"], cfg=cfg, name=f"ffn{i}_conv")
    return _mm_res(act, p[f"w_down{i}"], h, pf=cfg.PF, name=f"ffn{i}_down"), (h, u, up, act)


def _ffn_bwd(dh, saved, p, i, cfg, grads):
    h, u, up, act = saved
    dact = _mm_nt(dh, p[f"w_down{i}"], out_dtype=BF16, pf=cfg.PF, name=f"ffn{i}_dact")
    grads[f"w_down{i}"] = _mm_tn(act, dh, name=f"ffn{i}_dwdown")
    dhc, grads[f"fcw{i}"], grads[f"fcb{i}"] = _ffn_conv_bwd_pre(up, dact, p[f"fcw{i}"], p[f"fcb{i}"], cfg=cfg,
                                                                 name=f"ffn{i}_dconv")
    dup = _conv_bwd_input(dhc, p[f"fcw{i}"], width=FFN_CONV, name=f"ffn{i}_dup")
    grads[f"w_up{i}"] = _mm_tn(u, dup, name=f"ffn{i}_dwup")
    dh, grads[f"ffn_norm{i}"] = _mm_nt_normbwd(dup, p[f"w_up{i}"], h, p[f"ffn_norm{i}"], dh, pf=cfg.PF,
                                               name=f"ffn{i}_dh")
    return dh


EARLY_GRADS = ("ssd_w_out", "w_kv", "sb_w_q", "sb_w_o", "ffn_w_up", "ffn_conv_w", "ffn_w_down")


def _local_step(x, target, p, cfg, exchange=None, late_weights=None):
    d, di, cd, h_, lp, pf = cfg.D, cfg.DI, cfg.CONV_DIM, cfg.H, cfg.LP, cfg.PF
    g = {}
    h0 = jnp.concatenate([jnp.zeros((pf, d), F32), p["meta"], x], axis=0)
    zx, u1 = _norm_mm(h0, p["ssd_norm"], p["w_in"], out_dtype=F32, name="ssd_in")
    dtr = zx[:, di + cd:di + cd + h_].reshape(lp, SSM_GROUPS, cfg.HG).transpose(1, 0, 2)
    xbc = _ssd_conv_fwd(zx, p["conv_w"], p["conv_b"], cfg=cfg, name="ssd_conv")
    if late_weights is None:
        y, hn, states = _ssd_fwd(xbc, zx, dtr, p["prm"], p["gate"], cfg=cfg, name="ssd_scan")
    else:
        block, finish = late_weights
        y, hn, states, gathered = _ssd_fwd(xbc, zx, dtr, p["prm"], p["gate"], cfg=cfg, name="ssd_scan", gather=block)
        p = dict(p, **finish(gathered))
    h1 = _mm_res(hn, p["w_out"], h0, pf=pf, name="ssd_out")
    h2, ffn0 = _ffn_fwd(h1, p, 0, cfg)
    kv, ukv = _norm_mm(h2, p["kv_norm"], p["w_kv"], out_dtype=BF16, name="kv_proj")
    q, uq = _norm_mm(h2, p["sb_norm"], p["w_q"], out_dtype=BF16, name="q_proj")
    o = _attn_fwd(q, kv, cfg=cfg, name="attn_fwd")
    h3 = _mm_res(o, p["w_o"], h2, pf=pf, name="attn_out")
    h4, ffn1 = _ffn_fwd(h3, p, 1, cfg)
    loss8, dh, g["final_norm"] = _loss_head(h4, p["final_norm"], target, cfg=cfg, name="loss_head")
    dh = _ffn_bwd(dh, ffn1, p, 1, cfg, g)
    do = _mm_nt(dh, p["w_o"], out_dtype=BF16, pf=pf, name="attn_do")
    g["w_o"] = _mm_tn(o, dh, name="attn_dwo")
    dq, dkv = _attn_bwd(q, kv, o, do, cfg=cfg, name="attn_bwd")
    g["w_q"] = _mm_tn(uq, dq, name="attn_dwq")
    g["w_kv"] = _mm_tn(ukv, dkv, name="attn_dwkv")
    dh, g["sb_norm"] = _mm_nt_normbwd(dq, p["w_q"], h2, p["sb_norm"], dh, pf=pf, name="attn_dhq")
    dh, g["kv_norm"] = _mm_nt_normbwd(dkv, p["w_kv"], h2, p["kv_norm"], dh, pf=pf, name="attn_dhkv")
    dh = _ffn_bwd(dh, ffn0, p, 0, cfg, g)
    dhn = _mm_nt(dh, p["w_out"], out_dtype=BF16, pf=pf, name="ssd_dhn")
    g["w_out"] = _mm_tn(hn, dh, name="ssd_dwout")
    done = dict(
        ssd_w_out=g["w_out"][None], w_kv=_unpair(g["w_kv"], cfg), sb_w_q=g["w_q"][None], sb_w_o=g["w_o"][None],
        ffn_w_up=jnp.stack([_unil(g["w_up0"], cfg), _unil(g["w_up1"], cfg)]),
        ffn_conv_w=jnp.stack([_unil(g["fcw0"], cfg), _unil(g["fcw1"], cfg)]),
        ffn_w_down=jnp.stack([g["w_down0"], g["w_down1"]]),
    )
    res = _ssd_bwd(xbc, zx, dtr, p["prm"], p["gate"], y, states, dhn, cfg=cfg, name="ssd_scan_bwd",
                   send=None if exchange is None else exchange(done))
    dx, db, dc, dz, ddtr, dprm, dgate = res[:7]
    dconv, g["conv_w"], g["conv_b"] = _ssd_conv_bwd_pre(zx, jnp.concatenate([dx, db, dc], axis=1), p["conv_w"],
                                                        p["conv_b"], cfg=cfg, name="ssd_dconv")
    dxpre = _conv_bwd_input(dconv, p["conv_w"], width=SSM_CONV, name="ssd_dxpre")
    ddt = jnp.pad(ddtr.transpose(1, 0, 2).reshape(lp, h_), ((0, 0), (0, DT_PAD - h_))).astype(BF16)
    dzx = jnp.concatenate([dz, dxpre, ddt], axis=1)
    dw_in = _mm_tn(u1, dzx, name="ssd_dwin")
    dh, g["ssd_norm"] = _mm_nt_normbwd(dzx, p["w_in"], h0, p["ssd_norm"], dh, pf=pf, name="ssd_dh")
    heads = lambda r: dprm[:, r, :].reshape(1, h_)
    out = dict(
        done,
        meta_tokens=dh[pf:pf + N_META], ssd_norm=g["ssd_norm"],
        ssd_w_in=dw_in[:, :di + cd + h_][None],
        ssd_conv_w=g["conv_w"][None], ssd_conv_b=g["conv_b"],
        ssd_dt_bias=heads(0), ssd_a_log=heads(1), ssd_d_skip=heads(2),
        ssd_gate_norm=dgate[:, 0, :].reshape(1, di),
        kv_norm=g["kv_norm"].reshape(d), sb_norm=g["sb_norm"],
        ffn_norm=jnp.concatenate([g["ffn_norm0"], g["ffn_norm1"]], axis=0),
        ffn_conv_b=jnp.concatenate([_unil(g["fcb0"], cfg), _unil(g["fcb1"], cfg)], axis=0),
        final_norm=g["final_norm"].reshape(d),
    )
    return loss8[0, 0], dh[pf + N_META:], out, (res[7] if exchange is not None else None)


MESH_AXES = ("x", "y", "c")
HBM_SPEC = pl.BlockSpec(memory_space=pltpu.HBM)


class _Gather:
    def __init__(self, x_ref, out_ref, send_sems, recv_sems, local_sem):
        x, y, c = lax.axis_index("x"), lax.axis_index("y"), lax.axis_index("c")
        me, sibling = (x, y, c), (x, y, 1 - c)
        chips = [(1 - x, y), (x, 1 - y), (1 - x, 1 - y)]

        def slot(px, py, pc):
            return out_ref.at[4 * px + 2 * py + pc]

        def copy(k, block, to, src=None):
            return pltpu.make_async_remote_copy(
                src_ref=slot(*block) if src is None else src, dst_ref=slot(*block),
                send_sem=send_sems.at[k], recv_sem=recv_sems.at[k], device_id=to, device_id_type=pl.DeviceIdType.MESH)

        self.mine = lambda: pltpu.make_async_copy(x_ref, slot(*me), local_sem)
        self.first = lambda: [copy(0, me, sibling, src=x_ref)] + [copy(1 + j, me, (*ch, c), src=x_ref)
                                                                  for j, ch in enumerate(chips)]
        self.passed = lambda: [copy(4 + j, (*ch, c), sibling) for j, ch in enumerate(chips)]
        self.over_ici = lambda: [copy(1 + j, (*ch, c), me) for j, ch in enumerate(chips)]
        self.from_sibling = lambda: [copy(0, sibling, me)] + [copy(4 + j, (*ch, 1 - c), me) for j, ch in enumerate(chips)]

    def start(self):
        self.mine().start()
        for cp in self.first():
            cp.start()

    def forward(self):
        for arrived, onward in zip(self.over_ici(), self.passed()):
            arrived.wait_recv()
            onward.start()

    def finish(self):
        for cp in self.from_sibling():
            cp.wait_recv()
        for cp in self.first() + self.passed():
            cp.wait_send()
        self.mine().wait()


GATHER_SEMS = [pltpu.SemaphoreType.DMA((7,)), pltpu.SemaphoreType.DMA((7,)), pltpu.SemaphoreType.DMA]


def _all_gather(blk, *, name):
    r, w = blk.shape

    def body(x_ref, out_ref, send_sems, recv_sems, local_sem):
        ag = _Gather(x_ref, out_ref, send_sems, recv_sems, local_sem)
        ag.start()
        ag.forward()
        ag.finish()

    return pl.pallas_call(
        body, name=name, out_shape=jax.ShapeDtypeStruct((N_DEV, r, w), blk.dtype),
        in_specs=[HBM_SPEC], out_specs=HBM_SPEC, scratch_shapes=list(GATHER_SEMS),
    )(blk)


def _all_to_all_copies(x_ref, out_ref, send_sems, recv_sems, local_sem):
    x, y, c = lax.axis_index("x"), lax.axis_index("y"), lax.axis_index("c")
    me_id = 4 * x + 2 * y + c
    copies = [pltpu.make_async_copy(x_ref.at[me_id], out_ref.at[me_id], local_sem)]
    for k in range(1, N_DEV):
        fx, fy, fc = (k >> 2) & 1, (k >> 1) & 1, k & 1
        px = 1 - x if fx else x
        py = 1 - y if fy else y
        pc = 1 - c if fc else c
        copies.append(pltpu.make_async_remote_copy(
            src_ref=x_ref.at[4 * px + 2 * py + pc], dst_ref=out_ref.at[me_id],
            send_sem=send_sems.at[k - 1], recv_sem=recv_sems.at[k - 1],
            device_id=(px, py, pc), device_id_type=pl.DeviceIdType.MESH))
    return copies


def _all_to_all(buf, *, name):
    n, r, w = buf.shape

    def body(x_ref, out_ref, send_sems, recv_sems, local_sem):
        copies = _all_to_all_copies(x_ref, out_ref, send_sems, recv_sems, local_sem)
        for cp in copies:
            cp.start()
        for cp in copies:
            cp.wait()

    return pl.pallas_call(
        body, name=name, out_shape=jax.ShapeDtypeStruct((n, r, w), buf.dtype),
        in_specs=[HBM_SPEC], out_specs=HBM_SPEC,
        scratch_shapes=[pltpu.SemaphoreType.DMA((7,)), pltpu.SemaphoreType.DMA((7,)), pltpu.SemaphoreType.DMA],
    )(buf)


def _sum_slots(buf, *, name):
    n, r, w = buf.shape
    tr = _tile(r, 1024, 16)

    def body(x_ref, o_ref):
        acc = x_ref[0].astype(F32)
        for s in range(1, n):
            acc = acc + x_ref[s].astype(F32)
        o_ref[...] = acc

    return pl.pallas_call(
        body, name=name, grid=(r // tr,),
        in_specs=[pl.BlockSpec((n, tr, w), lambda i: (0, i, 0))], out_specs=pl.BlockSpec((tr, w), lambda i: (i, 0)),
        out_shape=jax.ShapeDtypeStruct((r, w), F32), compiler_params=_cparams(("arbitrary",)),
    )(buf)


def _adamw(w, g, m, v, *, name):
    r, lanes = w.shape
    tr = _tile(r, ADAM_ROWS, 8)

    def body(w_ref, g_ref, m_ref, v_ref, d_ref, nm_ref, nv_ref):
        gg = g_ref[...]
        nm = ADAM_B1 * m_ref[...] + (1.0 - ADAM_B1) * gg
        nv = ADAM_B2 * v_ref[...] + (1.0 - ADAM_B2) * (gg * gg)
        m_hat = nm / (1.0 - ADAM_B1 ** ADAM_STEP)
        v_hat = nv / (1.0 - ADAM_B2 ** ADAM_STEP)
        d_ref[...] = -ADAM_LR * (m_hat / (jnp.sqrt(v_hat) + ADAM_EPS) + ADAM_WD * w_ref[...])
        nm_ref[...] = nm
        nv_ref[...] = nv

    spec = pl.BlockSpec((tr, lanes), lambda i: (i, 0))
    shp = jax.ShapeDtypeStruct((r, lanes), F32)
    return pl.pallas_call(body, name=name, grid=(r // tr,), in_specs=[spec] * 4, out_specs=[spec] * 3,
                          out_shape=[shp] * 3, compiler_params=_cparams(("arbitrary",)))(w, g, m, v)


PACK_QUANTUM = 16 * 128
ADAM_ROWS = 2048
INPUT_NAMES = ("x", "meta_tokens", "ssd_norm", "ssd_w_in", "ssd_conv_w", "ssd_conv_b", "ssd_dt_bias", "ssd_a_log",
               "ssd_d_skip", "ssd_gate_norm", "ssd_w_out", "kv_norm", "w_kv", "sb_norm", "sb_w_q", "sb_w_o", "ffn_norm",
               "ffn_w_up", "ffn_conv_w", "ffn_conv_b", "ffn_w_down", "final_norm")
WEIGHT_NAMES = INPUT_NAMES[1:]
SHARD_AXIS = dict(meta_tokens=1, ssd_norm=1, ssd_w_in=2, ssd_conv_w=2, ssd_conv_b=1, ssd_gate_norm=1, ssd_w_out=1, w_kv=1,
                  sb_w_q=1, sb_w_o=1, ffn_w_up=2, ffn_conv_w=2, ffn_w_down=1)
SMALL_SHARDED = ("meta_tokens", "ssd_norm", "ssd_conv_w", "ssd_conv_b", "ssd_gate_norm", "ffn_conv_w")
SHARDED = MATMUL_WEIGHTS + SMALL_SHARDED
REPLICATED = tuple(n for n in WEIGHT_NAMES if n not in SHARD_AXIS)


def _rows(shape):
    n = math.prod(shape)
    return -(-n // PACK_QUANTUM) * (PACK_QUANTUM // 128)


def _pack(arrs, dtype, lead=0):
    parts = []
    for a in arrs:
        ld = a.shape[:lead]
        n = math.prod(a.shape[lead:])
        f = a.reshape(ld + (n,)).astype(dtype)
        pad = _rows(a.shape[lead:]) * 128 - n
        if pad:
            f = jnp.pad(f, [(0, 0)] * lead + [(0, pad)])
        parts.append(f.reshape(ld + (-1, 128)))
    return jnp.concatenate(parts, axis=lead)


def _unpack(flat, shapes, lead=0):
    out, r0 = [], 0
    ld = flat.shape[:lead]
    for shp in shapes:
        rows, n = _rows(shp), math.prod(shp)
        piece = lax.slice_in_dim(flat, r0, r0 + rows, axis=lead).reshape(ld + (rows * 128,))
        out.append(lax.slice_in_dim(piece, 0, n, axis=lead).reshape(ld + tuple(shp)))
        r0 += rows
    return out


def _unshard(stacked, axis):
    a = jnp.moveaxis(stacked, 0, axis)
    shp = a.shape
    return a.reshape(shp[:axis] + (shp[axis] * shp[axis + 1],) + shp[axis + 2:])


def _to_shards(full, axis):
    shp = full.shape
    a = full.reshape(shp[:axis] + (N_DEV, shp[axis] // N_DEV) + shp[axis + 1:])
    return jnp.moveaxis(a, axis, 0)


def kernel(x, meta_tokens, ssd_norm, ssd_w_in, ssd_conv_w, ssd_conv_b, ssd_dt_bias, ssd_a_log, ssd_d_skip, ssd_gate_norm, ssd_w_out, kv_norm, w_kv, sb_norm, sb_w_q, sb_w_o, ffn_norm, ffn_w_up, ffn_conv_w, ffn_conv_b, ffn_w_down, final_norm, loss_target, m_meta_tokens, m_ssd_norm, m_ssd_w_in, m_ssd_conv_w, m_ssd_conv_b, m_ssd_dt_bias, m_ssd_a_log, m_ssd_d_skip, m_ssd_gate_norm, m_ssd_w_out, m_kv_norm, m_w_kv, m_sb_norm, m_sb_w_q, m_sb_w_o, m_ffn_norm, m_ffn_w_up, m_ffn_conv_w, m_ffn_conv_b, m_ffn_w_down, m_final_norm, v_meta_tokens, v_ssd_norm, v_ssd_w_in, v_ssd_conv_w, v_ssd_conv_b, v_ssd_dt_bias, v_ssd_a_log, v_ssd_d_skip, v_ssd_gate_norm, v_ssd_w_out, v_kv_norm, v_w_kv, v_sb_norm, v_sb_w_q, v_sb_w_o, v_ffn_norm, v_ffn_w_up, v_ffn_conv_w, v_ffn_conv_b, v_ffn_w_down, v_final_norm):
    local = dict(zip(WEIGHT_NAMES, (meta_tokens, ssd_norm, ssd_w_in, ssd_conv_w, ssd_conv_b, ssd_dt_bias, ssd_a_log, ssd_d_skip, ssd_gate_norm, ssd_w_out, kv_norm, w_kv, sb_norm, sb_w_q, sb_w_o, ffn_norm, ffn_w_up, ffn_conv_w, ffn_conv_b, ffn_w_down, final_norm)))
    mom = dict(zip(WEIGHT_NAMES, (m_meta_tokens, m_ssd_norm, m_ssd_w_in, m_ssd_conv_w, m_ssd_conv_b, m_ssd_dt_bias, m_ssd_a_log, m_ssd_d_skip, m_ssd_gate_norm, m_ssd_w_out, m_kv_norm, m_w_kv, m_sb_norm, m_sb_w_q, m_sb_w_o, m_ffn_norm, m_ffn_w_up, m_ffn_conv_w, m_ffn_conv_b, m_ffn_w_down, m_final_norm)))
    var = dict(zip(WEIGHT_NAMES, (v_meta_tokens, v_ssd_norm, v_ssd_w_in, v_ssd_conv_w, v_ssd_conv_b, v_ssd_dt_bias, v_ssd_a_log, v_ssd_d_skip, v_ssd_gate_norm, v_ssd_w_out, v_kv_norm, v_w_kv, v_sb_norm, v_sb_w_q, v_sb_w_o, v_ffn_norm, v_ffn_w_up, v_ffn_conv_w, v_ffn_conv_b, v_ffn_w_down, v_final_norm)))
    seq, d = x.shape[1], x.shape[2]
    cfg = _make_cfg(d, seq, ffn_w_down.shape[1] * N_DEV)

    first = tuple(n for n in MATMUL_WEIGHTS if n not in LATE_WEIGHTS)
    big = _all_gather(_pack([local[n] for n in first], BF16), name="gather_weights")
    small = _all_gather(_pack([local[n] for n in SMALL_SHARDED], F32), name="gather_small")

    def unpacked(names, buf):
        return {n: _unshard(stacked, SHARD_AXIS[n])
                for n, stacked in zip(names, _unpack(buf, [local[n].shape for n in names], lead=1))}

    full = {n: local[n] for n in REPLICATED}
    full.update(unpacked(first, big))
    full.update(unpacked(SMALL_SHARDED, small))
    late_weights = (_pack([local[n] for n in LATE_WEIGHTS], BF16),
                    lambda gathered: _prepare_late(unpacked(LATE_WEIGHTS, gathered), cfg))
    full_shape = {n: tuple(s * (N_DEV if a == SHARD_AXIS.get(n) else 1) for a, s in enumerate(local[n].shape))
                  for n in WEIGHT_NAMES}

    late = tuple(n for n in SHARDED if n not in EARLY_GRADS)
    for_peers = lambda gr, names: _pack([_to_shards(gr[n].reshape(full_shape[n]), SHARD_AXIS[n]) for n in names], BF16,
                                        lead=1)
    loss, grad_x, grads, got_early = _local_step(x[0], loss_target[0], _prepare(full, cfg), cfg,
                                                 exchange=lambda done: for_peers(done, EARLY_GRADS),
                                                 late_weights=late_weights)
    g_early = _sum_slots(got_early, name="sum_grads_early")
    g_late = _sum_slots(_all_to_all(for_peers(grads, late), name="scatter_grads"), name="sum_grads")
    rep = _all_gather(_pack([grads[n].reshape(local[n].shape) for n in REPLICATED], F32), name="gather_rep_grads")
    g_rep = _sum_slots(rep, name="sum_rep_grads")
    fill = jnp.zeros(((-(g_early.shape[0] + g_late.shape[0] + g_rep.shape[0])) % ADAM_ROWS, 128), F32)
    g_flat = jnp.concatenate([g_early, g_late, g_rep, fill], axis=0)

    order = EARLY_GRADS + late + REPLICATED
    flat = lambda src: jnp.concatenate([_pack([src[n] for n in order], F32), fill], axis=0)
    delta, new_m, new_v = _adamw(flat(local), g_flat, flat(mom), flat(var), name="adamw")
    shapes = [local[n].shape for n in order]
    pick = lambda buf: dict(zip(order, _unpack(buf, shapes)))
    g_out, d_out, m_out, v_out = pick(g_flat), pick(delta), pick(new_m), pick(new_v)
    loss = lax.psum(loss, MESH_AXES)
    return (loss, grad_x[None], *[g_out[n] for n in WEIGHT_NAMES], *[d_out[n] for n in WEIGHT_NAMES],
            *[m_out[n] for n in WEIGHT_NAMES], *[v_out[n] for n in WEIGHT_NAMES])
```
